```python
import jax, jax.numpy as jnp
from jax import lax
import numpy as np

D_MODEL = 1024
BATCH = 8
SEQ = 4096
DEPTH = 1

HGRN_HEADS = 4
HGRN_HEAD_DIM = 128
HGRN_WIDTH = HGRN_HEADS * HGRN_HEAD_DIM
HGRN_CHUNK = 64
FOX_HEADS = 8
FOX_HEAD_DIM = 64
FOX_WIDTH = FOX_HEADS * FOX_HEAD_DIM
FOX_QBLOCK = 128
PEER_HEADS = 8
PEER_DK = 256
PEER_DK_HALF = PEER_DK // 2
N_KEYS = 128
N_EXPERTS = N_KEYS * N_KEYS
PEER_TOPK = 16
PEER_CHUNK = 128

RMS_EPS = 1e-6

_IN_SIZES = (HGRN_WIDTH, HGRN_WIDTH, HGRN_WIDTH, HGRN_WIDTH,
             FOX_WIDTH, FOX_WIDTH, FOX_WIDTH, FOX_HEADS,
             D_MODEL, D_MODEL)
IN_WIDTH = sum(_IN_SIZES)
IN_SPLITS = tuple(int(v) for v in np.cumsum(_IN_SIZES)[:-1])

kernel_name = "hgrn2_fox_gated_peer_block"


def rms_norm(x, w):
    xf = x.astype(jnp.float32)
    y = xf * lax.rsqrt(jnp.mean(xf * xf, axis=-1, keepdims=True) + RMS_EPS)
    return (y * w.astype(jnp.float32)).astype(x.dtype)


def hgrn2_mixer(q, f_logit, i, g, lb, norm_w):
    B, S, _ = q.shape
    C, H, dh = HGRN_CHUNK, HGRN_HEADS, HGRN_HEAD_DIM
    n_chunks = S // C
    f32 = jnp.float32
    qf = jax.nn.silu(q.astype(f32))
    f = lb + (1.0 - lb) * jax.nn.sigmoid(f_logit.astype(f32))
    log_f = jnp.log(f)
    k = 1.0 - f

    def to_chunks(t):
        return t.reshape(B, n_chunks, C, H, dh).transpose(1, 0, 3, 2, 4)

    qc, kc, vc, lfc = to_chunks(qf), to_chunks(k), to_chunks(i.astype(f32)), to_chunks(log_f)
    causal = jnp.tril(jnp.ones((C, C), dtype=bool))

    def step(state, inp):
        qb, kb, vb, lfb = inp
        cum = jnp.cumsum(lfb, axis=-2)
        diff = cum[..., :, None, :] - cum[..., None, :, :]
        decay = jnp.exp(jnp.where(causal[:, :, None], diff, -jnp.inf))
        scores = jnp.einsum('bhtk,bhsk,bhtsk->bhts', qb, kb, decay)
        o = jnp.einsum('bhts,bhsv->bhtv', scores, vb) \
            + jnp.einsum('bhtk,bhkv->bhtv', qb * jnp.exp(cum), state)
        last = cum[..., -1:, :]
        new_state = jnp.exp(last[..., 0, :])[..., None] * state \
            + jnp.einsum('bhsk,bhsv->bhkv', kb * jnp.exp(last - cum), vb)
        return new_state, o

    s0 = jnp.zeros((B, H, dh, dh), f32)
    _, o = lax.scan(step, s0, (qc, kc, vc, lfc))
    o = o.transpose(1, 0, 3, 2, 4).reshape(B, S, H, dh)
    o = o * lax.rsqrt(jnp.mean(o * o, axis=-1, keepdims=True) + RMS_EPS)
    o = o * norm_w.astype(f32).reshape(H, dh)
    o = o.reshape(B, S, HGRN_WIDTH) * jax.nn.silu(g.astype(f32))
    return o.astype(q.dtype)


def fox_attention(q, k, v, f_logit, f_bias):
    B, S, _ = q.shape
    H, dh, QB = FOX_HEADS, FOX_HEAD_DIM, FOX_QBLOCK
    nb = S // QB
    log_f = jax.nn.log_sigmoid(f_logit.astype(jnp.float32) + f_bias.astype(jnp.float32))
    c = jnp.cumsum(log_f, axis=1).transpose(0, 2, 1)
    scale = dh ** -0.5
    qh = q.reshape(B, S, H, dh).transpose(0, 2, 1, 3) * scale
    kh = k.reshape(B, S, H, dh).transpose(0, 2, 1, 3)
    vh = v.reshape(B, S, H, dh).transpose(0, 2, 1, 3)
    q_blocks = qh.reshape(B, H, nb, QB, dh).transpose(2, 0, 1, 3, 4)
    cq_blocks = c.reshape(B, H, nb, QB).transpose(2, 0, 1, 3)
    key_pos = jnp.arange(S)

    def block(args):
        qb, cqb, bi = args
        s = jnp.einsum('bhqd,bhkd->bhqk', qb, kh).astype(jnp.float32)
        s = s + cqb[..., :, None] - c[..., None, :]
        q_pos = bi * QB + jnp.arange(QB)
        mask = key_pos[None, :] <= q_pos[:, None]
        p = jax.nn.softmax(jnp.where(mask, s, -jnp.inf), axis=-1)
        return jnp.einsum('bhqk,bhkd->bhqd', p.astype(vh.dtype), vh)

    o = lax.map(block, (q_blocks, cq_blocks, jnp.arange(nb)))
    return o.transpose(1, 0, 3, 2, 4).reshape(B, S, FOX_WIDTH)


def peer_ffn(x, w_q, sub_keys, u_table, v_table):
    B, S, D = x.shape
    T = B * S
    xt = x.reshape(T, D)
    q = (xt @ w_q).reshape(T, PEER_HEADS, 2, PEER_DK_HALF)
    s = jnp.einsum('thpd,hpnd->thpn', q, sub_keys).astype(jnp.float32)
    s1, i1 = lax.top_k(s[:, :, 0], PEER_TOPK)
    s2, i2 = lax.top_k(s[:, :, 1], PEER_TOPK)
    cand = (s1[..., :, None] + s2[..., None, :]).reshape(T, PEER_HEADS, PEER_TOPK * PEER_TOPK)
    cand_idx = (i1[..., :, None] * N_KEYS + i2[..., None, :]).reshape(T, PEER_HEADS, PEER_TOPK * PEER_TOPK)
    top_s, pos = lax.top_k(cand, PEER_TOPK)
    idx = jnp.take_along_axis(cand_idx, pos, axis=-1)
    gates = jax.nn.softmax(top_s, axis=-1).astype(x.dtype)
    nc = T // PEER_CHUNK
    HK = PEER_HEADS * PEER_TOPK

    def chunk(args):
        xc, ic, gc = args
        u = u_table[ic]
        a = jax.nn.gelu(jnp.einsum('cd,ced->ce', xc, u), approximate=False) * gc
        return jnp.einsum('ce,ced->cd', a, v_table[ic])

    out = lax.map(chunk, (xt.reshape(nc, PEER_CHUNK, D),
                          idx.reshape(nc, PEER_CHUNK, HK),
                          gates.reshape(nc, PEER_CHUNK, HK)))
    return out.reshape(B, S, D)


def setup_inputs(seed: int = 0) -> dict:
    key = jax.random.key(seed)
    ks = jax.random.split(key, 16)
    f32 = jnp.float32
    nrm = lambda k, shape, s: jax.random.normal(k, shape, f32) * s
    return {
        "x": nrm(ks[0], (BATCH, SEQ, D_MODEL), 1.0),
        "norm_mix_w": 1.0 + nrm(ks[1], (DEPTH, D_MODEL), 0.02),
        "w_in": nrm(ks[2], (DEPTH, D_MODEL, IN_WIDTH), D_MODEL ** -0.5),
        "hgrn_lb_logits": nrm(ks[3], (DEPTH + 1, HGRN_WIDTH), 0.5),
        "hgrn_norm_w": 1.0 + nrm(ks[4], (DEPTH, HGRN_WIDTH), 0.02),
        "fox_f_bias": 2.0 + nrm(ks[5], (DEPTH, FOX_HEADS), 0.5),
        "w_branch_hgrn": nrm(ks[6], (DEPTH, HGRN_WIDTH, D_MODEL), HGRN_WIDTH ** -0.5),
        "w_branch_fox": nrm(ks[7], (DEPTH, FOX_WIDTH, D_MODEL), FOX_WIDTH ** -0.5),
        "w_out": nrm(ks[8], (DEPTH, D_MODEL, D_MODEL), D_MODEL ** -0.5),
        "norm_ffn_w": 1.0 + nrm(ks[9], (DEPTH, D_MODEL), 0.02),
        "peer_w_q": nrm(ks[10], (DEPTH, D_MODEL, PEER_HEADS * PEER_DK), D_MODEL ** -0.5),
        "peer_sub_keys": nrm(ks[11], (DEPTH, PEER_HEADS, 2, N_KEYS, PEER_DK_HALF), PEER_DK_HALF ** -0.5),
        "peer_u": nrm(ks[12], (DEPTH, N_EXPERTS, D_MODEL), D_MODEL ** -0.5),
        "peer_v": nrm(ks[13], (DEPTH, N_EXPERTS, D_MODEL), 0.5),
        "norm_final_w": 1.0 + nrm(ks[14], (D_MODEL,), 0.02),
    }


def reference(x, norm_mix_w, w_in, hgrn_lb_logits, hgrn_norm_w, fox_f_bias, w_branch_hgrn,
              w_branch_fox, w_out, norm_ffn_w, peer_w_q, peer_sub_keys, peer_u, peer_v,
              norm_final_w):
    lb_all = jnp.cumsum(jax.nn.softmax(hgrn_lb_logits.astype(jnp.float32), axis=0), axis=0)
    h = x
    for l in range(DEPTH):
        xn = rms_norm(h, norm_mix_w[l])
        proj = xn @ w_in[l]
        hq, hf, hi, hg, fq, fk, fv, ff, ga, gb = jnp.split(proj, IN_SPLITS, axis=-1)
        a = hgrn2_mixer(hq, hf, hi, hg, lb_all[l], hgrn_norm_w[l])
        b = fox_attention(fq, fk, fv, ff, fox_f_bias[l])
        merged = jax.nn.sigmoid(ga) * (a @ w_branch_hgrn[l]) \
            + jax.nn.sigmoid(gb) * (b @ w_branch_fox[l])
        h = h + merged @ w_out[l]
        h = h + peer_ffn(rms_norm(h, norm_ffn_w[l]), peer_w_q[l], peer_sub_keys[l],
                         peer_u[l], peer_v[l])
    return rms_norm(h, norm_final_w)
```

```python
import functools
import math

import jax
import jax.numpy as jnp
from jax import lax
from jax.experimental import pallas as pl
from jax.experimental.pallas import tpu as pltpu

F32 = jnp.float32
BF16 = jnp.bfloat16
RMS_EPS = 1e-6
NEG_BIG = -1e30

HGRN_HEADS = 4
HGRN_DH = 128
HGRN_W = HGRN_HEADS * HGRN_DH
FOX_HEADS = 8
FOX_DH = 64
FOX_W = FOX_HEADS * FOX_DH
PEER_HEADS = 8
PEER_HALF = 128
N_KEYS = 128
TOPK = 16
LANES = 128

VMEM_LIMIT = 56 * 1024 * 1024


def _nt(a, b):
    return lax.dot_general(a, b, (((1,), (1,)), ((), ())), preferred_element_type=F32)


def _tn(a, b):
    return lax.dot_general(a, b, (((0,), (0,)), ((), ())), preferred_element_type=F32)


def _nn(a, b):
    return jnp.dot(a, b, preferred_element_type=F32)


def _split_dot(fn, tri, x):
    hi = x.astype(BF16)
    lo = (x - hi.astype(F32)).astype(BF16)
    return fn(tri, hi) + fn(tri, lo)


def _log_sigmoid(x):
    return jnp.minimum(x, 0.0) - jnp.log1p(jnp.exp(-jnp.abs(x)))


def _rms(x, w):
    return x * lax.rsqrt(jnp.mean(x * x, axis=-1, keepdims=True) + RMS_EPS) * w


def _inproj_kernel(x_ref, nw_ref, w_ref, wffc_ref, wffr_ref, fbc_ref, fbr_ref,
                   hg_ref, fox_ref, gate_ref, ccol_ref, crow_ref,
                   carry_col, carry_row):
    i = pl.program_id(1)
    tm = x_ref.shape[0]

    @pl.when(i == 0)
    def _():
        carry_col[...] = jnp.zeros_like(carry_col)
        carry_row[...] = jnp.zeros_like(carry_row)

    xn = _rms(x_ref[...], nw_ref[...]).astype(BF16)
    proj = _nn(xn, w_ref[...])
    n_h = 4 * HGRN_W
    hg_ref[...] = proj[:, :n_h]
    fox_ref[:, :FOX_W] = (proj[:, n_h:n_h + FOX_W] * (FOX_DH ** -0.5)).astype(BF16)
    fox_ref[:, FOX_W:] = proj[:, n_h + FOX_W:n_h + 3 * FOX_W].astype(BF16)
    gate_ref[...] = jax.nn.sigmoid(proj[:, n_h + 3 * FOX_W:])

    r = lax.broadcasted_iota(jnp.int32, (tm, tm), 0)
    c = lax.broadcasted_iota(jnp.int32, (tm, tm), 1)
    tril = (c <= r).astype(BF16)
    triu = (r <= c).astype(BF16)
    ls_col = _log_sigmoid(_nn(xn, wffc_ref[...]) + fbc_ref[...])
    ccol = _split_dot(_nn, tril, ls_col) + carry_col[...]
    ccol_ref[...] = ccol
    carry_col[...] = ccol[tm - 1:tm, :]
    ls_row = _log_sigmoid(_nt(wffr_ref[...], xn) + fbr_ref[:, 0:1])
    hi = ls_row.astype(BF16)
    lo = (ls_row - hi.astype(F32)).astype(BF16)
    crow = _nn(hi, triu) + _nn(lo, triu) + carry_row[:, 0:1]
    crow_ref[...] = crow
    carry_row[...] = jnp.broadcast_to(crow[:, tm - 1:tm], carry_row.shape)


def _in_proj(x2d, nw, w_all, wff_col, wff_row, fb_col, fb_row, batch, seq, tm):
    T, D = x2d.shape
    nt = seq // tm
    n_all = w_all.shape[1]
    row = lambda b, i: (b * nt + i, 0)
    const = lambda b, i: (0, 0)
    return pl.pallas_call(
        _inproj_kernel,
        grid=(batch, nt),
        in_specs=[
            pl.BlockSpec((tm, D), row),
            pl.BlockSpec((1, D), const),
            pl.BlockSpec((D, n_all), const),
            pl.BlockSpec((D, LANES), const),
            pl.BlockSpec((FOX_HEADS, D), const),
            pl.BlockSpec((1, LANES), const),
            pl.BlockSpec((FOX_HEADS, LANES), const),
        ],
        out_specs=[
            pl.BlockSpec((tm, 4 * HGRN_W), row),
            pl.BlockSpec((tm, 3 * FOX_W), row),
            pl.BlockSpec((tm, 2 * D), row),
            pl.BlockSpec((tm, LANES), row),
            pl.BlockSpec((FOX_HEADS, tm), lambda b, i: (0, b * nt + i)),
        ],
        out_shape=[
            jax.ShapeDtypeStruct((T, 4 * HGRN_W), F32),
            jax.ShapeDtypeStruct((T, 3 * FOX_W), BF16),
            jax.ShapeDtypeStruct((T, 2 * D), F32),
            jax.ShapeDtypeStruct((T, LANES), F32),
            jax.ShapeDtypeStruct((FOX_HEADS, T), F32),
        ],
        scratch_shapes=[pltpu.VMEM((1, LANES), F32), pltpu.VMEM((FOX_HEADS, LANES), F32)],
        compiler_params=pltpu.CompilerParams(
            dimension_semantics=("arbitrary", "arbitrary"), vmem_limit_bytes=VMEM_LIMIT),
        name="in_proj",
    )(x2d, nw, w_all, wff_col, wff_row, fb_col, fb_row)


HGRN_CHUNK = 128
HGRN_SUB = 16


def _hgrn_kernel(hg_ref, lbl_ref, nw_ref, rsel_ref, a_ref, state_ref):
    ci = pl.program_id(1)
    C = HGRN_CHUNK
    dh = HGRN_DH

    @pl.when(ci == 0)
    def _():
        state_ref[...] = jnp.zeros_like(state_ref)

    lg = lbl_ref[...]
    e = jnp.exp(lg - jnp.max(lg, axis=0, keepdims=True))
    lb_all = e[0:1, :] / jnp.sum(e, axis=0, keepdims=True)

    r = lax.broadcasted_iota(jnp.int32, (C, C), 0)
    c = lax.broadcasted_iota(jnp.int32, (C, C), 1)
    tril = (c <= r).astype(BF16)
    sub = HGRN_SUB
    sh = sub.bit_length() - 1
    diag_mask = ((r >> sh) == (c >> sh)) & ((c & (sub - 1)) <= (r & (sub - 1)))
    levels = []
    m = sub
    while m < C:
        sh = m.bit_length() - 1
        levels.append((m, ((r >> (sh + 1)) == (c >> (sh + 1))) & (((r >> sh) & 1) == 1) & (((c >> sh) & 1) == 0)))
        m *= 2

    for h in range(HGRN_HEADS):
        sl = slice(h * dh, (h + 1) * dh)
        qraw = hg_ref[:, sl]
        q = qraw * jax.nn.sigmoid(qraw)
        lb = lb_all[:, sl]
        f = lb + (1.0 - lb) * jax.nn.sigmoid(hg_ref[:, HGRN_W + h * dh:HGRN_W + (h + 1) * dh])
        logf = jnp.log(f)
        k = 1.0 - f
        v = hg_ref[:, 2 * HGRN_W + h * dh:2 * HGRN_W + (h + 1) * dh]
        g = hg_ref[:, 3 * HGRN_W + h * dh:3 * HGRN_W + (h + 1) * dh]
        v_bf = v.astype(BF16)
        cum = _split_dot(_nn, tril, logf)

        nb = C // sub
        q3 = q.reshape(nb, sub, dh)
        k3 = k.reshape(nb, sub, dh)
        c3 = cum.reshape(nb, sub, dh)
        parts = []
        for s in range(sub):
            kb = jnp.broadcast_to(k3[:, s:s + 1, :], (nb, sub, dh))
            cb = jnp.broadcast_to(c3[:, s:s + 1, :], (nb, sub, dh))
            es = q3 * kb * jnp.exp(jnp.minimum(c3 - cb, 0.0))
            parts.append(es.reshape(C, dh).astype(BF16))
        p_mat = jnp.where(diag_mask, _nn(jnp.concatenate(parts, axis=1), rsel_ref[...]), 0.0)

        for m, mask in levels:
            nbm = C // m
            qm = q.reshape(nbm, m, dh)
            km = k.reshape(nbm, m, dh)
            cm = cum.reshape(nbm, m, dh)
            end = cm[:, m - 1:m, :]
            prev_end = jnp.concatenate([jnp.zeros((1, 1, dh), F32), end[:nbm - 1]], axis=0)
            qd = qm * jnp.exp(jnp.minimum(cm - jnp.broadcast_to(prev_end, (nbm, m, dh)), 0.0))
            kd = km * jnp.exp(jnp.minimum(jnp.broadcast_to(end, (nbm, m, dh)) - cm, 0.0))
            sc = _nt(qd.reshape(C, dh).astype(BF16), kd.reshape(C, dh).astype(BF16))
            p_mat = p_mat + jnp.where(mask, sc, 0.0)

        st = state_ref[h]
        o = _nn(p_mat.astype(BF16), v_bf) + _nt((q * jnp.exp(cum)).astype(BF16), st.astype(BF16))
        last = cum[C - 1:C, :]
        kdec = (k * jnp.exp(last - cum)).astype(BF16)
        state_ref[h] = jnp.exp(last) * st + _tn(v_bf, kdec)

        o = o * lax.rsqrt(jnp.mean(o * o, axis=-1, keepdims=True) + RMS_EPS) * nw_ref[:, sl]
        a_ref[:, sl] = (o * (g * jax.nn.sigmoid(g))).astype(BF16)


def _hgrn(hg, lb_logits, norm_w, rsel, batch, seq):
    T = hg.shape[0]
    C = HGRN_CHUNK
    nc = seq // C
    row = lambda b, i: (b * nc + i, 0)
    const = lambda b, i: (0, 0)
    return pl.pallas_call(
        _hgrn_kernel,
        grid=(batch, nc),
        in_specs=[
            pl.BlockSpec((C, 4 * HGRN_W), row),
            pl.BlockSpec(lb_logits.shape, const),
            pl.BlockSpec((1, HGRN_W), const),
            pl.BlockSpec(rsel.shape, const),
        ],
        out_specs=pl.BlockSpec((C, HGRN_W), row),
        out_shape=jax.ShapeDtypeStruct((T, HGRN_W), BF16),
        scratch_shapes=[pltpu.VMEM((HGRN_HEADS, HGRN_DH, HGRN_DH), F32)],
        compiler_params=pltpu.CompilerParams(
            dimension_semantics=("arbitrary", "arbitrary"), vmem_limit_bytes=VMEM_LIMIT),
        name="hgrn",
    )(hg, lb_logits, norm_w, rsel)


def _fox_kernel(q_ref, k_ref, v_ref, ccol_ref, crow_ref, o_ref, m_ref, acc_ref):
    qi = pl.program_id(1)
    tq = q_ref.shape[0]
    tk = tq
    lane = lax.broadcasted_iota(jnp.int32, (1, LANES), 1)
    low = lane < FOX_DH
    sel_lo = jnp.where(low, 1.0, 0.0).astype(BF16)
    sel_hi = jnp.where(low, 0.0, 1.0).astype(BF16)
    ones_lane = (FOX_DH, 0)
    one_hot = tuple(jnp.where(lane == ol, 1.0, 0.0).astype(BF16) for ol in ones_lane)
    rr = lax.broadcasted_iota(jnp.int32, (tq, tk), 0)
    cc = lax.broadcasted_iota(jnp.int32, (tq, tk), 1)
    causal = cc <= rr

    for p in range(FOX_HEADS // 2):
        cols = slice(p * LANES, (p + 1) * LANES)
        q = q_ref[:, cols]
        qs = (q * sel_lo, q * sel_hi)
        cq = (ccol_ref[:, 2 * p:2 * p + 1], ccol_ref[:, 2 * p + 1:2 * p + 2])
        m_ref[...] = jnp.full_like(m_ref, NEG_BIG)
        acc_ref[...] = jnp.zeros_like(acc_ref)

        def kv_block(j, masked):
            start = pl.multiple_of(j * tk, tk)
            kb = k_ref[pl.ds(start, tk), cols]
            vb = v_ref[pl.ds(start, tk), cols]
            vs = (vb * sel_lo + one_hot[0], vb * sel_hi + one_hot[1])
            for hh in range(2):
                ck = crow_ref[2 * p + hh:2 * p + hh + 1, pl.ds(start, tk)]
                s = _nt(qs[hh], kb) + (cq[hh] - ck)
                if masked:
                    s = jnp.where(causal, s, NEG_BIG)
                m_prev = m_ref[hh]
                m_next = jnp.maximum(m_prev, jnp.max(s, axis=1, keepdims=True))
                pexp = jnp.exp(s - m_next[:, 0:1])
                alpha = jnp.exp(m_prev - m_next)
                acc_ref[hh] = alpha * acc_ref[hh] + _nn(pexp.astype(BF16), vs[hh])
                m_ref[hh] = m_next

        def body(j, carry):
            kv_block(j, False)
            return carry

        lax.fori_loop(0, qi, body, 0)
        kv_block(qi, True)

        a0 = acc_ref[0]
        a1 = acc_ref[1]
        o0 = a0 / a0[:, ones_lane[0]:ones_lane[0] + 1]
        o1 = a1 / a1[:, ones_lane[1]:ones_lane[1] + 1]
        o_ref[:, cols] = jnp.where(low, o0, o1).astype(BF16)


def _fox(fox, ccol, crow, batch, seq, tq):
    T = fox.shape[0]
    nq = seq // tq
    return pl.pallas_call(
        _fox_kernel,
        grid=(batch, nq),
        in_specs=[
            pl.BlockSpec((tq, FOX_W), lambda b, i: (b * nq + i, 0)),
            pl.BlockSpec((seq, FOX_W), lambda b, i: (b, 1)),
            pl.BlockSpec((seq, FOX_W), lambda b, i: (b, 2)),
            pl.BlockSpec((tq, LANES), lambda b, i: (b * nq + i, 0)),
            pl.BlockSpec((FOX_HEADS, seq), lambda b, i: (0, b)),
        ],
        out_specs=pl.BlockSpec((tq, FOX_W), lambda b, i: (b * nq + i, 0)),
        out_shape=jax.ShapeDtypeStruct((T, FOX_W), BF16),
        scratch_shapes=[pltpu.VMEM((2, tq, LANES), F32), pltpu.VMEM((2, tq, LANES), F32)],
        compiler_params=pltpu.CompilerParams(
            dimension_semantics=("arbitrary", "arbitrary"), vmem_limit_bytes=VMEM_LIMIT),
        name="fox",
    )(fox, fox, fox, ccol, crow)


def _merge_kernel(x_ref, a_ref, b_ref, gate_ref, wa_ref, wb_ref, wo_ref, nw_ref, wq_ref, keys_ref,
                  h1_ref, xn_ref, st_ref):
    D = x_ref.shape[1]
    merged = gate_ref[:, :D] * _nn(a_ref[...], wa_ref[...]) + gate_ref[:, D:] * _nn(b_ref[...], wb_ref[...])
    h1 = x_ref[...] + _nn(merged.astype(BF16), wo_ref[...])
    h1_ref[...] = h1
    xn = _rms(h1, nw_ref[...]).astype(BF16)
    xn_ref[...] = xn
    q = _nn(xn, wq_ref[...]).astype(BF16)
    for hp in range(2 * PEER_HEADS):
        st_ref[hp] = _nt(keys_ref[hp], q[:, hp * PEER_HALF:(hp + 1) * PEER_HALF])


def _merge(x2d, a, b, gates, wa, wb, wo, nw, wq, keys, tm):
    T, D = x2d.shape
    row = lambda i: (i, 0)
    const = lambda i: (0, 0)
    return pl.pallas_call(
        _merge_kernel,
        grid=(T // tm,),
        in_specs=[
            pl.BlockSpec((tm, D), row),
            pl.BlockSpec((tm, HGRN_W), row),
            pl.BlockSpec((tm, FOX_W), row),
            pl.BlockSpec((tm, 2 * D), row),
            pl.BlockSpec(wa.shape, const),
            pl.BlockSpec(wb.shape, const),
            pl.BlockSpec(wo.shape, const),
            pl.BlockSpec((1, D), const),
            pl.BlockSpec(wq.shape, const),
            pl.BlockSpec(keys.shape, lambda i: (0, 0, 0)),
        ],
        out_specs=[
            pl.BlockSpec((tm, D), row),
            pl.BlockSpec((tm, D), row),
            pl.BlockSpec((2 * PEER_HEADS, N_KEYS, tm), lambda i: (0, 0, i)),
        ],
        out_shape=[
            jax.ShapeDtypeStruct((T, D), F32),
            jax.ShapeDtypeStruct((T, D), BF16),
            jax.ShapeDtypeStruct((2 * PEER_HEADS, N_KEYS, T), F32),
        ],
        compiler_params=pltpu.CompilerParams(
            dimension_semantics=("arbitrary",), vmem_limit_bytes=VMEM_LIMIT),
        name="merge",
    )(x2d, a, b, gates, wa, wb, wo, nw, wq, keys)


def _rank_pairs():
    n = TOPK + 1
    return [(r, c) for r in range(n) for c in range(n) if (r + 1) * (c + 1) <= n]


def _topk_kernel(st_ref, theta_ref, p1_ref, p2_ref):
    tt = st_ref.shape[2]

    def top_sorted(s):
        vals = []
        cur = s
        for _ in range(TOPK + 1):
            mx = jnp.max(cur, axis=0, keepdims=True)
            vals.append(mx)
            cur = jnp.where(cur == mx, -jnp.inf, cur)
        return vals

    tops = [[top_sorted(st_ref[2 * h + half]) for half in range(2)] for h in range(PEER_HEADS)]
    a = [jnp.concatenate([tops[h][0][r] for h in range(PEER_HEADS)], axis=0) for r in range(TOPK + 1)]
    b = [jnp.concatenate([tops[h][1][r] for h in range(PEER_HEADS)], axis=0) for r in range(TOPK + 1)]
    cand = [a[r] + b[c] for r, c in _rank_pairs()]
    tau = jnp.full((PEER_HEADS, tt), -jnp.inf, F32)
    nxt = jnp.full((PEER_HEADS, tt), -jnp.inf, F32)
    for xi in cand:
        cnt = jnp.zeros((PEER_HEADS, tt), F32)
        for xj in cand:
            cnt = cnt + jnp.where(xj >= xi, 1.0, 0.0)
        tau = jnp.maximum(tau, jnp.where(cnt >= TOPK, xi, -jnp.inf))
        nxt = jnp.maximum(nxt, jnp.where(cnt >= TOPK + 1, xi, -jnp.inf))
    top = a[0] + b[0]
    z = jnp.zeros((PEER_HEADS, tt), F32)
    for xi in cand:
        z = z + jnp.where(xi >= tau, jnp.exp(xi - top), 0.0)
    inv_z = 1.0 / z
    cut = 0.5 * (tau + nxt)
    for h in range(PEER_HEADS):
        s1 = st_ref[2 * h]
        s2 = st_ref[2 * h + 1]
        theta_ref[h] = cut[h:h + 1, :] - s1
        p1_ref[h] = jnp.exp(s1 - a[0][h:h + 1, :]) * inv_z[h:h + 1, :]
        p2_ref[h] = jnp.exp(s2 - b[0][h:h + 1, :])


def _topk(st, tt):
    n, nk, T = st.shape
    out = jax.ShapeDtypeStruct((PEER_HEADS, nk, T), F32)
    spec = pl.BlockSpec((PEER_HEADS, nk, tt), lambda i: (0, 0, i))
    return pl.pallas_call(
        _topk_kernel,
        grid=(T // tt,),
        in_specs=[pl.BlockSpec((n, nk, tt), lambda i: (0, 0, i))],
        out_specs=[spec, spec, spec],
        out_shape=[out, out, out],
        compiler_params=pltpu.CompilerParams(
            dimension_semantics=("arbitrary",), vmem_limit_bytes=VMEM_LIMIT),
        name="topk",
    )(st)


def _peer_kernel(xn_ref, u_ref, v_ref, theta_ref, p1_ref, s2_ref, p2_ref, h1_ref, nw_ref,
                 o_ref, acc_ref):
    j = pl.program_id(1)
    nj = pl.num_programs(1)
    eb = u_ref.shape[0]
    groups = eb // N_KEYS

    @pl.when(j == 0)
    def _():
        acc_ref[...] = jnp.zeros_like(acc_ref)

    ht = _nt(u_ref[...], xn_ref[...])
    act = 0.5 * ht * (1.0 + lax.erf(ht * (2.0 ** -0.5)))
    gate = [None] * groups
    for h in range(PEER_HEADS):
        s2 = s2_ref[h, 0]
        p2 = p2_ref[h]
        for gi in range(groups):
            i1 = j * groups + gi
            th = theta_ref[h, pl.ds(i1, 1), :]
            p1 = p1_ref[h, pl.ds(i1, 1), :]
            term = jnp.where(s2 >= th, p2, 0.0) * p1
            gate[gi] = term if gate[gi] is None else gate[gi] + term
    weighted = (act * jnp.concatenate(gate, axis=0)).astype(BF16)
    acc_ref[...] += _tn(weighted, v_ref[...])

    @pl.when(j == nj - 1)
    def _():
        o_ref[...] = _rms(h1_ref[...] + acc_ref[...], nw_ref[...])


def _peer(xn, u, v, theta, p1, st4, p2, h1, nw, tb, eb):
    T, D = xn.shape
    n_exp = u.shape[0]
    tok3 = lambda i, j: (0, 0, i)
    return pl.pallas_call(
        _peer_kernel,
        grid=(T // tb, n_exp // eb),
        in_specs=[
            pl.BlockSpec((tb, D), lambda i, j: (i, 0)),
            pl.BlockSpec((eb, D), lambda i, j: (j, 0)),
            pl.BlockSpec((eb, D), lambda i, j: (j, 0)),
            pl.BlockSpec((PEER_HEADS, N_KEYS, tb), tok3),
            pl.BlockSpec((PEER_HEADS, N_KEYS, tb), tok3),
            pl.BlockSpec((PEER_HEADS, 1, N_KEYS, tb), lambda i, j: (0, 1, 0, i)),
            pl.BlockSpec((PEER_HEADS, N_KEYS, tb), tok3),
            pl.BlockSpec((tb, D), lambda i, j: (i, 0)),
            pl.BlockSpec((1, D), lambda i, j: (0, 0)),
        ],
        out_specs=pl.BlockSpec((tb, D), lambda i, j: (i, 0)),
        out_shape=jax.ShapeDtypeStruct((T, D), F32),
        scratch_shapes=[pltpu.VMEM((tb, D), F32)],
        compiler_params=pltpu.CompilerParams(
            dimension_semantics=("arbitrary", "arbitrary"), vmem_limit_bytes=VMEM_LIMIT),
        name="peer",
    )(xn, u, v, theta, p1, st4, p2, h1, nw)


def _block_select_matrix():
    part = jnp.arange(HGRN_SUB * HGRN_DH, dtype=jnp.int32) // HGRN_DH
    col = jnp.arange(HGRN_CHUNK, dtype=jnp.int32) % HGRN_SUB
    return (part[:, None] == col[None, :]).astype(BF16)


def _forward(x, norm_mix_w, w_in, hgrn_lb_logits, hgrn_norm_w, fox_f_bias, w_branch_hgrn,
             w_branch_fox, w_out, norm_ffn_w, peer_w_q, peer_sub_keys, peer_u, peer_v,
             norm_final_w, *, tm_in, tq, tm_merge, tt, tb, eb):
    B, S, D = x.shape
    T = B * S
    x2d = x.reshape(T, D)
    n_h = 4 * HGRN_W
    n_f = 3 * FOX_W
    wi = w_in[0]
    w_all = jnp.concatenate([wi[:, :n_h + n_f], wi[:, n_h + n_f + FOX_HEADS:]], axis=1).astype(BF16)
    wff = wi[:, n_h + n_f:n_h + n_f + FOX_HEADS]
    wff_col = jnp.pad(wff, ((0, 0), (0, LANES - FOX_HEADS))).astype(BF16)
    wff_row = wff.T.astype(BF16)
    fb = fox_f_bias[0].astype(F32)
    fb_col = jnp.pad(fb, (0, LANES - FOX_HEADS)).reshape(1, LANES)
    fb_row = jnp.broadcast_to(fb.reshape(FOX_HEADS, 1), (FOX_HEADS, LANES))

    hg, fox, gates, ccol, crow = _in_proj(
        x2d, norm_mix_w[0].reshape(1, D), w_all, wff_col, wff_row, fb_col, fb_row, B, S, tm_in)
    a = _hgrn(hg, hgrn_lb_logits, hgrn_norm_w[0].reshape(1, HGRN_W), _block_select_matrix(), B, S)
    b = _fox(fox, ccol, crow, B, S, tq)
    keys = peer_sub_keys[0].reshape(2 * PEER_HEADS, N_KEYS, PEER_HALF).astype(BF16)
    h1, xn2, st = _merge(
        x2d, a, b, gates, w_branch_hgrn[0].astype(BF16), w_branch_fox[0].astype(BF16),
        w_out[0].astype(BF16), norm_ffn_w[0].reshape(1, D), peer_w_q[0].astype(BF16), keys, tm_merge)
    theta, p1, p2 = _topk(st, tt)
    st4 = st.reshape(PEER_HEADS, 2, N_KEYS, T)
    out = _peer(xn2, peer_u[0].astype(BF16), peer_v[0].astype(BF16), theta, p1, st4, p2, h1,
                norm_final_w.reshape(1, D), tb, eb)
    return out.reshape(B, S, D)


def kernel(x, norm_mix_w, w_in, hgrn_lb_logits, hgrn_norm_w, fox_f_bias, w_branch_hgrn, w_branch_fox, w_out, norm_ffn_w, peer_w_q, peer_sub_keys, peer_u, peer_v, norm_final_w):
    return _forward(x, norm_mix_w, w_in, hgrn_lb_logits, hgrn_norm_w, fox_f_bias, w_branch_hgrn,
                    w_branch_fox, w_out, norm_ffn_w, peer_w_q, peer_sub_keys, peer_u, peer_v,
                    norm_final_w, tm_in=256, tq=512, tm_merge=256, tt=256, tb=512, eb=256)
```

```python
import functools
import math

import jax
import jax.numpy as jnp
from jax import lax
from jax.experimental import pallas as pl
from jax.experimental.pallas import tpu as pltpu

F32 = jnp.float32
BF16 = jnp.bfloat16
RMS_EPS = 1e-6
NEG_BIG = -1e30

HGRN_HEADS = 4
HGRN_DH = 128
HGRN_W = HGRN_HEADS * HGRN_DH
FOX_HEADS = 8
FOX_DH = 64
FOX_W = FOX_HEADS * FOX_DH
PEER_HEADS = 8
PEER_HALF = 128
N_KEYS = 128
TOPK = 16
LANES = 128

VMEM_LIMIT = 56 * 1024 * 1024


def _nt(a, b):
    return lax.dot_general(a, b, (((1,), (1,)), ((), ())), preferred_element_type=F32)


def _tn(a, b):
    return lax.dot_general(a, b, (((0,), (0,)), ((), ())), preferred_element_type=F32)


def _nn(a, b):
    return jnp.dot(a, b, preferred_element_type=F32)


def _split_dot(fn, tri, x):
    hi = x.astype(BF16)
    lo = (x - hi.astype(F32)).astype(BF16)
    return fn(tri, hi) + fn(tri, lo)


def _log_sigmoid(x):
    return jnp.minimum(x, 0.0) - jnp.log1p(jnp.exp(-jnp.abs(x)))


def _rms(x, w):
    return x * lax.rsqrt(jnp.mean(x * x, axis=-1, keepdims=True) + RMS_EPS) * w


def _inproj_kernel(x_ref, nw_ref, w_ref, wffc_ref, wffr_ref, fbc_ref, fbr_ref,
                   hg_ref, fox_ref, gate_ref, ccol_ref, crow_ref,
                   carry_col, carry_row):
    i = pl.program_id(1)
    tm = x_ref.shape[0]

    @pl.when(i == 0)
    def _():
        carry_col[...] = jnp.zeros_like(carry_col)
        carry_row[...] = jnp.zeros_like(carry_row)

    xn = _rms(x_ref[...], nw_ref[...]).astype(BF16)
    proj = _nn(xn, w_ref[...])
    n_h = 4 * HGRN_W
    hg_ref[...] = proj[:, :n_h]
    fox_ref[:, :FOX_W] = (proj[:, n_h:n_h + FOX_W] * (FOX_DH ** -0.5)).astype(BF16)
    fox_ref[:, FOX_W:] = proj[:, n_h + FOX_W:n_h + 3 * FOX_W].astype(BF16)
    gate_ref[...] = jax.nn.sigmoid(proj[:, n_h + 3 * FOX_W:])

    r = lax.broadcasted_iota(jnp.int32, (tm, tm), 0)
    c = lax.broadcasted_iota(jnp.int32, (tm, tm), 1)
    tril = (c <= r).astype(BF16)
    triu = (r <= c).astype(BF16)
    ls_col = _log_sigmoid(_nn(xn, wffc_ref[...]) + fbc_ref[...])
    ccol = _split_dot(_nn, tril, ls_col) + carry_col[...]
    ccol_ref[...] = ccol
    carry_col[...] = ccol[tm - 1:tm, :]
    ls_row = _log_sigmoid(_nt(wffr_ref[...], xn) + fbr_ref[:, 0:1])
    hi = ls_row.astype(BF16)
    lo = (ls_row - hi.astype(F32)).astype(BF16)
    crow = _nn(hi, triu) + _nn(lo, triu) + carry_row[:, 0:1]
    crow_ref[...] = crow
    carry_row[...] = jnp.broadcast_to(crow[:, tm - 1:tm], carry_row.shape)


def _in_proj(x2d, nw, w_all, wff_col, wff_row, fb_col, fb_row, batch, seq, tm):
    T, D = x2d.shape
    nt = seq // tm
    n_all = w_all.shape[1]
    row = lambda b, i: (b * nt + i, 0)
    const = lambda b, i: (0, 0)
    return pl.pallas_call(
        _inproj_kernel,
        grid=(batch, nt),
        in_specs=[
            pl.BlockSpec((tm, D), row),
            pl.BlockSpec((1, D), const),
            pl.BlockSpec((D, n_all), const),
            pl.BlockSpec((D, LANES), const),
            pl.BlockSpec((FOX_HEADS, D), const),
            pl.BlockSpec((1, LANES), const),
            pl.BlockSpec((FOX_HEADS, LANES), const),
        ],
        out_specs=[
            pl.BlockSpec((tm, 4 * HGRN_W), row),
            pl.BlockSpec((tm, 3 * FOX_W), row),
            pl.BlockSpec((tm, 2 * D), row),
            pl.BlockSpec((tm, LANES), row),
            pl.BlockSpec((FOX_HEADS, tm), lambda b, i: (0, b * nt + i)),
        ],
        out_shape=[
            jax.ShapeDtypeStruct((T, 4 * HGRN_W), F32),
            jax.ShapeDtypeStruct((T, 3 * FOX_W), BF16),
            jax.ShapeDtypeStruct((T, 2 * D), F32),
            jax.ShapeDtypeStruct((T, LANES), F32),
            jax.ShapeDtypeStruct((FOX_HEADS, T), F32),
        ],
        scratch_shapes=[pltpu.VMEM((1, LANES), F32), pltpu.VMEM((FOX_HEADS, LANES), F32)],
        compiler_params=pltpu.CompilerParams(
            dimension_semantics=("arbitrary", "arbitrary"), vmem_limit_bytes=VMEM_LIMIT),
        name="in_proj",
    )(x2d, nw, w_all, wff_col, wff_row, fb_col, fb_row)


HGRN_CHUNK = 128
HGRN_SUB = 16


def _hgrn_kernel(hg_ref, lbl_ref, nw_ref, rsel_ref, a_ref, state_ref):
    ci = pl.program_id(1)
    C = HGRN_CHUNK
    dh = HGRN_DH

    @pl.when(ci == 0)
    def _():
        state_ref[...] = jnp.zeros_like(state_ref)

    lg = lbl_ref[...]
    e = jnp.exp(lg - jnp.max(lg, axis=0, keepdims=True))
    lb_all = e[0:1, :] / jnp.sum(e, axis=0, keepdims=True)

    r = lax.broadcasted_iota(jnp.int32, (C, C), 0)
    c = lax.broadcasted_iota(jnp.int32, (C, C), 1)
    tril = (c <= r).astype(BF16)
    sub = HGRN_SUB
    sh = sub.bit_length() - 1
    diag_mask = ((r >> sh) == (c >> sh)) & ((c & (sub - 1)) <= (r & (sub - 1)))
    levels = []
    m = sub
    while m < C:
        sh = m.bit_length() - 1
        levels.append((m, ((r >> (sh + 1)) == (c >> (sh + 1))) & (((r >> sh) & 1) == 1) & (((c >> sh) & 1) == 0)))
        m *= 2

    for h in range(HGRN_HEADS):
        sl = slice(h * dh, (h + 1) * dh)
        qraw = hg_ref[:, sl]
        q = qraw * jax.nn.sigmoid(qraw)
        lb = lb_all[:, sl]
        f = lb + (1.0 - lb) * jax.nn.sigmoid(hg_ref[:, HGRN_W + h * dh:HGRN_W + (h + 1) * dh])
        logf = jnp.log(f)
        k = 1.0 - f
        v = hg_ref[:, 2 * HGRN_W + h * dh:2 * HGRN_W + (h + 1) * dh]
        g = hg_ref[:, 3 * HGRN_W + h * dh:3 * HGRN_W + (h + 1) * dh]
        v_bf = v.astype(BF16)
        cum = _split_dot(_nn, tril, logf)

        nb = C // sub
        q3 = q.reshape(nb, sub, dh)
        k3 = k.reshape(nb, sub, dh)
        c3 = cum.reshape(nb, sub, dh)
        parts = []
        for s in range(sub):
            kb = jnp.broadcast_to(k3[:, s:s + 1, :], (nb, sub, dh))
            cb = jnp.broadcast_to(c3[:, s:s + 1, :], (nb, sub, dh))
            es = q3 * kb * jnp.exp(jnp.minimum(c3 - cb, 0.0))
            parts.append(es.reshape(C, dh).astype(BF16))
        p_mat = jnp.where(diag_mask, _nn(jnp.concatenate(parts, axis=1), rsel_ref[...]), 0.0)

        for m, mask in levels:
            nbm = C // m
            qm = q.reshape(nbm, m, dh)
            km = k.reshape(nbm, m, dh)
            cm = cum.reshape(nbm, m, dh)
            end = cm[:, m - 1:m, :]
            prev_end = jnp.concatenate([jnp.zeros((1, 1, dh), F32), end[:nbm - 1]], axis=0)
            qd = qm * jnp.exp(jnp.minimum(cm - jnp.broadcast_to(prev_end, (nbm, m, dh)), 0.0))
            kd = km * jnp.exp(jnp.minimum(jnp.broadcast_to(end, (nbm, m, dh)) - cm, 0.0))
            sc = _nt(qd.reshape(C, dh).astype(BF16), kd.reshape(C, dh).astype(BF16))
            p_mat = p_mat + jnp.where(mask, sc, 0.0)

        st = state_ref[h]
        o = _nn(p_mat.astype(BF16), v_bf) + _nt((q * jnp.exp(cum)).astype(BF16), st.astype(BF16))
        last = cum[C - 1:C, :]
        kdec = (k * jnp.exp(last - cum)).astype(BF16)
        state_ref[h] = jnp.exp(last) * st + _tn(v_bf, kdec)

        o = o * lax.rsqrt(jnp.mean(o * o, axis=-1, keepdims=True) + RMS_EPS) * nw_ref[:, sl]
        a_ref[:, sl] = (o * (g * jax.nn.sigmoid(g))).astype(BF16)


def _hgrn(hg, lb_logits, norm_w, rsel, batch, seq):
    T = hg.shape[0]
    C = HGRN_CHUNK
    nc = seq // C
    row = lambda b, i: (b * nc + i, 0)
    const = lambda b, i: (0, 0)
    return pl.pallas_call(
        _hgrn_kernel,
        grid=(batch, nc),
        in_specs=[
            pl.BlockSpec((C, 4 * HGRN_W), row),
            pl.BlockSpec(lb_logits.shape, const),
            pl.BlockSpec((1, HGRN_W), const),
            pl.BlockSpec(rsel.shape, const),
        ],
        out_specs=pl.BlockSpec((C, HGRN_W), row),
        out_shape=jax.ShapeDtypeStruct((T, HGRN_W), BF16),
        scratch_shapes=[pltpu.VMEM((HGRN_HEADS, HGRN_DH, HGRN_DH), F32)],
        compiler_params=pltpu.CompilerParams(
            dimension_semantics=("arbitrary", "arbitrary"), vmem_limit_bytes=VMEM_LIMIT),
        name="hgrn",
    )(hg, lb_logits, norm_w, rsel)


def _fox_kernel(q_ref, k_ref, v_ref, ccol_ref, crow_ref, o_ref, m_ref, acc_ref):
    qi = pl.program_id(1)
    tq = q_ref.shape[0]
    tk = tq
    lane = lax.broadcasted_iota(jnp.int32, (1, LANES), 1)
    low = lane < FOX_DH
    sel_lo = jnp.where(low, 1.0, 0.0).astype(BF16)
    sel_hi = jnp.where(low, 0.0, 1.0).astype(BF16)
    ones_lane = (FOX_DH, 0)
    one_hot = tuple(jnp.where(lane == ol, 1.0, 0.0).astype(BF16) for ol in ones_lane)
    rr = lax.broadcasted_iota(jnp.int32, (tq, tk), 0)
    cc = lax.broadcasted_iota(jnp.int32, (tq, tk), 1)
    causal = cc <= rr

    for p in range(FOX_HEADS // 2):
        cols = slice(p * LANES, (p + 1) * LANES)
        q = q_ref[:, cols]
        qs = (q * sel_lo, q * sel_hi)
        cq = (ccol_ref[:, 2 * p:2 * p + 1], ccol_ref[:, 2 * p + 1:2 * p + 2])
        m_ref[...] = jnp.full_like(m_ref, NEG_BIG)
        acc_ref[...] = jnp.zeros_like(acc_ref)

        def kv_block(j, masked):
            start = pl.multiple_of(j * tk, tk)
            kb = k_ref[pl.ds(start, tk), cols]
            vb = v_ref[pl.ds(start, tk), cols]
            vs = (vb * sel_lo + one_hot[0], vb * sel_hi + one_hot[1])
            for hh in range(2):
                ck = crow_ref[2 * p + hh:2 * p + hh + 1, pl.ds(start, tk)]
                s = _nt(qs[hh], kb) + (cq[hh] - ck)
                if masked:
                    s = jnp.where(causal, s, NEG_BIG)
                m_prev = m_ref[hh]
                m_next = jnp.maximum(m_prev, jnp.max(s, axis=1, keepdims=True))
                pexp = jnp.exp(s - m_next[:, 0:1])
                alpha = jnp.exp(m_prev - m_next)
                acc_ref[hh] = alpha * acc_ref[hh] + _nn(pexp.astype(BF16), vs[hh])
                m_ref[hh] = m_next

        def body(j, carry):
            kv_block(j, False)
            return carry

        lax.fori_loop(0, qi, body, 0)
        kv_block(qi, True)

        a0 = acc_ref[0]
        a1 = acc_ref[1]
        o0 = a0 / a0[:, ones_lane[0]:ones_lane[0] + 1]
        o1 = a1 / a1[:, ones_lane[1]:ones_lane[1] + 1]
        o_ref[:, cols] = jnp.where(low, o0, o1).astype(BF16)


def _fox(fox, ccol, crow, batch, seq, tq):
    T = fox.shape[0]
    nq = seq // tq
    return pl.pallas_call(
        _fox_kernel,
        grid=(batch, nq),
        in_specs=[
            pl.BlockSpec((tq, FOX_W), lambda b, i: (b * nq + i, 0)),
            pl.BlockSpec((seq, FOX_W), lambda b, i: (b, 1)),
            pl.BlockSpec((seq, FOX_W), lambda b, i: (b, 2)),
            pl.BlockSpec((tq, LANES), lambda b, i: (b * nq + i, 0)),
            pl.BlockSpec((FOX_HEADS, seq), lambda b, i: (0, b)),
        ],
        out_specs=pl.BlockSpec((tq, FOX_W), lambda b, i: (b * nq + i, 0)),
        out_shape=jax.ShapeDtypeStruct((T, FOX_W), BF16),
        scratch_shapes=[pltpu.VMEM((2, tq, LANES), F32), pltpu.VMEM((2, tq, LANES), F32)],
        compiler_params=pltpu.CompilerParams(
            dimension_semantics=("arbitrary", "arbitrary"), vmem_limit_bytes=VMEM_LIMIT),
        name="fox",
    )(fox, fox, fox, ccol, crow)


def _merge_kernel(x_ref, a_ref, b_ref, gate_ref, wa_ref, wb_ref, wo_ref, nw_ref, wq_ref, keys_ref,
                  h1_ref, xn_ref, st_ref):
    D = x_ref.shape[1]
    merged = gate_ref[:, :D] * _nn(a_ref[...], wa_ref[...]) + gate_ref[:, D:] * _nn(b_ref[...], wb_ref[...])
    h1 = x_ref[...] + _nn(merged.astype(BF16), wo_ref[...])
    h1_ref[...] = h1
    xn = _rms(h1, nw_ref[...]).astype(BF16)
    xn_ref[...] = xn
    q = _nn(xn, wq_ref[...]).astype(BF16)
    for hp in range(2 * PEER_HEADS):
        st_ref[hp] = _nt(keys_ref[hp], q[:, hp * PEER_HALF:(hp + 1) * PEER_HALF])


def _merge(x2d, a, b, gates, wa, wb, wo, nw, wq, keys, tm):
    T, D = x2d.shape
    row = lambda i: (i, 0)
    const = lambda i: (0, 0)
    return pl.pallas_call(
        _merge_kernel,
        grid=(T // tm,),
        in_specs=[
            pl.BlockSpec((tm, D), row),
            pl.BlockSpec((tm, HGRN_W), row),
            pl.BlockSpec((tm, FOX_W), row),
            pl.BlockSpec((tm, 2 * D), row),
            pl.BlockSpec(wa.shape, const),
            pl.BlockSpec(wb.shape, const),
            pl.BlockSpec(wo.shape, const),
            pl.BlockSpec((1, D), const),
            pl.BlockSpec(wq.shape, const),
            pl.BlockSpec(keys.shape, lambda i: (0, 0, 0)),
        ],
        out_specs=[
            pl.BlockSpec((tm, D), row),
            pl.BlockSpec((tm, D), row),
            pl.BlockSpec((2 * PEER_HEADS, N_KEYS, tm), lambda i: (0, 0, i)),
        ],
        out_shape=[
            jax.ShapeDtypeStruct((T, D), F32),
            jax.ShapeDtypeStruct((T, D), BF16),
            jax.ShapeDtypeStruct((2 * PEER_HEADS, N_KEYS, T), F32),
        ],
        compiler_params=pltpu.CompilerParams(
            dimension_semantics=("arbitrary",), vmem_limit_bytes=VMEM_LIMIT),
        name="merge",
    )(x2d, a, b, gates, wa, wb, wo, nw, wq, keys)


def _rank_pairs():
    n = TOPK + 1
    return [(r, c) for r in range(n) for c in range(n) if (r + 1) * (c + 1) <= n]


def _topk_kernel(st_ref, cnt_ref, p1_ref, rank_ref, p2_ref, rank1_ref):
    tt = st_ref.shape[2]
    n_top = TOPK + 1

    def top_sorted(s):
        vals = []
        cur = s
        rank = jnp.full(s.shape, float(n_top), F32)
        for r in range(n_top):
            mx = jnp.max(cur, axis=0, keepdims=True)
            vals.append(mx)
            hit = cur == mx
            rank = jnp.where(hit, float(r), rank)
            cur = jnp.where(hit, -jnp.inf, cur)
        return vals, rank

    tops = []
    for h in range(PEER_HEADS):
        v1, r1 = top_sorted(st_ref[2 * h])
        v2, r2 = top_sorted(st_ref[2 * h + 1])
        rank1_ref[h] = r1
        rank_ref[h] = r2
        tops.append((v1, v2))
    a = [jnp.concatenate([tops[h][0][r] for h in range(PEER_HEADS)], axis=0) for r in range(n_top)]
    b = [jnp.concatenate([tops[h][1][r] for h in range(PEER_HEADS)], axis=0) for r in range(n_top)]
    cand = [a[r] + b[c] for r, c in _rank_pairs()]
    tau = jnp.full((PEER_HEADS, tt), -jnp.inf, F32)
    nxt = jnp.full((PEER_HEADS, tt), -jnp.inf, F32)
    for xi in cand:
        cnt = jnp.zeros((PEER_HEADS, tt), F32)
        for xj in cand:
            cnt = cnt + jnp.where(xj >= xi, 1.0, 0.0)
        tau = jnp.maximum(tau, jnp.where(cnt >= TOPK, xi, -jnp.inf))
        nxt = jnp.maximum(nxt, jnp.where(cnt >= TOPK + 1, xi, -jnp.inf))
    top = a[0] + b[0]
    z = jnp.zeros((PEER_HEADS, tt), F32)
    for xi in cand:
        z = z + jnp.where(xi >= tau, jnp.exp(xi - top), 0.0)
    inv_z = 1.0 / z
    cut = 0.5 * (tau + nxt)
    n_sel = [jnp.zeros((PEER_HEADS, tt), F32) for _ in range(n_top)]
    for r, c in _rank_pairs():
        n_sel[r] = n_sel[r] + jnp.where(a[r] + b[c] >= cut, 1.0, 0.0)
    for h in range(PEER_HEADS):
        s1 = st_ref[2 * h]
        s2 = st_ref[2 * h + 1]
        r1 = rank1_ref[h]
        cnt = jnp.zeros(r1.shape, F32)
        for r in range(n_top):
            cnt = jnp.where(r1 == float(r), n_sel[r][h:h + 1, :], cnt)
        cnt_ref[h] = cnt
        p1_ref[h] = jnp.exp(s1 - a[0][h:h + 1, :]) * inv_z[h:h + 1, :]
        p2_ref[h] = jnp.exp(s2 - b[0][h:h + 1, :])


def _topk(st, tt):
    n, nk, T = st.shape
    out32 = jax.ShapeDtypeStruct((PEER_HEADS, nk, T), F32)
    spec = pl.BlockSpec((PEER_HEADS, nk, tt), lambda i: (0, 0, i))
    return pl.pallas_call(
        _topk_kernel,
        grid=(T // tt,),
        in_specs=[pl.BlockSpec((n, nk, tt), lambda i: (0, 0, i))],
        out_specs=[spec, spec, spec, spec],
        out_shape=[out32, out32, out32, out32],
        scratch_shapes=[pltpu.VMEM((PEER_HEADS, nk, tt), F32)],
        compiler_params=pltpu.CompilerParams(
            dimension_semantics=("arbitrary",), vmem_limit_bytes=VMEM_LIMIT),
        name="topk",
    )(st)


PEER_KEY_TILE = 64
PEER_TOK_TILE = 128
PEER_PAIR = 2


def _peer_kernel(xn_ref, u_ref, v_ref, cnt_ref, p1_ref, rank_ref, p2_ref, h1_ref, nw_ref,
                 o_ref, acc_ref, ht_ref, g_ref, cnt_rows, p1_rows, rank_bf, p2_bf):
    j = pl.program_id(1)
    nj = pl.num_programs(1)
    eb, tb = ht_ref.shape
    groups = eb // N_KEYS

    @pl.when(j == 0)
    def _():
        acc_ref[...] = jnp.zeros_like(acc_ref)
        for h in range(PEER_HEADS):
            rank_bf[h * N_KEYS:(h + 1) * N_KEYS, :] = rank_ref[h].astype(BF16)
            p2_bf[h * N_KEYS:(h + 1) * N_KEYS, :] = p2_ref[h].astype(BF16)

    ht_ref[...] = _nt(u_ref[...], xn_ref[...])
    sub = cnt_rows.shape[1]
    kt = PEER_KEY_TILE // sub
    for g0 in range(0, groups, PEER_PAIR):
        for gi in range(PEER_PAIR):
            i1 = j * groups + g0 + gi
            for h in range(PEER_HEADS):
                r = gi * PEER_HEADS + h
                cnt_rows[r] = jnp.broadcast_to(cnt_ref[h, pl.ds(i1, 1), :], (sub, tb)).astype(BF16)
                p1_rows[r] = jnp.broadcast_to(p1_ref[h, pl.ds(i1, 1), :], (sub, tb)).astype(BF16)
        for t0 in range(0, tb, PEER_TOK_TILE):
            ts = slice(t0, t0 + PEER_TOK_TILE)
            for k0 in range(0, N_KEYS, PEER_KEY_TILE):
                ks = slice(k0, k0 + PEER_KEY_TILE)
                gate = [None] * PEER_PAIR
                for h in range(PEER_HEADS):
                    hs = slice(h * N_KEYS + k0, h * N_KEYS + k0 + PEER_KEY_TILE)
                    rk = rank_bf[hs, ts]
                    p2 = p2_bf[hs, ts]
                    for gi in range(PEER_PAIR):
                        r = gi * PEER_HEADS + h
                        cn = jnp.concatenate([cnt_rows[r, :, ts]] * kt, axis=0)
                        p1 = jnp.concatenate([p1_rows[r, :, ts]] * kt, axis=0)
                        term = jnp.where(rk < cn, p2, jnp.zeros_like(p2)) * p1
                        gate[gi] = term if gate[gi] is None else gate[gi] + term
                for gi in range(PEER_PAIR):
                    es = slice((g0 + gi) * N_KEYS + k0, (g0 + gi) * N_KEYS + k0 + PEER_KEY_TILE)
                    ht = ht_ref[es, ts]
                    act = 0.5 * ht * (1.0 + lax.erf(ht * (2.0 ** -0.5)))
                    g_ref[es, ts] = act.astype(BF16) * gate[gi]
    acc_ref[...] += _tn(g_ref[...], v_ref[...])

    @pl.when(j == nj - 1)
    def _():
        o_ref[...] = _rms(h1_ref[...] + acc_ref[...], nw_ref[...])


def _peer(xn, u, v, cnt, p1, rank2, p2, h1, nw, tb, eb):
    T, D = xn.shape
    n_exp = u.shape[0]
    tok3 = lambda i, j: (0, 0, i)
    return pl.pallas_call(
        _peer_kernel,
        grid=(T // tb, n_exp // eb),
        in_specs=[
            pl.BlockSpec((tb, D), lambda i, j: (i, 0)),
            pl.BlockSpec((eb, D), lambda i, j: (j, 0)),
            pl.BlockSpec((eb, D), lambda i, j: (j, 0)),
            pl.BlockSpec((PEER_HEADS, N_KEYS, tb), tok3),
            pl.BlockSpec((PEER_HEADS, N_KEYS, tb), tok3),
            pl.BlockSpec((PEER_HEADS, N_KEYS, tb), tok3),
            pl.BlockSpec((PEER_HEADS, N_KEYS, tb), tok3),
            pl.BlockSpec((tb, D), lambda i, j: (i, 0)),
            pl.BlockSpec((1, D), lambda i, j: (0, 0)),
        ],
        out_specs=pl.BlockSpec((tb, D), lambda i, j: (i, 0)),
        out_shape=jax.ShapeDtypeStruct((T, D), F32),
        scratch_shapes=[pltpu.VMEM((tb, D), F32), pltpu.VMEM((eb, tb), F32), pltpu.VMEM((eb, tb), BF16),
                        pltpu.VMEM((PEER_PAIR * PEER_HEADS, 16, tb), BF16),
                        pltpu.VMEM((PEER_PAIR * PEER_HEADS, 16, tb), BF16),
                        pltpu.VMEM((PEER_HEADS * N_KEYS, tb), BF16),
                        pltpu.VMEM((PEER_HEADS * N_KEYS, tb), BF16)],
        compiler_params=pltpu.CompilerParams(
            dimension_semantics=("arbitrary", "arbitrary"), vmem_limit_bytes=VMEM_LIMIT),
        name="peer",
    )(xn, u, v, cnt, p1, rank2, p2, h1, nw)


def _block_select_matrix():
    part = jnp.arange(HGRN_SUB * HGRN_DH, dtype=jnp.int32) // HGRN_DH
    col = jnp.arange(HGRN_CHUNK, dtype=jnp.int32) % HGRN_SUB
    return (part[:, None] == col[None, :]).astype(BF16)


def _forward(x, norm_mix_w, w_in, hgrn_lb_logits, hgrn_norm_w, fox_f_bias, w_branch_hgrn,
             w_branch_fox, w_out, norm_ffn_w, peer_w_q, peer_sub_keys, peer_u, peer_v,
             norm_final_w, *, tm_in, tq, tm_merge, tt, tb, eb):
    B, S, D = x.shape
    T = B * S
    x2d = x.reshape(T, D)
    n_h = 4 * HGRN_W
    n_f = 3 * FOX_W
    wi = w_in[0]
    w_all = jnp.concatenate([wi[:, :n_h + n_f], wi[:, n_h + n_f + FOX_HEADS:]], axis=1).astype(BF16)
    wff = wi[:, n_h + n_f:n_h + n_f + FOX_HEADS]
    wff_col = jnp.pad(wff, ((0, 0), (0, LANES - FOX_HEADS))).astype(BF16)
    wff_row = wff.T.astype(BF16)
    fb = fox_f_bias[0].astype(F32)
    fb_col = jnp.pad(fb, (0, LANES - FOX_HEADS)).reshape(1, LANES)
    fb_row = jnp.broadcast_to(fb.reshape(FOX_HEADS, 1), (FOX_HEADS, LANES))

    hg, fox, gates, ccol, crow = _in_proj(
        x2d, norm_mix_w[0].reshape(1, D), w_all, wff_col, wff_row, fb_col, fb_row, B, S, tm_in)
    a = _hgrn(hg, hgrn_lb_logits, hgrn_norm_w[0].reshape(1, HGRN_W), _block_select_matrix(), B, S)
    b = _fox(fox, ccol, crow, B, S, tq)
    keys = peer_sub_keys[0].reshape(2 * PEER_HEADS, N_KEYS, PEER_HALF).astype(BF16)
    h1, xn2, st = _merge(
        x2d, a, b, gates, w_branch_hgrn[0].astype(BF16), w_branch_fox[0].astype(BF16),
        w_out[0].astype(BF16), norm_ffn_w[0].reshape(1, D), peer_w_q[0].astype(BF16), keys, tm_merge)
    cnt, p1, rank2, p2 = _topk(st, tt)
    out = _peer(xn2, peer_u[0].astype(BF16), peer_v[0].astype(BF16), cnt, p1, rank2, p2, h1,
                norm_final_w.reshape(1, D), tb, eb)
    return out.reshape(B, S, D)


def kernel(x, norm_mix_w, w_in, hgrn_lb_logits, hgrn_norm_w, fox_f_bias, w_branch_hgrn, w_branch_fox, w_out, norm_ffn_w, peer_w_q, peer_sub_keys, peer_u, peer_v, norm_final_w):
    return _forward(x, norm_mix_w, w_in, hgrn_lb_logits, hgrn_norm_w, fox_f_bias, w_branch_hgrn,
                    w_branch_fox, w_out, norm_ffn_w, peer_w_q, peer_sub_keys, peer_u, peer_v,
                    norm_final_w, tm_in=256, tq=512, tm_merge=256, tt=256, tb=512, eb=512)
```

```python
import functools
import math

import jax
import jax.numpy as jnp
from jax import lax
from jax.experimental import pallas as pl
from jax.experimental.pallas import tpu as pltpu

F32 = jnp.float32
BF16 = jnp.bfloat16
RMS_EPS = 1e-6
NEG_BIG = -1e30

HGRN_HEADS = 4
HGRN_DH = 128
HGRN_W = HGRN_HEADS * HGRN_DH
FOX_HEADS = 8
FOX_DH = 64
FOX_W = FOX_HEADS * FOX_DH
PEER_HEADS = 8
PEER_HALF = 128
N_KEYS = 128
TOPK = 16
LANES = 128

VMEM_LIMIT = 56 * 1024 * 1024


def _nt(a, b):
    return lax.dot_general(a, b, (((1,), (1,)), ((), ())), preferred_element_type=F32)


def _tn(a, b):
    return lax.dot_general(a, b, (((0,), (0,)), ((), ())), preferred_element_type=F32)


def _nn(a, b):
    return jnp.dot(a, b, preferred_element_type=F32)


def _split_dot(fn, tri, x):
    hi = x.astype(BF16)
    lo = (x - hi.astype(F32)).astype(BF16)
    return fn(tri, hi) + fn(tri, lo)


def _log_sigmoid(x):
    return jnp.minimum(x, 0.0) - jnp.log1p(jnp.exp(-jnp.abs(x)))


def _rms(x, w):
    return x * lax.rsqrt(jnp.mean(x * x, axis=-1, keepdims=True) + RMS_EPS) * w


def _inproj_kernel(x_ref, nw_ref, w_ref, wffc_ref, wffr_ref, fbc_ref, fbr_ref,
                   hg_ref, fox_ref, gate_ref, ccol_ref, crow_ref,
                   carry_col, carry_row):
    i = pl.program_id(1)
    tm = x_ref.shape[0]

    @pl.when(i == 0)
    def _():
        carry_col[...] = jnp.zeros_like(carry_col)
        carry_row[...] = jnp.zeros_like(carry_row)

    xn = _rms(x_ref[...], nw_ref[...]).astype(BF16)
    proj = _nn(xn, w_ref[...])
    n_h = 4 * HGRN_W
    hg_ref[...] = proj[:, :n_h]
    fox_ref[:, :FOX_W] = (proj[:, n_h:n_h + FOX_W] * (FOX_DH ** -0.5)).astype(BF16)
    fox_ref[:, FOX_W:] = proj[:, n_h + FOX_W:n_h + 3 * FOX_W].astype(BF16)
    gate_ref[...] = jax.nn.sigmoid(proj[:, n_h + 3 * FOX_W:])

    r = lax.broadcasted_iota(jnp.int32, (tm, tm), 0)
    c = lax.broadcasted_iota(jnp.int32, (tm, tm), 1)
    tril = (c <= r).astype(BF16)
    triu = (r <= c).astype(BF16)
    ls_col = _log_sigmoid(_nn(xn, wffc_ref[...]) + fbc_ref[...])
    ccol = _split_dot(_nn, tril, ls_col) + carry_col[...]
    ccol_ref[...] = ccol
    carry_col[...] = ccol[tm - 1:tm, :]
    ls_row = _log_sigmoid(_nt(wffr_ref[...], xn) + fbr_ref[:, 0:1])
    hi = ls_row.astype(BF16)
    lo = (ls_row - hi.astype(F32)).astype(BF16)
    crow = _nn(hi, triu) + _nn(lo, triu) + carry_row[:, 0:1]
    crow_ref[...] = crow
    carry_row[...] = jnp.broadcast_to(crow[:, tm - 1:tm], carry_row.shape)


def _in_proj(x2d, nw, w_all, wff_col, wff_row, fb_col, fb_row, batch, seq, tm):
    T, D = x2d.shape
    nt = seq // tm
    n_all = w_all.shape[1]
    row = lambda b, i: (b * nt + i, 0)
    const = lambda b, i: (0, 0)
    return pl.pallas_call(
        _inproj_kernel,
        grid=(batch, nt),
        in_specs=[
            pl.BlockSpec((tm, D), row),
            pl.BlockSpec((1, D), const),
            pl.BlockSpec((D, n_all), const),
            pl.BlockSpec((D, LANES), const),
            pl.BlockSpec((FOX_HEADS, D), const),
            pl.BlockSpec((1, LANES), const),
            pl.BlockSpec((FOX_HEADS, LANES), const),
        ],
        out_specs=[
            pl.BlockSpec((tm, 4 * HGRN_W), row),
            pl.BlockSpec((tm, 3 * FOX_W), row),
            pl.BlockSpec((tm, 2 * D), row),
            pl.BlockSpec((tm, LANES), row),
            pl.BlockSpec((FOX_HEADS, tm), lambda b, i: (0, b * nt + i)),
        ],
        out_shape=[
            jax.ShapeDtypeStruct((T, 4 * HGRN_W), F32),
            jax.ShapeDtypeStruct((T, 3 * FOX_W), BF16),
            jax.ShapeDtypeStruct((T, 2 * D), F32),
            jax.ShapeDtypeStruct((T, LANES), F32),
            jax.ShapeDtypeStruct((FOX_HEADS, T), F32),
        ],
        scratch_shapes=[pltpu.VMEM((1, LANES), F32), pltpu.VMEM((FOX_HEADS, LANES), F32)],
        compiler_params=pltpu.CompilerParams(
            dimension_semantics=("arbitrary", "arbitrary"), vmem_limit_bytes=VMEM_LIMIT),
        name="in_proj",
    )(x2d, nw, w_all, wff_col, wff_row, fb_col, fb_row)


HGRN_CHUNK = 128
HGRN_SUB = 16


def _hgrn_kernel(hg_ref, lbl_ref, nw_ref, rsel_ref, a_ref, state_ref):
    ci = pl.program_id(1)
    C = HGRN_CHUNK
    dh = HGRN_DH

    @pl.when(ci == 0)
    def _():
        state_ref[...] = jnp.zeros_like(state_ref)

    lg = lbl_ref[...]
    e = jnp.exp(lg - jnp.max(lg, axis=0, keepdims=True))
    lb_all = e[0:1, :] / jnp.sum(e, axis=0, keepdims=True)

    r = lax.broadcasted_iota(jnp.int32, (C, C), 0)
    c = lax.broadcasted_iota(jnp.int32, (C, C), 1)
    tril = (c <= r).astype(BF16)
    sub = HGRN_SUB
    sh = sub.bit_length() - 1
    diag_mask = ((r >> sh) == (c >> sh)) & ((c & (sub - 1)) <= (r & (sub - 1)))
    levels = []
    m = sub
    while m < C:
        sh = m.bit_length() - 1
        levels.append((m, ((r >> (sh + 1)) == (c >> (sh + 1))) & (((r >> sh) & 1) == 1) & (((c >> sh) & 1) == 0)))
        m *= 2

    for h in range(HGRN_HEADS):
        sl = slice(h * dh, (h + 1) * dh)
        qraw = hg_ref[:, sl]
        q = qraw * jax.nn.sigmoid(qraw)
        lb = lb_all[:, sl]
        f = lb + (1.0 - lb) * jax.nn.sigmoid(hg_ref[:, HGRN_W + h * dh:HGRN_W + (h + 1) * dh])
        logf = jnp.log(f)
        k = 1.0 - f
        v = hg_ref[:, 2 * HGRN_W + h * dh:2 * HGRN_W + (h + 1) * dh]
        g = hg_ref[:, 3 * HGRN_W + h * dh:3 * HGRN_W + (h + 1) * dh]
        v_bf = v.astype(BF16)
        cum = _split_dot(_nn, tril, logf)

        nb = C // sub
        q3 = q.reshape(nb, sub, dh)
        k3 = k.reshape(nb, sub, dh)
        c3 = cum.reshape(nb, sub, dh)
        parts = []
        for s in range(sub):
            kb = jnp.broadcast_to(k3[:, s:s + 1, :], (nb, sub, dh))
            cb = jnp.broadcast_to(c3[:, s:s + 1, :], (nb, sub, dh))
            es = q3 * kb * jnp.exp(jnp.minimum(c3 - cb, 0.0))
            parts.append(es.reshape(C, dh).astype(BF16))
        p_mat = jnp.where(diag_mask, _nn(jnp.concatenate(parts, axis=1), rsel_ref[...]), 0.0)

        for m, mask in levels:
            nbm = C // m
            qm = q.reshape(nbm, m, dh)
            km = k.reshape(nbm, m, dh)
            cm = cum.reshape(nbm, m, dh)
            end = cm[:, m - 1:m, :]
            prev_end = jnp.concatenate([jnp.zeros((1, 1, dh), F32), end[:nbm - 1]], axis=0)
            qd = qm * jnp.exp(jnp.minimum(cm - jnp.broadcast_to(prev_end, (nbm, m, dh)), 0.0))
            kd = km * jnp.exp(jnp.minimum(jnp.broadcast_to(end, (nbm, m, dh)) - cm, 0.0))
            sc = _nt(qd.reshape(C, dh).astype(BF16), kd.reshape(C, dh).astype(BF16))
            p_mat = p_mat + jnp.where(mask, sc, 0.0)

        st = state_ref[h]
        o = _nn(p_mat.astype(BF16), v_bf) + _nt((q * jnp.exp(cum)).astype(BF16), st.astype(BF16))
        last = cum[C - 1:C, :]
        kdec = (k * jnp.exp(last - cum)).astype(BF16)
        state_ref[h] = jnp.exp(last) * st + _tn(v_bf, kdec)

        o = o * lax.rsqrt(jnp.mean(o * o, axis=-1, keepdims=True) + RMS_EPS) * nw_ref[:, sl]
        a_ref[:, sl] = (o * (g * jax.nn.sigmoid(g))).astype(BF16)


def _hgrn(hg, lb_logits, norm_w, rsel, batch, seq):
    T = hg.shape[0]
    C = HGRN_CHUNK
    nc = seq // C
    row = lambda b, i: (b * nc + i, 0)
    const = lambda b, i: (0, 0)
    return pl.pallas_call(
        _hgrn_kernel,
        grid=(batch, nc),
        in_specs=[
            pl.BlockSpec((C, 4 * HGRN_W), row),
            pl.BlockSpec(lb_logits.shape, const),
            pl.BlockSpec((1, HGRN_W), const),
            pl.BlockSpec(rsel.shape, const),
        ],
        out_specs=pl.BlockSpec((C, HGRN_W), row),
        out_shape=jax.ShapeDtypeStruct((T, HGRN_W), BF16),
        scratch_shapes=[pltpu.VMEM((HGRN_HEADS, HGRN_DH, HGRN_DH), F32)],
        compiler_params=pltpu.CompilerParams(
            dimension_semantics=("arbitrary", "arbitrary"), vmem_limit_bytes=VMEM_LIMIT),
        name="hgrn",
    )(hg, lb_logits, norm_w, rsel)


def _fox_kernel(q_ref, k_ref, v_ref, ccol_ref, crow_ref, o_ref, m_ref, acc_ref):
    qi = pl.program_id(1)
    tq = q_ref.shape[0]
    tk = tq
    lane = lax.broadcasted_iota(jnp.int32, (1, LANES), 1)
    low = lane < FOX_DH
    sel_lo = jnp.where(low, 1.0, 0.0).astype(BF16)
    sel_hi = jnp.where(low, 0.0, 1.0).astype(BF16)
    ones_lane = (FOX_DH, 0)
    one_hot = tuple(jnp.where(lane == ol, 1.0, 0.0).astype(BF16) for ol in ones_lane)
    rr = lax.broadcasted_iota(jnp.int32, (tq, tk), 0)
    cc = lax.broadcasted_iota(jnp.int32, (tq, tk), 1)
    causal = cc <= rr

    for p in range(FOX_HEADS // 2):
        cols = slice(p * LANES, (p + 1) * LANES)
        q = q_ref[:, cols]
        qs = (q * sel_lo, q * sel_hi)
        cq = (ccol_ref[:, 2 * p:2 * p + 1], ccol_ref[:, 2 * p + 1:2 * p + 2])
        m_ref[...] = jnp.full_like(m_ref, NEG_BIG)
        acc_ref[...] = jnp.zeros_like(acc_ref)

        def kv_block(j, masked):
            start = pl.multiple_of(j * tk, tk)
            kb = k_ref[pl.ds(start, tk), cols]
            vb = v_ref[pl.ds(start, tk), cols]
            vs = (vb * sel_lo + one_hot[0], vb * sel_hi + one_hot[1])
            for hh in range(2):
                ck = crow_ref[2 * p + hh:2 * p + hh + 1, pl.ds(start, tk)]
                s = _nt(qs[hh], kb) + (cq[hh] - ck)
                if masked:
                    s = jnp.where(causal, s, NEG_BIG)
                m_prev = m_ref[hh]
                m_next = jnp.maximum(m_prev, jnp.max(s, axis=1, keepdims=True))
                pexp = jnp.exp(s - m_next[:, 0:1])
                alpha = jnp.exp(m_prev - m_next)
                acc_ref[hh] = alpha * acc_ref[hh] + _nn(pexp.astype(BF16), vs[hh])
                m_ref[hh] = m_next

        def body(j, carry):
            kv_block(j, False)
            return carry

        lax.fori_loop(0, qi, body, 0)
        kv_block(qi, True)

        a0 = acc_ref[0]
        a1 = acc_ref[1]
        o0 = a0 / a0[:, ones_lane[0]:ones_lane[0] + 1]
        o1 = a1 / a1[:, ones_lane[1]:ones_lane[1] + 1]
        o_ref[:, cols] = jnp.where(low, o0, o1).astype(BF16)


def _fox(fox, ccol, crow, batch, seq, tq):
    T = fox.shape[0]
    nq = seq // tq
    return pl.pallas_call(
        _fox_kernel,
        grid=(batch, nq),
        in_specs=[
            pl.BlockSpec((tq, FOX_W), lambda b, i: (b * nq + i, 0)),
            pl.BlockSpec((seq, FOX_W), lambda b, i: (b, 1)),
            pl.BlockSpec((seq, FOX_W), lambda b, i: (b, 2)),
            pl.BlockSpec((tq, LANES), lambda b, i: (b * nq + i, 0)),
            pl.BlockSpec((FOX_HEADS, seq), lambda b, i: (0, b)),
        ],
        out_specs=pl.BlockSpec((tq, FOX_W), lambda b, i: (b * nq + i, 0)),
        out_shape=jax.ShapeDtypeStruct((T, FOX_W), BF16),
        scratch_shapes=[pltpu.VMEM((2, tq, LANES), F32), pltpu.VMEM((2, tq, LANES), F32)],
        compiler_params=pltpu.CompilerParams(
            dimension_semantics=("arbitrary", "arbitrary"), vmem_limit_bytes=VMEM_LIMIT),
        name="fox",
    )(fox, fox, fox, ccol, crow)


def _merge_kernel(x_ref, a_ref, b_ref, gate_ref, wa_ref, wb_ref, wo_ref, nw_ref, wq_ref, keys_ref,
                  h1_ref, xn_ref, st_ref):
    D = x_ref.shape[1]
    merged = gate_ref[:, :D] * _nn(a_ref[...], wa_ref[...]) + gate_ref[:, D:] * _nn(b_ref[...], wb_ref[...])
    h1 = x_ref[...] + _nn(merged.astype(BF16), wo_ref[...])
    h1_ref[...] = h1
    xn = _rms(h1, nw_ref[...]).astype(BF16)
    xn_ref[...] = xn
    q = _nn(xn, wq_ref[...]).astype(BF16)
    for hp in range(2 * PEER_HEADS):
        st_ref[hp] = _nt(keys_ref[hp], q[:, hp * PEER_HALF:(hp + 1) * PEER_HALF])


def _merge(x2d, a, b, gates, wa, wb, wo, nw, wq, keys, tm):
    T, D = x2d.shape
    row = lambda i: (i, 0)
    const = lambda i: (0, 0)
    return pl.pallas_call(
        _merge_kernel,
        grid=(T // tm,),
        in_specs=[
            pl.BlockSpec((tm, D), row),
            pl.BlockSpec((tm, HGRN_W), row),
            pl.BlockSpec((tm, FOX_W), row),
            pl.BlockSpec((tm, 2 * D), row),
            pl.BlockSpec(wa.shape, const),
            pl.BlockSpec(wb.shape, const),
            pl.BlockSpec(wo.shape, const),
            pl.BlockSpec((1, D), const),
            pl.BlockSpec(wq.shape, const),
            pl.BlockSpec(keys.shape, lambda i: (0, 0, 0)),
        ],
        out_specs=[
            pl.BlockSpec((tm, D), row),
            pl.BlockSpec((tm, D), row),
            pl.BlockSpec((2 * PEER_HEADS, N_KEYS, tm), lambda i: (0, 0, i)),
        ],
        out_shape=[
            jax.ShapeDtypeStruct((T, D), F32),
            jax.ShapeDtypeStruct((T, D), BF16),
            jax.ShapeDtypeStruct((2 * PEER_HEADS, N_KEYS, T), F32),
        ],
        compiler_params=pltpu.CompilerParams(
            dimension_semantics=("arbitrary",), vmem_limit_bytes=VMEM_LIMIT),
        name="merge",
    )(x2d, a, b, gates, wa, wb, wo, nw, wq, keys)


def _rank_pairs():
    n = TOPK + 1
    return [(r, c) for r in range(n) for c in range(n) if (r + 1) * (c + 1) <= n]


def _topk_kernel(st_ref, cnt_ref, p1_ref, rank_ref, p2_ref, rank1_ref):
    tt = st_ref.shape[2]
    n_top = TOPK + 1

    def top_sorted(s):
        vals = []
        cur = s
        rank = jnp.full(s.shape, float(n_top), F32)
        for r in range(n_top):
            mx = jnp.max(cur, axis=0, keepdims=True)
            vals.append(mx)
            hit = cur == mx
            rank = jnp.where(hit, float(r), rank)
            cur = jnp.where(hit, -jnp.inf, cur)
        return vals, rank

    tops = []
    for h in range(PEER_HEADS):
        v1, r1 = top_sorted(st_ref[2 * h])
        v2, r2 = top_sorted(st_ref[2 * h + 1])
        rank1_ref[h] = r1
        rank_ref[h] = r2
        tops.append((v1, v2))
    a = [jnp.concatenate([tops[h][0][r] for h in range(PEER_HEADS)], axis=0) for r in range(n_top)]
    b = [jnp.concatenate([tops[h][1][r] for h in range(PEER_HEADS)], axis=0) for r in range(n_top)]
    cand = [a[r] + b[c] for r, c in _rank_pairs()]
    tau = jnp.full((PEER_HEADS, tt), -jnp.inf, F32)
    nxt = jnp.full((PEER_HEADS, tt), -jnp.inf, F32)
    for xi in cand:
        cnt = jnp.zeros((PEER_HEADS, tt), F32)
        for xj in cand:
            cnt = cnt + jnp.where(xj >= xi, 1.0, 0.0)
        tau = jnp.maximum(tau, jnp.where(cnt >= TOPK, xi, -jnp.inf))
        nxt = jnp.maximum(nxt, jnp.where(cnt >= TOPK + 1, xi, -jnp.inf))
    top = a[0] + b[0]
    z = jnp.zeros((PEER_HEADS, tt), F32)
    for xi in cand:
        z = z + jnp.where(xi >= tau, jnp.exp(xi - top), 0.0)
    inv_z = 1.0 / z
    cut = 0.5 * (tau + nxt)
    n_sel = [jnp.zeros((PEER_HEADS, tt), F32) for _ in range(n_top)]
    for r, c in _rank_pairs():
        n_sel[r] = n_sel[r] + jnp.where(a[r] + b[c] >= cut, 1.0, 0.0)
    for h in range(PEER_HEADS):
        s1 = st_ref[2 * h]
        s2 = st_ref[2 * h + 1]
        r1 = rank1_ref[h]
        cnt = jnp.zeros(r1.shape, F32)
        for r in range(n_top):
            cnt = jnp.where(r1 == float(r), n_sel[r][h:h + 1, :], cnt)
        cnt_ref[h] = cnt
        p1_ref[h] = jnp.exp(s1 - a[0][h:h + 1, :]) * inv_z[h:h + 1, :]
        p2_ref[h] = jnp.exp(s2 - b[0][h:h + 1, :])


def _topk(st, tt):
    n, nk, T = st.shape
    out32 = jax.ShapeDtypeStruct((PEER_HEADS, nk, T), F32)
    spec = pl.BlockSpec((PEER_HEADS, nk, tt), lambda i: (0, 0, i))
    return pl.pallas_call(
        _topk_kernel,
        grid=(T // tt,),
        in_specs=[pl.BlockSpec((n, nk, tt), lambda i: (0, 0, i))],
        out_specs=[spec, spec, spec, spec],
        out_shape=[out32, out32, out32, out32],
        scratch_shapes=[pltpu.VMEM((PEER_HEADS, nk, tt), F32)],
        compiler_params=pltpu.CompilerParams(
            dimension_semantics=("arbitrary",), vmem_limit_bytes=VMEM_LIMIT),
        name="topk",
    )(st)


PEER_KEY_TILE = 64
PEER_TOK_TILE = 128
PEER_PAIR = 2
SUBLANES = 8


def _peer_kernel(xn_ref, u_ref, v_ref, cnt_ref, p1_ref, rank_ref, p2_ref, h1_ref, nw_ref,
                 o_ref, acc_ref, ht_ref, g_ref, cnt_rows, p1_rows):
    j = pl.program_id(1)
    nj = pl.num_programs(1)
    eb, tb = ht_ref.shape
    groups = eb // N_KEYS
    cur = j % 2
    jb = jnp.minimum(j, nj - 2)

    @pl.when(j == 0)
    def _():
        acc_ref[...] = jnp.zeros_like(acc_ref)
        g_ref[...] = jnp.zeros_like(g_ref)

    acc_ref[...] += _tn(g_ref[1 - cur], v_ref[...])

    ht_ref[...] = _nt(u_ref[...], xn_ref[...])
    kt = PEER_KEY_TILE // SUBLANES
    for g0 in range(0, groups, PEER_PAIR):
        for gi in range(PEER_PAIR):
            i1 = jb * groups + g0 + gi
            for h in range(PEER_HEADS):
                r = gi * PEER_HEADS + h
                cnt_rows[r] = jnp.broadcast_to(cnt_ref[h, pl.ds(i1, 1), :], (SUBLANES, tb))
                p1_rows[r] = jnp.broadcast_to(p1_ref[h, pl.ds(i1, 1), :], (SUBLANES, tb))
        for t0 in range(0, tb, PEER_TOK_TILE):
            ts = slice(t0, t0 + PEER_TOK_TILE)
            for k0 in range(0, N_KEYS, PEER_KEY_TILE):
                ks = slice(k0, k0 + PEER_KEY_TILE)
                gate = [None] * PEER_PAIR
                for h in range(PEER_HEADS):
                    rk = rank_ref[h, ks, ts]
                    p2 = p2_ref[h, ks, ts]
                    for gi in range(PEER_PAIR):
                        r = gi * PEER_HEADS + h
                        cn = jnp.concatenate([cnt_rows[r, :, ts]] * kt, axis=0)
                        p1 = jnp.concatenate([p1_rows[r, :, ts]] * kt, axis=0)
                        term = jnp.where(rk < cn, p2, 0.0) * p1
                        gate[gi] = term if gate[gi] is None else gate[gi] + term
                for gi in range(PEER_PAIR):
                    es = slice((g0 + gi) * N_KEYS + k0, (g0 + gi) * N_KEYS + k0 + PEER_KEY_TILE)
                    ht = ht_ref[es, ts]
                    act = 0.5 * ht * (1.0 + lax.erf(ht * (2.0 ** -0.5)))
                    g_ref[cur, es, ts] = (act * gate[gi]).astype(BF16)

    @pl.when(j == nj - 1)
    def _():
        o_ref[...] = _rms(h1_ref[...] + acc_ref[...], nw_ref[...])


def _peer(xn, u, v, cnt, p1, rank2, p2, h1, nw, tb, eb):
    T, D = xn.shape
    n_exp = u.shape[0]
    tok3 = lambda i, j: (0, 0, i)
    nb = n_exp // eb
    return pl.pallas_call(
        _peer_kernel,
        grid=(T // tb, nb + 1),
        in_specs=[
            pl.BlockSpec((tb, D), lambda i, j: (i, 0)),
            pl.BlockSpec((eb, D), lambda i, j: (jnp.minimum(j, nb - 1), 0)),
            pl.BlockSpec((eb, D), lambda i, j: (jnp.maximum(j - 1, 0), 0)),
            pl.BlockSpec((PEER_HEADS, N_KEYS, tb), tok3),
            pl.BlockSpec((PEER_HEADS, N_KEYS, tb), tok3),
            pl.BlockSpec((PEER_HEADS, N_KEYS, tb), tok3),
            pl.BlockSpec((PEER_HEADS, N_KEYS, tb), tok3),
            pl.BlockSpec((tb, D), lambda i, j: (i, 0)),
            pl.BlockSpec((1, D), lambda i, j: (0, 0)),
        ],
        out_specs=pl.BlockSpec((tb, D), lambda i, j: (i, 0)),
        out_shape=jax.ShapeDtypeStruct((T, D), F32),
        scratch_shapes=[pltpu.VMEM((tb, D), F32), pltpu.VMEM((eb, tb), F32), pltpu.VMEM((2, eb, tb), BF16),
                        pltpu.VMEM((PEER_PAIR * PEER_HEADS, SUBLANES, tb), F32),
                        pltpu.VMEM((PEER_PAIR * PEER_HEADS, SUBLANES, tb), F32)],
        compiler_params=pltpu.CompilerParams(
            dimension_semantics=("arbitrary", "arbitrary"), vmem_limit_bytes=VMEM_LIMIT),
        name="peer",
    )(xn, u, v, cnt, p1, rank2, p2, h1, nw)


def _block_select_matrix():
    part = jnp.arange(HGRN_SUB * HGRN_DH, dtype=jnp.int32) // HGRN_DH
    col = jnp.arange(HGRN_CHUNK, dtype=jnp.int32) % HGRN_SUB
    return (part[:, None] == col[None, :]).astype(BF16)


def _forward(x, norm_mix_w, w_in, hgrn_lb_logits, hgrn_norm_w, fox_f_bias, w_branch_hgrn,
             w_branch_fox, w_out, norm_ffn_w, peer_w_q, peer_sub_keys, peer_u, peer_v,
             norm_final_w, *, tm_in, tq, tm_merge, tt, tb, eb):
    B, S, D = x.shape
    T = B * S
    x2d = x.reshape(T, D)
    n_h = 4 * HGRN_W
    n_f = 3 * FOX_W
    wi = w_in[0]
    w_all = jnp.concatenate([wi[:, :n_h + n_f], wi[:, n_h + n_f + FOX_HEADS:]], axis=1).astype(BF16)
    wff = wi[:, n_h + n_f:n_h + n_f + FOX_HEADS]
    wff_col = jnp.pad(wff, ((0, 0), (0, LANES - FOX_HEADS))).astype(BF16)
    wff_row = wff.T.astype(BF16)
    fb = fox_f_bias[0].astype(F32)
    fb_col = jnp.pad(fb, (0, LANES - FOX_HEADS)).reshape(1, LANES)
    fb_row = jnp.broadcast_to(fb.reshape(FOX_HEADS, 1), (FOX_HEADS, LANES))

    hg, fox, gates, ccol, crow = _in_proj(
        x2d, norm_mix_w[0].reshape(1, D), w_all, wff_col, wff_row, fb_col, fb_row, B, S, tm_in)
    a = _hgrn(hg, hgrn_lb_logits, hgrn_norm_w[0].reshape(1, HGRN_W), _block_select_matrix(), B, S)
    b = _fox(fox, ccol, crow, B, S, tq)
    keys = peer_sub_keys[0].reshape(2 * PEER_HEADS, N_KEYS, PEER_HALF).astype(BF16)
    h1, xn2, st = _merge(
        x2d, a, b, gates, w_branch_hgrn[0].astype(BF16), w_branch_fox[0].astype(BF16),
        w_out[0].astype(BF16), norm_ffn_w[0].reshape(1, D), peer_w_q[0].astype(BF16), keys, tm_merge)
    cnt, p1, rank2, p2 = _topk(st, tt)
    out = _peer(xn2, peer_u[0].astype(BF16), peer_v[0].astype(BF16), cnt, p1, rank2, p2, h1,
                norm_final_w.reshape(1, D), tb, eb)
    return out.reshape(B, S, D)


def kernel(x, norm_mix_w, w_in, hgrn_lb_logits, hgrn_norm_w, fox_f_bias, w_branch_hgrn, w_branch_fox, w_out, norm_ffn_w, peer_w_q, peer_sub_keys, peer_u, peer_v, norm_final_w):
    return _forward(x, norm_mix_w, w_in, hgrn_lb_logits, hgrn_norm_w, fox_f_bias, w_branch_hgrn,
                    w_branch_fox, w_out, norm_ffn_w, peer_w_q, peer_sub_keys, peer_u, peer_v,
                    norm_final_w, tm_in=256, tq=512, tm_merge=256, tt=256, tb=512, eb=512)
```

```python
import functools
import math

import jax
import jax.numpy as jnp
from jax import lax
from jax.experimental import pallas as pl
from jax.experimental.pallas import tpu as pltpu

F32 = jnp.float32
BF16 = jnp.bfloat16
RMS_EPS = 1e-6
NEG_BIG = -1e30

HGRN_HEADS = 4
HGRN_DH = 128
HGRN_W = HGRN_HEADS * HGRN_DH
FOX_HEADS = 8
FOX_DH = 64
FOX_W = FOX_HEADS * FOX_DH
PEER_HEADS = 8
PEER_HALF = 128
N_KEYS = 128
TOPK = 16
LANES = 128

VMEM_LIMIT = 56 * 1024 * 1024


def _nt(a, b):
    return lax.dot_general(a, b, (((1,), (1,)), ((), ())), preferred_element_type=F32)


def _tn(a, b):
    return lax.dot_general(a, b, (((0,), (0,)), ((), ())), preferred_element_type=F32)


def _nn(a, b):
    return jnp.dot(a, b, preferred_element_type=F32)


def _split_dot(fn, tri, x):
    hi = x.astype(BF16)
    lo = (x - hi.astype(F32)).astype(BF16)
    return fn(tri, hi) + fn(tri, lo)


def _log_sigmoid(x):
    return jnp.minimum(x, 0.0) - jnp.log1p(jnp.exp(-jnp.abs(x)))


def _rms(x, w):
    return x * lax.rsqrt(jnp.mean(x * x, axis=-1, keepdims=True) + RMS_EPS) * w


LOG2E = math.log2(math.e)


def _inproj_kernel(x_ref, nw_ref, w_ref, wffr_ref, fbr_ref,
                   hg_ref, fox_ref, gate_ref, crow_ref, carry_row):
    i = pl.program_id(1)
    tm = x_ref.shape[0]

    @pl.when(i == 0)
    def _():
        carry_row[...] = jnp.zeros_like(carry_row)

    xn = _rms(x_ref[...], nw_ref[...]).astype(BF16)
    proj = _nn(xn, w_ref[...])
    n_h = 4 * HGRN_W
    hg_ref[...] = proj[:, :n_h]
    fox_ref[:, :FOX_W] = (proj[:, n_h:n_h + FOX_W] * (FOX_DH ** -0.5 * LOG2E)).astype(BF16)
    fox_ref[:, FOX_W:] = proj[:, n_h + FOX_W:n_h + 3 * FOX_W].astype(BF16)
    gate_ref[...] = jax.nn.sigmoid(proj[:, n_h + 3 * FOX_W:])

    r = lax.broadcasted_iota(jnp.int32, (tm, tm), 0)
    c = lax.broadcasted_iota(jnp.int32, (tm, tm), 1)
    triu = (r <= c).astype(BF16)
    ls_row = _log_sigmoid(_nt(wffr_ref[...], xn) + fbr_ref[:, 0:1]) * LOG2E
    hi = ls_row.astype(BF16)
    lo = (ls_row - hi.astype(F32)).astype(BF16)
    crow = _nn(hi, triu) + _nn(lo, triu) + carry_row[:, 0:1]
    crow_ref[...] = crow
    carry_row[...] = jnp.broadcast_to(crow[:, tm - 1:tm], carry_row.shape)


def _in_proj(x2d, nw, w_all, wff_row, fb_row, batch, seq, tm):
    T, D = x2d.shape
    nt = seq // tm
    n_all = w_all.shape[1]
    row = lambda b, i: (b * nt + i, 0)
    const = lambda b, i: (0, 0)
    return pl.pallas_call(
        _inproj_kernel,
        grid=(batch, nt),
        in_specs=[
            pl.BlockSpec((tm, D), row),
            pl.BlockSpec((1, D), const),
            pl.BlockSpec((D, n_all), const),
            pl.BlockSpec((FOX_HEADS, D), const),
            pl.BlockSpec((FOX_HEADS, LANES), const),
        ],
        out_specs=[
            pl.BlockSpec((tm, 4 * HGRN_W), row),
            pl.BlockSpec((tm, 3 * FOX_W), row),
            pl.BlockSpec((tm, 2 * D), row),
            pl.BlockSpec((FOX_HEADS, tm), lambda b, i: (0, b * nt + i)),
        ],
        out_shape=[
            jax.ShapeDtypeStruct((T, 4 * HGRN_W), F32),
            jax.ShapeDtypeStruct((T, 3 * FOX_W), BF16),
            jax.ShapeDtypeStruct((T, 2 * D), F32),
            jax.ShapeDtypeStruct((FOX_HEADS, T), F32),
        ],
        scratch_shapes=[pltpu.VMEM((FOX_HEADS, LANES), F32)],
        compiler_params=pltpu.CompilerParams(
            dimension_semantics=("arbitrary", "arbitrary"), vmem_limit_bytes=VMEM_LIMIT),
        name="in_proj",
    )(x2d, nw, w_all, wff_row, fb_row)


HGRN_CHUNK = 128
HGRN_SUB = 16


def _hgrn_kernel(hg_ref, lbl_ref, nw_ref, rsel_ref, a_ref, state_ref):
    ci = pl.program_id(1)
    C = HGRN_CHUNK
    dh = HGRN_DH

    @pl.when(ci == 0)
    def _():
        state_ref[...] = jnp.zeros_like(state_ref)

    lg = lbl_ref[...]
    e = jnp.exp(lg - jnp.max(lg, axis=0, keepdims=True))
    lb_all = e[0:1, :] / jnp.sum(e, axis=0, keepdims=True)

    r = lax.broadcasted_iota(jnp.int32, (C, C), 0)
    c = lax.broadcasted_iota(jnp.int32, (C, C), 1)
    tril = (c <= r).astype(BF16)
    sub = HGRN_SUB
    sh = sub.bit_length() - 1
    diag_mask = ((r >> sh) == (c >> sh)) & ((c & (sub - 1)) <= (r & (sub - 1)))
    levels = []
    m = sub
    while m < C:
        sh = m.bit_length() - 1
        levels.append((m, ((r >> (sh + 1)) == (c >> (sh + 1))) & (((r >> sh) & 1) == 1) & (((c >> sh) & 1) == 0)))
        m *= 2

    for h in range(HGRN_HEADS):
        sl = slice(h * dh, (h + 1) * dh)
        qraw = hg_ref[:, sl]
        q = qraw * jax.nn.sigmoid(qraw)
        lb = lb_all[:, sl]
        f = lb + (1.0 - lb) * jax.nn.sigmoid(hg_ref[:, HGRN_W + h * dh:HGRN_W + (h + 1) * dh])
        logf = jnp.log(f)
        k = 1.0 - f
        v = hg_ref[:, 2 * HGRN_W + h * dh:2 * HGRN_W + (h + 1) * dh]
        g = hg_ref[:, 3 * HGRN_W + h * dh:3 * HGRN_W + (h + 1) * dh]
        v_bf = v.astype(BF16)
        cum = _split_dot(_nn, tril, logf)

        nb = C // sub
        q3 = q.reshape(nb, sub, dh)
        k3 = k.reshape(nb, sub, dh)
        c3 = cum.reshape(nb, sub, dh)
        parts = []
        for s in range(sub):
            kb = jnp.broadcast_to(k3[:, s:s + 1, :], (nb, sub, dh))
            cb = jnp.broadcast_to(c3[:, s:s + 1, :], (nb, sub, dh))
            es = q3 * kb * jnp.exp(jnp.minimum(c3 - cb, 0.0))
            parts.append(es.reshape(C, dh).astype(BF16))
        p_mat = jnp.where(diag_mask, _nn(jnp.concatenate(parts, axis=1), rsel_ref[...]), 0.0)

        for m, mask in levels:
            nbm = C // m
            qm = q.reshape(nbm, m, dh)
            km = k.reshape(nbm, m, dh)
            cm = cum.reshape(nbm, m, dh)
            end = cm[:, m - 1:m, :]
            prev_end = jnp.concatenate([jnp.zeros((1, 1, dh), F32), end[:nbm - 1]], axis=0)
            qd = qm * jnp.exp(jnp.minimum(cm - jnp.broadcast_to(prev_end, (nbm, m, dh)), 0.0))
            kd = km * jnp.exp(jnp.minimum(jnp.broadcast_to(end, (nbm, m, dh)) - cm, 0.0))
            sc = _nt(qd.reshape(C, dh).astype(BF16), kd.reshape(C, dh).astype(BF16))
            p_mat = p_mat + jnp.where(mask, sc, 0.0)

        st = state_ref[h]
        o = _nn(p_mat.astype(BF16), v_bf) + _nt((q * jnp.exp(cum)).astype(BF16), st.astype(BF16))
        last = cum[C - 1:C, :]
        kdec = (k * jnp.exp(last - cum)).astype(BF16)
        state_ref[h] = jnp.exp(last) * st + _tn(v_bf, kdec)

        o = o * lax.rsqrt(jnp.mean(o * o, axis=-1, keepdims=True) + RMS_EPS) * nw_ref[:, sl]
        a_ref[:, sl] = (o * (g * jax.nn.sigmoid(g))).astype(BF16)


def _hgrn(hg, lb_logits, norm_w, rsel, batch, seq):
    T = hg.shape[0]
    C = HGRN_CHUNK
    nc = seq // C
    row = lambda b, i: (b * nc + i, 0)
    const = lambda b, i: (0, 0)
    return pl.pallas_call(
        _hgrn_kernel,
        grid=(batch, nc),
        in_specs=[
            pl.BlockSpec((C, 4 * HGRN_W), row),
            pl.BlockSpec(lb_logits.shape, const),
            pl.BlockSpec((1, HGRN_W), const),
            pl.BlockSpec(rsel.shape, const),
        ],
        out_specs=pl.BlockSpec((C, HGRN_W), row),
        out_shape=jax.ShapeDtypeStruct((T, HGRN_W), BF16),
        scratch_shapes=[pltpu.VMEM((HGRN_HEADS, HGRN_DH, HGRN_DH), F32)],
        compiler_params=pltpu.CompilerParams(
            dimension_semantics=("arbitrary", "arbitrary"), vmem_limit_bytes=VMEM_LIMIT),
        name="hgrn",
    )(hg, lb_logits, norm_w, rsel)


def _fox_kernel(q_ref, k_ref, v_ref, crow_ref, o_ref, m_ref, acc_ref):
    qi = pl.program_id(1)
    tq = q_ref.shape[0]
    tk = tq
    lane = lax.broadcasted_iota(jnp.int32, (1, LANES), 1)
    low = lane < FOX_DH
    sel_lo = jnp.where(low, 1.0, 0.0).astype(BF16)
    sel_hi = jnp.where(low, 0.0, 1.0).astype(BF16)
    ones_lane = (FOX_DH, 0)
    one_hot = tuple(jnp.where(lane == ol, 1.0, 0.0).astype(BF16) for ol in ones_lane)
    rr = lax.broadcasted_iota(jnp.int32, (tq, tk), 0)
    cc = lax.broadcasted_iota(jnp.int32, (tq, tk), 1)
    causal = cc <= rr

    for p in range(FOX_HEADS // 2):
        cols = slice(p * LANES, (p + 1) * LANES)
        q = q_ref[:, cols]
        qs = (q * sel_lo, q * sel_hi)
        m_ref[...] = jnp.full_like(m_ref, NEG_BIG)
        acc_ref[...] = jnp.zeros_like(acc_ref)

        def kv_block(j, masked):
            start = pl.multiple_of(j * tk, tk)
            kb = k_ref[pl.ds(start, tk), cols]
            vb = v_ref[pl.ds(start, tk), cols]
            vs = (vb * sel_lo + one_hot[0], vb * sel_hi + one_hot[1])
            for hh in range(2):
                ck = crow_ref[2 * p + hh:2 * p + hh + 1, pl.ds(start, tk)]
                s = _nt(qs[hh], kb) - ck
                if masked:
                    s = jnp.where(causal, s, NEG_BIG)
                m_prev = m_ref[hh]
                m_next = jnp.maximum(m_prev, jnp.max(s, axis=1, keepdims=True))
                pexp = jnp.exp2(s - jnp.concatenate([m_next] * (tk // LANES), axis=1))
                alpha = jnp.exp2(m_prev - m_next)
                acc_ref[hh] = alpha * acc_ref[hh] + _nn(pexp.astype(BF16), vs[hh])
                m_ref[hh] = m_next

        def body(j, carry):
            kv_block(j, False)
            return carry

        lax.fori_loop(0, qi, body, 0)
        kv_block(qi, True)

        a0 = acc_ref[0]
        a1 = acc_ref[1]
        o0 = a0 / a0[:, ones_lane[0]:ones_lane[0] + 1]
        o1 = a1 / a1[:, ones_lane[1]:ones_lane[1] + 1]
        o_ref[:, cols] = jnp.where(low, o0, o1).astype(BF16)


def _fox(fox, crow, batch, seq, tq):
    T = fox.shape[0]
    nq = seq // tq
    return pl.pallas_call(
        _fox_kernel,
        grid=(batch, nq),
        in_specs=[
            pl.BlockSpec((tq, FOX_W), lambda b, i: (b * nq + i, 0)),
            pl.BlockSpec((seq, FOX_W), lambda b, i: (b, 1)),
            pl.BlockSpec((seq, FOX_W), lambda b, i: (b, 2)),
            pl.BlockSpec((FOX_HEADS, seq), lambda b, i: (0, b)),
        ],
        out_specs=pl.BlockSpec((tq, FOX_W), lambda b, i: (b * nq + i, 0)),
        out_shape=jax.ShapeDtypeStruct((T, FOX_W), BF16),
        scratch_shapes=[pltpu.VMEM((2, tq, LANES), F32), pltpu.VMEM((2, tq, LANES), F32)],
        compiler_params=pltpu.CompilerParams(
            dimension_semantics=("arbitrary", "arbitrary"), vmem_limit_bytes=VMEM_LIMIT),
        name="fox",
    )(fox, fox, fox, crow)


def _merge_kernel(x_ref, a_ref, b_ref, gate_ref, wa_ref, wb_ref, wo_ref, nw_ref, wq_ref, keys_ref,
                  h1_ref, xn_ref, st_ref):
    D = x_ref.shape[1]
    merged = gate_ref[:, :D] * _nn(a_ref[...], wa_ref[...]) + gate_ref[:, D:] * _nn(b_ref[...], wb_ref[...])
    h1 = x_ref[...] + _nn(merged.astype(BF16), wo_ref[...])
    h1_ref[...] = h1
    xn = _rms(h1, nw_ref[...]).astype(BF16)
    xn_ref[...] = xn
    q = _nn(xn, wq_ref[...]).astype(BF16)
    for hp in range(2 * PEER_HEADS):
        st_ref[hp] = _nt(keys_ref[hp], q[:, hp * PEER_HALF:(hp + 1) * PEER_HALF])


def _merge(x2d, a, b, gates, wa, wb, wo, nw, wq, keys, tm):
    T, D = x2d.shape
    row = lambda i: (i, 0)
    const = lambda i: (0, 0)
    return pl.pallas_call(
        _merge_kernel,
        grid=(T // tm,),
        in_specs=[
            pl.BlockSpec((tm, D), row),
            pl.BlockSpec((tm, HGRN_W), row),
            pl.BlockSpec((tm, FOX_W), row),
            pl.BlockSpec((tm, 2 * D), row),
            pl.BlockSpec(wa.shape, const),
            pl.BlockSpec(wb.shape, const),
            pl.BlockSpec(wo.shape, const),
            pl.BlockSpec((1, D), const),
            pl.BlockSpec(wq.shape, const),
            pl.BlockSpec(keys.shape, lambda i: (0, 0, 0)),
        ],
        out_specs=[
            pl.BlockSpec((tm, D), row),
            pl.BlockSpec((tm, D), row),
            pl.BlockSpec((2 * PEER_HEADS, N_KEYS, tm), lambda i: (0, 0, i)),
        ],
        out_shape=[
            jax.ShapeDtypeStruct((T, D), F32),
            jax.ShapeDtypeStruct((T, D), BF16),
            jax.ShapeDtypeStruct((2 * PEER_HEADS, N_KEYS, T), F32),
        ],
        compiler_params=pltpu.CompilerParams(
            dimension_semantics=("arbitrary",), vmem_limit_bytes=VMEM_LIMIT),
        name="merge",
    )(x2d, a, b, gates, wa, wb, wo, nw, wq, keys)


def _rank_pairs():
    n = TOPK + 1
    return [(r, c) for r in range(n) for c in range(n) if (r + 1) * (c + 1) <= n]


def _topk_kernel(st_ref, cnt_ref, p1_ref, rank_ref, p2_ref, rank1_ref):
    tt = st_ref.shape[2]
    n_top = TOPK + 1

    def top_sorted(s):
        vals = []
        cur = s
        rank = jnp.full(s.shape, float(n_top), F32)
        for r in range(n_top):
            mx = jnp.max(cur, axis=0, keepdims=True)
            vals.append(mx)
            hit = cur == mx
            rank = jnp.where(hit, float(r), rank)
            cur = jnp.where(hit, -jnp.inf, cur)
        return vals, rank

    tops = []
    for h in range(PEER_HEADS):
        v1, r1 = top_sorted(st_ref[2 * h])
        v2, r2 = top_sorted(st_ref[2 * h + 1])
        rank1_ref[h] = r1
        rank_ref[h] = r2
        tops.append((v1, v2))
    a = [jnp.concatenate([tops[h][0][r] for h in range(PEER_HEADS)], axis=0) for r in range(n_top)]
    b = [jnp.concatenate([tops[h][1][r] for h in range(PEER_HEADS)], axis=0) for r in range(n_top)]
    cand = [a[r] + b[c] for r, c in _rank_pairs()]
    tau = jnp.full((PEER_HEADS, tt), -jnp.inf, F32)
    nxt = jnp.full((PEER_HEADS, tt), -jnp.inf, F32)
    for xi in cand:
        cnt = jnp.zeros((PEER_HEADS, tt), F32)
        for xj in cand:
            cnt = cnt + jnp.where(xj >= xi, 1.0, 0.0)
        tau = jnp.maximum(tau, jnp.where(cnt >= TOPK, xi, -jnp.inf))
        nxt = jnp.maximum(nxt, jnp.where(cnt >= TOPK + 1, xi, -jnp.inf))
    top = a[0] + b[0]
    z = jnp.zeros((PEER_HEADS, tt), F32)
    for xi in cand:
        z = z + jnp.where(xi >= tau, jnp.exp(xi - top), 0.0)
    inv_z = 1.0 / z
    cut = 0.5 * (tau + nxt)
    n_sel = [jnp.zeros((PEER_HEADS, tt), F32) for _ in range(n_top)]
    for r, c in _rank_pairs():
        n_sel[r] = n_sel[r] + jnp.where(a[r] + b[c] >= cut, 1.0, 0.0)
    for h in range(PEER_HEADS):
        s1 = st_ref[2 * h]
        s2 = st_ref[2 * h + 1]
        r1 = rank1_ref[h]
        cnt = jnp.zeros(r1.shape, F32)
        for r in range(n_top):
            cnt = jnp.where(r1 == float(r), n_sel[r][h:h + 1, :], cnt)
        cnt_ref[h] = cnt
        p1_ref[h] = jnp.exp(s1 - a[0][h:h + 1, :]) * inv_z[h:h + 1, :]
        p2_ref[h] = jnp.exp(s2 - b[0][h:h + 1, :])


def _topk(st, tt):
    n, nk, T = st.shape
    out32 = jax.ShapeDtypeStruct((PEER_HEADS, nk, T), F32)
    spec = pl.BlockSpec((PEER_HEADS, nk, tt), lambda i: (0, 0, i))
    return pl.pallas_call(
        _topk_kernel,
        grid=(T // tt,),
        in_specs=[pl.BlockSpec((n, nk, tt), lambda i: (0, 0, i))],
        out_specs=[spec, spec, spec, spec],
        out_shape=[out32, out32, out32, out32],
        scratch_shapes=[pltpu.VMEM((PEER_HEADS, nk, tt), F32)],
        compiler_params=pltpu.CompilerParams(
            dimension_semantics=("arbitrary",), vmem_limit_bytes=VMEM_LIMIT),
        name="topk",
    )(st)


PEER_KEY_TILE = 64
PEER_TOK_TILE = 128
PEER_PAIR = 2
SUBLANES = 8


def _peer_kernel(xn_ref, u_ref, v_ref, cnt_ref, p1_ref, rank_ref, p2_ref, h1_ref, nw_ref,
                 o_ref, acc_ref, ht_ref, g_ref, cnt_rows, p1_rows):
    j = pl.program_id(1)
    nj = pl.num_programs(1)
    eb, tb = ht_ref.shape
    groups = eb // N_KEYS
    cur = j % 2
    jb = jnp.minimum(j, nj - 2)

    @pl.when(j == 0)
    def _():
        acc_ref[...] = jnp.zeros_like(acc_ref)
        g_ref[...] = jnp.zeros_like(g_ref)

    acc_ref[...] += _tn(g_ref[1 - cur], v_ref[...])

    ht_ref[...] = _nt(u_ref[...], xn_ref[...])
    kt = PEER_KEY_TILE // SUBLANES
    for g0 in range(0, groups, PEER_PAIR):
        for gi in range(PEER_PAIR):
            i1 = jb * groups + g0 + gi
            for h in range(PEER_HEADS):
                r = gi * PEER_HEADS + h
                cnt_rows[r] = jnp.broadcast_to(cnt_ref[h, pl.ds(i1, 1), :], (SUBLANES, tb))
                p1_rows[r] = jnp.broadcast_to(p1_ref[h, pl.ds(i1, 1), :], (SUBLANES, tb))
        for t0 in range(0, tb, PEER_TOK_TILE):
            ts = slice(t0, t0 + PEER_TOK_TILE)
            for k0 in range(0, N_KEYS, PEER_KEY_TILE):
                ks = slice(k0, k0 + PEER_KEY_TILE)
                gate = [None] * PEER_PAIR
                for h in range(PEER_HEADS):
                    rk = rank_ref[h, ks, ts]
                    p2 = p2_ref[h, ks, ts]
                    for gi in range(PEER_PAIR):
                        r = gi * PEER_HEADS + h
                        cn = jnp.concatenate([cnt_rows[r, :, ts]] * kt, axis=0)
                        p1 = jnp.concatenate([p1_rows[r, :, ts]] * kt, axis=0)
                        term = jnp.where(rk < cn, p2, 0.0) * p1
                        gate[gi] = term if gate[gi] is None else gate[gi] + term
                for gi in range(PEER_PAIR):
                    es = slice((g0 + gi) * N_KEYS + k0, (g0 + gi) * N_KEYS + k0 + PEER_KEY_TILE)
                    ht = ht_ref[es, ts]
                    act = 0.5 * ht * (1.0 + lax.erf(ht * (2.0 ** -0.5)))
                    g_ref[cur, es, ts] = (act * gate[gi]).astype(BF16)

    @pl.when(j == nj - 1)
    def _():
        o_ref[...] = _rms(h1_ref[...] + acc_ref[...], nw_ref[...])


def _peer(xn, u, v, cnt, p1, rank2, p2, h1, nw, tb, eb):
    T, D = xn.shape
    n_exp = u.shape[0]
    tok3 = lambda i, j: (0, 0, i)
    nb = n_exp // eb
    return pl.pallas_call(
        _peer_kernel,
        grid=(T // tb, nb + 1),
        in_specs=[
            pl.BlockSpec((tb, D), lambda i, j: (i, 0)),
            pl.BlockSpec((eb, D), lambda i, j: (jnp.minimum(j, nb - 1), 0)),
            pl.BlockSpec((eb, D), lambda i, j: (jnp.maximum(j - 1, 0), 0)),
            pl.BlockSpec((PEER_HEADS, N_KEYS, tb), tok3),
            pl.BlockSpec((PEER_HEADS, N_KEYS, tb), tok3),
            pl.BlockSpec((PEER_HEADS, N_KEYS, tb), tok3),
            pl.BlockSpec((PEER_HEADS, N_KEYS, tb), tok3),
            pl.BlockSpec((tb, D), lambda i, j: (i, 0)),
            pl.BlockSpec((1, D), lambda i, j: (0, 0)),
        ],
        out_specs=pl.BlockSpec((tb, D), lambda i, j: (i, 0)),
        out_shape=jax.ShapeDtypeStruct((T, D), F32),
        scratch_shapes=[pltpu.VMEM((tb, D), F32), pltpu.VMEM((eb, tb), F32), pltpu.VMEM((2, eb, tb), BF16),
                        pltpu.VMEM((PEER_PAIR * PEER_HEADS, SUBLANES, tb), F32),
                        pltpu.VMEM((PEER_PAIR * PEER_HEADS, SUBLANES, tb), F32)],
        compiler_params=pltpu.CompilerParams(
            dimension_semantics=("arbitrary", "arbitrary"), vmem_limit_bytes=VMEM_LIMIT),
        name="peer",
    )(xn, u, v, cnt, p1, rank2, p2, h1, nw)


def _block_select_matrix():
    part = jnp.arange(HGRN_SUB * HGRN_DH, dtype=jnp.int32) // HGRN_DH
    col = jnp.arange(HGRN_CHUNK, dtype=jnp.int32) % HGRN_SUB
    return (part[:, None] == col[None, :]).astype(BF16)


def _forward(x, norm_mix_w, w_in, hgrn_lb_logits, hgrn_norm_w, fox_f_bias, w_branch_hgrn,
             w_branch_fox, w_out, norm_ffn_w, peer_w_q, peer_sub_keys, peer_u, peer_v,
             norm_final_w, *, tm_in, tq, tm_merge, tt, tb, eb):
    B, S, D = x.shape
    T = B * S
    x2d = x.reshape(T, D)
    n_h = 4 * HGRN_W
    n_f = 3 * FOX_W
    wi = w_in[0]
    w_all = jnp.concatenate([wi[:, :n_h + n_f], wi[:, n_h + n_f + FOX_HEADS:]], axis=1).astype(BF16)
    wff = wi[:, n_h + n_f:n_h + n_f + FOX_HEADS]
    wff_row = wff.T.astype(BF16)
    fb = fox_f_bias[0].astype(F32)
    fb_row = jnp.broadcast_to(fb.reshape(FOX_HEADS, 1), (FOX_HEADS, LANES))

    hg, fox, gates, crow = _in_proj(
        x2d, norm_mix_w[0].reshape(1, D), w_all, wff_row, fb_row, B, S, tm_in)
    a = _hgrn(hg, hgrn_lb_logits, hgrn_norm_w[0].reshape(1, HGRN_W), _block_select_matrix(), B, S)
    b = _fox(fox, crow, B, S, tq)
    keys = peer_sub_keys[0].reshape(2 * PEER_HEADS, N_KEYS, PEER_HALF).astype(BF16)
    h1, xn2, st = _merge(
        x2d, a, b, gates, w_branch_hgrn[0].astype(BF16), w_branch_fox[0].astype(BF16),
        w_out[0].astype(BF16), norm_ffn_w[0].reshape(1, D), peer_w_q[0].astype(BF16), keys, tm_merge)
    cnt, p1, rank2, p2 = _topk(st, tt)
    out = _peer(xn2, peer_u[0].astype(BF16), peer_v[0].astype(BF16), cnt, p1, rank2, p2, h1,
                norm_final_w.reshape(1, D), tb, eb)
    return out.reshape(B, S, D)


def kernel(x, norm_mix_w, w_in, hgrn_lb_logits, hgrn_norm_w, fox_f_bias, w_branch_hgrn, w_branch_fox, w_out, norm_ffn_w, peer_w_q, peer_sub_keys, peer_u, peer_v, norm_final_w):
    return _forward(x, norm_mix_w, w_in, hgrn_lb_logits, hgrn_norm_w, fox_f_bias, w_branch_hgrn,
                    w_branch_fox, w_out, norm_ffn_w, peer_w_q, peer_sub_keys, peer_u, peer_v,
                    norm_final_w, tm_in=256, tq=512, tm_merge=256, tt=256, tb=512, eb=512)
```

```python
import functools
import math

import jax
import jax.numpy as jnp
from jax import lax
from jax.experimental import pallas as pl
from jax.experimental.pallas import tpu as pltpu
from jax.experimental.pallas import tpu_sc as plsc

F32 = jnp.float32
BF16 = jnp.bfloat16
RMS_EPS = 1e-6
NEG_BIG = -1e30

HGRN_HEADS = 4
HGRN_DH = 128
HGRN_W = HGRN_HEADS * HGRN_DH
FOX_HEADS = 8
FOX_DH = 64
FOX_W = FOX_HEADS * FOX_DH
PEER_HEADS = 8
PEER_HALF = 128
N_KEYS = 128
TOPK = 16
LANES = 128

VMEM_LIMIT = 56 * 1024 * 1024


def _nt(a, b):
    return lax.dot_general(a, b, (((1,), (1,)), ((), ())), preferred_element_type=F32)


def _tn(a, b):
    return lax.dot_general(a, b, (((0,), (0,)), ((), ())), preferred_element_type=F32)


def _nn(a, b):
    return jnp.dot(a, b, preferred_element_type=F32)


def _split_dot(fn, tri, x):
    hi = x.astype(BF16)
    lo = (x - hi.astype(F32)).astype(BF16)
    return fn(tri, hi) + fn(tri, lo)


def _log_sigmoid(x):
    return jnp.minimum(x, 0.0) - jnp.log1p(jnp.exp(-jnp.abs(x)))


def _rms(x, w):
    return x * lax.rsqrt(jnp.mean(x * x, axis=-1, keepdims=True) + RMS_EPS) * w


LOG2E = math.log2(math.e)


def _inproj_kernel(x_ref, nw_ref, w_ref, wffr_ref, fbr_ref,
                   hg_ref, fox_ref, gate_ref, crow_ref, carry_row):
    i = pl.program_id(1)
    tm = x_ref.shape[0]

    @pl.when(i == 0)
    def _():
        carry_row[...] = jnp.zeros_like(carry_row)

    xn = _rms(x_ref[...], nw_ref[...]).astype(BF16)
    proj = _nn(xn, w_ref[...])
    n_h = 4 * HGRN_W
    hg_ref[...] = proj[:, :n_h]
    fox_ref[:, :FOX_W] = (proj[:, n_h:n_h + FOX_W] * (FOX_DH ** -0.5 * LOG2E)).astype(BF16)
    fox_ref[:, FOX_W:] = proj[:, n_h + FOX_W:n_h + 3 * FOX_W].astype(BF16)
    gate_ref[...] = jax.nn.sigmoid(proj[:, n_h + 3 * FOX_W:])

    r = lax.broadcasted_iota(jnp.int32, (tm, tm), 0)
    c = lax.broadcasted_iota(jnp.int32, (tm, tm), 1)
    triu = (r <= c).astype(BF16)
    ls_row = _log_sigmoid(_nt(wffr_ref[...], xn) + fbr_ref[:, 0:1]) * LOG2E
    hi = ls_row.astype(BF16)
    lo = (ls_row - hi.astype(F32)).astype(BF16)
    crow = _nn(hi, triu) + _nn(lo, triu) + carry_row[:, 0:1]
    crow_ref[...] = crow
    carry_row[...] = jnp.broadcast_to(crow[:, tm - 1:tm], carry_row.shape)


def _in_proj(x2d, nw, w_all, wff_row, fb_row, batch, seq, tm):
    T, D = x2d.shape
    nt = seq // tm
    n_all = w_all.shape[1]
    row = lambda b, i: (b * nt + i, 0)
    const = lambda b, i: (0, 0)
    return pl.pallas_call(
        _inproj_kernel,
        grid=(batch, nt),
        in_specs=[
            pl.BlockSpec((tm, D), row),
            pl.BlockSpec((1, D), const),
            pl.BlockSpec((D, n_all), const),
            pl.BlockSpec((FOX_HEADS, D), const),
            pl.BlockSpec((FOX_HEADS, LANES), const),
        ],
        out_specs=[
            pl.BlockSpec((tm, 4 * HGRN_W), row),
            pl.BlockSpec((tm, 3 * FOX_W), row),
            pl.BlockSpec((tm, 2 * D), row),
            pl.BlockSpec((FOX_HEADS, tm), lambda b, i: (0, b * nt + i)),
        ],
        out_shape=[
            jax.ShapeDtypeStruct((T, 4 * HGRN_W), F32),
            jax.ShapeDtypeStruct((T, 3 * FOX_W), BF16),
            jax.ShapeDtypeStruct((T, 2 * D), F32),
            jax.ShapeDtypeStruct((FOX_HEADS, T), F32),
        ],
        scratch_shapes=[pltpu.VMEM((FOX_HEADS, LANES), F32)],
        compiler_params=pltpu.CompilerParams(
            dimension_semantics=("arbitrary", "arbitrary"), vmem_limit_bytes=VMEM_LIMIT),
        name="in_proj",
    )(x2d, nw, w_all, wff_row, fb_row)


HGRN_CHUNK = 128
HGRN_SUB = 16


def _hgrn_kernel(hg_ref, lbl_ref, nw_ref, rsel_ref, a_ref, state_ref):
    ci = pl.program_id(1)
    C = HGRN_CHUNK
    dh = HGRN_DH

    @pl.when(ci == 0)
    def _():
        state_ref[...] = jnp.zeros_like(state_ref)

    lg = lbl_ref[...]
    e = jnp.exp(lg - jnp.max(lg, axis=0, keepdims=True))
    lb_all = e[0:1, :] / jnp.sum(e, axis=0, keepdims=True)

    r = lax.broadcasted_iota(jnp.int32, (C, C), 0)
    c = lax.broadcasted_iota(jnp.int32, (C, C), 1)
    tril = (c <= r).astype(BF16)
    sub = HGRN_SUB
    sh = sub.bit_length() - 1
    diag_mask = ((r >> sh) == (c >> sh)) & ((c & (sub - 1)) <= (r & (sub - 1)))
    levels = []
    m = sub
    while m < C:
        sh = m.bit_length() - 1
        levels.append((m, ((r >> (sh + 1)) == (c >> (sh + 1))) & (((r >> sh) & 1) == 1) & (((c >> sh) & 1) == 0)))
        m *= 2

    for h in range(HGRN_HEADS):
        sl = slice(h * dh, (h + 1) * dh)
        qraw = hg_ref[:, sl]
        q = qraw * jax.nn.sigmoid(qraw)
        lb = lb_all[:, sl]
        f = lb + (1.0 - lb) * jax.nn.sigmoid(hg_ref[:, HGRN_W + h * dh:HGRN_W + (h + 1) * dh])
        logf = jnp.log(f)
        k = 1.0 - f
        v = hg_ref[:, 2 * HGRN_W + h * dh:2 * HGRN_W + (h + 1) * dh]
        g = hg_ref[:, 3 * HGRN_W + h * dh:3 * HGRN_W + (h + 1) * dh]
        v_bf = v.astype(BF16)
        cum = _split_dot(_nn, tril, logf)

        nb = C // sub
        q3 = q.reshape(nb, sub, dh)
        k3 = k.reshape(nb, sub, dh)
        c3 = cum.reshape(nb, sub, dh)
        parts = []
        for s in range(sub):
            kb = jnp.broadcast_to(k3[:, s:s + 1, :], (nb, sub, dh))
            cb = jnp.broadcast_to(c3[:, s:s + 1, :], (nb, sub, dh))
            es = q3 * kb * jnp.exp(jnp.minimum(c3 - cb, 0.0))
            parts.append(es.reshape(C, dh).astype(BF16))
        p_mat = jnp.where(diag_mask, _nn(jnp.concatenate(parts, axis=1), rsel_ref[...]), 0.0)

        for m, mask in levels:
            nbm = C // m
            qm = q.reshape(nbm, m, dh)
            km = k.reshape(nbm, m, dh)
            cm = cum.reshape(nbm, m, dh)
            end = cm[:, m - 1:m, :]
            prev_end = jnp.concatenate([jnp.zeros((1, 1, dh), F32), end[:nbm - 1]], axis=0)
            qd = qm * jnp.exp(jnp.minimum(cm - jnp.broadcast_to(prev_end, (nbm, m, dh)), 0.0))
            kd = km * jnp.exp(jnp.minimum(jnp.broadcast_to(end, (nbm, m, dh)) - cm, 0.0))
            sc = _nt(qd.reshape(C, dh).astype(BF16), kd.reshape(C, dh).astype(BF16))
            p_mat = p_mat + jnp.where(mask, sc, 0.0)

        st = state_ref[h]
        o = _nn(p_mat.astype(BF16), v_bf) + _nt((q * jnp.exp(cum)).astype(BF16), st.astype(BF16))
        last = cum[C - 1:C, :]
        kdec = (k * jnp.exp(last - cum)).astype(BF16)
        state_ref[h] = jnp.exp(last) * st + _tn(v_bf, kdec)

        o = o * lax.rsqrt(jnp.mean(o * o, axis=-1, keepdims=True) + RMS_EPS) * nw_ref[:, sl]
        a_ref[:, sl] = (o * (g * jax.nn.sigmoid(g))).astype(BF16)


def _hgrn(hg, lb_logits, norm_w, rsel, batch, seq):
    T = hg.shape[0]
    C = HGRN_CHUNK
    nc = seq // C
    row = lambda b, i: (b * nc + i, 0)
    const = lambda b, i: (0, 0)
    return pl.pallas_call(
        _hgrn_kernel,
        grid=(batch, nc),
        in_specs=[
            pl.BlockSpec((C, 4 * HGRN_W), row),
            pl.BlockSpec(lb_logits.shape, const),
            pl.BlockSpec((1, HGRN_W), const),
            pl.BlockSpec(rsel.shape, const),
        ],
        out_specs=pl.BlockSpec((C, HGRN_W), row),
        out_shape=jax.ShapeDtypeStruct((T, HGRN_W), BF16),
        scratch_shapes=[pltpu.VMEM((HGRN_HEADS, HGRN_DH, HGRN_DH), F32)],
        compiler_params=pltpu.CompilerParams(
            dimension_semantics=("arbitrary", "arbitrary"), vmem_limit_bytes=VMEM_LIMIT),
        name="hgrn",
    )(hg, lb_logits, norm_w, rsel)


def _fox_kernel(q_ref, k_ref, v_ref, crow_ref, o_ref, m_ref, acc_ref):
    qi = pl.program_id(1)
    tq = q_ref.shape[0]
    tk = tq
    lane = lax.broadcasted_iota(jnp.int32, (1, LANES), 1)
    low = lane < FOX_DH
    sel_lo = jnp.where(low, 1.0, 0.0).astype(BF16)
    sel_hi = jnp.where(low, 0.0, 1.0).astype(BF16)
    ones_lane = (FOX_DH, 0)
    one_hot = tuple(jnp.where(lane == ol, 1.0, 0.0).astype(BF16) for ol in ones_lane)
    rr = lax.broadcasted_iota(jnp.int32, (tq, tk), 0)
    cc = lax.broadcasted_iota(jnp.int32, (tq, tk), 1)
    causal = cc <= rr

    for p in range(FOX_HEADS // 2):
        cols = slice(p * LANES, (p + 1) * LANES)
        q = q_ref[:, cols]
        qs = (q * sel_lo, q * sel_hi)
        m_ref[...] = jnp.full_like(m_ref, NEG_BIG)
        acc_ref[...] = jnp.zeros_like(acc_ref)

        def kv_block(j, masked):
            start = pl.multiple_of(j * tk, tk)
            kb = k_ref[pl.ds(start, tk), cols]
            vb = v_ref[pl.ds(start, tk), cols]
            vs = (vb * sel_lo + one_hot[0], vb * sel_hi + one_hot[1])
            for hh in range(2):
                ck = crow_ref[2 * p + hh:2 * p + hh + 1, pl.ds(start, tk)]
                s = _nt(qs[hh], kb) - ck
                if masked:
                    s = jnp.where(causal, s, NEG_BIG)
                m_prev = m_ref[hh]
                m_next = jnp.maximum(m_prev, jnp.max(s, axis=1, keepdims=True))
                pexp = jnp.exp2(s - jnp.concatenate([m_next] * (tk // LANES), axis=1))
                alpha = jnp.exp2(m_prev - m_next)
                acc_ref[hh] = alpha * acc_ref[hh] + _nn(pexp.astype(BF16), vs[hh])
                m_ref[hh] = m_next

        def body(j, carry):
            kv_block(j, False)
            return carry

        lax.fori_loop(0, qi, body, 0)
        kv_block(qi, True)

        a0 = acc_ref[0]
        a1 = acc_ref[1]
        o0 = a0 / a0[:, ones_lane[0]:ones_lane[0] + 1]
        o1 = a1 / a1[:, ones_lane[1]:ones_lane[1] + 1]
        o_ref[:, cols] = jnp.where(low, o0, o1).astype(BF16)


def _fox(fox, crow, batch, seq, tq):
    T = fox.shape[0]
    nq = seq // tq
    return pl.pallas_call(
        _fox_kernel,
        grid=(batch, nq),
        in_specs=[
            pl.BlockSpec((tq, FOX_W), lambda b, i: (b * nq + i, 0)),
            pl.BlockSpec((seq, FOX_W), lambda b, i: (b, 1)),
            pl.BlockSpec((seq, FOX_W), lambda b, i: (b, 2)),
            pl.BlockSpec((FOX_HEADS, seq), lambda b, i: (0, b)),
        ],
        out_specs=pl.BlockSpec((tq, FOX_W), lambda b, i: (b * nq + i, 0)),
        out_shape=jax.ShapeDtypeStruct((T, FOX_W), BF16),
        scratch_shapes=[pltpu.VMEM((2, tq, LANES), F32), pltpu.VMEM((2, tq, LANES), F32)],
        compiler_params=pltpu.CompilerParams(
            dimension_semantics=("arbitrary", "arbitrary"), vmem_limit_bytes=VMEM_LIMIT),
        name="fox",
    )(fox, fox, fox, crow)


def _merge_kernel(x_ref, a_ref, b_ref, gate_ref, wa_ref, wb_ref, wo_ref, nw_ref, wq_ref, keys_ref,
                  h1_ref, xn_ref, st_ref):
    D = x_ref.shape[1]
    merged = gate_ref[:, :D] * _nn(a_ref[...], wa_ref[...]) + gate_ref[:, D:] * _nn(b_ref[...], wb_ref[...])
    h1 = x_ref[...] + _nn(merged.astype(BF16), wo_ref[...])
    h1_ref[...] = h1
    xn = _rms(h1, nw_ref[...]).astype(BF16)
    xn_ref[...] = xn
    q = _nn(xn, wq_ref[...]).astype(BF16)
    for hp in range(2 * PEER_HEADS):
        st_ref[hp] = _nt(keys_ref[hp], q[:, hp * PEER_HALF:(hp + 1) * PEER_HALF])


def _merge(x2d, a, b, gates, wa, wb, wo, nw, wq, keys, tm):
    T, D = x2d.shape
    row = lambda i: (i, 0)
    const = lambda i: (0, 0)
    return pl.pallas_call(
        _merge_kernel,
        grid=(T // tm,),
        in_specs=[
            pl.BlockSpec((tm, D), row),
            pl.BlockSpec((tm, HGRN_W), row),
            pl.BlockSpec((tm, FOX_W), row),
            pl.BlockSpec((tm, 2 * D), row),
            pl.BlockSpec(wa.shape, const),
            pl.BlockSpec(wb.shape, const),
            pl.BlockSpec(wo.shape, const),
            pl.BlockSpec((1, D), const),
            pl.BlockSpec(wq.shape, const),
            pl.BlockSpec(keys.shape, lambda i: (0, 0, 0)),
        ],
        out_specs=[
            pl.BlockSpec((tm, D), row),
            pl.BlockSpec((tm, D), row),
            pl.BlockSpec((2 * PEER_HEADS, N_KEYS, tm), lambda i: (0, 0, i)),
        ],
        out_shape=[
            jax.ShapeDtypeStruct((T, D), F32),
            jax.ShapeDtypeStruct((T, D), BF16),
            jax.ShapeDtypeStruct((2 * PEER_HEADS, N_KEYS, T), F32),
        ],
        compiler_params=pltpu.CompilerParams(
            dimension_semantics=("arbitrary",), vmem_limit_bytes=VMEM_LIMIT),
        name="merge",
    )(x2d, a, b, gates, wa, wb, wo, nw, wq, keys)


def _rank_pairs():
    n = TOPK + 1
    return [(r, c) for r in range(n) for c in range(n) if (r + 1) * (c + 1) <= n]


def _topk_kernel(st_ref, cnt_ref, p1_ref, rank_ref, p2_ref, rank1_ref):
    tt = st_ref.shape[2]
    n_top = TOPK + 1

    def top_sorted(s):
        vals = []
        cur = s
        rank = jnp.full(s.shape, float(n_top), F32)
        for r in range(n_top):
            mx = jnp.max(cur, axis=0, keepdims=True)
            vals.append(mx)
            hit = cur == mx
            rank = jnp.where(hit, float(r), rank)
            cur = jnp.where(hit, -jnp.inf, cur)
        return vals, rank

    tops = []
    for h in range(PEER_HEADS):
        v1, r1 = top_sorted(st_ref[2 * h])
        v2, r2 = top_sorted(st_ref[2 * h + 1])
        rank1_ref[h] = r1
        rank_ref[h] = r2
        tops.append((v1, v2))
    a = [jnp.concatenate([tops[h][0][r] for h in range(PEER_HEADS)], axis=0) for r in range(n_top)]
    b = [jnp.concatenate([tops[h][1][r] for h in range(PEER_HEADS)], axis=0) for r in range(n_top)]
    cand = [a[r] + b[c] for r, c in _rank_pairs()]
    tau = jnp.full((PEER_HEADS, tt), -jnp.inf, F32)
    nxt = jnp.full((PEER_HEADS, tt), -jnp.inf, F32)
    for xi in cand:
        cnt = jnp.zeros((PEER_HEADS, tt), F32)
        for xj in cand:
            cnt = cnt + jnp.where(xj >= xi, 1.0, 0.0)
        tau = jnp.maximum(tau, jnp.where(cnt >= TOPK, xi, -jnp.inf))
        nxt = jnp.maximum(nxt, jnp.where(cnt >= TOPK + 1, xi, -jnp.inf))
    top = a[0] + b[0]
    z = jnp.zeros((PEER_HEADS, tt), F32)
    for xi in cand:
        z = z + jnp.where(xi >= tau, jnp.exp(xi - top), 0.0)
    inv_z = 1.0 / z
    cut = 0.5 * (tau + nxt)
    n_sel = [jnp.zeros((PEER_HEADS, tt), F32) for _ in range(n_top)]
    for r, c in _rank_pairs():
        n_sel[r] = n_sel[r] + jnp.where(a[r] + b[c] >= cut, 1.0, 0.0)
    for h in range(PEER_HEADS):
        s1 = st_ref[2 * h]
        s2 = st_ref[2 * h + 1]
        r1 = rank1_ref[h]
        cnt = jnp.zeros(r1.shape, F32)
        for r in range(n_top):
            cnt = jnp.where(r1 == float(r), n_sel[r][h:h + 1, :], cnt)
        cnt_ref[h] = cnt
        p1_ref[h] = jnp.exp(s1 - a[0][h:h + 1, :]) * inv_z[h:h + 1, :]
        p2_ref[h] = jnp.exp(s2 - b[0][h:h + 1, :])


def _topk(st, tt):
    n, nk, T = st.shape
    out32 = jax.ShapeDtypeStruct((PEER_HEADS, nk, T), F32)
    spec = pl.BlockSpec((PEER_HEADS, nk, tt), lambda i: (0, 0, i))
    return pl.pallas_call(
        _topk_kernel,
        grid=(T // tt,),
        in_specs=[pl.BlockSpec((n, nk, tt), lambda i: (0, 0, i))],
        out_specs=[spec, spec, spec, spec],
        out_shape=[out32, out32, out32, out32],
        scratch_shapes=[pltpu.VMEM((PEER_HEADS, nk, tt), F32)],
        compiler_params=pltpu.CompilerParams(
            dimension_semantics=("arbitrary",), vmem_limit_bytes=VMEM_LIMIT),
        name="topk",
    )(st)


PEER_KEY_TILE = 64
PEER_TOK_TILE = 128
PEER_PAIR = 2
SUBLANES = 8


def _peer_kernel(xn_ref, u_ref, v_ref, cnt_ref, p1_ref, rank_ref, p2_ref, h1_ref, nw_ref,
                 o_ref, acc_ref, ht_ref, g_ref, cnt_rows, p1_rows):
    j = pl.program_id(1)
    nj = pl.num_programs(1)
    eb, tb = ht_ref.shape
    groups = eb // N_KEYS
    cur = j % 2
    jb = jnp.minimum(j, nj - 2)

    @pl.when(j == 0)
    def _():
        acc_ref[...] = jnp.zeros_like(acc_ref)
        g_ref[...] = jnp.zeros_like(g_ref)

    acc_ref[...] += _tn(g_ref[1 - cur], v_ref[...])

    ht_ref[...] = _nt(u_ref[...], xn_ref[...])
    kt = PEER_KEY_TILE // SUBLANES
    for g0 in range(0, groups, PEER_PAIR):
        for gi in range(PEER_PAIR):
            i1 = jb * groups + g0 + gi
            for h in range(PEER_HEADS):
                r = gi * PEER_HEADS + h
                cnt_rows[r] = jnp.broadcast_to(cnt_ref[h, pl.ds(i1, 1), :], (SUBLANES, tb))
                p1_rows[r] = jnp.broadcast_to(p1_ref[h, pl.ds(i1, 1), :], (SUBLANES, tb))
        for t0 in range(0, tb, PEER_TOK_TILE):
            ts = slice(t0, t0 + PEER_TOK_TILE)
            for k0 in range(0, N_KEYS, PEER_KEY_TILE):
                ks = slice(k0, k0 + PEER_KEY_TILE)
                gate = [None] * PEER_PAIR
                for h in range(PEER_HEADS):
                    rk = rank_ref[h, ks, ts]
                    p2 = p2_ref[h, ks, ts]
                    for gi in range(PEER_PAIR):
                        r = gi * PEER_HEADS + h
                        cn = jnp.concatenate([cnt_rows[r, :, ts]] * kt, axis=0)
                        p1 = jnp.concatenate([p1_rows[r, :, ts]] * kt, axis=0)
                        term = jnp.where(rk < cn, p2, 0.0) * p1
                        gate[gi] = term if gate[gi] is None else gate[gi] + term
                for gi in range(PEER_PAIR):
                    es = slice((g0 + gi) * N_KEYS + k0, (g0 + gi) * N_KEYS + k0 + PEER_KEY_TILE)
                    ht = ht_ref[es, ts]
                    act = 0.5 * ht * (1.0 + lax.erf(ht * (2.0 ** -0.5)))
                    g_ref[cur, es, ts] = (act * gate[gi]).astype(BF16)

    @pl.when(j == nj - 1)
    def _():
        o_ref[...] = _rms(h1_ref[...] + acc_ref[...], nw_ref[...])


def _peer(xn, u, v, cnt, p1, rank2, p2, h1, nw, tb, eb):
    T, D = xn.shape
    n_exp = u.shape[0]
    tok3 = lambda i, j: (0, 0, i)
    nb = n_exp // eb
    return pl.pallas_call(
        _peer_kernel,
        grid=(T // tb, nb + 1),
        in_specs=[
            pl.BlockSpec((tb, D), lambda i, j: (i, 0)),
            pl.BlockSpec((eb, D), lambda i, j: (jnp.minimum(j, nb - 1), 0)),
            pl.BlockSpec((eb, D), lambda i, j: (jnp.maximum(j - 1, 0), 0)),
            pl.BlockSpec((PEER_HEADS, N_KEYS, tb), tok3),
            pl.BlockSpec((PEER_HEADS, N_KEYS, tb), tok3),
            pl.BlockSpec((PEER_HEADS, N_KEYS, tb), tok3),
            pl.BlockSpec((PEER_HEADS, N_KEYS, tb), tok3),
            pl.BlockSpec((tb, D), lambda i, j: (i, 0)),
            pl.BlockSpec((1, D), lambda i, j: (0, 0)),
        ],
        out_specs=pl.BlockSpec((tb, D), lambda i, j: (i, 0)),
        out_shape=jax.ShapeDtypeStruct((T, D), F32),
        scratch_shapes=[pltpu.VMEM((tb, D), F32), pltpu.VMEM((eb, tb), F32), pltpu.VMEM((2, eb, tb), BF16),
                        pltpu.VMEM((PEER_PAIR * PEER_HEADS, SUBLANES, tb), F32),
                        pltpu.VMEM((PEER_PAIR * PEER_HEADS, SUBLANES, tb), F32)],
        compiler_params=pltpu.CompilerParams(
            dimension_semantics=("arbitrary", "arbitrary"), vmem_limit_bytes=VMEM_LIMIT),
        name="peer",
    )(xn, u, v, cnt, p1, rank2, p2, h1, nw)


ROUTE_SLOTS = 64


def _route_pairs():
    return [(r, c) for r in range(TOPK) for c in range(TOPK) if (r + 1) * (c + 1) <= TOPK]


def _routes_kernel(st_ref, ids_ref, w_ref):
    tt = st_ref.shape[2]
    key_id = lax.broadcasted_iota(jnp.int32, (N_KEYS, tt), 0).astype(F32)

    def top_sorted(s):
        vals, idxs = [], []
        cur = s
        for _ in range(TOPK):
            mx = jnp.max(cur, axis=0, keepdims=True)
            hit = cur == mx
            vals.append(mx)
            idxs.append(jnp.max(jnp.where(hit, key_id, -1.0), axis=0, keepdims=True))
            cur = jnp.where(hit, -jnp.inf, cur)
        return vals, idxs

    tops = [(top_sorted(st_ref[2 * h]), top_sorted(st_ref[2 * h + 1])) for h in range(PEER_HEADS)]

    def stack(half, which, r):
        return jnp.concatenate([tops[h][half][which][r] for h in range(PEER_HEADS)], axis=0)

    a = [stack(0, 0, r) for r in range(TOPK)]
    b = [stack(1, 0, r) for r in range(TOPK)]
    ia = [stack(0, 1, r) for r in range(TOPK)]
    ib = [stack(1, 1, r) for r in range(TOPK)]
    pairs = _route_pairs()
    cand = [a[r] + b[c] for r, c in pairs]
    tau = jnp.full((PEER_HEADS, tt), -jnp.inf, F32)
    for xi in cand:
        cnt = jnp.zeros((PEER_HEADS, tt), F32)
        for xj in cand:
            cnt = cnt + jnp.where(xj >= xi, 1.0, 0.0)
        tau = jnp.maximum(tau, jnp.where(cnt >= TOPK, xi, -jnp.inf))
    top = a[0] + b[0]
    e = [jnp.where(xi >= tau, jnp.exp(xi - top), 0.0) for xi in cand]
    z = e[0]
    for ei in e[1:]:
        z = z + ei
    inv_z = 1.0 / z
    for k, (r, c) in enumerate(pairs):
        ids_ref[k] = jnp.clip(ia[r] * float(N_KEYS) + ib[c], 0.0, float(N_KEYS * N_KEYS - 1)).astype(jnp.int32)
        w_ref[k] = e[k] * inv_z
    for k in range(len(pairs), ROUTE_SLOTS):
        ids_ref[k] = jnp.zeros((PEER_HEADS, tt), jnp.int32)
        w_ref[k] = jnp.zeros((PEER_HEADS, tt), F32)


def _routes(st, tt):
    n, nk, T = st.shape
    spec = pl.BlockSpec((ROUTE_SLOTS, PEER_HEADS, tt), lambda i: (0, 0, i))
    return pl.pallas_call(
        _routes_kernel,
        grid=(T // tt,),
        in_specs=[pl.BlockSpec((n, nk, tt), lambda i: (0, 0, i))],
        out_specs=[spec, spec],
        out_shape=[jax.ShapeDtypeStruct((ROUTE_SLOTS, PEER_HEADS, T), jnp.int32),
                   jax.ShapeDtypeStruct((ROUTE_SLOTS, PEER_HEADS, T), F32)],
        compiler_params=pltpu.CompilerParams(
            dimension_semantics=("arbitrary",), vmem_limit_bytes=VMEM_LIMIT),
        name="routes",
    )(st)


SC_LANES = 16
SC_TOKENS_PER_CHUNK = 8


def _gate_matrix(ids, w, n_exp):
    T, E = ids.shape
    info = plsc.get_sparse_core_info()
    workers = info.num_cores * info.num_subcores
    per_worker = T // workers
    ch = SC_TOKENS_PER_CHUNK
    mesh = plsc.VectorSubcoreMesh(core_axis_name="c", subcore_axis_name="s")
    ids_flat = ids.reshape(T * E)
    w_flat = w.reshape(T * E)

    @functools.partial(
        pl.kernel, mesh=mesh,
        out_type=jax.ShapeDtypeStruct((T, n_exp), F32),
        scratch_types=[pltpu.VMEM((ch * E,), jnp.int32), pltpu.VMEM((ch * E,), F32),
                       pltpu.VMEM((n_exp,), F32)],
        compiler_params=pltpu.CompilerParams(needs_layout_passes=False),
        name="gate_matrix",
    )
    def scatter(ids_hbm, w_hbm, out_hbm, ids_v, w_v, row_v):
        wid = lax.axis_index("s") * info.num_cores + lax.axis_index("c")
        base = wid * per_worker
        zeros = jnp.zeros((SC_LANES,), F32)

        @pl.loop(0, n_exp, step=SC_LANES)
        def _(i):
            row_v[pl.ds(i, SC_LANES)] = zeros

        @pl.loop(0, per_worker // ch)
        def _(ci):
            t0 = base + ci * ch
            pltpu.sync_copy(ids_hbm.at[pl.ds(t0 * E, ch * E)], ids_v)
            pltpu.sync_copy(w_hbm.at[pl.ds(t0 * E, ch * E)], w_v)
            for tl in range(ch):
                @pl.loop(0, E, step=SC_LANES)
                def _(e0):
                    sl = pl.ds(tl * E + e0, SC_LANES)
                    plsc.addupdate_scatter(row_v, [ids_v[sl]], w_v[sl])

                pltpu.sync_copy(row_v, out_hbm.at[t0 + tl])

                @pl.loop(0, E, step=SC_LANES)
                def _(e0):
                    sl = pl.ds(tl * E + e0, SC_LANES)
                    plsc.store_scatter(row_v, [ids_v[sl]], zeros)

    return scatter(ids_flat, w_flat)


def _experts_kernel(xn_ref, u_ref, v_ref, gate_ref, h1_ref, nw_ref, o_ref, acc_ref):
    j = pl.program_id(1)
    nj = pl.num_programs(1)

    @pl.when(j == 0)
    def _():
        acc_ref[...] = jnp.zeros_like(acc_ref)

    h = _nt(xn_ref[...], u_ref[...])
    act = 0.5 * h * (1.0 + lax.erf(h * (2.0 ** -0.5)))
    acc_ref[...] += _nn((act * gate_ref[...]).astype(BF16), v_ref[...])

    @pl.when(j == nj - 1)
    def _():
        o_ref[...] = _rms(h1_ref[...] + acc_ref[...], nw_ref[...])


def _experts(xn, u, v, gate, h1, nw, tb, eb):
    T, D = xn.shape
    n_exp = u.shape[0]
    return pl.pallas_call(
        _experts_kernel,
        grid=(T // tb, n_exp // eb),
        in_specs=[
            pl.BlockSpec((tb, D), lambda i, j: (i, 0)),
            pl.BlockSpec((eb, D), lambda i, j: (j, 0)),
            pl.BlockSpec((eb, D), lambda i, j: (j, 0)),
            pl.BlockSpec((tb, eb), lambda i, j: (i, j)),
            pl.BlockSpec((tb, D), lambda i, j: (i, 0)),
            pl.BlockSpec((1, D), lambda i, j: (0, 0)),
        ],
        out_specs=pl.BlockSpec((tb, D), lambda i, j: (i, 0)),
        out_shape=jax.ShapeDtypeStruct((T, D), F32),
        scratch_shapes=[pltpu.VMEM((tb, D), F32)],
        compiler_params=pltpu.CompilerParams(
            dimension_semantics=("arbitrary", "arbitrary"), vmem_limit_bytes=VMEM_LIMIT),
        name="experts",
    )(xn, u, v, gate, h1, nw)


def _block_select_matrix():
    part = jnp.arange(HGRN_SUB * HGRN_DH, dtype=jnp.int32) // HGRN_DH
    col = jnp.arange(HGRN_CHUNK, dtype=jnp.int32) % HGRN_SUB
    return (part[:, None] == col[None, :]).astype(BF16)


def _forward(x, norm_mix_w, w_in, hgrn_lb_logits, hgrn_norm_w, fox_f_bias, w_branch_hgrn,
             w_branch_fox, w_out, norm_ffn_w, peer_w_q, peer_sub_keys, peer_u, peer_v,
             norm_final_w, *, tm_in, tq, tm_merge, tt, tb, eb):
    B, S, D = x.shape
    T = B * S
    x2d = x.reshape(T, D)
    n_h = 4 * HGRN_W
    n_f = 3 * FOX_W
    wi = w_in[0]
    w_all = jnp.concatenate([wi[:, :n_h + n_f], wi[:, n_h + n_f + FOX_HEADS:]], axis=1).astype(BF16)
    wff = wi[:, n_h + n_f:n_h + n_f + FOX_HEADS]
    wff_row = wff.T.astype(BF16)
    fb = fox_f_bias[0].astype(F32)
    fb_row = jnp.broadcast_to(fb.reshape(FOX_HEADS, 1), (FOX_HEADS, LANES))

    hg, fox, gates, crow = _in_proj(
        x2d, norm_mix_w[0].reshape(1, D), w_all, wff_row, fb_row, B, S, tm_in)
    a = _hgrn(hg, hgrn_lb_logits, hgrn_norm_w[0].reshape(1, HGRN_W), _block_select_matrix(), B, S)
    b = _fox(fox, crow, B, S, tq)
    keys = peer_sub_keys[0].reshape(2 * PEER_HEADS, N_KEYS, PEER_HALF).astype(BF16)
    h1, xn2, st = _merge(
        x2d, a, b, gates, w_branch_hgrn[0].astype(BF16), w_branch_fox[0].astype(BF16),
        w_out[0].astype(BF16), norm_ffn_w[0].reshape(1, D), peer_w_q[0].astype(BF16), keys, tm_merge)
    ids3, w3 = _routes(st, tt)
    n_exp = peer_u.shape[1]
    ids = ids3.transpose(2, 1, 0).reshape(T, PEER_HEADS * ROUTE_SLOTS)
    wts = w3.transpose(2, 1, 0).reshape(T, PEER_HEADS * ROUTE_SLOTS)
    gate = _gate_matrix(ids, wts, n_exp)
    out = _experts(xn2, peer_u[0].astype(BF16), peer_v[0].astype(BF16), gate, h1,
                   norm_final_w.reshape(1, D), tb, eb)
    return out.reshape(B, S, D)


def kernel(x, norm_mix_w, w_in, hgrn_lb_logits, hgrn_norm_w, fox_f_bias, w_branch_hgrn, w_branch_fox, w_out, norm_ffn_w, peer_w_q, peer_sub_keys, peer_u, peer_v, norm_final_w):
    return _forward(x, norm_mix_w, w_in, hgrn_lb_logits, hgrn_norm_w, fox_f_bias, w_branch_hgrn,
                    w_branch_fox, w_out, norm_ffn_w, peer_w_q, peer_sub_keys, peer_u, peer_v,
                    norm_final_w, tm_in=256, tq=512, tm_merge=256, tt=256, tb=512, eb=512)
```

```python
import functools
import math

import jax
import jax.numpy as jnp
from jax import lax
from jax.experimental import pallas as pl
from jax.experimental.pallas import tpu as pltpu
from jax.experimental.pallas import tpu_sc as plsc

F32 = jnp.float32
BF16 = jnp.bfloat16
RMS_EPS = 1e-6
NEG_BIG = -1e30

HGRN_HEADS = 4
HGRN_DH = 128
HGRN_W = HGRN_HEADS * HGRN_DH
FOX_HEADS = 8
FOX_DH = 64
FOX_W = FOX_HEADS * FOX_DH
PEER_HEADS = 8
PEER_HALF = 128
N_KEYS = 128
TOPK = 16
LANES = 128

VMEM_LIMIT = 56 * 1024 * 1024


def _nt(a, b):
    return lax.dot_general(a, b, (((1,), (1,)), ((), ())), preferred_element_type=F32)


def _tn(a, b):
    return lax.dot_general(a, b, (((0,), (0,)), ((), ())), preferred_element_type=F32)


def _nn(a, b):
    return jnp.dot(a, b, preferred_element_type=F32)


def _split_dot(fn, tri, x):
    hi = x.astype(BF16)
    lo = (x - hi.astype(F32)).astype(BF16)
    return fn(tri, hi) + fn(tri, lo)


def _log_sigmoid(x):
    return jnp.minimum(x, 0.0) - jnp.log1p(jnp.exp(-jnp.abs(x)))


def _rms(x, w):
    return x * lax.rsqrt(jnp.mean(x * x, axis=-1, keepdims=True) + RMS_EPS) * w


LOG2E = math.log2(math.e)


def _inproj_kernel(x_ref, nw_ref, w_ref, wffr_ref, fbr_ref,
                   hg_ref, fox_ref, gate_ref, crow_ref, carry_row):
    i = pl.program_id(1)
    tm = x_ref.shape[0]

    @pl.when(i == 0)
    def _():
        carry_row[...] = jnp.zeros_like(carry_row)

    xn = _rms(x_ref[...], nw_ref[...]).astype(BF16)
    proj = _nn(xn, w_ref[...])
    n_h = 4 * HGRN_W
    hg_ref[...] = proj[:, :n_h]
    fox_ref[:, :FOX_W] = (proj[:, n_h:n_h + FOX_W] * (FOX_DH ** -0.5 * LOG2E)).astype(BF16)
    fox_ref[:, FOX_W:] = proj[:, n_h + FOX_W:n_h + 3 * FOX_W].astype(BF16)
    gate_ref[...] = jax.nn.sigmoid(proj[:, n_h + 3 * FOX_W:])

    r = lax.broadcasted_iota(jnp.int32, (tm, tm), 0)
    c = lax.broadcasted_iota(jnp.int32, (tm, tm), 1)
    triu = (r <= c).astype(BF16)
    ls_row = _log_sigmoid(_nt(wffr_ref[...], xn) + fbr_ref[:, 0:1]) * LOG2E
    hi = ls_row.astype(BF16)
    lo = (ls_row - hi.astype(F32)).astype(BF16)
    crow = _nn(hi, triu) + _nn(lo, triu) + carry_row[:, 0:1]
    crow_ref[...] = crow
    carry_row[...] = jnp.broadcast_to(crow[:, tm - 1:tm], carry_row.shape)


def _in_proj(x2d, nw, w_all, wff_row, fb_row, batch, seq, tm):
    T, D = x2d.shape
    nt = seq // tm
    n_all = w_all.shape[1]
    row = lambda b, i: (b * nt + i, 0)
    const = lambda b, i: (0, 0)
    return pl.pallas_call(
        _inproj_kernel,
        grid=(batch, nt),
        in_specs=[
            pl.BlockSpec((tm, D), row),
            pl.BlockSpec((1, D), const),
            pl.BlockSpec((D, n_all), const),
            pl.BlockSpec((FOX_HEADS, D), const),
            pl.BlockSpec((FOX_HEADS, LANES), const),
        ],
        out_specs=[
            pl.BlockSpec((tm, 4 * HGRN_W), row),
            pl.BlockSpec((tm, 3 * FOX_W), row),
            pl.BlockSpec((tm, 2 * D), row),
            pl.BlockSpec((FOX_HEADS, tm), lambda b, i: (0, b * nt + i)),
        ],
        out_shape=[
            jax.ShapeDtypeStruct((T, 4 * HGRN_W), F32),
            jax.ShapeDtypeStruct((T, 3 * FOX_W), BF16),
            jax.ShapeDtypeStruct((T, 2 * D), F32),
            jax.ShapeDtypeStruct((FOX_HEADS, T), F32),
        ],
        scratch_shapes=[pltpu.VMEM((FOX_HEADS, LANES), F32)],
        compiler_params=pltpu.CompilerParams(
            dimension_semantics=("arbitrary", "arbitrary"), vmem_limit_bytes=VMEM_LIMIT),
        name="in_proj",
    )(x2d, nw, w_all, wff_row, fb_row)


HGRN_CHUNK = 128
HGRN_SUB = 16


def _hgrn_kernel(hg_ref, lbl_ref, nw_ref, rsel_ref, a_ref, state_ref):
    ci = pl.program_id(1)
    C = HGRN_CHUNK
    dh = HGRN_DH

    @pl.when(ci == 0)
    def _():
        state_ref[...] = jnp.zeros_like(state_ref)

    lg = lbl_ref[...]
    e = jnp.exp(lg - jnp.max(lg, axis=0, keepdims=True))
    lb_all = e[0:1, :] / jnp.sum(e, axis=0, keepdims=True)

    r = lax.broadcasted_iota(jnp.int32, (C, C), 0)
    c = lax.broadcasted_iota(jnp.int32, (C, C), 1)
    tril = (c <= r).astype(BF16)
    sub = HGRN_SUB
    sh = sub.bit_length() - 1
    diag_mask = ((r >> sh) == (c >> sh)) & ((c & (sub - 1)) <= (r & (sub - 1)))
    levels = []
    m = sub
    while m < C:
        sh = m.bit_length() - 1
        levels.append((m, ((r >> (sh + 1)) == (c >> (sh + 1))) & (((r >> sh) & 1) == 1) & (((c >> sh) & 1) == 0)))
        m *= 2

    for h in range(HGRN_HEADS):
        sl = slice(h * dh, (h + 1) * dh)
        qraw = hg_ref[:, sl]
        q = qraw * jax.nn.sigmoid(qraw)
        lb = lb_all[:, sl]
        f = lb + (1.0 - lb) * jax.nn.sigmoid(hg_ref[:, HGRN_W + h * dh:HGRN_W + (h + 1) * dh])
        logf = jnp.log(f)
        k = 1.0 - f
        v = hg_ref[:, 2 * HGRN_W + h * dh:2 * HGRN_W + (h + 1) * dh]
        g = hg_ref[:, 3 * HGRN_W + h * dh:3 * HGRN_W + (h + 1) * dh]
        v_bf = v.astype(BF16)
        cum = _split_dot(_nn, tril, logf)

        nb = C // sub
        q3 = q.reshape(nb, sub, dh)
        k3 = k.reshape(nb, sub, dh)
        c3 = cum.reshape(nb, sub, dh)
        parts = []
        for s in range(sub):
            kb = jnp.broadcast_to(k3[:, s:s + 1, :], (nb, sub, dh))
            cb = jnp.broadcast_to(c3[:, s:s + 1, :], (nb, sub, dh))
            es = q3 * kb * jnp.exp(jnp.minimum(c3 - cb, 0.0))
            parts.append(es.reshape(C, dh).astype(BF16))
        p_mat = jnp.where(diag_mask, _nn(jnp.concatenate(parts, axis=1), rsel_ref[...]), 0.0)

        for m, mask in levels:
            nbm = C // m
            qm = q.reshape(nbm, m, dh)
            km = k.reshape(nbm, m, dh)
            cm = cum.reshape(nbm, m, dh)
            end = cm[:, m - 1:m, :]
            prev_end = jnp.concatenate([jnp.zeros((1, 1, dh), F32), end[:nbm - 1]], axis=0)
            qd = qm * jnp.exp(jnp.minimum(cm - jnp.broadcast_to(prev_end, (nbm, m, dh)), 0.0))
            kd = km * jnp.exp(jnp.minimum(jnp.broadcast_to(end, (nbm, m, dh)) - cm, 0.0))
            sc = _nt(qd.reshape(C, dh).astype(BF16), kd.reshape(C, dh).astype(BF16))
            p_mat = p_mat + jnp.where(mask, sc, 0.0)

        st = state_ref[h]
        o = _nn(p_mat.astype(BF16), v_bf) + _nt((q * jnp.exp(cum)).astype(BF16), st.astype(BF16))
        last = cum[C - 1:C, :]
        kdec = (k * jnp.exp(last - cum)).astype(BF16)
        state_ref[h] = jnp.exp(last) * st + _tn(v_bf, kdec)

        o = o * lax.rsqrt(jnp.mean(o * o, axis=-1, keepdims=True) + RMS_EPS) * nw_ref[:, sl]
        a_ref[:, sl] = (o * (g * jax.nn.sigmoid(g))).astype(BF16)


def _hgrn(hg, lb_logits, norm_w, rsel, batch, seq):
    T = hg.shape[0]
    C = HGRN_CHUNK
    nc = seq // C
    row = lambda b, i: (b * nc + i, 0)
    const = lambda b, i: (0, 0)
    return pl.pallas_call(
        _hgrn_kernel,
        grid=(batch, nc),
        in_specs=[
            pl.BlockSpec((C, 4 * HGRN_W), row),
            pl.BlockSpec(lb_logits.shape, const),
            pl.BlockSpec((1, HGRN_W), const),
            pl.BlockSpec(rsel.shape, const),
        ],
        out_specs=pl.BlockSpec((C, HGRN_W), row),
        out_shape=jax.ShapeDtypeStruct((T, HGRN_W), BF16),
        scratch_shapes=[pltpu.VMEM((HGRN_HEADS, HGRN_DH, HGRN_DH), F32)],
        compiler_params=pltpu.CompilerParams(
            dimension_semantics=("arbitrary", "arbitrary"), vmem_limit_bytes=VMEM_LIMIT),
        name="hgrn",
    )(hg, lb_logits, norm_w, rsel)


def _fox_kernel(q_ref, k_ref, v_ref, crow_ref, o_ref, m_ref, acc_ref):
    qi = pl.program_id(1)
    tq = q_ref.shape[0]
    tk = tq
    lane = lax.broadcasted_iota(jnp.int32, (1, LANES), 1)
    low = lane < FOX_DH
    sel_lo = jnp.where(low, 1.0, 0.0).astype(BF16)
    sel_hi = jnp.where(low, 0.0, 1.0).astype(BF16)
    ones_lane = (FOX_DH, 0)
    one_hot = tuple(jnp.where(lane == ol, 1.0, 0.0).astype(BF16) for ol in ones_lane)
    rr = lax.broadcasted_iota(jnp.int32, (tq, tk), 0)
    cc = lax.broadcasted_iota(jnp.int32, (tq, tk), 1)
    causal = cc <= rr

    for p in range(FOX_HEADS // 2):
        cols = slice(p * LANES, (p + 1) * LANES)
        q = q_ref[:, cols]
        qs = (q * sel_lo, q * sel_hi)
        m_ref[...] = jnp.full_like(m_ref, NEG_BIG)
        acc_ref[...] = jnp.zeros_like(acc_ref)

        def kv_block(j, masked):
            start = pl.multiple_of(j * tk, tk)
            kb = k_ref[pl.ds(start, tk), cols]
            vb = v_ref[pl.ds(start, tk), cols]
            vs = (vb * sel_lo + one_hot[0], vb * sel_hi + one_hot[1])
            for hh in range(2):
                ck = crow_ref[2 * p + hh:2 * p + hh + 1, pl.ds(start, tk)]
                s = _nt(qs[hh], kb) - ck
                if masked:
                    s = jnp.where(causal, s, NEG_BIG)
                m_prev = m_ref[hh]
                m_next = jnp.maximum(m_prev, jnp.max(s, axis=1, keepdims=True))
                pexp = jnp.exp2(s - jnp.concatenate([m_next] * (tk // LANES), axis=1))
                alpha = jnp.exp2(m_prev - m_next)
                acc_ref[hh] = alpha * acc_ref[hh] + _nn(pexp.astype(BF16), vs[hh])
                m_ref[hh] = m_next

        def body(j, carry):
            kv_block(j, False)
            return carry

        lax.fori_loop(0, qi, body, 0)
        kv_block(qi, True)

        a0 = acc_ref[0]
        a1 = acc_ref[1]
        o0 = a0 / a0[:, ones_lane[0]:ones_lane[0] + 1]
        o1 = a1 / a1[:, ones_lane[1]:ones_lane[1] + 1]
        o_ref[:, cols] = jnp.where(low, o0, o1).astype(BF16)


def _fox(fox, crow, batch, seq, tq):
    T = fox.shape[0]
    nq = seq // tq
    return pl.pallas_call(
        _fox_kernel,
        grid=(batch, nq),
        in_specs=[
            pl.BlockSpec((tq, FOX_W), lambda b, i: (b * nq + i, 0)),
            pl.BlockSpec((seq, FOX_W), lambda b, i: (b, 1)),
            pl.BlockSpec((seq, FOX_W), lambda b, i: (b, 2)),
            pl.BlockSpec((FOX_HEADS, seq), lambda b, i: (0, b)),
        ],
        out_specs=pl.BlockSpec((tq, FOX_W), lambda b, i: (b * nq + i, 0)),
        out_shape=jax.ShapeDtypeStruct((T, FOX_W), BF16),
        scratch_shapes=[pltpu.VMEM((2, tq, LANES), F32), pltpu.VMEM((2, tq, LANES), F32)],
        compiler_params=pltpu.CompilerParams(
            dimension_semantics=("arbitrary", "arbitrary"), vmem_limit_bytes=VMEM_LIMIT),
        name="fox",
    )(fox, fox, fox, crow)


def _merge_kernel(x_ref, a_ref, b_ref, gate_ref, wa_ref, wb_ref, wo_ref, nw_ref, wq_ref, keys_ref,
                  h1_ref, xn_ref, st_ref):
    D = x_ref.shape[1]
    merged = gate_ref[:, :D] * _nn(a_ref[...], wa_ref[...]) + gate_ref[:, D:] * _nn(b_ref[...], wb_ref[...])
    h1 = x_ref[...] + _nn(merged.astype(BF16), wo_ref[...])
    h1_ref[...] = h1
    xn = _rms(h1, nw_ref[...]).astype(BF16)
    xn_ref[...] = xn
    q = _nn(xn, wq_ref[...]).astype(BF16)
    for hp in range(2 * PEER_HEADS):
        st_ref[hp] = _nt(keys_ref[hp], q[:, hp * PEER_HALF:(hp + 1) * PEER_HALF])


def _merge(x2d, a, b, gates, wa, wb, wo, nw, wq, keys, tm):
    T, D = x2d.shape
    row = lambda i: (i, 0)
    const = lambda i: (0, 0)
    return pl.pallas_call(
        _merge_kernel,
        grid=(T // tm,),
        in_specs=[
            pl.BlockSpec((tm, D), row),
            pl.BlockSpec((tm, HGRN_W), row),
            pl.BlockSpec((tm, FOX_W), row),
            pl.BlockSpec((tm, 2 * D), row),
            pl.BlockSpec(wa.shape, const),
            pl.BlockSpec(wb.shape, const),
            pl.BlockSpec(wo.shape, const),
            pl.BlockSpec((1, D), const),
            pl.BlockSpec(wq.shape, const),
            pl.BlockSpec(keys.shape, lambda i: (0, 0, 0)),
        ],
        out_specs=[
            pl.BlockSpec((tm, D), row),
            pl.BlockSpec((tm, D), row),
            pl.BlockSpec((2 * PEER_HEADS, N_KEYS, tm), lambda i: (0, 0, i)),
        ],
        out_shape=[
            jax.ShapeDtypeStruct((T, D), F32),
            jax.ShapeDtypeStruct((T, D), BF16),
            jax.ShapeDtypeStruct((2 * PEER_HEADS, N_KEYS, T), F32),
        ],
        compiler_params=pltpu.CompilerParams(
            dimension_semantics=("arbitrary",), vmem_limit_bytes=VMEM_LIMIT),
        name="merge",
    )(x2d, a, b, gates, wa, wb, wo, nw, wq, keys)


def _rank_pairs():
    n = TOPK + 1
    return [(r, c) for r in range(n) for c in range(n) if (r + 1) * (c + 1) <= n]


def _topk_kernel(st_ref, cnt_ref, p1_ref, rank_ref, p2_ref, rank1_ref):
    tt = st_ref.shape[2]
    n_top = TOPK + 1

    def top_sorted(s):
        vals = []
        cur = s
        rank = jnp.full(s.shape, float(n_top), F32)
        for r in range(n_top):
            mx = jnp.max(cur, axis=0, keepdims=True)
            vals.append(mx)
            hit = cur == mx
            rank = jnp.where(hit, float(r), rank)
            cur = jnp.where(hit, -jnp.inf, cur)
        return vals, rank

    tops = []
    for h in range(PEER_HEADS):
        v1, r1 = top_sorted(st_ref[2 * h])
        v2, r2 = top_sorted(st_ref[2 * h + 1])
        rank1_ref[h] = r1
        rank_ref[h] = r2
        tops.append((v1, v2))
    a = [jnp.concatenate([tops[h][0][r] for h in range(PEER_HEADS)], axis=0) for r in range(n_top)]
    b = [jnp.concatenate([tops[h][1][r] for h in range(PEER_HEADS)], axis=0) for r in range(n_top)]
    cand = [a[r] + b[c] for r, c in _rank_pairs()]
    tau = jnp.full((PEER_HEADS, tt), -jnp.inf, F32)
    nxt = jnp.full((PEER_HEADS, tt), -jnp.inf, F32)
    for xi in cand:
        cnt = jnp.zeros((PEER_HEADS, tt), F32)
        for xj in cand:
            cnt = cnt + jnp.where(xj >= xi, 1.0, 0.0)
        tau = jnp.maximum(tau, jnp.where(cnt >= TOPK, xi, -jnp.inf))
        nxt = jnp.maximum(nxt, jnp.where(cnt >= TOPK + 1, xi, -jnp.inf))
    top = a[0] + b[0]
    z = jnp.zeros((PEER_HEADS, tt), F32)
    for xi in cand:
        z = z + jnp.where(xi >= tau, jnp.exp(xi - top), 0.0)
    inv_z = 1.0 / z
    cut = 0.5 * (tau + nxt)
    n_sel = [jnp.zeros((PEER_HEADS, tt), F32) for _ in range(n_top)]
    for r, c in _rank_pairs():
        n_sel[r] = n_sel[r] + jnp.where(a[r] + b[c] >= cut, 1.0, 0.0)
    for h in range(PEER_HEADS):
        s1 = st_ref[2 * h]
        s2 = st_ref[2 * h + 1]
        r1 = rank1_ref[h]
        cnt = jnp.zeros(r1.shape, F32)
        for r in range(n_top):
            cnt = jnp.where(r1 == float(r), n_sel[r][h:h + 1, :], cnt)
        cnt_ref[h] = cnt
        p1_ref[h] = jnp.exp(s1 - a[0][h:h + 1, :]) * inv_z[h:h + 1, :]
        p2_ref[h] = jnp.exp(s2 - b[0][h:h + 1, :])


def _topk(st, tt):
    n, nk, T = st.shape
    out32 = jax.ShapeDtypeStruct((PEER_HEADS, nk, T), F32)
    spec = pl.BlockSpec((PEER_HEADS, nk, tt), lambda i: (0, 0, i))
    return pl.pallas_call(
        _topk_kernel,
        grid=(T // tt,),
        in_specs=[pl.BlockSpec((n, nk, tt), lambda i: (0, 0, i))],
        out_specs=[spec, spec, spec, spec],
        out_shape=[out32, out32, out32, out32],
        scratch_shapes=[pltpu.VMEM((PEER_HEADS, nk, tt), F32)],
        compiler_params=pltpu.CompilerParams(
            dimension_semantics=("arbitrary",), vmem_limit_bytes=VMEM_LIMIT),
        name="topk",
    )(st)


PEER_KEY_TILE = 64
PEER_TOK_TILE = 128
PEER_PAIR = 2
SUBLANES = 8


def _peer_kernel(xn_ref, u_ref, v_ref, cnt_ref, p1_ref, rank_ref, p2_ref, h1_ref, nw_ref,
                 o_ref, acc_ref, ht_ref, g_ref, cnt_rows, p1_rows):
    j = pl.program_id(1)
    nj = pl.num_programs(1)
    eb, tb = ht_ref.shape
    groups = eb // N_KEYS
    cur = j % 2
    jb = jnp.minimum(j, nj - 2)

    @pl.when(j == 0)
    def _():
        acc_ref[...] = jnp.zeros_like(acc_ref)
        g_ref[...] = jnp.zeros_like(g_ref)

    acc_ref[...] += _tn(g_ref[1 - cur], v_ref[...])

    ht_ref[...] = _nt(u_ref[...], xn_ref[...])
    kt = PEER_KEY_TILE // SUBLANES
    for g0 in range(0, groups, PEER_PAIR):
        for gi in range(PEER_PAIR):
            i1 = jb * groups + g0 + gi
            for h in range(PEER_HEADS):
                r = gi * PEER_HEADS + h
                cnt_rows[r] = jnp.broadcast_to(cnt_ref[h, pl.ds(i1, 1), :], (SUBLANES, tb))
                p1_rows[r] = jnp.broadcast_to(p1_ref[h, pl.ds(i1, 1), :], (SUBLANES, tb))
        for t0 in range(0, tb, PEER_TOK_TILE):
            ts = slice(t0, t0 + PEER_TOK_TILE)
            for k0 in range(0, N_KEYS, PEER_KEY_TILE):
                ks = slice(k0, k0 + PEER_KEY_TILE)
                gate = [None] * PEER_PAIR
                for h in range(PEER_HEADS):
                    rk = rank_ref[h, ks, ts]
                    p2 = p2_ref[h, ks, ts]
                    for gi in range(PEER_PAIR):
                        r = gi * PEER_HEADS + h
                        cn = jnp.concatenate([cnt_rows[r, :, ts]] * kt, axis=0)
                        p1 = jnp.concatenate([p1_rows[r, :, ts]] * kt, axis=0)
                        term = jnp.where(rk < cn, p2, 0.0) * p1
                        gate[gi] = term if gate[gi] is None else gate[gi] + term
                for gi in range(PEER_PAIR):
                    es = slice((g0 + gi) * N_KEYS + k0, (g0 + gi) * N_KEYS + k0 + PEER_KEY_TILE)
                    ht = ht_ref[es, ts]
                    act = 0.5 * ht * (1.0 + lax.erf(ht * (2.0 ** -0.5)))
                    g_ref[cur, es, ts] = (act * gate[gi]).astype(BF16)

    @pl.when(j == nj - 1)
    def _():
        o_ref[...] = _rms(h1_ref[...] + acc_ref[...], nw_ref[...])


def _peer(xn, u, v, cnt, p1, rank2, p2, h1, nw, tb, eb):
    T, D = xn.shape
    n_exp = u.shape[0]
    tok3 = lambda i, j: (0, 0, i)
    nb = n_exp // eb
    return pl.pallas_call(
        _peer_kernel,
        grid=(T // tb, nb + 1),
        in_specs=[
            pl.BlockSpec((tb, D), lambda i, j: (i, 0)),
            pl.BlockSpec((eb, D), lambda i, j: (jnp.minimum(j, nb - 1), 0)),
            pl.BlockSpec((eb, D), lambda i, j: (jnp.maximum(j - 1, 0), 0)),
            pl.BlockSpec((PEER_HEADS, N_KEYS, tb), tok3),
            pl.BlockSpec((PEER_HEADS, N_KEYS, tb), tok3),
            pl.BlockSpec((PEER_HEADS, N_KEYS, tb), tok3),
            pl.BlockSpec((PEER_HEADS, N_KEYS, tb), tok3),
            pl.BlockSpec((tb, D), lambda i, j: (i, 0)),
            pl.BlockSpec((1, D), lambda i, j: (0, 0)),
        ],
        out_specs=pl.BlockSpec((tb, D), lambda i, j: (i, 0)),
        out_shape=jax.ShapeDtypeStruct((T, D), F32),
        scratch_shapes=[pltpu.VMEM((tb, D), F32), pltpu.VMEM((eb, tb), F32), pltpu.VMEM((2, eb, tb), BF16),
                        pltpu.VMEM((PEER_PAIR * PEER_HEADS, SUBLANES, tb), F32),
                        pltpu.VMEM((PEER_PAIR * PEER_HEADS, SUBLANES, tb), F32)],
        compiler_params=pltpu.CompilerParams(
            dimension_semantics=("arbitrary", "arbitrary"), vmem_limit_bytes=VMEM_LIMIT),
        name="peer",
    )(xn, u, v, cnt, p1, rank2, p2, h1, nw)


ROUTE_SLOTS = 64


def _route_pairs():
    return [(r, c) for r in range(TOPK) for c in range(TOPK) if (r + 1) * (c + 1) <= TOPK]


def _routes_kernel(st_ref, ids_ref, w_ref):
    tt = st_ref.shape[2]
    key_id = lax.broadcasted_iota(jnp.int32, (N_KEYS, tt), 0).astype(F32)

    def top_sorted(s):
        vals, idxs = [], []
        cur = s
        for _ in range(TOPK):
            mx = jnp.max(cur, axis=0, keepdims=True)
            hit = cur == mx
            vals.append(mx)
            idxs.append(jnp.max(jnp.where(hit, key_id, -1.0), axis=0, keepdims=True))
            cur = jnp.where(hit, -jnp.inf, cur)
        return vals, idxs

    tops = [(top_sorted(st_ref[2 * h]), top_sorted(st_ref[2 * h + 1])) for h in range(PEER_HEADS)]

    def stack(half, which, r):
        return jnp.concatenate([tops[h][half][which][r] for h in range(PEER_HEADS)], axis=0)

    a = [stack(0, 0, r) for r in range(TOPK)]
    b = [stack(1, 0, r) for r in range(TOPK)]
    ia = [stack(0, 1, r) for r in range(TOPK)]
    ib = [stack(1, 1, r) for r in range(TOPK)]
    pairs = _route_pairs()
    cand = [a[r] + b[c] for r, c in pairs]
    tau = jnp.full((PEER_HEADS, tt), -jnp.inf, F32)
    for xi in cand:
        cnt = jnp.zeros((PEER_HEADS, tt), F32)
        for xj in cand:
            cnt = cnt + jnp.where(xj >= xi, 1.0, 0.0)
        tau = jnp.maximum(tau, jnp.where(cnt >= TOPK, xi, -jnp.inf))
    top = a[0] + b[0]
    e = [jnp.where(xi >= tau, jnp.exp(xi - top), 0.0) for xi in cand]
    z = e[0]
    for ei in e[1:]:
        z = z + ei
    inv_z = 1.0 / z
    for k, (r, c) in enumerate(pairs):
        ids_ref[k] = jnp.clip(ia[r] * float(N_KEYS) + ib[c], 0.0, float(N_KEYS * N_KEYS - 1)).astype(jnp.int32)
        w_ref[k] = e[k] * inv_z
    for k in range(len(pairs), ROUTE_SLOTS):
        ids_ref[k] = jnp.zeros((PEER_HEADS, tt), jnp.int32)
        w_ref[k] = jnp.zeros((PEER_HEADS, tt), F32)


def _routes(st, tt):
    n, nk, T = st.shape
    spec = pl.BlockSpec((ROUTE_SLOTS, PEER_HEADS, tt), lambda i: (0, 0, i))
    return pl.pallas_call(
        _routes_kernel,
        grid=(T // tt,),
        in_specs=[pl.BlockSpec((n, nk, tt), lambda i: (0, 0, i))],
        out_specs=[spec, spec],
        out_shape=[jax.ShapeDtypeStruct((ROUTE_SLOTS, PEER_HEADS, T), jnp.int32),
                   jax.ShapeDtypeStruct((ROUTE_SLOTS, PEER_HEADS, T), F32)],
        compiler_params=pltpu.CompilerParams(
            dimension_semantics=("arbitrary",), vmem_limit_bytes=VMEM_LIMIT),
        name="routes",
    )(st)


SC_LANES = 16
SC_TOKENS_PER_CHUNK = 8


def _gate_matrix(ids, w, n_exp):
    T, E = ids.shape
    info = plsc.get_sparse_core_info()
    workers = info.num_cores * info.num_subcores
    per_worker = T // workers
    ch = SC_TOKENS_PER_CHUNK
    mesh = plsc.VectorSubcoreMesh(core_axis_name="c", subcore_axis_name="s")
    ids_flat = ids.reshape(T * E)
    w_flat = w.reshape(T * E)

    @functools.partial(
        pl.kernel, mesh=mesh,
        out_type=jax.ShapeDtypeStruct((T, n_exp), F32),
        scratch_types=[pltpu.VMEM((ch * E,), jnp.int32), pltpu.VMEM((ch * E,), F32),
                       pltpu.VMEM((n_exp,), F32)],
        compiler_params=pltpu.CompilerParams(needs_layout_passes=False),
        name="gate_matrix",
    )
    def scatter(ids_hbm, w_hbm, out_hbm, ids_v, w_v, row_v):
        wid = lax.axis_index("s") * info.num_cores + lax.axis_index("c")
        base = wid * per_worker
        zeros = jnp.zeros((SC_LANES,), F32)

        @pl.loop(0, n_exp, step=SC_LANES)
        def _(i):
            row_v[pl.ds(i, SC_LANES)] = zeros

        @pl.loop(0, per_worker // ch)
        def _(ci):
            t0 = base + ci * ch
            pltpu.sync_copy(ids_hbm.at[pl.ds(t0 * E, ch * E)], ids_v)
            pltpu.sync_copy(w_hbm.at[pl.ds(t0 * E, ch * E)], w_v)
            for tl in range(ch):
                @pl.loop(0, E, step=SC_LANES)
                def _(e0):
                    sl = pl.ds(tl * E + e0, SC_LANES)
                    plsc.addupdate_scatter(row_v, [ids_v[sl]], w_v[sl])

                pltpu.sync_copy(row_v, out_hbm.at[t0 + tl])

                @pl.loop(0, E, step=SC_LANES)
                def _(e0):
                    sl = pl.ds(tl * E + e0, SC_LANES)
                    plsc.store_scatter(row_v, [ids_v[sl]], zeros)

    return scatter(ids_flat, w_flat)


def _experts_kernel(xn_ref, u_ref, v_ref, gate_ref, h1_ref, nw_ref, o_ref, acc_ref):
    j = pl.program_id(1)
    nj = pl.num_programs(1)

    @pl.when(j == 0)
    def _():
        acc_ref[...] = jnp.zeros_like(acc_ref)

    h = _nt(xn_ref[...], u_ref[...])
    act = 0.5 * h * (1.0 + lax.erf(h * (2.0 ** -0.5)))
    acc_ref[...] += _nn((act * gate_ref[...]).astype(BF16), v_ref[...])

    @pl.when(j == nj - 1)
    def _():
        o_ref[...] = _rms(h1_ref[...] + acc_ref[...], nw_ref[...])


def _experts(xn, u, v, gate, h1, nw, tb, eb):
    T, D = xn.shape
    n_exp = u.shape[0]
    return pl.pallas_call(
        _experts_kernel,
        grid=(T // tb, n_exp // eb),
        in_specs=[
            pl.BlockSpec((tb, D), lambda i, j: (i, 0)),
            pl.BlockSpec((eb, D), lambda i, j: (j, 0)),
            pl.BlockSpec((eb, D), lambda i, j: (j, 0)),
            pl.BlockSpec((tb, eb), lambda i, j: (i, j)),
            pl.BlockSpec((tb, D), lambda i, j: (i, 0)),
            pl.BlockSpec((1, D), lambda i, j: (0, 0)),
        ],
        out_specs=pl.BlockSpec((tb, D), lambda i, j: (i, 0)),
        out_shape=jax.ShapeDtypeStruct((T, D), F32),
        scratch_shapes=[pltpu.VMEM((tb, D), F32)],
        compiler_params=pltpu.CompilerParams(
            dimension_semantics=("arbitrary", "arbitrary"), vmem_limit_bytes=VMEM_LIMIT),
        name="experts",
    )(xn, u, v, gate, h1, nw)


def _block_select_matrix():
    part = jnp.arange(HGRN_SUB * HGRN_DH, dtype=jnp.int32) // HGRN_DH
    col = jnp.arange(HGRN_CHUNK, dtype=jnp.int32) % HGRN_SUB
    return (part[:, None] == col[None, :]).astype(BF16)


def _forward(x, norm_mix_w, w_in, hgrn_lb_logits, hgrn_norm_w, fox_f_bias, w_branch_hgrn,
             w_branch_fox, w_out, norm_ffn_w, peer_w_q, peer_sub_keys, peer_u, peer_v,
             norm_final_w, *, tm_in, tq, tm_merge, tt, tb, eb):
    B, S, D = x.shape
    T = B * S
    x2d = x.reshape(T, D)
    n_h = 4 * HGRN_W
    n_f = 3 * FOX_W
    wi = w_in[0]
    w_all = jnp.concatenate([wi[:, :n_h + n_f], wi[:, n_h + n_f + FOX_HEADS:]], axis=1).astype(BF16)
    wff = wi[:, n_h + n_f:n_h + n_f + FOX_HEADS]
    wff_row = wff.T.astype(BF16)
    fb = fox_f_bias[0].astype(F32)
    fb_row = jnp.broadcast_to(fb.reshape(FOX_HEADS, 1), (FOX_HEADS, LANES))

    hg, fox, gates, crow = _in_proj(
        x2d, norm_mix_w[0].reshape(1, D), w_all, wff_row, fb_row, B, S, tm_in)
    a = _hgrn(hg, hgrn_lb_logits, hgrn_norm_w[0].reshape(1, HGRN_W), _block_select_matrix(), B, S)
    b = _fox(fox, crow, B, S, tq)
    keys = peer_sub_keys[0].reshape(2 * PEER_HEADS, N_KEYS, PEER_HALF).astype(BF16)
    h1, xn2, st = _merge(
        x2d, a, b, gates, w_branch_hgrn[0].astype(BF16), w_branch_fox[0].astype(BF16),
        w_out[0].astype(BF16), norm_ffn_w[0].reshape(1, D), peer_w_q[0].astype(BF16), keys, tm_merge)
    ids3, w3 = _routes(st, tt)
    n_exp = peer_u.shape[1]
    ids = ids3.transpose(2, 1, 0).reshape(T, PEER_HEADS * ROUTE_SLOTS)
    wts = w3.transpose(2, 1, 0).reshape(T, PEER_HEADS * ROUTE_SLOTS)
    gate = _gate_matrix(ids, wts, n_exp)
    out = _experts(xn2, peer_u[0].astype(BF16), peer_v[0].astype(BF16), gate, h1,
                   norm_final_w.reshape(1, D), tb, eb)
    return out.reshape(B, S, D)


def kernel(x, norm_mix_w, w_in, hgrn_lb_logits, hgrn_norm_w, fox_f_bias, w_branch_hgrn, w_branch_fox, w_out, norm_ffn_w, peer_w_q, peer_sub_keys, peer_u, peer_v, norm_final_w):
    return _forward(x, norm_mix_w, w_in, hgrn_lb_logits, hgrn_norm_w, fox_f_bias, w_branch_hgrn,
                    w_branch_fox, w_out, norm_ffn_w, peer_w_q, peer_sub_keys, peer_u, peer_v,
                    norm_final_w, tm_in=256, tq=512, tm_merge=256, tt=256, tb=1024, eb=512)
```

```python
import functools
import math

import jax
import jax.numpy as jnp
from jax import lax
from jax.experimental import pallas as pl
from jax.experimental.pallas import tpu as pltpu
from jax.experimental.pallas import tpu_sc as plsc

F32 = jnp.float32
BF16 = jnp.bfloat16
RMS_EPS = 1e-6
NEG_BIG = -1e30

HGRN_HEADS = 4
HGRN_DH = 128
HGRN_W = HGRN_HEADS * HGRN_DH
FOX_HEADS = 8
FOX_DH = 64
FOX_W = FOX_HEADS * FOX_DH
PEER_HEADS = 8
PEER_HALF = 128
N_KEYS = 128
TOPK = 16
LANES = 128

VMEM_LIMIT = 56 * 1024 * 1024


def _nt(a, b):
    return lax.dot_general(a, b, (((1,), (1,)), ((), ())), preferred_element_type=F32)


def _tn(a, b):
    return lax.dot_general(a, b, (((0,), (0,)), ((), ())), preferred_element_type=F32)


def _nn(a, b):
    return jnp.dot(a, b, preferred_element_type=F32)


def _split_dot(fn, tri, x):
    hi = x.astype(BF16)
    lo = (x - hi.astype(F32)).astype(BF16)
    return fn(tri, hi) + fn(tri, lo)


def _log_sigmoid(x):
    return jnp.minimum(x, 0.0) - jnp.log1p(jnp.exp(-jnp.abs(x)))


def _rms(x, w):
    return x * lax.rsqrt(jnp.mean(x * x, axis=-1, keepdims=True) + RMS_EPS) * w


LOG2E = math.log2(math.e)


def _inproj_kernel(x_ref, nw_ref, w_ref, wffr_ref, fbr_ref,
                   hg_ref, fox_ref, gate_ref, crow_ref, carry_row):
    i = pl.program_id(1)
    tm = x_ref.shape[0]

    @pl.when(i == 0)
    def _():
        carry_row[...] = jnp.zeros_like(carry_row)

    xn = _rms(x_ref[...], nw_ref[...]).astype(BF16)
    proj = _nn(xn, w_ref[...])
    n_h = 4 * HGRN_W
    hg_ref[...] = proj[:, :n_h]
    fox_ref[:, :FOX_W] = (proj[:, n_h:n_h + FOX_W] * (FOX_DH ** -0.5 * LOG2E)).astype(BF16)
    fox_ref[:, FOX_W:] = proj[:, n_h + FOX_W:n_h + 3 * FOX_W].astype(BF16)
    gate_ref[...] = jax.nn.sigmoid(proj[:, n_h + 3 * FOX_W:])

    r = lax.broadcasted_iota(jnp.int32, (tm, tm), 0)
    c = lax.broadcasted_iota(jnp.int32, (tm, tm), 1)
    triu = (r <= c).astype(BF16)
    ls_row = _log_sigmoid(_nt(wffr_ref[...], xn) + fbr_ref[:, 0:1]) * LOG2E
    hi = ls_row.astype(BF16)
    lo = (ls_row - hi.astype(F32)).astype(BF16)
    crow = _nn(hi, triu) + _nn(lo, triu) + carry_row[:, 0:1]
    crow_ref[...] = crow
    carry_row[...] = jnp.broadcast_to(crow[:, tm - 1:tm], carry_row.shape)


def _in_proj(x2d, nw, w_all, wff_row, fb_row, batch, seq, tm):
    T, D = x2d.shape
    nt = seq // tm
    n_all = w_all.shape[1]
    row = lambda b, i: (b * nt + i, 0)
    const = lambda b, i: (0, 0)
    return pl.pallas_call(
        _inproj_kernel,
        grid=(batch, nt),
        in_specs=[
            pl.BlockSpec((tm, D), row),
            pl.BlockSpec((1, D), const),
            pl.BlockSpec((D, n_all), const),
            pl.BlockSpec((FOX_HEADS, D), const),
            pl.BlockSpec((FOX_HEADS, LANES), const),
        ],
        out_specs=[
            pl.BlockSpec((tm, 4 * HGRN_W), row),
            pl.BlockSpec((tm, 3 * FOX_W), row),
            pl.BlockSpec((tm, 2 * D), row),
            pl.BlockSpec((FOX_HEADS, tm), lambda b, i: (0, b * nt + i)),
        ],
        out_shape=[
            jax.ShapeDtypeStruct((T, 4 * HGRN_W), F32),
            jax.ShapeDtypeStruct((T, 3 * FOX_W), BF16),
            jax.ShapeDtypeStruct((T, 2 * D), F32),
            jax.ShapeDtypeStruct((FOX_HEADS, T), F32),
        ],
        scratch_shapes=[pltpu.VMEM((FOX_HEADS, LANES), F32)],
        compiler_params=pltpu.CompilerParams(
            dimension_semantics=("arbitrary", "arbitrary"), vmem_limit_bytes=VMEM_LIMIT),
        name="in_proj",
    )(x2d, nw, w_all, wff_row, fb_row)


HGRN_CHUNK = 128
HGRN_SUB = 16


def _hgrn_kernel(hg_ref, lbl_ref, nw_ref, rsel_ref, a_ref, state_ref):
    ci = pl.program_id(1)
    C = HGRN_CHUNK
    dh = HGRN_DH

    @pl.when(ci == 0)
    def _():
        state_ref[...] = jnp.zeros_like(state_ref)

    lg = lbl_ref[...]
    e = jnp.exp(lg - jnp.max(lg, axis=0, keepdims=True))
    lb_all = e[0:1, :] / jnp.sum(e, axis=0, keepdims=True)

    r = lax.broadcasted_iota(jnp.int32, (C, C), 0)
    c = lax.broadcasted_iota(jnp.int32, (C, C), 1)
    tril = (c <= r).astype(BF16)
    sub = HGRN_SUB
    sh = sub.bit_length() - 1
    diag_mask = ((r >> sh) == (c >> sh)) & ((c & (sub - 1)) <= (r & (sub - 1)))
    levels = []
    m = sub
    while m < C:
        sh = m.bit_length() - 1
        levels.append((m, ((r >> (sh + 1)) == (c >> (sh + 1))) & (((r >> sh) & 1) == 1) & (((c >> sh) & 1) == 0)))
        m *= 2

    for h in range(HGRN_HEADS):
        sl = slice(h * dh, (h + 1) * dh)
        qraw = hg_ref[:, sl]
        q = qraw * jax.nn.sigmoid(qraw)
        lb = lb_all[:, sl]
        f = lb + (1.0 - lb) * jax.nn.sigmoid(hg_ref[:, HGRN_W + h * dh:HGRN_W + (h + 1) * dh])
        logf = jnp.log(f)
        k = 1.0 - f
        v = hg_ref[:, 2 * HGRN_W + h * dh:2 * HGRN_W + (h + 1) * dh]
        g = hg_ref[:, 3 * HGRN_W + h * dh:3 * HGRN_W + (h + 1) * dh]
        v_bf = v.astype(BF16)
        cum = _split_dot(_nn, tril, logf)

        nb = C // sub
        q3 = q.reshape(nb, sub, dh)
        k3 = k.reshape(nb, sub, dh)
        c3 = cum.reshape(nb, sub, dh)
        parts = []
        for s in range(sub):
            kb = jnp.broadcast_to(k3[:, s:s + 1, :], (nb, sub, dh))
            cb = jnp.broadcast_to(c3[:, s:s + 1, :], (nb, sub, dh))
            es = q3 * kb * jnp.exp(jnp.minimum(c3 - cb, 0.0))
            parts.append(es.reshape(C, dh).astype(BF16))
        p_mat = jnp.where(diag_mask, _nn(jnp.concatenate(parts, axis=1), rsel_ref[...]), 0.0)

        for m, mask in levels:
            nbm = C // m
            qm = q.reshape(nbm, m, dh)
            km = k.reshape(nbm, m, dh)
            cm = cum.reshape(nbm, m, dh)
            end = cm[:, m - 1:m, :]
            prev_end = jnp.concatenate([jnp.zeros((1, 1, dh), F32), end[:nbm - 1]], axis=0)
            qd = qm * jnp.exp(jnp.minimum(cm - jnp.broadcast_to(prev_end, (nbm, m, dh)), 0.0))
            kd = km * jnp.exp(jnp.minimum(jnp.broadcast_to(end, (nbm, m, dh)) - cm, 0.0))
            sc = _nt(qd.reshape(C, dh).astype(BF16), kd.reshape(C, dh).astype(BF16))
            p_mat = p_mat + jnp.where(mask, sc, 0.0)

        st = state_ref[h]
        o = _nn(p_mat.astype(BF16), v_bf) + _nt((q * jnp.exp(cum)).astype(BF16), st.astype(BF16))
        last = cum[C - 1:C, :]
        kdec = (k * jnp.exp(last - cum)).astype(BF16)
        state_ref[h] = jnp.exp(last) * st + _tn(v_bf, kdec)

        o = o * lax.rsqrt(jnp.mean(o * o, axis=-1, keepdims=True) + RMS_EPS) * nw_ref[:, sl]
        a_ref[:, sl] = (o * (g * jax.nn.sigmoid(g))).astype(BF16)


def _hgrn(hg, lb_logits, norm_w, rsel, batch, seq):
    T = hg.shape[0]
    C = HGRN_CHUNK
    nc = seq // C
    row = lambda b, i: (b * nc + i, 0)
    const = lambda b, i: (0, 0)
    return pl.pallas_call(
        _hgrn_kernel,
        grid=(batch, nc),
        in_specs=[
            pl.BlockSpec((C, 4 * HGRN_W), row),
            pl.BlockSpec(lb_logits.shape, const),
            pl.BlockSpec((1, HGRN_W), const),
            pl.BlockSpec(rsel.shape, const),
        ],
        out_specs=pl.BlockSpec((C, HGRN_W), row),
        out_shape=jax.ShapeDtypeStruct((T, HGRN_W), BF16),
        scratch_shapes=[pltpu.VMEM((HGRN_HEADS, HGRN_DH, HGRN_DH), F32)],
        compiler_params=pltpu.CompilerParams(
            dimension_semantics=("arbitrary", "arbitrary"), vmem_limit_bytes=VMEM_LIMIT),
        name="hgrn",
    )(hg, lb_logits, norm_w, rsel)


def _fox_kernel(q_ref, k_ref, v_ref, crow_ref, o_ref, m_ref, acc_ref):
    qi = pl.program_id(1)
    tq = q_ref.shape[0]
    tk = tq
    lane = lax.broadcasted_iota(jnp.int32, (1, LANES), 1)
    low = lane < FOX_DH
    sel_lo = jnp.where(low, 1.0, 0.0).astype(BF16)
    sel_hi = jnp.where(low, 0.0, 1.0).astype(BF16)
    ones_lane = (FOX_DH, 0)
    one_hot = tuple(jnp.where(lane == ol, 1.0, 0.0).astype(BF16) for ol in ones_lane)
    rr = lax.broadcasted_iota(jnp.int32, (tq, tk), 0)
    cc = lax.broadcasted_iota(jnp.int32, (tq, tk), 1)
    causal = cc <= rr

    for p in range(FOX_HEADS // 2):
        cols = slice(p * LANES, (p + 1) * LANES)
        q = q_ref[:, cols]
        qs = (q * sel_lo, q * sel_hi)
        m_ref[...] = jnp.full_like(m_ref, NEG_BIG)
        acc_ref[...] = jnp.zeros_like(acc_ref)

        def kv_block(j, masked):
            start = pl.multiple_of(j * tk, tk)
            kb = k_ref[pl.ds(start, tk), cols]
            vb = v_ref[pl.ds(start, tk), cols]
            vs = (vb * sel_lo + one_hot[0], vb * sel_hi + one_hot[1])
            for hh in range(2):
                ck = crow_ref[2 * p + hh:2 * p + hh + 1, pl.ds(start, tk)]
                s = _nt(qs[hh], kb) - ck
                if masked:
                    s = jnp.where(causal, s, NEG_BIG)
                m_prev = m_ref[hh]
                m_next = jnp.maximum(m_prev, jnp.max(s, axis=1, keepdims=True))
                pexp = jnp.exp2(s - jnp.concatenate([m_next] * (tk // LANES), axis=1))
                alpha = jnp.exp2(m_prev - m_next)
                acc_ref[hh] = alpha * acc_ref[hh] + _nn(pexp.astype(BF16), vs[hh])
                m_ref[hh] = m_next

        def body(j, carry):
            kv_block(j, False)
            return carry

        lax.fori_loop(0, qi, body, 0)
        kv_block(qi, True)

        a0 = acc_ref[0]
        a1 = acc_ref[1]
        o0 = a0 / a0[:, ones_lane[0]:ones_lane[0] + 1]
        o1 = a1 / a1[:, ones_lane[1]:ones_lane[1] + 1]
        o_ref[:, cols] = jnp.where(low, o0, o1).astype(BF16)


def _fox(fox, crow, batch, seq, tq):
    T = fox.shape[0]
    nq = seq // tq
    return pl.pallas_call(
        _fox_kernel,
        grid=(batch, nq),
        in_specs=[
            pl.BlockSpec((tq, FOX_W), lambda b, i: (b * nq + i, 0)),
            pl.BlockSpec((seq, FOX_W), lambda b, i: (b, 1)),
            pl.BlockSpec((seq, FOX_W), lambda b, i: (b, 2)),
            pl.BlockSpec((FOX_HEADS, seq), lambda b, i: (0, b)),
        ],
        out_specs=pl.BlockSpec((tq, FOX_W), lambda b, i: (b * nq + i, 0)),
        out_shape=jax.ShapeDtypeStruct((T, FOX_W), BF16),
        scratch_shapes=[pltpu.VMEM((2, tq, LANES), F32), pltpu.VMEM((2, tq, LANES), F32)],
        compiler_params=pltpu.CompilerParams(
            dimension_semantics=("arbitrary", "arbitrary"), vmem_limit_bytes=VMEM_LIMIT),
        name="fox",
    )(fox, fox, fox, crow)


def _merge_kernel(x_ref, a_ref, b_ref, gate_ref, wa_ref, wb_ref, wo_ref, nw_ref, wq_ref, keys_ref,
                  h1_ref, xn_ref, st_ref):
    D = x_ref.shape[1]
    merged = gate_ref[:, :D] * _nn(a_ref[...], wa_ref[...]) + gate_ref[:, D:] * _nn(b_ref[...], wb_ref[...])
    h1 = x_ref[...] + _nn(merged.astype(BF16), wo_ref[...])
    h1_ref[...] = h1
    xn = _rms(h1, nw_ref[...]).astype(BF16)
    xn_ref[...] = xn
    q = _nn(xn, wq_ref[...]).astype(BF16)
    for hp in range(2 * PEER_HEADS):
        st_ref[hp] = _nt(keys_ref[hp], q[:, hp * PEER_HALF:(hp + 1) * PEER_HALF])


def _merge(x2d, a, b, gates, wa, wb, wo, nw, wq, keys, tm):
    T, D = x2d.shape
    row = lambda i: (i, 0)
    const = lambda i: (0, 0)
    return pl.pallas_call(
        _merge_kernel,
        grid=(T // tm,),
        in_specs=[
            pl.BlockSpec((tm, D), row),
            pl.BlockSpec((tm, HGRN_W), row),
            pl.BlockSpec((tm, FOX_W), row),
            pl.BlockSpec((tm, 2 * D), row),
            pl.BlockSpec(wa.shape, const),
            pl.BlockSpec(wb.shape, const),
            pl.BlockSpec(wo.shape, const),
            pl.BlockSpec((1, D), const),
            pl.BlockSpec(wq.shape, const),
            pl.BlockSpec(keys.shape, lambda i: (0, 0, 0)),
        ],
        out_specs=[
            pl.BlockSpec((tm, D), row),
            pl.BlockSpec((tm, D), row),
            pl.BlockSpec((2 * PEER_HEADS, N_KEYS, tm), lambda i: (0, 0, i)),
        ],
        out_shape=[
            jax.ShapeDtypeStruct((T, D), F32),
            jax.ShapeDtypeStruct((T, D), BF16),
            jax.ShapeDtypeStruct((2 * PEER_HEADS, N_KEYS, T), F32),
        ],
        compiler_params=pltpu.CompilerParams(
            dimension_semantics=("arbitrary",), vmem_limit_bytes=VMEM_LIMIT),
        name="merge",
    )(x2d, a, b, gates, wa, wb, wo, nw, wq, keys)


def _rank_pairs():
    n = TOPK + 1
    return [(r, c) for r in range(n) for c in range(n) if (r + 1) * (c + 1) <= n]


def _topk_kernel(st_ref, cnt_ref, p1_ref, rank_ref, p2_ref, rank1_ref):
    tt = st_ref.shape[2]
    n_top = TOPK + 1

    def top_sorted(s):
        vals = []
        cur = s
        rank = jnp.full(s.shape, float(n_top), F32)
        for r in range(n_top):
            mx = jnp.max(cur, axis=0, keepdims=True)
            vals.append(mx)
            hit = cur == mx
            rank = jnp.where(hit, float(r), rank)
            cur = jnp.where(hit, -jnp.inf, cur)
        return vals, rank

    tops = []
    for h in range(PEER_HEADS):
        v1, r1 = top_sorted(st_ref[2 * h])
        v2, r2 = top_sorted(st_ref[2 * h + 1])
        rank1_ref[h] = r1
        rank_ref[h] = r2
        tops.append((v1, v2))
    a = [jnp.concatenate([tops[h][0][r] for h in range(PEER_HEADS)], axis=0) for r in range(n_top)]
    b = [jnp.concatenate([tops[h][1][r] for h in range(PEER_HEADS)], axis=0) for r in range(n_top)]
    cand = [a[r] + b[c] for r, c in _rank_pairs()]
    tau = jnp.full((PEER_HEADS, tt), -jnp.inf, F32)
    nxt = jnp.full((PEER_HEADS, tt), -jnp.inf, F32)
    for xi in cand:
        cnt = jnp.zeros((PEER_HEADS, tt), F32)
        for xj in cand:
            cnt = cnt + jnp.where(xj >= xi, 1.0, 0.0)
        tau = jnp.maximum(tau, jnp.where(cnt >= TOPK, xi, -jnp.inf))
        nxt = jnp.maximum(nxt, jnp.where(cnt >= TOPK + 1, xi, -jnp.inf))
    top = a[0] + b[0]
    z = jnp.zeros((PEER_HEADS, tt), F32)
    for xi in cand:
        z = z + jnp.where(xi >= tau, jnp.exp(xi - top), 0.0)
    inv_z = 1.0 / z
    cut = 0.5 * (tau + nxt)
    n_sel = [jnp.zeros((PEER_HEADS, tt), F32) for _ in range(n_top)]
    for r, c in _rank_pairs():
        n_sel[r] = n_sel[r] + jnp.where(a[r] + b[c] >= cut, 1.0, 0.0)
    for h in range(PEER_HEADS):
        s1 = st_ref[2 * h]
        s2 = st_ref[2 * h + 1]
        r1 = rank1_ref[h]
        cnt = jnp.zeros(r1.shape, F32)
        for r in range(n_top):
            cnt = jnp.where(r1 == float(r), n_sel[r][h:h + 1, :], cnt)
        cnt_ref[h] = cnt
        p1_ref[h] = jnp.exp(s1 - a[0][h:h + 1, :]) * inv_z[h:h + 1, :]
        p2_ref[h] = jnp.exp(s2 - b[0][h:h + 1, :])


def _topk(st, tt):
    n, nk, T = st.shape
    out32 = jax.ShapeDtypeStruct((PEER_HEADS, nk, T), F32)
    spec = pl.BlockSpec((PEER_HEADS, nk, tt), lambda i: (0, 0, i))
    return pl.pallas_call(
        _topk_kernel,
        grid=(T // tt,),
        in_specs=[pl.BlockSpec((n, nk, tt), lambda i: (0, 0, i))],
        out_specs=[spec, spec, spec, spec],
        out_shape=[out32, out32, out32, out32],
        scratch_shapes=[pltpu.VMEM((PEER_HEADS, nk, tt), F32)],
        compiler_params=pltpu.CompilerParams(
            dimension_semantics=("arbitrary",), vmem_limit_bytes=VMEM_LIMIT),
        name="topk",
    )(st)


PEER_KEY_TILE = 64
PEER_TOK_TILE = 128
PEER_PAIR = 2
SUBLANES = 8


def _peer_kernel(xn_ref, u_ref, v_ref, cnt_ref, p1_ref, rank_ref, p2_ref, h1_ref, nw_ref,
                 o_ref, acc_ref, ht_ref, g_ref, cnt_rows, p1_rows):
    j = pl.program_id(1)
    nj = pl.num_programs(1)
    eb, tb = ht_ref.shape
    groups = eb // N_KEYS
    cur = j % 2
    jb = jnp.minimum(j, nj - 2)

    @pl.when(j == 0)
    def _():
        acc_ref[...] = jnp.zeros_like(acc_ref)
        g_ref[...] = jnp.zeros_like(g_ref)

    acc_ref[...] += _tn(g_ref[1 - cur], v_ref[...])

    ht_ref[...] = _nt(u_ref[...], xn_ref[...])
    kt = PEER_KEY_TILE // SUBLANES
    for g0 in range(0, groups, PEER_PAIR):
        for gi in range(PEER_PAIR):
            i1 = jb * groups + g0 + gi
            for h in range(PEER_HEADS):
                r = gi * PEER_HEADS + h
                cnt_rows[r] = jnp.broadcast_to(cnt_ref[h, pl.ds(i1, 1), :], (SUBLANES, tb))
                p1_rows[r] = jnp.broadcast_to(p1_ref[h, pl.ds(i1, 1), :], (SUBLANES, tb))
        for t0 in range(0, tb, PEER_TOK_TILE):
            ts = slice(t0, t0 + PEER_TOK_TILE)
            for k0 in range(0, N_KEYS, PEER_KEY_TILE):
                ks = slice(k0, k0 + PEER_KEY_TILE)
                gate = [None] * PEER_PAIR
                for h in range(PEER_HEADS):
                    rk = rank_ref[h, ks, ts]
                    p2 = p2_ref[h, ks, ts]
                    for gi in range(PEER_PAIR):
                        r = gi * PEER_HEADS + h
                        cn = jnp.concatenate([cnt_rows[r, :, ts]] * kt, axis=0)
                        p1 = jnp.concatenate([p1_rows[r, :, ts]] * kt, axis=0)
                        term = jnp.where(rk < cn, p2, 0.0) * p1
                        gate[gi] = term if gate[gi] is None else gate[gi] + term
                for gi in range(PEER_PAIR):
                    es = slice((g0 + gi) * N_KEYS + k0, (g0 + gi) * N_KEYS + k0 + PEER_KEY_TILE)
                    ht = ht_ref[es, ts]
                    act = 0.5 * ht * (1.0 + lax.erf(ht * (2.0 ** -0.5)))
                    g_ref[cur, es, ts] = (act * gate[gi]).astype(BF16)

    @pl.when(j == nj - 1)
    def _():
        o_ref[...] = _rms(h1_ref[...] + acc_ref[...], nw_ref[...])


def _peer(xn, u, v, cnt, p1, rank2, p2, h1, nw, tb, eb):
    T, D = xn.shape
    n_exp = u.shape[0]
    tok3 = lambda i, j: (0, 0, i)
    nb = n_exp // eb
    return pl.pallas_call(
        _peer_kernel,
        grid=(T // tb, nb + 1),
        in_specs=[
            pl.BlockSpec((tb, D), lambda i, j: (i, 0)),
            pl.BlockSpec((eb, D), lambda i, j: (jnp.minimum(j, nb - 1), 0)),
            pl.BlockSpec((eb, D), lambda i, j: (jnp.maximum(j - 1, 0), 0)),
            pl.BlockSpec((PEER_HEADS, N_KEYS, tb), tok3),
            pl.BlockSpec((PEER_HEADS, N_KEYS, tb), tok3),
            pl.BlockSpec((PEER_HEADS, N_KEYS, tb), tok3),
            pl.BlockSpec((PEER_HEADS, N_KEYS, tb), tok3),
            pl.BlockSpec((tb, D), lambda i, j: (i, 0)),
            pl.BlockSpec((1, D), lambda i, j: (0, 0)),
        ],
        out_specs=pl.BlockSpec((tb, D), lambda i, j: (i, 0)),
        out_shape=jax.ShapeDtypeStruct((T, D), F32),
        scratch_shapes=[pltpu.VMEM((tb, D), F32), pltpu.VMEM((eb, tb), F32), pltpu.VMEM((2, eb, tb), BF16),
                        pltpu.VMEM((PEER_PAIR * PEER_HEADS, SUBLANES, tb), F32),
                        pltpu.VMEM((PEER_PAIR * PEER_HEADS, SUBLANES, tb), F32)],
        compiler_params=pltpu.CompilerParams(
            dimension_semantics=("arbitrary", "arbitrary"), vmem_limit_bytes=VMEM_LIMIT),
        name="peer",
    )(xn, u, v, cnt, p1, rank2, p2, h1, nw)


ROUTE_SLOTS = 64


def _route_pairs():
    return [(r, c) for r in range(TOPK) for c in range(TOPK) if (r + 1) * (c + 1) <= TOPK]


def _routes_kernel(st_ref, ids_ref, w_ref):
    tt = st_ref.shape[2]
    key_id = lax.broadcasted_iota(jnp.int32, (N_KEYS, tt), 0).astype(F32)

    def top_sorted(s):
        vals, idxs = [], []
        cur = s
        for _ in range(TOPK):
            mx = jnp.max(cur, axis=0, keepdims=True)
            hit = cur == mx
            vals.append(mx)
            idxs.append(jnp.max(jnp.where(hit, key_id, -1.0), axis=0, keepdims=True))
            cur = jnp.where(hit, -jnp.inf, cur)
        return vals, idxs

    tops = [(top_sorted(st_ref[2 * h]), top_sorted(st_ref[2 * h + 1])) for h in range(PEER_HEADS)]

    def stack(half, which, r):
        return jnp.concatenate([tops[h][half][which][r] for h in range(PEER_HEADS)], axis=0)

    a = [stack(0, 0, r) for r in range(TOPK)]
    b = [stack(1, 0, r) for r in range(TOPK)]
    ia = [stack(0, 1, r) for r in range(TOPK)]
    ib = [stack(1, 1, r) for r in range(TOPK)]
    pairs = _route_pairs()
    cand = [a[r] + b[c] for r, c in pairs]
    tau = jnp.full((PEER_HEADS, tt), -jnp.inf, F32)
    for xi in cand:
        cnt = jnp.zeros((PEER_HEADS, tt), F32)
        for xj in cand:
            cnt = cnt + jnp.where(xj >= xi, 1.0, 0.0)
        tau = jnp.maximum(tau, jnp.where(cnt >= TOPK, xi, -jnp.inf))
    top = a[0] + b[0]
    e = [jnp.where(xi >= tau, jnp.exp(xi - top), 0.0) for xi in cand]
    z = e[0]
    for ei in e[1:]:
        z = z + ei
    inv_z = 1.0 / z
    for k, (r, c) in enumerate(pairs):
        ids_ref[k] = jnp.clip(ia[r] * float(N_KEYS) + ib[c], 0.0, float(N_KEYS * N_KEYS - 1)).astype(jnp.int32)
        w_ref[k] = e[k] * inv_z
    for k in range(len(pairs), ROUTE_SLOTS):
        ids_ref[k] = jnp.zeros((PEER_HEADS, tt), jnp.int32)
        w_ref[k] = jnp.zeros((PEER_HEADS, tt), F32)


def _routes(st, tt):
    n, nk, T = st.shape
    spec = pl.BlockSpec((ROUTE_SLOTS, PEER_HEADS, tt), lambda i: (0, 0, i))
    return pl.pallas_call(
        _routes_kernel,
        grid=(T // tt,),
        in_specs=[pl.BlockSpec((n, nk, tt), lambda i: (0, 0, i))],
        out_specs=[spec, spec],
        out_shape=[jax.ShapeDtypeStruct((ROUTE_SLOTS, PEER_HEADS, T), jnp.int32),
                   jax.ShapeDtypeStruct((ROUTE_SLOTS, PEER_HEADS, T), F32)],
        compiler_params=pltpu.CompilerParams(
            dimension_semantics=("arbitrary",), vmem_limit_bytes=VMEM_LIMIT),
        name="routes",
    )(st)


SC_LANES = 16
SC_TOKENS_PER_CHUNK = 8


def _gate_matrix(ids, w, n_exp):
    T, E = ids.shape
    info = plsc.get_sparse_core_info()
    workers = info.num_cores * info.num_subcores
    per_worker = T // workers
    ch = SC_TOKENS_PER_CHUNK
    mesh = plsc.VectorSubcoreMesh(core_axis_name="c", subcore_axis_name="s")
    ids_flat = ids.reshape(T * E)
    w_flat = w.reshape(T * E)

    @functools.partial(
        pl.kernel, mesh=mesh,
        out_type=jax.ShapeDtypeStruct((T, n_exp), F32),
        scratch_types=[pltpu.VMEM((ch * E,), jnp.int32), pltpu.VMEM((ch * E,), F32),
                       pltpu.VMEM((n_exp,), F32)],
        compiler_params=pltpu.CompilerParams(needs_layout_passes=False),
        name="gate_matrix",
    )
    def scatter(ids_hbm, w_hbm, out_hbm, ids_v, w_v, row_v):
        wid = lax.axis_index("s") * info.num_cores + lax.axis_index("c")
        base = wid * per_worker
        zeros = jnp.zeros((SC_LANES,), F32)

        @pl.loop(0, n_exp, step=SC_LANES)
        def _(i):
            row_v[pl.ds(i, SC_LANES)] = zeros

        @pl.loop(0, per_worker // ch)
        def _(ci):
            t0 = base + ci * ch
            pltpu.sync_copy(ids_hbm.at[pl.ds(t0 * E, ch * E)], ids_v)
            pltpu.sync_copy(w_hbm.at[pl.ds(t0 * E, ch * E)], w_v)
            for tl in range(ch):
                @pl.loop(0, E, step=SC_LANES)
                def _(e0):
                    sl = pl.ds(tl * E + e0, SC_LANES)
                    plsc.addupdate_scatter(row_v, [ids_v[sl]], w_v[sl])

                pltpu.sync_copy(row_v, out_hbm.at[t0 + tl])

                @pl.loop(0, E, step=SC_LANES)
                def _(e0):
                    sl = pl.ds(tl * E + e0, SC_LANES)
                    plsc.store_scatter(row_v, [ids_v[sl]], zeros)

    return scatter(ids_flat, w_flat)


def _experts_kernel(xn_ref, u_ref, v_ref, gate_ref, h1_ref, nw_ref, o_ref, acc_ref):
    j = pl.program_id(1)
    nj = pl.num_programs(1)

    @pl.when(j == 0)
    def _():
        acc_ref[...] = jnp.zeros_like(acc_ref)

    h = _nt(xn_ref[...], u_ref[...])
    act = 0.5 * h * (1.0 + lax.erf(h * (2.0 ** -0.5)))
    acc_ref[...] += _nn((act * gate_ref[...]).astype(BF16), v_ref[...])

    @pl.when(j == nj - 1)
    def _():
        o_ref[...] = _rms(h1_ref[...] + acc_ref[...], nw_ref[...])


def _experts(xn, u, v, gate, h1, nw, tb, eb):
    T, D = xn.shape
    n_exp = u.shape[0]
    return pl.pallas_call(
        _experts_kernel,
        grid=(T // tb, n_exp // eb),
        in_specs=[
            pl.BlockSpec((tb, D), lambda i, j: (i, 0)),
            pl.BlockSpec((eb, D), lambda i, j: (j, 0)),
            pl.BlockSpec((eb, D), lambda i, j: (j, 0)),
            pl.BlockSpec((tb, eb), lambda i, j: (i, j)),
            pl.BlockSpec((tb, D), lambda i, j: (i, 0)),
            pl.BlockSpec((1, D), lambda i, j: (0, 0)),
        ],
        out_specs=pl.BlockSpec((tb, D), lambda i, j: (i, 0)),
        out_shape=jax.ShapeDtypeStruct((T, D), F32),
        scratch_shapes=[pltpu.VMEM((tb, D), F32)],
        compiler_params=pltpu.CompilerParams(
            dimension_semantics=("arbitrary", "arbitrary"), vmem_limit_bytes=VMEM_LIMIT),
        name="experts",
    )(xn, u, v, gate, h1, nw)


def _block_select_matrix():
    part = jnp.arange(HGRN_SUB * HGRN_DH, dtype=jnp.int32) // HGRN_DH
    col = jnp.arange(HGRN_CHUNK, dtype=jnp.int32) % HGRN_SUB
    return (part[:, None] == col[None, :]).astype(BF16)


def _forward(x, norm_mix_w, w_in, hgrn_lb_logits, hgrn_norm_w, fox_f_bias, w_branch_hgrn,
             w_branch_fox, w_out, norm_ffn_w, peer_w_q, peer_sub_keys, peer_u, peer_v,
             norm_final_w, *, tm_in, tq, tm_merge, tt, tb, eb):
    B, S, D = x.shape
    T = B * S
    x2d = x.reshape(T, D)
    n_h = 4 * HGRN_W
    n_f = 3 * FOX_W
    wi = w_in[0]
    w_all = jnp.concatenate([wi[:, :n_h + n_f], wi[:, n_h + n_f + FOX_HEADS:]], axis=1).astype(BF16)
    wff = wi[:, n_h + n_f:n_h + n_f + FOX_HEADS]
    wff_row = wff.T.astype(BF16)
    fb = fox_f_bias[0].astype(F32)
    fb_row = jnp.broadcast_to(fb.reshape(FOX_HEADS, 1), (FOX_HEADS, LANES))

    hg, fox, gates, crow = _in_proj(
        x2d, norm_mix_w[0].reshape(1, D), w_all, wff_row, fb_row, B, S, tm_in)
    a = _hgrn(hg, hgrn_lb_logits, hgrn_norm_w[0].reshape(1, HGRN_W), _block_select_matrix(), B, S)
    b = _fox(fox, crow, B, S, tq)
    keys = peer_sub_keys[0].reshape(2 * PEER_HEADS, N_KEYS, PEER_HALF).astype(BF16)
    h1, xn2, st = _merge(
        x2d, a, b, gates, w_branch_hgrn[0].astype(BF16), w_branch_fox[0].astype(BF16),
        w_out[0].astype(BF16), norm_ffn_w[0].reshape(1, D), peer_w_q[0].astype(BF16), keys, tm_merge)
    ids3, w3 = _routes(st, tt)
    n_exp = peer_u.shape[1]
    ids = ids3.transpose(2, 1, 0).reshape(T, PEER_HEADS * ROUTE_SLOTS)
    wts = w3.transpose(2, 1, 0).reshape(T, PEER_HEADS * ROUTE_SLOTS)
    gate = _gate_matrix(ids, wts, n_exp)
    out = _experts(xn2, peer_u[0].astype(BF16), peer_v[0].astype(BF16), gate, h1,
                   norm_final_w.reshape(1, D), tb, eb)
    return out.reshape(B, S, D)


def kernel(x, norm_mix_w, w_in, hgrn_lb_logits, hgrn_norm_w, fox_f_bias, w_branch_hgrn, w_branch_fox, w_out, norm_ffn_w, peer_w_q, peer_sub_keys, peer_u, peer_v, norm_final_w):
    return _forward(x, norm_mix_w, w_in, hgrn_lb_logits, hgrn_norm_w, fox_f_bias, w_branch_hgrn,
                    w_branch_fox, w_out, norm_ffn_w, peer_w_q, peer_sub_keys, peer_u, peer_v,
                    norm_final_w, tm_in=256, tq=512, tm_merge=256, tt=256, tb=1024, eb=1024)
```

```python
import functools
import math

import jax
import jax.numpy as jnp
from jax import lax
from jax.experimental import pallas as pl
from jax.experimental.pallas import tpu as pltpu
from jax.experimental.pallas import tpu_sc as plsc

F32 = jnp.float32
BF16 = jnp.bfloat16
RMS_EPS = 1e-6
NEG_BIG = -1e30

HGRN_HEADS = 4
HGRN_DH = 128
HGRN_W = HGRN_HEADS * HGRN_DH
FOX_HEADS = 8
FOX_DH = 64
FOX_W = FOX_HEADS * FOX_DH
PEER_HEADS = 8
PEER_HALF = 128
N_KEYS = 128
TOPK = 16
LANES = 128

VMEM_LIMIT = 56 * 1024 * 1024


def _nt(a, b):
    return lax.dot_general(a, b, (((1,), (1,)), ((), ())), preferred_element_type=F32)


def _tn(a, b):
    return lax.dot_general(a, b, (((0,), (0,)), ((), ())), preferred_element_type=F32)


def _nn(a, b):
    return jnp.dot(a, b, preferred_element_type=F32)


def _split_dot(fn, tri, x):
    hi = x.astype(BF16)
    lo = (x - hi.astype(F32)).astype(BF16)
    return fn(tri, hi) + fn(tri, lo)


def _log_sigmoid(x):
    return jnp.minimum(x, 0.0) - jnp.log1p(jnp.exp(-jnp.abs(x)))


def _rms(x, w):
    return x * lax.rsqrt(jnp.mean(x * x, axis=-1, keepdims=True) + RMS_EPS) * w


LOG2E = math.log2(math.e)


def _inproj_kernel(x_ref, nw_ref, w_ref, wffr_ref, fbr_ref,
                   hg_ref, fox_ref, gate_ref, crow_ref, carry_row):
    i = pl.program_id(1)
    tm = x_ref.shape[0]

    @pl.when(i == 0)
    def _():
        carry_row[...] = jnp.zeros_like(carry_row)

    xn = _rms(x_ref[...], nw_ref[...]).astype(BF16)
    proj = _nn(xn, w_ref[...])
    n_h = 4 * HGRN_W
    hg_ref[...] = proj[:, :n_h]
    fox_ref[:, :FOX_W] = (proj[:, n_h:n_h + FOX_W] * (FOX_DH ** -0.5 * LOG2E)).astype(BF16)
    fox_ref[:, FOX_W:] = proj[:, n_h + FOX_W:n_h + 3 * FOX_W].astype(BF16)
    gate_ref[...] = jax.nn.sigmoid(proj[:, n_h + 3 * FOX_W:])

    r = lax.broadcasted_iota(jnp.int32, (tm, tm), 0)
    c = lax.broadcasted_iota(jnp.int32, (tm, tm), 1)
    triu = (r <= c).astype(BF16)
    ls_row = _log_sigmoid(_nt(wffr_ref[...], xn) + fbr_ref[:, 0:1]) * LOG2E
    hi = ls_row.astype(BF16)
    lo = (ls_row - hi.astype(F32)).astype(BF16)
    crow = _nn(hi, triu) + _nn(lo, triu) + carry_row[:, 0:1]
    crow_ref[...] = crow
    carry_row[...] = jnp.broadcast_to(crow[:, tm - 1:tm], carry_row.shape)


def _in_proj(x2d, nw, w_all, wff_row, fb_row, batch, seq, tm):
    T, D = x2d.shape
    nt = seq // tm
    n_all = w_all.shape[1]
    row = lambda b, i: (b * nt + i, 0)
    const = lambda b, i: (0, 0)
    return pl.pallas_call(
        _inproj_kernel,
        grid=(batch, nt),
        in_specs=[
            pl.BlockSpec((tm, D), row),
            pl.BlockSpec((1, D), const),
            pl.BlockSpec((D, n_all), const),
            pl.BlockSpec((FOX_HEADS, D), const),
            pl.BlockSpec((FOX_HEADS, LANES), const),
        ],
        out_specs=[
            pl.BlockSpec((tm, 4 * HGRN_W), row),
            pl.BlockSpec((tm, 3 * FOX_W), row),
            pl.BlockSpec((tm, 2 * D), row),
            pl.BlockSpec((FOX_HEADS, tm), lambda b, i: (0, b * nt + i)),
        ],
        out_shape=[
            jax.ShapeDtypeStruct((T, 4 * HGRN_W), F32),
            jax.ShapeDtypeStruct((T, 3 * FOX_W), BF16),
            jax.ShapeDtypeStruct((T, 2 * D), F32),
            jax.ShapeDtypeStruct((FOX_HEADS, T), F32),
        ],
        scratch_shapes=[pltpu.VMEM((FOX_HEADS, LANES), F32)],
        compiler_params=pltpu.CompilerParams(
            dimension_semantics=("arbitrary", "arbitrary"), vmem_limit_bytes=VMEM_LIMIT),
        name="in_proj",
    )(x2d, nw, w_all, wff_row, fb_row)


HGRN_CHUNK = 128
HGRN_SUB = 16


def _hgrn_kernel(hg_ref, lbl_ref, nw_ref, rsel_ref, a_ref, state_ref):
    ci = pl.program_id(1)
    C = HGRN_CHUNK
    dh = HGRN_DH

    @pl.when(ci == 0)
    def _():
        state_ref[...] = jnp.zeros_like(state_ref)

    lg = lbl_ref[...]
    e = jnp.exp(lg - jnp.max(lg, axis=0, keepdims=True))
    lb_all = e[0:1, :] / jnp.sum(e, axis=0, keepdims=True)

    r = lax.broadcasted_iota(jnp.int32, (C, C), 0)
    c = lax.broadcasted_iota(jnp.int32, (C, C), 1)
    tril = (c <= r).astype(BF16)
    sub = HGRN_SUB
    sh = sub.bit_length() - 1
    diag_mask = ((r >> sh) == (c >> sh)) & ((c & (sub - 1)) <= (r & (sub - 1)))
    levels = []
    m = sub
    while m < C:
        sh = m.bit_length() - 1
        levels.append((m, ((r >> (sh + 1)) == (c >> (sh + 1))) & (((r >> sh) & 1) == 1) & (((c >> sh) & 1) == 0)))
        m *= 2

    for h in range(HGRN_HEADS):
        sl = slice(h * dh, (h + 1) * dh)
        qraw = hg_ref[:, sl]
        q = qraw * jax.nn.sigmoid(qraw)
        lb = lb_all[:, sl]
        f = lb + (1.0 - lb) * jax.nn.sigmoid(hg_ref[:, HGRN_W + h * dh:HGRN_W + (h + 1) * dh])
        logf = jnp.log(f)
        k = 1.0 - f
        v = hg_ref[:, 2 * HGRN_W + h * dh:2 * HGRN_W + (h + 1) * dh]
        g = hg_ref[:, 3 * HGRN_W + h * dh:3 * HGRN_W + (h + 1) * dh]
        v_bf = v.astype(BF16)
        cum = _split_dot(_nn, tril, logf)

        nb = C // sub
        q3 = q.reshape(nb, sub, dh)
        k3 = k.reshape(nb, sub, dh)
        c3 = cum.reshape(nb, sub, dh)
        parts = []
        for s in range(sub):
            kb = jnp.broadcast_to(k3[:, s:s + 1, :], (nb, sub, dh))
            cb = jnp.broadcast_to(c3[:, s:s + 1, :], (nb, sub, dh))
            es = q3 * kb * jnp.exp(jnp.minimum(c3 - cb, 0.0))
            parts.append(es.reshape(C, dh).astype(BF16))
        p_mat = jnp.where(diag_mask, _nn(jnp.concatenate(parts, axis=1), rsel_ref[...]), 0.0)

        for m, mask in levels:
            nbm = C // m
            qm = q.reshape(nbm, m, dh)
            km = k.reshape(nbm, m, dh)
            cm = cum.reshape(nbm, m, dh)
            end = cm[:, m - 1:m, :]
            prev_end = jnp.concatenate([jnp.zeros((1, 1, dh), F32), end[:nbm - 1]], axis=0)
            qd = qm * jnp.exp(jnp.minimum(cm - jnp.broadcast_to(prev_end, (nbm, m, dh)), 0.0))
            kd = km * jnp.exp(jnp.minimum(jnp.broadcast_to(end, (nbm, m, dh)) - cm, 0.0))
            sc = _nt(qd.reshape(C, dh).astype(BF16), kd.reshape(C, dh).astype(BF16))
            p_mat = p_mat + jnp.where(mask, sc, 0.0)

        st = state_ref[h]
        o = _nn(p_mat.astype(BF16), v_bf) + _nt((q * jnp.exp(cum)).astype(BF16), st.astype(BF16))
        last = cum[C - 1:C, :]
        kdec = (k * jnp.exp(last - cum)).astype(BF16)
        state_ref[h] = jnp.exp(last) * st + _tn(v_bf, kdec)

        o = o * lax.rsqrt(jnp.mean(o * o, axis=-1, keepdims=True) + RMS_EPS) * nw_ref[:, sl]
        a_ref[:, sl] = (o * (g * jax.nn.sigmoid(g))).astype(BF16)


def _hgrn(hg, lb_logits, norm_w, rsel, batch, seq):
    T = hg.shape[0]
    C = HGRN_CHUNK
    nc = seq // C
    row = lambda b, i: (b * nc + i, 0)
    const = lambda b, i: (0, 0)
    return pl.pallas_call(
        _hgrn_kernel,
        grid=(batch, nc),
        in_specs=[
            pl.BlockSpec((C, 4 * HGRN_W), row),
            pl.BlockSpec(lb_logits.shape, const),
            pl.BlockSpec((1, HGRN_W), const),
            pl.BlockSpec(rsel.shape, const),
        ],
        out_specs=pl.BlockSpec((C, HGRN_W), row),
        out_shape=jax.ShapeDtypeStruct((T, HGRN_W), BF16),
        scratch_shapes=[pltpu.VMEM((HGRN_HEADS, HGRN_DH, HGRN_DH), F32)],
        compiler_params=pltpu.CompilerParams(
            dimension_semantics=("arbitrary", "arbitrary"), vmem_limit_bytes=VMEM_LIMIT),
        name="hgrn",
    )(hg, lb_logits, norm_w, rsel)


def _fox_kernel(q_ref, k_ref, v_ref, crow_ref, o_ref, m_ref, acc_ref):
    qi = pl.program_id(1)
    tq = q_ref.shape[0]
    tk = tq
    lane = lax.broadcasted_iota(jnp.int32, (1, LANES), 1)
    low = lane < FOX_DH
    sel_lo = jnp.where(low, 1.0, 0.0).astype(BF16)
    sel_hi = jnp.where(low, 0.0, 1.0).astype(BF16)
    ones_lane = (FOX_DH, 0)
    one_hot = tuple(jnp.where(lane == ol, 1.0, 0.0).astype(BF16) for ol in ones_lane)
    rr = lax.broadcasted_iota(jnp.int32, (tq, tk), 0)
    cc = lax.broadcasted_iota(jnp.int32, (tq, tk), 1)
    causal = cc <= rr

    for p in range(FOX_HEADS // 2):
        cols = slice(p * LANES, (p + 1) * LANES)
        q = q_ref[:, cols]
        qs = (q * sel_lo, q * sel_hi)
        m_ref[...] = jnp.full_like(m_ref, NEG_BIG)
        acc_ref[...] = jnp.zeros_like(acc_ref)

        def kv_block(j, masked):
            start = pl.multiple_of(j * tk, tk)
            kb = k_ref[pl.ds(start, tk), cols]
            vb = v_ref[pl.ds(start, tk), cols]
            vs = (vb * sel_lo + one_hot[0], vb * sel_hi + one_hot[1])
            for hh in range(2):
                ck = crow_ref[2 * p + hh:2 * p + hh + 1, pl.ds(start, tk)]
                s = _nt(qs[hh], kb) - ck
                if masked:
                    s = jnp.where(causal, s, NEG_BIG)
                m_prev = m_ref[hh]
                m_next = jnp.maximum(m_prev, jnp.max(s, axis=1, keepdims=True))
                pexp = jnp.exp2(s - jnp.concatenate([m_next] * (tk // LANES), axis=1))
                alpha = jnp.exp2(m_prev - m_next)
                acc_ref[hh] = alpha * acc_ref[hh] + _nn(pexp.astype(BF16), vs[hh])
                m_ref[hh] = m_next

        def body(j, carry):
            kv_block(j, False)
            return carry

        lax.fori_loop(0, qi, body, 0)
        kv_block(qi, True)

        a0 = acc_ref[0]
        a1 = acc_ref[1]
        o0 = a0 / a0[:, ones_lane[0]:ones_lane[0] + 1]
        o1 = a1 / a1[:, ones_lane[1]:ones_lane[1] + 1]
        o_ref[:, cols] = jnp.where(low, o0, o1).astype(BF16)


def _fox(fox, crow, batch, seq, tq):
    T = fox.shape[0]
    nq = seq // tq
    return pl.pallas_call(
        _fox_kernel,
        grid=(batch, nq),
        in_specs=[
            pl.BlockSpec((tq, FOX_W), lambda b, i: (b * nq + i, 0)),
            pl.BlockSpec((seq, FOX_W), lambda b, i: (b, 1)),
            pl.BlockSpec((seq, FOX_W), lambda b, i: (b, 2)),
            pl.BlockSpec((FOX_HEADS, seq), lambda b, i: (0, b)),
        ],
        out_specs=pl.BlockSpec((tq, FOX_W), lambda b, i: (b * nq + i, 0)),
        out_shape=jax.ShapeDtypeStruct((T, FOX_W), BF16),
        scratch_shapes=[pltpu.VMEM((2, tq, LANES), F32), pltpu.VMEM((2, tq, LANES), F32)],
        compiler_params=pltpu.CompilerParams(
            dimension_semantics=("arbitrary", "arbitrary"), vmem_limit_bytes=VMEM_LIMIT),
        name="fox",
    )(fox, fox, fox, crow)


def _merge_kernel(x_ref, a_ref, b_ref, gate_ref, wa_ref, wb_ref, wo_ref, nw_ref, wq_ref, keys_ref,
                  h1_ref, xn_ref, st_ref):
    D = x_ref.shape[1]
    merged = gate_ref[:, :D] * _nn(a_ref[...], wa_ref[...]) + gate_ref[:, D:] * _nn(b_ref[...], wb_ref[...])
    h1 = x_ref[...] + _nn(merged.astype(BF16), wo_ref[...])
    h1_ref[...] = h1
    xn = _rms(h1, nw_ref[...]).astype(BF16)
    xn_ref[...] = xn
    q = _nn(xn, wq_ref[...]).astype(BF16)
    for hp in range(2 * PEER_HEADS):
        st_ref[hp] = _nt(keys_ref[hp], q[:, hp * PEER_HALF:(hp + 1) * PEER_HALF])


def _merge(x2d, a, b, gates, wa, wb, wo, nw, wq, keys, tm):
    T, D = x2d.shape
    row = lambda i: (i, 0)
    const = lambda i: (0, 0)
    return pl.pallas_call(
        _merge_kernel,
        grid=(T // tm,),
        in_specs=[
            pl.BlockSpec((tm, D), row),
            pl.BlockSpec((tm, HGRN_W), row),
            pl.BlockSpec((tm, FOX_W), row),
            pl.BlockSpec((tm, 2 * D), row),
            pl.BlockSpec(wa.shape, const),
            pl.BlockSpec(wb.shape, const),
            pl.BlockSpec(wo.shape, const),
            pl.BlockSpec((1, D), const),
            pl.BlockSpec(wq.shape, const),
            pl.BlockSpec(keys.shape, lambda i: (0, 0, 0)),
        ],
        out_specs=[
            pl.BlockSpec((tm, D), row),
            pl.BlockSpec((tm, D), row),
            pl.BlockSpec((2 * PEER_HEADS, N_KEYS, tm), lambda i: (0, 0, i)),
        ],
        out_shape=[
            jax.ShapeDtypeStruct((T, D), F32),
            jax.ShapeDtypeStruct((T, D), BF16),
            jax.ShapeDtypeStruct((2 * PEER_HEADS, N_KEYS, T), F32),
        ],
        compiler_params=pltpu.CompilerParams(
            dimension_semantics=("arbitrary",), vmem_limit_bytes=VMEM_LIMIT),
        name="merge",
    )(x2d, a, b, gates, wa, wb, wo, nw, wq, keys)


def _rank_pairs():
    n = TOPK + 1
    return [(r, c) for r in range(n) for c in range(n) if (r + 1) * (c + 1) <= n]


def _topk_kernel(st_ref, cnt_ref, p1_ref, rank_ref, p2_ref, rank1_ref):
    tt = st_ref.shape[2]
    n_top = TOPK + 1

    def top_sorted(s):
        vals = []
        cur = s
        rank = jnp.full(s.shape, float(n_top), F32)
        for r in range(n_top):
            mx = jnp.max(cur, axis=0, keepdims=True)
            vals.append(mx)
            hit = cur == mx
            rank = jnp.where(hit, float(r), rank)
            cur = jnp.where(hit, -jnp.inf, cur)
        return vals, rank

    tops = []
    for h in range(PEER_HEADS):
        v1, r1 = top_sorted(st_ref[2 * h])
        v2, r2 = top_sorted(st_ref[2 * h + 1])
        rank1_ref[h] = r1
        rank_ref[h] = r2
        tops.append((v1, v2))
    a = [jnp.concatenate([tops[h][0][r] for h in range(PEER_HEADS)], axis=0) for r in range(n_top)]
    b = [jnp.concatenate([tops[h][1][r] for h in range(PEER_HEADS)], axis=0) for r in range(n_top)]
    cand = [a[r] + b[c] for r, c in _rank_pairs()]
    tau = jnp.full((PEER_HEADS, tt), -jnp.inf, F32)
    nxt = jnp.full((PEER_HEADS, tt), -jnp.inf, F32)
    for xi in cand:
        cnt = jnp.zeros((PEER_HEADS, tt), F32)
        for xj in cand:
            cnt = cnt + jnp.where(xj >= xi, 1.0, 0.0)
        tau = jnp.maximum(tau, jnp.where(cnt >= TOPK, xi, -jnp.inf))
        nxt = jnp.maximum(nxt, jnp.where(cnt >= TOPK + 1, xi, -jnp.inf))
    top = a[0] + b[0]
    z = jnp.zeros((PEER_HEADS, tt), F32)
    for xi in cand:
        z = z + jnp.where(xi >= tau, jnp.exp(xi - top), 0.0)
    inv_z = 1.0 / z
    cut = 0.5 * (tau + nxt)
    n_sel = [jnp.zeros((PEER_HEADS, tt), F32) for _ in range(n_top)]
    for r, c in _rank_pairs():
        n_sel[r] = n_sel[r] + jnp.where(a[r] + b[c] >= cut, 1.0, 0.0)
    for h in range(PEER_HEADS):
        s1 = st_ref[2 * h]
        s2 = st_ref[2 * h + 1]
        r1 = rank1_ref[h]
        cnt = jnp.zeros(r1.shape, F32)
        for r in range(n_top):
            cnt = jnp.where(r1 == float(r), n_sel[r][h:h + 1, :], cnt)
        cnt_ref[h] = cnt
        p1_ref[h] = jnp.exp(s1 - a[0][h:h + 1, :]) * inv_z[h:h + 1, :]
        p2_ref[h] = jnp.exp(s2 - b[0][h:h + 1, :])


def _topk(st, tt):
    n, nk, T = st.shape
    out32 = jax.ShapeDtypeStruct((PEER_HEADS, nk, T), F32)
    spec = pl.BlockSpec((PEER_HEADS, nk, tt), lambda i: (0, 0, i))
    return pl.pallas_call(
        _topk_kernel,
        grid=(T // tt,),
        in_specs=[pl.BlockSpec((n, nk, tt), lambda i: (0, 0, i))],
        out_specs=[spec, spec, spec, spec],
        out_shape=[out32, out32, out32, out32],
        scratch_shapes=[pltpu.VMEM((PEER_HEADS, nk, tt), F32)],
        compiler_params=pltpu.CompilerParams(
            dimension_semantics=("arbitrary",), vmem_limit_bytes=VMEM_LIMIT),
        name="topk",
    )(st)


PEER_KEY_TILE = 64
PEER_TOK_TILE = 128
PEER_PAIR = 2
SUBLANES = 8


def _peer_kernel(xn_ref, u_ref, v_ref, cnt_ref, p1_ref, rank_ref, p2_ref, h1_ref, nw_ref,
                 o_ref, acc_ref, ht_ref, g_ref, cnt_rows, p1_rows):
    j = pl.program_id(1)
    nj = pl.num_programs(1)
    eb, tb = ht_ref.shape
    groups = eb // N_KEYS
    cur = j % 2
    jb = jnp.minimum(j, nj - 2)

    @pl.when(j == 0)
    def _():
        acc_ref[...] = jnp.zeros_like(acc_ref)
        g_ref[...] = jnp.zeros_like(g_ref)

    acc_ref[...] += _tn(g_ref[1 - cur], v_ref[...])

    ht_ref[...] = _nt(u_ref[...], xn_ref[...])
    kt = PEER_KEY_TILE // SUBLANES
    for g0 in range(0, groups, PEER_PAIR):
        for gi in range(PEER_PAIR):
            i1 = jb * groups + g0 + gi
            for h in range(PEER_HEADS):
                r = gi * PEER_HEADS + h
                cnt_rows[r] = jnp.broadcast_to(cnt_ref[h, pl.ds(i1, 1), :], (SUBLANES, tb))
                p1_rows[r] = jnp.broadcast_to(p1_ref[h, pl.ds(i1, 1), :], (SUBLANES, tb))
        for t0 in range(0, tb, PEER_TOK_TILE):
            ts = slice(t0, t0 + PEER_TOK_TILE)
            for k0 in range(0, N_KEYS, PEER_KEY_TILE):
                ks = slice(k0, k0 + PEER_KEY_TILE)
                gate = [None] * PEER_PAIR
                for h in range(PEER_HEADS):
                    rk = rank_ref[h, ks, ts]
                    p2 = p2_ref[h, ks, ts]
                    for gi in range(PEER_PAIR):
                        r = gi * PEER_HEADS + h
                        cn = jnp.concatenate([cnt_rows[r, :, ts]] * kt, axis=0)
                        p1 = jnp.concatenate([p1_rows[r, :, ts]] * kt, axis=0)
                        term = jnp.where(rk < cn, p2, 0.0) * p1
                        gate[gi] = term if gate[gi] is None else gate[gi] + term
                for gi in range(PEER_PAIR):
                    es = slice((g0 + gi) * N_KEYS + k0, (g0 + gi) * N_KEYS + k0 + PEER_KEY_TILE)
                    ht = ht_ref[es, ts]
                    act = 0.5 * ht * (1.0 + lax.erf(ht * (2.0 ** -0.5)))
                    g_ref[cur, es, ts] = (act * gate[gi]).astype(BF16)

    @pl.when(j == nj - 1)
    def _():
        o_ref[...] = _rms(h1_ref[...] + acc_ref[...], nw_ref[...])


def _peer(xn, u, v, cnt, p1, rank2, p2, h1, nw, tb, eb):
    T, D = xn.shape
    n_exp = u.shape[0]
    tok3 = lambda i, j: (0, 0, i)
    nb = n_exp // eb
    return pl.pallas_call(
        _peer_kernel,
        grid=(T // tb, nb + 1),
        in_specs=[
            pl.BlockSpec((tb, D), lambda i, j: (i, 0)),
            pl.BlockSpec((eb, D), lambda i, j: (jnp.minimum(j, nb - 1), 0)),
            pl.BlockSpec((eb, D), lambda i, j: (jnp.maximum(j - 1, 0), 0)),
            pl.BlockSpec((PEER_HEADS, N_KEYS, tb), tok3),
            pl.BlockSpec((PEER_HEADS, N_KEYS, tb), tok3),
            pl.BlockSpec((PEER_HEADS, N_KEYS, tb), tok3),
            pl.BlockSpec((PEER_HEADS, N_KEYS, tb), tok3),
            pl.BlockSpec((tb, D), lambda i, j: (i, 0)),
            pl.BlockSpec((1, D), lambda i, j: (0, 0)),
        ],
        out_specs=pl.BlockSpec((tb, D), lambda i, j: (i, 0)),
        out_shape=jax.ShapeDtypeStruct((T, D), F32),
        scratch_shapes=[pltpu.VMEM((tb, D), F32), pltpu.VMEM((eb, tb), F32), pltpu.VMEM((2, eb, tb), BF16),
                        pltpu.VMEM((PEER_PAIR * PEER_HEADS, SUBLANES, tb), F32),
                        pltpu.VMEM((PEER_PAIR * PEER_HEADS, SUBLANES, tb), F32)],
        compiler_params=pltpu.CompilerParams(
            dimension_semantics=("arbitrary", "arbitrary"), vmem_limit_bytes=VMEM_LIMIT),
        name="peer",
    )(xn, u, v, cnt, p1, rank2, p2, h1, nw)


ROUTE_SLOTS = 64


def _route_pairs():
    return [(r, c) for r in range(TOPK) for c in range(TOPK) if (r + 1) * (c + 1) <= TOPK]


def _routes_kernel(st_ref, ids_ref, w_ref):
    tt = st_ref.shape[2]
    key_id = lax.broadcasted_iota(jnp.int32, (N_KEYS, tt), 0).astype(F32)

    def top_sorted(s):
        vals, idxs = [], []
        cur = s
        for _ in range(TOPK):
            mx = jnp.max(cur, axis=0, keepdims=True)
            hit = cur == mx
            vals.append(mx)
            idxs.append(jnp.max(jnp.where(hit, key_id, -1.0), axis=0, keepdims=True))
            cur = jnp.where(hit, -jnp.inf, cur)
        return vals, idxs

    tops = [(top_sorted(st_ref[2 * h]), top_sorted(st_ref[2 * h + 1])) for h in range(PEER_HEADS)]

    def stack(half, which, r):
        return jnp.concatenate([tops[h][half][which][r] for h in range(PEER_HEADS)], axis=0)

    a = [stack(0, 0, r) for r in range(TOPK)]
    b = [stack(1, 0, r) for r in range(TOPK)]
    ia = [stack(0, 1, r) for r in range(TOPK)]
    ib = [stack(1, 1, r) for r in range(TOPK)]
    pairs = _route_pairs()
    cand = [a[r] + b[c] for r, c in pairs]
    tau = jnp.full((PEER_HEADS, tt), -jnp.inf, F32)
    for xi in cand:
        cnt = jnp.zeros((PEER_HEADS, tt), F32)
        for xj in cand:
            cnt = cnt + jnp.where(xj >= xi, 1.0, 0.0)
        tau = jnp.maximum(tau, jnp.where(cnt >= TOPK, xi, -jnp.inf))
    top = a[0] + b[0]
    e = [jnp.where(xi >= tau, jnp.exp(xi - top), 0.0) for xi in cand]
    z = e[0]
    for ei in e[1:]:
        z = z + ei
    inv_z = 1.0 / z
    for k, (r, c) in enumerate(pairs):
        ids_ref[k] = jnp.clip(ia[r] * float(N_KEYS) + ib[c], 0.0, float(N_KEYS * N_KEYS - 1)).astype(jnp.int32)
        w_ref[k] = e[k] * inv_z
    for k in range(len(pairs), ROUTE_SLOTS):
        ids_ref[k] = jnp.zeros((PEER_HEADS, tt), jnp.int32)
        w_ref[k] = jnp.zeros((PEER_HEADS, tt), F32)


def _routes(st, tt, t0, tc):
    n, nk, _ = st.shape
    off = t0 // tt
    spec = pl.BlockSpec((ROUTE_SLOTS, PEER_HEADS, tt), lambda i: (0, 0, i))
    return pl.pallas_call(
        _routes_kernel,
        grid=(tc // tt,),
        in_specs=[pl.BlockSpec((n, nk, tt), lambda i: (0, 0, i + off))],
        out_specs=[spec, spec],
        out_shape=[jax.ShapeDtypeStruct((ROUTE_SLOTS, PEER_HEADS, tc), jnp.int32),
                   jax.ShapeDtypeStruct((ROUTE_SLOTS, PEER_HEADS, tc), F32)],
        compiler_params=pltpu.CompilerParams(
            dimension_semantics=("arbitrary",), vmem_limit_bytes=VMEM_LIMIT),
        name="routes",
    )(st)


SC_LANES = 16
SC_TOKENS_PER_CHUNK = 8
PEER_CHUNKS = 8


def _gate_matrix(ids, w, n_exp):
    T, E = ids.shape
    info = plsc.get_sparse_core_info()
    workers = info.num_cores * info.num_subcores
    per_worker = T // workers
    ch = SC_TOKENS_PER_CHUNK
    mesh = plsc.VectorSubcoreMesh(core_axis_name="c", subcore_axis_name="s")
    ids_flat = ids.reshape(T * E)
    w_flat = w.reshape(T * E)

    @functools.partial(
        pl.kernel, mesh=mesh,
        out_type=jax.ShapeDtypeStruct((T, n_exp), F32),
        scratch_types=[pltpu.VMEM((ch * E,), jnp.int32), pltpu.VMEM((ch * E,), F32),
                       pltpu.VMEM((n_exp,), F32)],
        compiler_params=pltpu.CompilerParams(needs_layout_passes=False),
        name="gate_matrix",
    )
    def scatter(ids_hbm, w_hbm, out_hbm, ids_v, w_v, row_v):
        wid = lax.axis_index("s") * info.num_cores + lax.axis_index("c")
        base = wid * per_worker
        zeros = jnp.zeros((SC_LANES,), F32)

        @pl.loop(0, n_exp, step=SC_LANES)
        def _(i):
            row_v[pl.ds(i, SC_LANES)] = zeros

        @pl.loop(0, per_worker // ch)
        def _(ci):
            t0 = base + ci * ch
            pltpu.sync_copy(ids_hbm.at[pl.ds(t0 * E, ch * E)], ids_v)
            pltpu.sync_copy(w_hbm.at[pl.ds(t0 * E, ch * E)], w_v)
            for tl in range(ch):
                @pl.loop(0, E, step=SC_LANES)
                def _(e0):
                    sl = pl.ds(tl * E + e0, SC_LANES)
                    plsc.addupdate_scatter(row_v, [ids_v[sl]], w_v[sl])

                pltpu.sync_copy(row_v, out_hbm.at[t0 + tl])

                @pl.loop(0, E, step=SC_LANES)
                def _(e0):
                    sl = pl.ds(tl * E + e0, SC_LANES)
                    plsc.store_scatter(row_v, [ids_v[sl]], zeros)

    return scatter(ids_flat, w_flat)


def _experts_kernel(xn_ref, u_ref, v_ref, gate_ref, h1_ref, nw_ref, o_ref, acc_ref):
    j = pl.program_id(1)
    nj = pl.num_programs(1)

    @pl.when(j == 0)
    def _():
        acc_ref[...] = jnp.zeros_like(acc_ref)

    h = _nt(xn_ref[...], u_ref[...])
    act = 0.5 * h * (1.0 + lax.erf(h * (2.0 ** -0.5)))
    acc_ref[...] += _nn((act * gate_ref[...]).astype(BF16), v_ref[...])

    @pl.when(j == nj - 1)
    def _():
        o_ref[...] = _rms(h1_ref[...] + acc_ref[...], nw_ref[...])


def _experts(xn, u, v, gate, h1, nw, tb, eb, t0):
    tc = gate.shape[0]
    D = xn.shape[1]
    n_exp = u.shape[0]
    off = t0 // tb
    return pl.pallas_call(
        _experts_kernel,
        grid=(tc // tb, n_exp // eb),
        in_specs=[
            pl.BlockSpec((tb, D), lambda i, j: (i + off, 0)),
            pl.BlockSpec((eb, D), lambda i, j: (j, 0)),
            pl.BlockSpec((eb, D), lambda i, j: (j, 0)),
            pl.BlockSpec((tb, eb), lambda i, j: (i, j)),
            pl.BlockSpec((tb, D), lambda i, j: (i + off, 0)),
            pl.BlockSpec((1, D), lambda i, j: (0, 0)),
        ],
        out_specs=pl.BlockSpec((tb, D), lambda i, j: (i, 0)),
        out_shape=jax.ShapeDtypeStruct((tc, D), F32),
        scratch_shapes=[pltpu.VMEM((tb, D), F32)],
        compiler_params=pltpu.CompilerParams(
            dimension_semantics=("arbitrary", "arbitrary"), vmem_limit_bytes=VMEM_LIMIT),
        name="experts",
    )(xn, u, v, gate, h1, nw)


def _block_select_matrix():
    part = jnp.arange(HGRN_SUB * HGRN_DH, dtype=jnp.int32) // HGRN_DH
    col = jnp.arange(HGRN_CHUNK, dtype=jnp.int32) % HGRN_SUB
    return (part[:, None] == col[None, :]).astype(BF16)


def _forward(x, norm_mix_w, w_in, hgrn_lb_logits, hgrn_norm_w, fox_f_bias, w_branch_hgrn,
             w_branch_fox, w_out, norm_ffn_w, peer_w_q, peer_sub_keys, peer_u, peer_v,
             norm_final_w, *, tm_in, tq, tm_merge, tt, tb, eb):
    B, S, D = x.shape
    T = B * S
    x2d = x.reshape(T, D)
    n_h = 4 * HGRN_W
    n_f = 3 * FOX_W
    wi = w_in[0]
    w_all = jnp.concatenate([wi[:, :n_h + n_f], wi[:, n_h + n_f + FOX_HEADS:]], axis=1).astype(BF16)
    wff = wi[:, n_h + n_f:n_h + n_f + FOX_HEADS]
    wff_row = wff.T.astype(BF16)
    fb = fox_f_bias[0].astype(F32)
    fb_row = jnp.broadcast_to(fb.reshape(FOX_HEADS, 1), (FOX_HEADS, LANES))

    hg, fox, gates, crow = _in_proj(
        x2d, norm_mix_w[0].reshape(1, D), w_all, wff_row, fb_row, B, S, tm_in)
    a = _hgrn(hg, hgrn_lb_logits, hgrn_norm_w[0].reshape(1, HGRN_W), _block_select_matrix(), B, S)
    b = _fox(fox, crow, B, S, tq)
    keys = peer_sub_keys[0].reshape(2 * PEER_HEADS, N_KEYS, PEER_HALF).astype(BF16)
    h1, xn2, st = _merge(
        x2d, a, b, gates, w_branch_hgrn[0].astype(BF16), w_branch_fox[0].astype(BF16),
        w_out[0].astype(BF16), norm_ffn_w[0].reshape(1, D), peer_w_q[0].astype(BF16), keys, tm_merge)
    n_exp = peer_u.shape[1]
    u_bf = peer_u[0].astype(BF16)
    v_bf = peer_v[0].astype(BF16)
    tc = T // PEER_CHUNKS
    outs = []
    for c in range(PEER_CHUNKS):
        ids3, w3 = _routes(st, tt, c * tc, tc)
        ids = ids3.transpose(2, 1, 0).reshape(tc, PEER_HEADS * ROUTE_SLOTS)
        wts = w3.transpose(2, 1, 0).reshape(tc, PEER_HEADS * ROUTE_SLOTS)
        gate = _gate_matrix(ids, wts, n_exp)
        outs.append(_experts(xn2, u_bf, v_bf, gate, h1, norm_final_w.reshape(1, D), tb, eb, c * tc))
    return jnp.concatenate(outs, axis=0).reshape(B, S, D)


def kernel(x, norm_mix_w, w_in, hgrn_lb_logits, hgrn_norm_w, fox_f_bias, w_branch_hgrn, w_branch_fox, w_out, norm_ffn_w, peer_w_q, peer_sub_keys, peer_u, peer_v, norm_final_w):
    return _forward(x, norm_mix_w, w_in, hgrn_lb_logits, hgrn_norm_w, fox_f_bias, w_branch_hgrn,
                    w_branch_fox, w_out, norm_ffn_w, peer_w_q, peer_sub_keys, peer_u, peer_v,
                    norm_final_w, tm_in=256, tq=512, tm_merge=256, tt=256, tb=1024, eb=1024)
```

```python
import functools
import math

import jax
import jax.numpy as jnp
from jax import lax
from jax.experimental import pallas as pl
from jax.experimental.pallas import tpu as pltpu
from jax.experimental.pallas import tpu_sc as plsc

F32 = jnp.float32
BF16 = jnp.bfloat16
RMS_EPS = 1e-6
NEG_BIG = -1e30

HGRN_HEADS = 4
HGRN_DH = 128
HGRN_W = HGRN_HEADS * HGRN_DH
FOX_HEADS = 8
FOX_DH = 64
FOX_W = FOX_HEADS * FOX_DH
PEER_HEADS = 8
PEER_HALF = 128
N_KEYS = 128
TOPK = 16
LANES = 128

VMEM_LIMIT = 56 * 1024 * 1024


def _nt(a, b):
    return lax.dot_general(a, b, (((1,), (1,)), ((), ())), preferred_element_type=F32)


def _tn(a, b):
    return lax.dot_general(a, b, (((0,), (0,)), ((), ())), preferred_element_type=F32)


def _nn(a, b):
    return jnp.dot(a, b, preferred_element_type=F32)


def _split_dot(fn, tri, x):
    hi = x.astype(BF16)
    lo = (x - hi.astype(F32)).astype(BF16)
    return fn(tri, hi) + fn(tri, lo)


def _log_sigmoid(x):
    return jnp.minimum(x, 0.0) - jnp.log1p(jnp.exp(-jnp.abs(x)))


def _rms(x, w):
    return x * lax.rsqrt(jnp.mean(x * x, axis=-1, keepdims=True) + RMS_EPS) * w


LOG2E = math.log2(math.e)


def _inproj_kernel(x_ref, nw_ref, w_ref, wffr_ref, fbr_ref,
                   hg_ref, fox_ref, gate_ref, crow_ref, carry_row):
    i = pl.program_id(1)
    tm = x_ref.shape[0]

    @pl.when(i == 0)
    def _():
        carry_row[...] = jnp.zeros_like(carry_row)

    xn = _rms(x_ref[...], nw_ref[...]).astype(BF16)
    proj = _nn(xn, w_ref[...])
    n_h = 4 * HGRN_W
    hg_ref[...] = proj[:, :n_h]
    fox_ref[:, :FOX_W] = (proj[:, n_h:n_h + FOX_W] * (FOX_DH ** -0.5 * LOG2E)).astype(BF16)
    fox_ref[:, FOX_W:] = proj[:, n_h + FOX_W:n_h + 3 * FOX_W].astype(BF16)
    gate_ref[...] = jax.nn.sigmoid(proj[:, n_h + 3 * FOX_W:])

    r = lax.broadcasted_iota(jnp.int32, (tm, tm), 0)
    c = lax.broadcasted_iota(jnp.int32, (tm, tm), 1)
    triu = (r <= c).astype(BF16)
    ls_row = _log_sigmoid(_nt(wffr_ref[...], xn) + fbr_ref[:, 0:1]) * LOG2E
    hi = ls_row.astype(BF16)
    lo = (ls_row - hi.astype(F32)).astype(BF16)
    crow = _nn(hi, triu) + _nn(lo, triu) + carry_row[:, 0:1]
    crow_ref[...] = crow
    carry_row[...] = jnp.broadcast_to(crow[:, tm - 1:tm], carry_row.shape)


def _in_proj(x2d, nw, w_all, wff_row, fb_row, batch, seq, tm):
    T, D = x2d.shape
    nt = seq // tm
    n_all = w_all.shape[1]
    row = lambda b, i: (b * nt + i, 0)
    const = lambda b, i: (0, 0)
    return pl.pallas_call(
        _inproj_kernel,
        grid=(batch, nt),
        in_specs=[
            pl.BlockSpec((tm, D), row),
            pl.BlockSpec((1, D), const),
            pl.BlockSpec((D, n_all), const),
            pl.BlockSpec((FOX_HEADS, D), const),
            pl.BlockSpec((FOX_HEADS, LANES), const),
        ],
        out_specs=[
            pl.BlockSpec((tm, 4 * HGRN_W), row),
            pl.BlockSpec((tm, 3 * FOX_W), row),
            pl.BlockSpec((tm, 2 * D), row),
            pl.BlockSpec((FOX_HEADS, tm), lambda b, i: (0, b * nt + i)),
        ],
        out_shape=[
            jax.ShapeDtypeStruct((T, 4 * HGRN_W), F32),
            jax.ShapeDtypeStruct((T, 3 * FOX_W), BF16),
            jax.ShapeDtypeStruct((T, 2 * D), F32),
            jax.ShapeDtypeStruct((FOX_HEADS, T), F32),
        ],
        scratch_shapes=[pltpu.VMEM((FOX_HEADS, LANES), F32)],
        compiler_params=pltpu.CompilerParams(
            dimension_semantics=("arbitrary", "arbitrary"), vmem_limit_bytes=VMEM_LIMIT),
        name="in_proj",
    )(x2d, nw, w_all, wff_row, fb_row)


HGRN_CHUNK = 128
HGRN_SUB = 16


def _hgrn_kernel(hg_ref, lbl_ref, nw_ref, rsel_ref, a_ref, state_ref):
    ci = pl.program_id(1)
    C = HGRN_CHUNK
    dh = HGRN_DH

    @pl.when(ci == 0)
    def _():
        state_ref[...] = jnp.zeros_like(state_ref)

    lg = lbl_ref[...]
    e = jnp.exp(lg - jnp.max(lg, axis=0, keepdims=True))
    lb_all = e[0:1, :] / jnp.sum(e, axis=0, keepdims=True)

    r = lax.broadcasted_iota(jnp.int32, (C, C), 0)
    c = lax.broadcasted_iota(jnp.int32, (C, C), 1)
    tril = (c <= r).astype(BF16)
    sub = HGRN_SUB
    sh = sub.bit_length() - 1
    diag_mask = ((r >> sh) == (c >> sh)) & ((c & (sub - 1)) <= (r & (sub - 1)))
    levels = []
    m = sub
    while m < C:
        sh = m.bit_length() - 1
        levels.append((m, ((r >> (sh + 1)) == (c >> (sh + 1))) & (((r >> sh) & 1) == 1) & (((c >> sh) & 1) == 0)))
        m *= 2

    for h in range(HGRN_HEADS):
        sl = slice(h * dh, (h + 1) * dh)
        qraw = hg_ref[:, sl]
        q = qraw * jax.nn.sigmoid(qraw)
        lb = lb_all[:, sl]
        f = lb + (1.0 - lb) * jax.nn.sigmoid(hg_ref[:, HGRN_W + h * dh:HGRN_W + (h + 1) * dh])
        logf = jnp.log(f)
        k = 1.0 - f
        v = hg_ref[:, 2 * HGRN_W + h * dh:2 * HGRN_W + (h + 1) * dh]
        g = hg_ref[:, 3 * HGRN_W + h * dh:3 * HGRN_W + (h + 1) * dh]
        v_bf = v.astype(BF16)
        cum = _split_dot(_nn, tril, logf)

        nb = C // sub
        q3 = q.reshape(nb, sub, dh)
        k3 = k.reshape(nb, sub, dh)
        c3 = cum.reshape(nb, sub, dh)
        parts = []
        for s in range(sub):
            kb = jnp.broadcast_to(k3[:, s:s + 1, :], (nb, sub, dh))
            cb = jnp.broadcast_to(c3[:, s:s + 1, :], (nb, sub, dh))
            es = q3 * kb * jnp.exp(jnp.minimum(c3 - cb, 0.0))
            parts.append(es.reshape(C, dh).astype(BF16))
        p_mat = jnp.where(diag_mask, _nn(jnp.concatenate(parts, axis=1), rsel_ref[...]), 0.0)

        for m, mask in levels:
            nbm = C // m
            qm = q.reshape(nbm, m, dh)
            km = k.reshape(nbm, m, dh)
            cm = cum.reshape(nbm, m, dh)
            end = cm[:, m - 1:m, :]
            prev_end = jnp.concatenate([jnp.zeros((1, 1, dh), F32), end[:nbm - 1]], axis=0)
            qd = qm * jnp.exp(jnp.minimum(cm - jnp.broadcast_to(prev_end, (nbm, m, dh)), 0.0))
            kd = km * jnp.exp(jnp.minimum(jnp.broadcast_to(end, (nbm, m, dh)) - cm, 0.0))
            sc = _nt(qd.reshape(C, dh).astype(BF16), kd.reshape(C, dh).astype(BF16))
            p_mat = p_mat + jnp.where(mask, sc, 0.0)

        st = state_ref[h]
        o = _nn(p_mat.astype(BF16), v_bf) + _nt((q * jnp.exp(cum)).astype(BF16), st.astype(BF16))
        last = cum[C - 1:C, :]
        kdec = (k * jnp.exp(last - cum)).astype(BF16)
        state_ref[h] = jnp.exp(last) * st + _tn(v_bf, kdec)

        o = o * lax.rsqrt(jnp.mean(o * o, axis=-1, keepdims=True) + RMS_EPS) * nw_ref[:, sl]
        a_ref[:, sl] = (o * (g * jax.nn.sigmoid(g))).astype(BF16)


def _hgrn(hg, lb_logits, norm_w, rsel, batch, seq):
    T = hg.shape[0]
    C = HGRN_CHUNK
    nc = seq // C
    row = lambda b, i: (b * nc + i, 0)
    const = lambda b, i: (0, 0)
    return pl.pallas_call(
        _hgrn_kernel,
        grid=(batch, nc),
        in_specs=[
            pl.BlockSpec((C, 4 * HGRN_W), row),
            pl.BlockSpec(lb_logits.shape, const),
            pl.BlockSpec((1, HGRN_W), const),
            pl.BlockSpec(rsel.shape, const),
        ],
        out_specs=pl.BlockSpec((C, HGRN_W), row),
        out_shape=jax.ShapeDtypeStruct((T, HGRN_W), BF16),
        scratch_shapes=[pltpu.VMEM((HGRN_HEADS, HGRN_DH, HGRN_DH), F32)],
        compiler_params=pltpu.CompilerParams(
            dimension_semantics=("arbitrary", "arbitrary"), vmem_limit_bytes=VMEM_LIMIT),
        name="hgrn",
    )(hg, lb_logits, norm_w, rsel)


def _fox_kernel(q_ref, k_ref, v_ref, crow_ref, o_ref, m_ref, acc_ref):
    qi = pl.program_id(1)
    tq = q_ref.shape[0]
    tk = tq
    lane = lax.broadcasted_iota(jnp.int32, (1, LANES), 1)
    low = lane < FOX_DH
    sel_lo = jnp.where(low, 1.0, 0.0).astype(BF16)
    sel_hi = jnp.where(low, 0.0, 1.0).astype(BF16)
    ones_lane = (FOX_DH, 0)
    one_hot = tuple(jnp.where(lane == ol, 1.0, 0.0).astype(BF16) for ol in ones_lane)
    rr = lax.broadcasted_iota(jnp.int32, (tq, tk), 0)
    cc = lax.broadcasted_iota(jnp.int32, (tq, tk), 1)
    causal = cc <= rr

    for p in range(FOX_HEADS // 2):
        cols = slice(p * LANES, (p + 1) * LANES)
        q = q_ref[:, cols]
        qs = (q * sel_lo, q * sel_hi)
        m_ref[...] = jnp.full_like(m_ref, NEG_BIG)
        acc_ref[...] = jnp.zeros_like(acc_ref)

        def kv_block(j, masked):
            start = pl.multiple_of(j * tk, tk)
            kb = k_ref[pl.ds(start, tk), cols]
            vb = v_ref[pl.ds(start, tk), cols]
            vs = (vb * sel_lo + one_hot[0], vb * sel_hi + one_hot[1])
            for hh in range(2):
                ck = crow_ref[2 * p + hh:2 * p + hh + 1, pl.ds(start, tk)]
                s = _nt(qs[hh], kb) - ck
                if masked:
                    s = jnp.where(causal, s, NEG_BIG)
                m_prev = m_ref[hh]
                m_next = jnp.maximum(m_prev, jnp.max(s, axis=1, keepdims=True))
                pexp = jnp.exp2(s - jnp.concatenate([m_next] * (tk // LANES), axis=1))
                alpha = jnp.exp2(m_prev - m_next)
                acc_ref[hh] = alpha * acc_ref[hh] + _nn(pexp.astype(BF16), vs[hh])
                m_ref[hh] = m_next

        def body(j, carry):
            kv_block(j, False)
            return carry

        lax.fori_loop(0, qi, body, 0)
        kv_block(qi, True)

        a0 = acc_ref[0]
        a1 = acc_ref[1]
        o0 = a0 / a0[:, ones_lane[0]:ones_lane[0] + 1]
        o1 = a1 / a1[:, ones_lane[1]:ones_lane[1] + 1]
        o_ref[:, cols] = jnp.where(low, o0, o1).astype(BF16)


def _fox(fox, crow, batch, seq, tq):
    T = fox.shape[0]
    nq = seq // tq
    return pl.pallas_call(
        _fox_kernel,
        grid=(batch, nq),
        in_specs=[
            pl.BlockSpec((tq, FOX_W), lambda b, i: (b * nq + i, 0)),
            pl.BlockSpec((seq, FOX_W), lambda b, i: (b, 1)),
            pl.BlockSpec((seq, FOX_W), lambda b, i: (b, 2)),
            pl.BlockSpec((FOX_HEADS, seq), lambda b, i: (0, b)),
        ],
        out_specs=pl.BlockSpec((tq, FOX_W), lambda b, i: (b * nq + i, 0)),
        out_shape=jax.ShapeDtypeStruct((T, FOX_W), BF16),
        scratch_shapes=[pltpu.VMEM((2, tq, LANES), F32), pltpu.VMEM((2, tq, LANES), F32)],
        compiler_params=pltpu.CompilerParams(
            dimension_semantics=("arbitrary", "arbitrary"), vmem_limit_bytes=VMEM_LIMIT),
        name="fox",
    )(fox, fox, fox, crow)


def _merge_kernel(x_ref, a_ref, b_ref, gate_ref, wa_ref, wb_ref, wo_ref, nw_ref, wq_ref, keys_ref,
                  h1_ref, xn_ref, st_ref):
    D = x_ref.shape[1]
    merged = gate_ref[:, :D] * _nn(a_ref[...], wa_ref[...]) + gate_ref[:, D:] * _nn(b_ref[...], wb_ref[...])
    h1 = x_ref[...] + _nn(merged.astype(BF16), wo_ref[...])
    h1_ref[...] = h1
    xn = _rms(h1, nw_ref[...]).astype(BF16)
    xn_ref[...] = xn
    q = _nn(xn, wq_ref[...]).astype(BF16)
    for hp in range(2 * PEER_HEADS):
        st_ref[hp] = _nt(keys_ref[hp], q[:, hp * PEER_HALF:(hp + 1) * PEER_HALF])


def _merge(x2d, a, b, gates, wa, wb, wo, nw, wq, keys, tm):
    T, D = x2d.shape
    row = lambda i: (i, 0)
    const = lambda i: (0, 0)
    return pl.pallas_call(
        _merge_kernel,
        grid=(T // tm,),
        in_specs=[
            pl.BlockSpec((tm, D), row),
            pl.BlockSpec((tm, HGRN_W), row),
            pl.BlockSpec((tm, FOX_W), row),
            pl.BlockSpec((tm, 2 * D), row),
            pl.BlockSpec(wa.shape, const),
            pl.BlockSpec(wb.shape, const),
            pl.BlockSpec(wo.shape, const),
            pl.BlockSpec((1, D), const),
            pl.BlockSpec(wq.shape, const),
            pl.BlockSpec(keys.shape, lambda i: (0, 0, 0)),
        ],
        out_specs=[
            pl.BlockSpec((tm, D), row),
            pl.BlockSpec((tm, D), row),
            pl.BlockSpec((2 * PEER_HEADS, N_KEYS, tm), lambda i: (0, 0, i)),
        ],
        out_shape=[
            jax.ShapeDtypeStruct((T, D), F32),
            jax.ShapeDtypeStruct((T, D), BF16),
            jax.ShapeDtypeStruct((2 * PEER_HEADS, N_KEYS, T), F32),
        ],
        compiler_params=pltpu.CompilerParams(
            dimension_semantics=("arbitrary",), vmem_limit_bytes=VMEM_LIMIT),
        name="merge",
    )(x2d, a, b, gates, wa, wb, wo, nw, wq, keys)


def _rank_pairs():
    n = TOPK + 1
    return [(r, c) for r in range(n) for c in range(n) if (r + 1) * (c + 1) <= n]


def _topk_kernel(st_ref, cnt_ref, p1_ref, rank_ref, p2_ref, rank1_ref):
    tt = st_ref.shape[2]
    n_top = TOPK + 1

    def top_sorted(s):
        vals = []
        cur = s
        rank = jnp.full(s.shape, float(n_top), F32)
        for r in range(n_top):
            mx = jnp.max(cur, axis=0, keepdims=True)
            vals.append(mx)
            hit = cur == mx
            rank = jnp.where(hit, float(r), rank)
            cur = jnp.where(hit, -jnp.inf, cur)
        return vals, rank

    tops = []
    for h in range(PEER_HEADS):
        v1, r1 = top_sorted(st_ref[2 * h])
        v2, r2 = top_sorted(st_ref[2 * h + 1])
        rank1_ref[h] = r1
        rank_ref[h] = r2
        tops.append((v1, v2))
    a = [jnp.concatenate([tops[h][0][r] for h in range(PEER_HEADS)], axis=0) for r in range(n_top)]
    b = [jnp.concatenate([tops[h][1][r] for h in range(PEER_HEADS)], axis=0) for r in range(n_top)]
    cand = [a[r] + b[c] for r, c in _rank_pairs()]
    tau = jnp.full((PEER_HEADS, tt), -jnp.inf, F32)
    nxt = jnp.full((PEER_HEADS, tt), -jnp.inf, F32)
    for xi in cand:
        cnt = jnp.zeros((PEER_HEADS, tt), F32)
        for xj in cand:
            cnt = cnt + jnp.where(xj >= xi, 1.0, 0.0)
        tau = jnp.maximum(tau, jnp.where(cnt >= TOPK, xi, -jnp.inf))
        nxt = jnp.maximum(nxt, jnp.where(cnt >= TOPK + 1, xi, -jnp.inf))
    top = a[0] + b[0]
    z = jnp.zeros((PEER_HEADS, tt), F32)
    for xi in cand:
        z = z + jnp.where(xi >= tau, jnp.exp(xi - top), 0.0)
    inv_z = 1.0 / z
    cut = 0.5 * (tau + nxt)
    n_sel = [jnp.zeros((PEER_HEADS, tt), F32) for _ in range(n_top)]
    for r, c in _rank_pairs():
        n_sel[r] = n_sel[r] + jnp.where(a[r] + b[c] >= cut, 1.0, 0.0)
    for h in range(PEER_HEADS):
        s1 = st_ref[2 * h]
        s2 = st_ref[2 * h + 1]
        r1 = rank1_ref[h]
        cnt = jnp.zeros(r1.shape, F32)
        for r in range(n_top):
            cnt = jnp.where(r1 == float(r), n_sel[r][h:h + 1, :], cnt)
        cnt_ref[h] = cnt
        p1_ref[h] = jnp.exp(s1 - a[0][h:h + 1, :]) * inv_z[h:h + 1, :]
        p2_ref[h] = jnp.exp(s2 - b[0][h:h + 1, :])


def _topk(st, tt):
    n, nk, T = st.shape
    out32 = jax.ShapeDtypeStruct((PEER_HEADS, nk, T), F32)
    spec = pl.BlockSpec((PEER_HEADS, nk, tt), lambda i: (0, 0, i))
    return pl.pallas_call(
        _topk_kernel,
        grid=(T // tt,),
        in_specs=[pl.BlockSpec((n, nk, tt), lambda i: (0, 0, i))],
        out_specs=[spec, spec, spec, spec],
        out_shape=[out32, out32, out32, out32],
        scratch_shapes=[pltpu.VMEM((PEER_HEADS, nk, tt), F32)],
        compiler_params=pltpu.CompilerParams(
            dimension_semantics=("arbitrary",), vmem_limit_bytes=VMEM_LIMIT),
        name="topk",
    )(st)


PEER_KEY_TILE = 64
PEER_TOK_TILE = 128
PEER_PAIR = 2
SUBLANES = 8


def _peer_kernel(xn_ref, u_ref, v_ref, cnt_ref, p1_ref, rank_ref, p2_ref, h1_ref, nw_ref,
                 o_ref, acc_ref, ht_ref, g_ref, cnt_rows, p1_rows):
    j = pl.program_id(1)
    nj = pl.num_programs(1)
    eb, tb = ht_ref.shape
    groups = eb // N_KEYS
    cur = j % 2
    jb = jnp.minimum(j, nj - 2)

    @pl.when(j == 0)
    def _():
        acc_ref[...] = jnp.zeros_like(acc_ref)
        g_ref[...] = jnp.zeros_like(g_ref)

    acc_ref[...] += _tn(g_ref[1 - cur], v_ref[...])

    ht_ref[...] = _nt(u_ref[...], xn_ref[...])
    kt = PEER_KEY_TILE // SUBLANES
    for g0 in range(0, groups, PEER_PAIR):
        for gi in range(PEER_PAIR):
            i1 = jb * groups + g0 + gi
            for h in range(PEER_HEADS):
                r = gi * PEER_HEADS + h
                cnt_rows[r] = jnp.broadcast_to(cnt_ref[h, pl.ds(i1, 1), :], (SUBLANES, tb))
                p1_rows[r] = jnp.broadcast_to(p1_ref[h, pl.ds(i1, 1), :], (SUBLANES, tb))
        for t0 in range(0, tb, PEER_TOK_TILE):
            ts = slice(t0, t0 + PEER_TOK_TILE)
            for k0 in range(0, N_KEYS, PEER_KEY_TILE):
                ks = slice(k0, k0 + PEER_KEY_TILE)
                gate = [None] * PEER_PAIR
                for h in range(PEER_HEADS):
                    rk = rank_ref[h, ks, ts]
                    p2 = p2_ref[h, ks, ts]
                    for gi in range(PEER_PAIR):
                        r = gi * PEER_HEADS + h
                        cn = jnp.concatenate([cnt_rows[r, :, ts]] * kt, axis=0)
                        p1 = jnp.concatenate([p1_rows[r, :, ts]] * kt, axis=0)
                        term = jnp.where(rk < cn, p2, 0.0) * p1
                        gate[gi] = term if gate[gi] is None else gate[gi] + term
                for gi in range(PEER_PAIR):
                    es = slice((g0 + gi) * N_KEYS + k0, (g0 + gi) * N_KEYS + k0 + PEER_KEY_TILE)
                    ht = ht_ref[es, ts]
                    act = 0.5 * ht * (1.0 + lax.erf(ht * (2.0 ** -0.5)))
                    g_ref[cur, es, ts] = (act * gate[gi]).astype(BF16)

    @pl.when(j == nj - 1)
    def _():
        o_ref[...] = _rms(h1_ref[...] + acc_ref[...], nw_ref[...])


def _peer(xn, u, v, cnt, p1, rank2, p2, h1, nw, tb, eb):
    T, D = xn.shape
    n_exp = u.shape[0]
    tok3 = lambda i, j: (0, 0, i)
    nb = n_exp // eb
    return pl.pallas_call(
        _peer_kernel,
        grid=(T // tb, nb + 1),
        in_specs=[
            pl.BlockSpec((tb, D), lambda i, j: (i, 0)),
            pl.BlockSpec((eb, D), lambda i, j: (jnp.minimum(j, nb - 1), 0)),
            pl.BlockSpec((eb, D), lambda i, j: (jnp.maximum(j - 1, 0), 0)),
            pl.BlockSpec((PEER_HEADS, N_KEYS, tb), tok3),
            pl.BlockSpec((PEER_HEADS, N_KEYS, tb), tok3),
            pl.BlockSpec((PEER_HEADS, N_KEYS, tb), tok3),
            pl.BlockSpec((PEER_HEADS, N_KEYS, tb), tok3),
            pl.BlockSpec((tb, D), lambda i, j: (i, 0)),
            pl.BlockSpec((1, D), lambda i, j: (0, 0)),
        ],
        out_specs=pl.BlockSpec((tb, D), lambda i, j: (i, 0)),
        out_shape=jax.ShapeDtypeStruct((T, D), F32),
        scratch_shapes=[pltpu.VMEM((tb, D), F32), pltpu.VMEM((eb, tb), F32), pltpu.VMEM((2, eb, tb), BF16),
                        pltpu.VMEM((PEER_PAIR * PEER_HEADS, SUBLANES, tb), F32),
                        pltpu.VMEM((PEER_PAIR * PEER_HEADS, SUBLANES, tb), F32)],
        compiler_params=pltpu.CompilerParams(
            dimension_semantics=("arbitrary", "arbitrary"), vmem_limit_bytes=VMEM_LIMIT),
        name="peer",
    )(xn, u, v, cnt, p1, rank2, p2, h1, nw)


ROUTE_SLOTS = 64


def _route_pairs():
    return [(r, c) for r in range(TOPK) for c in range(TOPK) if (r + 1) * (c + 1) <= TOPK]


def _routes_kernel(st_ref, ids_ref, w_ref):
    tt = st_ref.shape[2]
    key_id = lax.broadcasted_iota(jnp.int32, (N_KEYS, tt), 0).astype(F32)

    def top_sorted(s):
        vals, idxs = [], []
        cur = s
        for _ in range(TOPK):
            mx = jnp.max(cur, axis=0, keepdims=True)
            hit = cur == mx
            vals.append(mx)
            idxs.append(jnp.max(jnp.where(hit, key_id, -1.0), axis=0, keepdims=True))
            cur = jnp.where(hit, -jnp.inf, cur)
        return vals, idxs

    tops = [(top_sorted(st_ref[2 * h]), top_sorted(st_ref[2 * h + 1])) for h in range(PEER_HEADS)]

    def stack(half, which, r):
        return jnp.concatenate([tops[h][half][which][r] for h in range(PEER_HEADS)], axis=0)

    a = [stack(0, 0, r) for r in range(TOPK)]
    b = [stack(1, 0, r) for r in range(TOPK)]
    ia = [stack(0, 1, r) for r in range(TOPK)]
    ib = [stack(1, 1, r) for r in range(TOPK)]
    pairs = _route_pairs()
    cand = [a[r] + b[c] for r, c in pairs]
    tau = jnp.full((PEER_HEADS, tt), -jnp.inf, F32)
    for xi in cand:
        cnt = jnp.zeros((PEER_HEADS, tt), F32)
        for xj in cand:
            cnt = cnt + jnp.where(xj >= xi, 1.0, 0.0)
        tau = jnp.maximum(tau, jnp.where(cnt >= TOPK, xi, -jnp.inf))
    top = a[0] + b[0]
    e = [jnp.where(xi >= tau, jnp.exp(xi - top), 0.0) for xi in cand]
    z = e[0]
    for ei in e[1:]:
        z = z + ei
    inv_z = 1.0 / z
    ids = [jnp.clip(ia[r] * float(N_KEYS) + ib[c], 0.0, float(N_KEYS * N_KEYS - 1)) for r, c in pairs]
    wts = [ek * inv_z for ek in e]
    pad = [jnp.zeros((PEER_HEADS, tt), F32)] * (ROUTE_SLOTS - len(pairs))
    ids_ref[...] = jnp.concatenate(ids + pad, axis=0).T.astype(jnp.int32)
    w_ref[...] = jnp.concatenate(wts + pad, axis=0).T


def _routes(st, tt, t0, tc):
    n, nk, _ = st.shape
    off = t0 // tt
    width = ROUTE_SLOTS * PEER_HEADS
    spec = pl.BlockSpec((tt, width), lambda i: (i, 0))
    return pl.pallas_call(
        _routes_kernel,
        grid=(tc // tt,),
        in_specs=[pl.BlockSpec((n, nk, tt), lambda i: (0, 0, i + off))],
        out_specs=[spec, spec],
        out_shape=[jax.ShapeDtypeStruct((tc, width), jnp.int32),
                   jax.ShapeDtypeStruct((tc, width), F32)],
        compiler_params=pltpu.CompilerParams(
            dimension_semantics=("arbitrary",), vmem_limit_bytes=VMEM_LIMIT),
        name="routes",
    )(st)


SC_LANES = 16
SC_TOKENS_PER_CHUNK = 8
PEER_CHUNKS = 8


def _gate_matrix(ids, w, n_exp, heads):
    T, E = ids.shape
    info = plsc.get_sparse_core_info()
    workers = info.num_cores * info.num_subcores
    per_worker = T // workers
    ch = SC_TOKENS_PER_CHUNK
    mesh = plsc.VectorSubcoreMesh(core_axis_name="c", subcore_axis_name="s")
    ids_flat = ids.reshape(T * E)
    w_flat = w.reshape(T * E)

    @functools.partial(
        pl.kernel, mesh=mesh,
        out_type=jax.ShapeDtypeStruct((T, n_exp), F32),
        scratch_types=[pltpu.VMEM((ch * E,), jnp.int32), pltpu.VMEM((ch * E,), F32),
                       pltpu.VMEM((n_exp,), F32)],
        compiler_params=pltpu.CompilerParams(needs_layout_passes=False),
        name="gate_matrix",
    )
    def scatter(ids_hbm, w_hbm, out_hbm, ids_v, w_v, row_v):
        wid = lax.axis_index("s") * info.num_cores + lax.axis_index("c")
        base = wid * per_worker
        zeros = jnp.zeros((SC_LANES,), F32)
        lane = lax.iota(jnp.int32, SC_LANES)

        @pl.loop(0, n_exp, step=SC_LANES)
        def _(i):
            row_v[pl.ds(i, SC_LANES)] = zeros

        @pl.loop(0, per_worker // ch)
        def _(ci):
            t0 = base + ci * ch
            pltpu.sync_copy(ids_hbm.at[pl.ds(t0 * E, ch * E)], ids_v)
            pltpu.sync_copy(w_hbm.at[pl.ds(t0 * E, ch * E)], w_v)
            for tl in range(ch):
                @pl.loop(0, heads)
                def _(h):
                    for m in range(E // heads // SC_LANES):
                        pos = (lane + (tl * E // heads + m * SC_LANES)) * heads + h
                        idx = plsc.load_gather(ids_v, [pos])
                        val = plsc.load_gather(w_v, [pos])
                        plsc.addupdate_scatter(row_v, [idx], val)

                pltpu.sync_copy(row_v, out_hbm.at[t0 + tl])

                @pl.loop(0, E, step=SC_LANES)
                def _(e0):
                    sl = pl.ds(tl * E + e0, SC_LANES)
                    plsc.store_scatter(row_v, [ids_v[sl]], zeros)

    return scatter(ids_flat, w_flat)


def _experts_kernel(xn_ref, u_ref, v_ref, gate_ref, h1_ref, nw_ref, o_ref, acc_ref):
    j = pl.program_id(1)
    nj = pl.num_programs(1)

    @pl.when(j == 0)
    def _():
        acc_ref[...] = jnp.zeros_like(acc_ref)

    h = _nt(xn_ref[...], u_ref[...])
    act = 0.5 * h * (1.0 + lax.erf(h * (2.0 ** -0.5)))
    acc_ref[...] += _nn((act * gate_ref[...]).astype(BF16), v_ref[...])

    @pl.when(j == nj - 1)
    def _():
        o_ref[...] = _rms(h1_ref[...] + acc_ref[...], nw_ref[...])


def _experts(xn, u, v, gate, h1, nw, tb, eb, t0):
    tc = gate.shape[0]
    D = xn.shape[1]
    n_exp = u.shape[0]
    off = t0 // tb
    return pl.pallas_call(
        _experts_kernel,
        grid=(tc // tb, n_exp // eb),
        in_specs=[
            pl.BlockSpec((tb, D), lambda i, j: (i + off, 0)),
            pl.BlockSpec((eb, D), lambda i, j: (j, 0)),
            pl.BlockSpec((eb, D), lambda i, j: (j, 0)),
            pl.BlockSpec((tb, eb), lambda i, j: (i, j)),
            pl.BlockSpec((tb, D), lambda i, j: (i + off, 0)),
            pl.BlockSpec((1, D), lambda i, j: (0, 0)),
        ],
        out_specs=pl.BlockSpec((tb, D), lambda i, j: (i, 0)),
        out_shape=jax.ShapeDtypeStruct((tc, D), F32),
        scratch_shapes=[pltpu.VMEM((tb, D), F32)],
        compiler_params=pltpu.CompilerParams(
            dimension_semantics=("arbitrary", "arbitrary"), vmem_limit_bytes=VMEM_LIMIT),
        name="experts",
    )(xn, u, v, gate, h1, nw)


def _block_select_matrix():
    part = jnp.arange(HGRN_SUB * HGRN_DH, dtype=jnp.int32) // HGRN_DH
    col = jnp.arange(HGRN_CHUNK, dtype=jnp.int32) % HGRN_SUB
    return (part[:, None] == col[None, :]).astype(BF16)


def _forward(x, norm_mix_w, w_in, hgrn_lb_logits, hgrn_norm_w, fox_f_bias, w_branch_hgrn,
             w_branch_fox, w_out, norm_ffn_w, peer_w_q, peer_sub_keys, peer_u, peer_v,
             norm_final_w, *, tm_in, tq, tm_merge, tt, tb, eb):
    B, S, D = x.shape
    T = B * S
    x2d = x.reshape(T, D)
    n_h = 4 * HGRN_W
    n_f = 3 * FOX_W
    wi = w_in[0]
    w_all = jnp.concatenate([wi[:, :n_h + n_f], wi[:, n_h + n_f + FOX_HEADS:]], axis=1).astype(BF16)
    wff = wi[:, n_h + n_f:n_h + n_f + FOX_HEADS]
    wff_row = wff.T.astype(BF16)
    fb = fox_f_bias[0].astype(F32)
    fb_row = jnp.broadcast_to(fb.reshape(FOX_HEADS, 1), (FOX_HEADS, LANES))

    hg, fox, gates, crow = _in_proj(
        x2d, norm_mix_w[0].reshape(1, D), w_all, wff_row, fb_row, B, S, tm_in)
    a = _hgrn(hg, hgrn_lb_logits, hgrn_norm_w[0].reshape(1, HGRN_W), _block_select_matrix(), B, S)
    b = _fox(fox, crow, B, S, tq)
    keys = peer_sub_keys[0].reshape(2 * PEER_HEADS, N_KEYS, PEER_HALF).astype(BF16)
    h1, xn2, st = _merge(
        x2d, a, b, gates, w_branch_hgrn[0].astype(BF16), w_branch_fox[0].astype(BF16),
        w_out[0].astype(BF16), norm_ffn_w[0].reshape(1, D), peer_w_q[0].astype(BF16), keys, tm_merge)
    n_exp = peer_u.shape[1]
    u_bf = peer_u[0].astype(BF16)
    v_bf = peer_v[0].astype(BF16)
    tc = T // PEER_CHUNKS
    outs = []
    for c in range(PEER_CHUNKS):
        ids, wts = _routes(st, tt, c * tc, tc)
        gate = _gate_matrix(ids, wts, n_exp, PEER_HEADS)
        outs.append(_experts(xn2, u_bf, v_bf, gate, h1, norm_final_w.reshape(1, D), tb, eb, c * tc))
    return jnp.concatenate(outs, axis=0).reshape(B, S, D)


def kernel(x, norm_mix_w, w_in, hgrn_lb_logits, hgrn_norm_w, fox_f_bias, w_branch_hgrn, w_branch_fox, w_out, norm_ffn_w, peer_w_q, peer_sub_keys, peer_u, peer_v, norm_final_w):
    return _forward(x, norm_mix_w, w_in, hgrn_lb_logits, hgrn_norm_w, fox_f_bias, w_branch_hgrn,
                    w_branch_fox, w_out, norm_ffn_w, peer_w_q, peer_sub_keys, peer_u, peer_v,
                    norm_final_w, tm_in=256, tq=512, tm_merge=256, tt=256, tb=1024, eb=1024)
```

```python
import functools
import math

import jax
import jax.numpy as jnp
from jax import lax
from jax.experimental import pallas as pl
from jax.experimental.pallas import tpu as pltpu
from jax.experimental.pallas import tpu_sc as plsc

F32 = jnp.float32
BF16 = jnp.bfloat16
RMS_EPS = 1e-6
NEG_BIG = -1e30

HGRN_HEADS = 4
HGRN_DH = 128
HGRN_W = HGRN_HEADS * HGRN_DH
FOX_HEADS = 8
FOX_DH = 64
FOX_W = FOX_HEADS * FOX_DH
PEER_HEADS = 8
PEER_HALF = 128
N_KEYS = 128
TOPK = 16
LANES = 128

VMEM_LIMIT = 56 * 1024 * 1024


def _nt(a, b):
    return lax.dot_general(a, b, (((1,), (1,)), ((), ())), preferred_element_type=F32)


def _tn(a, b):
    return lax.dot_general(a, b, (((0,), (0,)), ((), ())), preferred_element_type=F32)


def _nn(a, b):
    return jnp.dot(a, b, preferred_element_type=F32)


def _split_dot(fn, tri, x):
    hi = x.astype(BF16)
    lo = (x - hi.astype(F32)).astype(BF16)
    return fn(tri, hi) + fn(tri, lo)


def _log_sigmoid(x):
    return jnp.minimum(x, 0.0) - jnp.log1p(jnp.exp(-jnp.abs(x)))


def _rms(x, w):
    return x * lax.rsqrt(jnp.mean(x * x, axis=-1, keepdims=True) + RMS_EPS) * w


LOG2E = math.log2(math.e)


def _inproj_kernel(x_ref, nw_ref, w_ref, wffr_ref, fbr_ref,
                   hg_ref, fox_ref, gate_ref, crow_ref, carry_row):
    i = pl.program_id(1)
    tm = x_ref.shape[0]

    @pl.when(i == 0)
    def _():
        carry_row[...] = jnp.zeros_like(carry_row)

    xn = _rms(x_ref[...], nw_ref[...]).astype(BF16)
    proj = _nn(xn, w_ref[...])
    n_h = 4 * HGRN_W
    hg_ref[...] = proj[:, :n_h]
    fox_ref[:, :FOX_W] = (proj[:, n_h:n_h + FOX_W] * (FOX_DH ** -0.5 * LOG2E)).astype(BF16)
    fox_ref[:, FOX_W:] = proj[:, n_h + FOX_W:n_h + 3 * FOX_W].astype(BF16)
    gate_ref[...] = jax.nn.sigmoid(proj[:, n_h + 3 * FOX_W:])

    r = lax.broadcasted_iota(jnp.int32, (tm, tm), 0)
    c = lax.broadcasted_iota(jnp.int32, (tm, tm), 1)
    triu = (r <= c).astype(BF16)
    ls_row = _log_sigmoid(_nt(wffr_ref[...], xn) + fbr_ref[:, 0:1]) * LOG2E
    hi = ls_row.astype(BF16)
    lo = (ls_row - hi.astype(F32)).astype(BF16)
    crow = _nn(hi, triu) + _nn(lo, triu) + carry_row[:, 0:1]
    crow_ref[...] = crow
    carry_row[...] = jnp.broadcast_to(crow[:, tm - 1:tm], carry_row.shape)


def _in_proj(x2d, nw, w_all, wff_row, fb_row, batch, seq, tm):
    T, D = x2d.shape
    nt = seq // tm
    n_all = w_all.shape[1]
    row = lambda b, i: (b * nt + i, 0)
    const = lambda b, i: (0, 0)
    return pl.pallas_call(
        _inproj_kernel,
        grid=(batch, nt),
        in_specs=[
            pl.BlockSpec((tm, D), row),
            pl.BlockSpec((1, D), const),
            pl.BlockSpec((D, n_all), const),
            pl.BlockSpec((FOX_HEADS, D), const),
            pl.BlockSpec((FOX_HEADS, LANES), const),
        ],
        out_specs=[
            pl.BlockSpec((tm, 4 * HGRN_W), row),
            pl.BlockSpec((tm, 3 * FOX_W), row),
            pl.BlockSpec((tm, 2 * D), row),
            pl.BlockSpec((FOX_HEADS, tm), lambda b, i: (0, b * nt + i)),
        ],
        out_shape=[
            jax.ShapeDtypeStruct((T, 4 * HGRN_W), F32),
            jax.ShapeDtypeStruct((T, 3 * FOX_W), BF16),
            jax.ShapeDtypeStruct((T, 2 * D), F32),
            jax.ShapeDtypeStruct((FOX_HEADS, T), F32),
        ],
        scratch_shapes=[pltpu.VMEM((FOX_HEADS, LANES), F32)],
        compiler_params=pltpu.CompilerParams(
            dimension_semantics=("arbitrary", "arbitrary"), vmem_limit_bytes=VMEM_LIMIT),
        name="in_proj",
    )(x2d, nw, w_all, wff_row, fb_row)


HGRN_CHUNK = 128
HGRN_SUB = 16


def _hgrn_kernel(hg_ref, lbl_ref, nw_ref, rsel_ref, a_ref, state_ref):
    ci = pl.program_id(1)
    C = HGRN_CHUNK
    dh = HGRN_DH

    @pl.when(ci == 0)
    def _():
        state_ref[...] = jnp.zeros_like(state_ref)

    lg = lbl_ref[...]
    e = jnp.exp(lg - jnp.max(lg, axis=0, keepdims=True))
    lb_all = e[0:1, :] / jnp.sum(e, axis=0, keepdims=True)

    r = lax.broadcasted_iota(jnp.int32, (C, C), 0)
    c = lax.broadcasted_iota(jnp.int32, (C, C), 1)
    tril = (c <= r).astype(BF16)
    sub = HGRN_SUB
    sh = sub.bit_length() - 1
    diag_mask = ((r >> sh) == (c >> sh)) & ((c & (sub - 1)) <= (r & (sub - 1)))
    levels = []
    m = sub
    while m < C:
        sh = m.bit_length() - 1
        levels.append((m, ((r >> (sh + 1)) == (c >> (sh + 1))) & (((r >> sh) & 1) == 1) & (((c >> sh) & 1) == 0)))
        m *= 2

    for h in range(HGRN_HEADS):
        sl = slice(h * dh, (h + 1) * dh)
        qraw = hg_ref[:, sl]
        q = qraw * jax.nn.sigmoid(qraw)
        lb = lb_all[:, sl]
        f = lb + (1.0 - lb) * jax.nn.sigmoid(hg_ref[:, HGRN_W + h * dh:HGRN_W + (h + 1) * dh])
        logf = jnp.log(f)
        k = 1.0 - f
        v = hg_ref[:, 2 * HGRN_W + h * dh:2 * HGRN_W + (h + 1) * dh]
        g = hg_ref[:, 3 * HGRN_W + h * dh:3 * HGRN_W + (h + 1) * dh]
        v_bf = v.astype(BF16)
        cum = _split_dot(_nn, tril, logf)

        nb = C // sub
        q3 = q.reshape(nb, sub, dh)
        k3 = k.reshape(nb, sub, dh)
        c3 = cum.reshape(nb, sub, dh)
        parts = []
        for s in range(sub):
            kb = jnp.broadcast_to(k3[:, s:s + 1, :], (nb, sub, dh))
            cb = jnp.broadcast_to(c3[:, s:s + 1, :], (nb, sub, dh))
            es = q3 * kb * jnp.exp(jnp.minimum(c3 - cb, 0.0))
            parts.append(es.reshape(C, dh).astype(BF16))
        p_mat = jnp.where(diag_mask, _nn(jnp.concatenate(parts, axis=1), rsel_ref[...]), 0.0)

        for m, mask in levels:
            nbm = C // m
            qm = q.reshape(nbm, m, dh)
            km = k.reshape(nbm, m, dh)
            cm = cum.reshape(nbm, m, dh)
            end = cm[:, m - 1:m, :]
            prev_end = jnp.concatenate([jnp.zeros((1, 1, dh), F32), end[:nbm - 1]], axis=0)
            qd = qm * jnp.exp(jnp.minimum(cm - jnp.broadcast_to(prev_end, (nbm, m, dh)), 0.0))
            kd = km * jnp.exp(jnp.minimum(jnp.broadcast_to(end, (nbm, m, dh)) - cm, 0.0))
            sc = _nt(qd.reshape(C, dh).astype(BF16), kd.reshape(C, dh).astype(BF16))
            p_mat = p_mat + jnp.where(mask, sc, 0.0)

        st = state_ref[h]
        o = _nn(p_mat.astype(BF16), v_bf) + _nt((q * jnp.exp(cum)).astype(BF16), st.astype(BF16))
        last = cum[C - 1:C, :]
        kdec = (k * jnp.exp(last - cum)).astype(BF16)
        state_ref[h] = jnp.exp(last) * st + _tn(v_bf, kdec)

        o = o * lax.rsqrt(jnp.mean(o * o, axis=-1, keepdims=True) + RMS_EPS) * nw_ref[:, sl]
        a_ref[:, sl] = (o * (g * jax.nn.sigmoid(g))).astype(BF16)


def _hgrn(hg, lb_logits, norm_w, rsel, batch, seq):
    T = hg.shape[0]
    C = HGRN_CHUNK
    nc = seq // C
    row = lambda b, i: (b * nc + i, 0)
    const = lambda b, i: (0, 0)
    return pl.pallas_call(
        _hgrn_kernel,
        grid=(batch, nc),
        in_specs=[
            pl.BlockSpec((C, 4 * HGRN_W), row),
            pl.BlockSpec(lb_logits.shape, const),
            pl.BlockSpec((1, HGRN_W), const),
            pl.BlockSpec(rsel.shape, const),
        ],
        out_specs=pl.BlockSpec((C, HGRN_W), row),
        out_shape=jax.ShapeDtypeStruct((T, HGRN_W), BF16),
        scratch_shapes=[pltpu.VMEM((HGRN_HEADS, HGRN_DH, HGRN_DH), F32)],
        compiler_params=pltpu.CompilerParams(
            dimension_semantics=("arbitrary", "arbitrary"), vmem_limit_bytes=VMEM_LIMIT),
        name="hgrn",
    )(hg, lb_logits, norm_w, rsel)


def _fox_kernel(q_ref, k_ref, v_ref, crow_ref, o_ref, m_ref, acc_ref, qs_ref, vs_ref):
    qi = pl.program_id(1)
    tq = q_ref.shape[0]
    tk = tq
    pairs = FOX_HEADS // 2
    lane = lax.broadcasted_iota(jnp.int32, (1, LANES), 1)
    low = lane < FOX_DH
    sel = (jnp.where(low, 1.0, 0.0).astype(BF16), jnp.where(low, 0.0, 1.0).astype(BF16))
    ones_lane = (FOX_DH, 0)
    one_hot = tuple(jnp.where(lane == ol, 1.0, 0.0).astype(BF16) for ol in ones_lane)
    rr = lax.broadcasted_iota(jnp.int32, (tq, tk), 0)
    cc = lax.broadcasted_iota(jnp.int32, (tq, tk), 1)
    causal = cc <= rr

    @pl.when(qi == 0)
    def _():
        for p in range(pairs):
            vb = v_ref[:, p * LANES:(p + 1) * LANES]
            for hh in range(2):
                vs_ref[2 * p + hh] = vb * sel[hh] + one_hot[hh]

    for p in range(pairs):
        q = q_ref[:, p * LANES:(p + 1) * LANES]
        for hh in range(2):
            qs_ref[2 * p + hh] = q * sel[hh]
    m_ref[...] = jnp.full_like(m_ref, NEG_BIG)
    acc_ref[...] = jnp.zeros_like(acc_ref)

    def kv_block(j, masked):
        start = pl.multiple_of(j * tk, tk)
        for p in range(pairs):
            kb = k_ref[pl.ds(start, tk), p * LANES:(p + 1) * LANES]
            for hh in range(2):
                h = 2 * p + hh
                ck = crow_ref[h:h + 1, pl.ds(start, tk)]
                s = _nt(qs_ref[h], kb) - ck
                if masked:
                    s = jnp.where(causal, s, NEG_BIG)
                m_prev = m_ref[h]
                m_next = jnp.maximum(m_prev, jnp.max(s, axis=1, keepdims=True))
                pexp = jnp.exp2(s - jnp.concatenate([m_next] * (tk // LANES), axis=1))
                alpha = jnp.exp2(m_prev - m_next)
                acc_ref[h] = alpha * acc_ref[h] + _nn(pexp.astype(BF16), vs_ref[h, pl.ds(start, tk), :])
                m_ref[h] = m_next

    def body(j, carry):
        kv_block(j, False)
        return carry

    lax.fori_loop(0, qi, body, 0)
    kv_block(qi, True)

    for p in range(pairs):
        a0 = acc_ref[2 * p]
        a1 = acc_ref[2 * p + 1]
        o0 = a0 / a0[:, ones_lane[0]:ones_lane[0] + 1]
        o1 = a1 / a1[:, ones_lane[1]:ones_lane[1] + 1]
        o_ref[:, p * LANES:(p + 1) * LANES] = jnp.where(low, o0, o1).astype(BF16)


def _fox(fox, crow, batch, seq, tq):
    T = fox.shape[0]
    nq = seq // tq
    return pl.pallas_call(
        _fox_kernel,
        grid=(batch, nq),
        in_specs=[
            pl.BlockSpec((tq, FOX_W), lambda b, i: (b * nq + i, 0)),
            pl.BlockSpec((seq, FOX_W), lambda b, i: (b, 1)),
            pl.BlockSpec((seq, FOX_W), lambda b, i: (b, 2)),
            pl.BlockSpec((FOX_HEADS, seq), lambda b, i: (0, b)),
        ],
        out_specs=pl.BlockSpec((tq, FOX_W), lambda b, i: (b * nq + i, 0)),
        out_shape=jax.ShapeDtypeStruct((T, FOX_W), BF16),
        scratch_shapes=[pltpu.VMEM((FOX_HEADS, tq, LANES), F32), pltpu.VMEM((FOX_HEADS, tq, LANES), F32),
                        pltpu.VMEM((FOX_HEADS, tq, LANES), BF16), pltpu.VMEM((FOX_HEADS, seq, LANES), BF16)],
        compiler_params=pltpu.CompilerParams(
            dimension_semantics=("arbitrary", "arbitrary"), vmem_limit_bytes=VMEM_LIMIT),
        name="fox",
    )(fox, fox, fox, crow)


def _merge_kernel(x_ref, a_ref, b_ref, gate_ref, wa_ref, wb_ref, wo_ref, nw_ref, wq_ref, keys_ref,
                  h1_ref, xn_ref, st_ref):
    D = x_ref.shape[1]
    merged = gate_ref[:, :D] * _nn(a_ref[...], wa_ref[...]) + gate_ref[:, D:] * _nn(b_ref[...], wb_ref[...])
    h1 = x_ref[...] + _nn(merged.astype(BF16), wo_ref[...])
    h1_ref[...] = h1
    xn = _rms(h1, nw_ref[...]).astype(BF16)
    xn_ref[...] = xn
    q = _nn(xn, wq_ref[...]).astype(BF16)
    for hp in range(2 * PEER_HEADS):
        st_ref[hp] = _nt(keys_ref[hp], q[:, hp * PEER_HALF:(hp + 1) * PEER_HALF])


def _merge(x2d, a, b, gates, wa, wb, wo, nw, wq, keys, tm):
    T, D = x2d.shape
    row = lambda i: (i, 0)
    const = lambda i: (0, 0)
    return pl.pallas_call(
        _merge_kernel,
        grid=(T // tm,),
        in_specs=[
            pl.BlockSpec((tm, D), row),
            pl.BlockSpec((tm, HGRN_W), row),
            pl.BlockSpec((tm, FOX_W), row),
            pl.BlockSpec((tm, 2 * D), row),
            pl.BlockSpec(wa.shape, const),
            pl.BlockSpec(wb.shape, const),
            pl.BlockSpec(wo.shape, const),
            pl.BlockSpec((1, D), const),
            pl.BlockSpec(wq.shape, const),
            pl.BlockSpec(keys.shape, lambda i: (0, 0, 0)),
        ],
        out_specs=[
            pl.BlockSpec((tm, D), row),
            pl.BlockSpec((tm, D), row),
            pl.BlockSpec((2 * PEER_HEADS, N_KEYS, tm), lambda i: (0, 0, i)),
        ],
        out_shape=[
            jax.ShapeDtypeStruct((T, D), F32),
            jax.ShapeDtypeStruct((T, D), BF16),
            jax.ShapeDtypeStruct((2 * PEER_HEADS, N_KEYS, T), F32),
        ],
        compiler_params=pltpu.CompilerParams(
            dimension_semantics=("arbitrary",), vmem_limit_bytes=VMEM_LIMIT),
        name="merge",
    )(x2d, a, b, gates, wa, wb, wo, nw, wq, keys)


def _rank_pairs():
    n = TOPK + 1
    return [(r, c) for r in range(n) for c in range(n) if (r + 1) * (c + 1) <= n]


def _topk_kernel(st_ref, cnt_ref, p1_ref, rank_ref, p2_ref, rank1_ref):
    tt = st_ref.shape[2]
    n_top = TOPK + 1

    def top_sorted(s):
        vals = []
        cur = s
        rank = jnp.full(s.shape, float(n_top), F32)
        for r in range(n_top):
            mx = jnp.max(cur, axis=0, keepdims=True)
            vals.append(mx)
            hit = cur == mx
            rank = jnp.where(hit, float(r), rank)
            cur = jnp.where(hit, -jnp.inf, cur)
        return vals, rank

    tops = []
    for h in range(PEER_HEADS):
        v1, r1 = top_sorted(st_ref[2 * h])
        v2, r2 = top_sorted(st_ref[2 * h + 1])
        rank1_ref[h] = r1
        rank_ref[h] = r2
        tops.append((v1, v2))
    a = [jnp.concatenate([tops[h][0][r] for h in range(PEER_HEADS)], axis=0) for r in range(n_top)]
    b = [jnp.concatenate([tops[h][1][r] for h in range(PEER_HEADS)], axis=0) for r in range(n_top)]
    cand = [a[r] + b[c] for r, c in _rank_pairs()]
    tau = jnp.full((PEER_HEADS, tt), -jnp.inf, F32)
    nxt = jnp.full((PEER_HEADS, tt), -jnp.inf, F32)
    for xi in cand:
        cnt = jnp.zeros((PEER_HEADS, tt), F32)
        for xj in cand:
            cnt = cnt + jnp.where(xj >= xi, 1.0, 0.0)
        tau = jnp.maximum(tau, jnp.where(cnt >= TOPK, xi, -jnp.inf))
        nxt = jnp.maximum(nxt, jnp.where(cnt >= TOPK + 1, xi, -jnp.inf))
    top = a[0] + b[0]
    z = jnp.zeros((PEER_HEADS, tt), F32)
    for xi in cand:
        z = z + jnp.where(xi >= tau, jnp.exp(xi - top), 0.0)
    inv_z = 1.0 / z
    cut = 0.5 * (tau + nxt)
    n_sel = [jnp.zeros((PEER_HEADS, tt), F32) for _ in range(n_top)]
    for r, c in _rank_pairs():
        n_sel[r] = n_sel[r] + jnp.where(a[r] + b[c] >= cut, 1.0, 0.0)
    for h in range(PEER_HEADS):
        s1 = st_ref[2 * h]
        s2 = st_ref[2 * h + 1]
        r1 = rank1_ref[h]
        cnt = jnp.zeros(r1.shape, F32)
        for r in range(n_top):
            cnt = jnp.where(r1 == float(r), n_sel[r][h:h + 1, :], cnt)
        cnt_ref[h] = cnt
        p1_ref[h] = jnp.exp(s1 - a[0][h:h + 1, :]) * inv_z[h:h + 1, :]
        p2_ref[h] = jnp.exp(s2 - b[0][h:h + 1, :])


def _topk(st, tt):
    n, nk, T = st.shape
    out32 = jax.ShapeDtypeStruct((PEER_HEADS, nk, T), F32)
    spec = pl.BlockSpec((PEER_HEADS, nk, tt), lambda i: (0, 0, i))
    return pl.pallas_call(
        _topk_kernel,
        grid=(T // tt,),
        in_specs=[pl.BlockSpec((n, nk, tt), lambda i: (0, 0, i))],
        out_specs=[spec, spec, spec, spec],
        out_shape=[out32, out32, out32, out32],
        scratch_shapes=[pltpu.VMEM((PEER_HEADS, nk, tt), F32)],
        compiler_params=pltpu.CompilerParams(
            dimension_semantics=("arbitrary",), vmem_limit_bytes=VMEM_LIMIT),
        name="topk",
    )(st)


PEER_KEY_TILE = 64
PEER_TOK_TILE = 128
PEER_PAIR = 2
SUBLANES = 8


def _peer_kernel(xn_ref, u_ref, v_ref, cnt_ref, p1_ref, rank_ref, p2_ref, h1_ref, nw_ref,
                 o_ref, acc_ref, ht_ref, g_ref, cnt_rows, p1_rows):
    j = pl.program_id(1)
    nj = pl.num_programs(1)
    eb, tb = ht_ref.shape
    groups = eb // N_KEYS
    cur = j % 2
    jb = jnp.minimum(j, nj - 2)

    @pl.when(j == 0)
    def _():
        acc_ref[...] = jnp.zeros_like(acc_ref)
        g_ref[...] = jnp.zeros_like(g_ref)

    acc_ref[...] += _tn(g_ref[1 - cur], v_ref[...])

    ht_ref[...] = _nt(u_ref[...], xn_ref[...])
    kt = PEER_KEY_TILE // SUBLANES
    for g0 in range(0, groups, PEER_PAIR):
        for gi in range(PEER_PAIR):
            i1 = jb * groups + g0 + gi
            for h in range(PEER_HEADS):
                r = gi * PEER_HEADS + h
                cnt_rows[r] = jnp.broadcast_to(cnt_ref[h, pl.ds(i1, 1), :], (SUBLANES, tb))
                p1_rows[r] = jnp.broadcast_to(p1_ref[h, pl.ds(i1, 1), :], (SUBLANES, tb))
        for t0 in range(0, tb, PEER_TOK_TILE):
            ts = slice(t0, t0 + PEER_TOK_TILE)
            for k0 in range(0, N_KEYS, PEER_KEY_TILE):
                ks = slice(k0, k0 + PEER_KEY_TILE)
                gate = [None] * PEER_PAIR
                for h in range(PEER_HEADS):
                    rk = rank_ref[h, ks, ts]
                    p2 = p2_ref[h, ks, ts]
                    for gi in range(PEER_PAIR):
                        r = gi * PEER_HEADS + h
                        cn = jnp.concatenate([cnt_rows[r, :, ts]] * kt, axis=0)
                        p1 = jnp.concatenate([p1_rows[r, :, ts]] * kt, axis=0)
                        term = jnp.where(rk < cn, p2, 0.0) * p1
                        gate[gi] = term if gate[gi] is None else gate[gi] + term
                for gi in range(PEER_PAIR):
                    es = slice((g0 + gi) * N_KEYS + k0, (g0 + gi) * N_KEYS + k0 + PEER_KEY_TILE)
                    ht = ht_ref[es, ts]
                    act = 0.5 * ht * (1.0 + lax.erf(ht * (2.0 ** -0.5)))
                    g_ref[cur, es, ts] = (act * gate[gi]).astype(BF16)

    @pl.when(j == nj - 1)
    def _():
        o_ref[...] = _rms(h1_ref[...] + acc_ref[...], nw_ref[...])


def _peer(xn, u, v, cnt, p1, rank2, p2, h1, nw, tb, eb):
    T, D = xn.shape
    n_exp = u.shape[0]
    tok3 = lambda i, j: (0, 0, i)
    nb = n_exp // eb
    return pl.pallas_call(
        _peer_kernel,
        grid=(T // tb, nb + 1),
        in_specs=[
            pl.BlockSpec((tb, D), lambda i, j: (i, 0)),
            pl.BlockSpec((eb, D), lambda i, j: (jnp.minimum(j, nb - 1), 0)),
            pl.BlockSpec((eb, D), lambda i, j: (jnp.maximum(j - 1, 0), 0)),
            pl.BlockSpec((PEER_HEADS, N_KEYS, tb), tok3),
            pl.BlockSpec((PEER_HEADS, N_KEYS, tb), tok3),
            pl.BlockSpec((PEER_HEADS, N_KEYS, tb), tok3),
            pl.BlockSpec((PEER_HEADS, N_KEYS, tb), tok3),
            pl.BlockSpec((tb, D), lambda i, j: (i, 0)),
            pl.BlockSpec((1, D), lambda i, j: (0, 0)),
        ],
        out_specs=pl.BlockSpec((tb, D), lambda i, j: (i, 0)),
        out_shape=jax.ShapeDtypeStruct((T, D), F32),
        scratch_shapes=[pltpu.VMEM((tb, D), F32), pltpu.VMEM((eb, tb), F32), pltpu.VMEM((2, eb, tb), BF16),
                        pltpu.VMEM((PEER_PAIR * PEER_HEADS, SUBLANES, tb), F32),
                        pltpu.VMEM((PEER_PAIR * PEER_HEADS, SUBLANES, tb), F32)],
        compiler_params=pltpu.CompilerParams(
            dimension_semantics=("arbitrary", "arbitrary"), vmem_limit_bytes=VMEM_LIMIT),
        name="peer",
    )(xn, u, v, cnt, p1, rank2, p2, h1, nw)


ROUTE_SLOTS = 64


def _route_pairs():
    return [(r, c) for r in range(TOPK) for c in range(TOPK) if (r + 1) * (c + 1) <= TOPK]


def _routes_kernel(st_ref, ids_ref, w_ref):
    tt = st_ref.shape[2]
    key_id = lax.broadcasted_iota(jnp.int32, (N_KEYS, tt), 0).astype(F32)

    def top_sorted(s):
        vals, idxs = [], []
        cur = s
        for _ in range(TOPK):
            mx = jnp.max(cur, axis=0, keepdims=True)
            hit = cur == mx
            vals.append(mx)
            idxs.append(jnp.max(jnp.where(hit, key_id, -1.0), axis=0, keepdims=True))
            cur = jnp.where(hit, -jnp.inf, cur)
        return vals, idxs

    tops = [(top_sorted(st_ref[2 * h]), top_sorted(st_ref[2 * h + 1])) for h in range(PEER_HEADS)]

    def stack(half, which, r):
        return jnp.concatenate([tops[h][half][which][r] for h in range(PEER_HEADS)], axis=0)

    a = [stack(0, 0, r) for r in range(TOPK)]
    b = [stack(1, 0, r) for r in range(TOPK)]
    ia = [stack(0, 1, r) for r in range(TOPK)]
    ib = [stack(1, 1, r) for r in range(TOPK)]
    pairs = _route_pairs()
    cand = [a[r] + b[c] for r, c in pairs]
    tau = jnp.full((PEER_HEADS, tt), -jnp.inf, F32)
    for xi in cand:
        cnt = jnp.zeros((PEER_HEADS, tt), F32)
        for xj in cand:
            cnt = cnt + jnp.where(xj >= xi, 1.0, 0.0)
        tau = jnp.maximum(tau, jnp.where(cnt >= TOPK, xi, -jnp.inf))
    top = a[0] + b[0]
    e = [jnp.where(xi >= tau, jnp.exp(xi - top), 0.0) for xi in cand]
    z = e[0]
    for ei in e[1:]:
        z = z + ei
    inv_z = 1.0 / z
    ids = [jnp.clip(ia[r] * float(N_KEYS) + ib[c], 0.0, float(N_KEYS * N_KEYS - 1)) for r, c in pairs]
    wts = [ek * inv_z for ek in e]
    pad = [jnp.zeros((PEER_HEADS, tt), F32)] * (ROUTE_SLOTS - len(pairs))
    ids_ref[...] = jnp.concatenate(ids + pad, axis=0).T.astype(jnp.int32)
    w_ref[...] = jnp.concatenate(wts + pad, axis=0).T


def _routes(st, tt, t0, tc):
    n, nk, _ = st.shape
    off = t0 // tt
    width = ROUTE_SLOTS * PEER_HEADS
    spec = pl.BlockSpec((tt, width), lambda i: (i, 0))
    return pl.pallas_call(
        _routes_kernel,
        grid=(tc // tt,),
        in_specs=[pl.BlockSpec((n, nk, tt), lambda i: (0, 0, i + off))],
        out_specs=[spec, spec],
        out_shape=[jax.ShapeDtypeStruct((tc, width), jnp.int32),
                   jax.ShapeDtypeStruct((tc, width), F32)],
        compiler_params=pltpu.CompilerParams(
            dimension_semantics=("arbitrary",), vmem_limit_bytes=VMEM_LIMIT),
        name="routes",
    )(st)


SC_LANES = 16
SC_TOKENS_PER_CHUNK = 8
PEER_CHUNKS = 8


def _gate_matrix(ids, w, n_exp, heads):
    T, E = ids.shape
    info = plsc.get_sparse_core_info()
    workers = info.num_cores * info.num_subcores
    per_worker = T // workers
    ch = SC_TOKENS_PER_CHUNK
    mesh = plsc.VectorSubcoreMesh(core_axis_name="c", subcore_axis_name="s")
    ids_flat = ids.reshape(T * E)
    w_flat = w.reshape(T * E)

    @functools.partial(
        pl.kernel, mesh=mesh,
        out_type=jax.ShapeDtypeStruct((T, n_exp), F32),
        scratch_types=[pltpu.VMEM((ch * E,), jnp.int32), pltpu.VMEM((ch * E,), F32),
                       pltpu.VMEM((n_exp,), F32)],
        compiler_params=pltpu.CompilerParams(needs_layout_passes=False),
        name="gate_matrix",
    )
    def scatter(ids_hbm, w_hbm, out_hbm, ids_v, w_v, row_v):
        wid = lax.axis_index("s") * info.num_cores + lax.axis_index("c")
        base = wid * per_worker
        zeros = jnp.zeros((SC_LANES,), F32)
        lane = lax.iota(jnp.int32, SC_LANES)

        @pl.loop(0, n_exp, step=SC_LANES)
        def _(i):
            row_v[pl.ds(i, SC_LANES)] = zeros

        @pl.loop(0, per_worker // ch)
        def _(ci):
            t0 = base + ci * ch
            pltpu.sync_copy(ids_hbm.at[pl.ds(t0 * E, ch * E)], ids_v)
            pltpu.sync_copy(w_hbm.at[pl.ds(t0 * E, ch * E)], w_v)
            for tl in range(ch):
                @pl.loop(0, heads)
                def _(h):
                    for m in range(E // heads // SC_LANES):
                        pos = (lane + (tl * E // heads + m * SC_LANES)) * heads + h
                        idx = plsc.load_gather(ids_v, [pos])
                        val = plsc.load_gather(w_v, [pos])
                        plsc.addupdate_scatter(row_v, [idx], val)

                pltpu.sync_copy(row_v, out_hbm.at[t0 + tl])

                @pl.loop(0, E, step=SC_LANES)
                def _(e0):
                    sl = pl.ds(tl * E + e0, SC_LANES)
                    plsc.store_scatter(row_v, [ids_v[sl]], zeros)

    return scatter(ids_flat, w_flat)


def _experts_kernel(xn_ref, u_ref, v_ref, gate_ref, h1_ref, nw_ref, o_ref, acc_ref):
    j = pl.program_id(1)
    nj = pl.num_programs(1)

    @pl.when(j == 0)
    def _():
        acc_ref[...] = jnp.zeros_like(acc_ref)

    h = _nt(xn_ref[...], u_ref[...])
    act = 0.5 * h * (1.0 + lax.erf(h * (2.0 ** -0.5)))
    acc_ref[...] += _nn((act * gate_ref[...]).astype(BF16), v_ref[...])

    @pl.when(j == nj - 1)
    def _():
        o_ref[...] = _rms(h1_ref[...] + acc_ref[...], nw_ref[...])


def _experts(xn, u, v, gate, h1, nw, tb, eb, t0):
    tc = gate.shape[0]
    D = xn.shape[1]
    n_exp = u.shape[0]
    off = t0 // tb
    return pl.pallas_call(
        _experts_kernel,
        grid=(tc // tb, n_exp // eb),
        in_specs=[
            pl.BlockSpec((tb, D), lambda i, j: (i + off, 0)),
            pl.BlockSpec((eb, D), lambda i, j: (j, 0)),
            pl.BlockSpec((eb, D), lambda i, j: (j, 0)),
            pl.BlockSpec((tb, eb), lambda i, j: (i, j)),
            pl.BlockSpec((tb, D), lambda i, j: (i + off, 0)),
            pl.BlockSpec((1, D), lambda i, j: (0, 0)),
        ],
        out_specs=pl.BlockSpec((tb, D), lambda i, j: (i, 0)),
        out_shape=jax.ShapeDtypeStruct((tc, D), F32),
        scratch_shapes=[pltpu.VMEM((tb, D), F32)],
        compiler_params=pltpu.CompilerParams(
            dimension_semantics=("arbitrary", "arbitrary"), vmem_limit_bytes=VMEM_LIMIT),
        name="experts",
    )(xn, u, v, gate, h1, nw)


def _block_select_matrix():
    part = jnp.arange(HGRN_SUB * HGRN_DH, dtype=jnp.int32) // HGRN_DH
    col = jnp.arange(HGRN_CHUNK, dtype=jnp.int32) % HGRN_SUB
    return (part[:, None] == col[None, :]).astype(BF16)


def _forward(x, norm_mix_w, w_in, hgrn_lb_logits, hgrn_norm_w, fox_f_bias, w_branch_hgrn,
             w_branch_fox, w_out, norm_ffn_w, peer_w_q, peer_sub_keys, peer_u, peer_v,
             norm_final_w, *, tm_in, tq, tm_merge, tt, tb, eb):
    B, S, D = x.shape
    T = B * S
    x2d = x.reshape(T, D)
    n_h = 4 * HGRN_W
    n_f = 3 * FOX_W
    wi = w_in[0]
    w_all = jnp.concatenate([wi[:, :n_h + n_f], wi[:, n_h + n_f + FOX_HEADS:]], axis=1).astype(BF16)
    wff = wi[:, n_h + n_f:n_h + n_f + FOX_HEADS]
    wff_row = wff.T.astype(BF16)
    fb = fox_f_bias[0].astype(F32)
    fb_row = jnp.broadcast_to(fb.reshape(FOX_HEADS, 1), (FOX_HEADS, LANES))

    hg, fox, gates, crow = _in_proj(
        x2d, norm_mix_w[0].reshape(1, D), w_all, wff_row, fb_row, B, S, tm_in)
    a = _hgrn(hg, hgrn_lb_logits, hgrn_norm_w[0].reshape(1, HGRN_W), _block_select_matrix(), B, S)
    b = _fox(fox, crow, B, S, tq)
    keys = peer_sub_keys[0].reshape(2 * PEER_HEADS, N_KEYS, PEER_HALF).astype(BF16)
    h1, xn2, st = _merge(
        x2d, a, b, gates, w_branch_hgrn[0].astype(BF16), w_branch_fox[0].astype(BF16),
        w_out[0].astype(BF16), norm_ffn_w[0].reshape(1, D), peer_w_q[0].astype(BF16), keys, tm_merge)
    n_exp = peer_u.shape[1]
    u_bf = peer_u[0].astype(BF16)
    v_bf = peer_v[0].astype(BF16)
    tc = T // PEER_CHUNKS
    outs = []
    for c in range(PEER_CHUNKS):
        ids, wts = _routes(st, tt, c * tc, tc)
        gate = _gate_matrix(ids, wts, n_exp, PEER_HEADS)
        outs.append(_experts(xn2, u_bf, v_bf, gate, h1, norm_final_w.reshape(1, D), tb, eb, c * tc))
    return jnp.concatenate(outs, axis=0).reshape(B, S, D)


def kernel(x, norm_mix_w, w_in, hgrn_lb_logits, hgrn_norm_w, fox_f_bias, w_branch_hgrn, w_branch_fox, w_out, norm_ffn_w, peer_w_q, peer_sub_keys, peer_u, peer_v, norm_final_w):
    return _forward(x, norm_mix_w, w_in, hgrn_lb_logits, hgrn_norm_w, fox_f_bias, w_branch_hgrn,
                    w_branch_fox, w_out, norm_ffn_w, peer_w_q, peer_sub_keys, peer_u, peer_v,
                    norm_final_w, tm_in=256, tq=512, tm_merge=256, tt=256, tb=1024, eb=1024)
```

```python
import functools
import math

import jax
import jax.numpy as jnp
from jax import lax
from jax.experimental import pallas as pl
from jax.experimental.pallas import tpu as pltpu
from jax.experimental.pallas import tpu_sc as plsc

F32 = jnp.float32
BF16 = jnp.bfloat16
RMS_EPS = 1e-6
NEG_BIG = -1e30

HGRN_HEADS = 4
HGRN_DH = 128
HGRN_W = HGRN_HEADS * HGRN_DH
FOX_HEADS = 8
FOX_DH = 64
FOX_W = FOX_HEADS * FOX_DH
PEER_HEADS = 8
PEER_HALF = 128
N_KEYS = 128
TOPK = 16
LANES = 128

VMEM_LIMIT = 56 * 1024 * 1024


def _nt(a, b):
    return lax.dot_general(a, b, (((1,), (1,)), ((), ())), preferred_element_type=F32)


def _tn(a, b):
    return lax.dot_general(a, b, (((0,), (0,)), ((), ())), preferred_element_type=F32)


def _nn(a, b):
    return jnp.dot(a, b, preferred_element_type=F32)


def _split_dot(fn, tri, x):
    hi = x.astype(BF16)
    lo = (x - hi.astype(F32)).astype(BF16)
    return fn(tri, hi) + fn(tri, lo)


def _log_sigmoid(x):
    return jnp.minimum(x, 0.0) - jnp.log1p(jnp.exp(-jnp.abs(x)))


def _rms(x, w):
    return x * lax.rsqrt(jnp.mean(x * x, axis=-1, keepdims=True) + RMS_EPS) * w


LOG2E = math.log2(math.e)


def _inproj_kernel(x_ref, nw_ref, w_ref, wffr_ref, fbr_ref,
                   hg_ref, fox_ref, gate_ref, crow_ref, carry_row):
    i = pl.program_id(1)
    tm = x_ref.shape[0]

    @pl.when(i == 0)
    def _():
        carry_row[...] = jnp.zeros_like(carry_row)

    xn = _rms(x_ref[...], nw_ref[...]).astype(BF16)
    proj = _nn(xn, w_ref[...])
    n_h = 4 * HGRN_W
    hg_ref[...] = proj[:, :n_h]
    fox_ref[:, :FOX_W] = (proj[:, n_h:n_h + FOX_W] * (FOX_DH ** -0.5 * LOG2E)).astype(BF16)
    fox_ref[:, FOX_W:] = proj[:, n_h + FOX_W:n_h + 3 * FOX_W].astype(BF16)
    gate_ref[...] = jax.nn.sigmoid(proj[:, n_h + 3 * FOX_W:])

    r = lax.broadcasted_iota(jnp.int32, (tm, tm), 0)
    c = lax.broadcasted_iota(jnp.int32, (tm, tm), 1)
    triu = (r <= c).astype(BF16)
    ls_row = _log_sigmoid(_nt(wffr_ref[...], xn) + fbr_ref[:, 0:1]) * LOG2E
    hi = ls_row.astype(BF16)
    lo = (ls_row - hi.astype(F32)).astype(BF16)
    crow = _nn(hi, triu) + _nn(lo, triu) + carry_row[:, 0:1]
    crow_ref[...] = crow
    carry_row[...] = jnp.broadcast_to(crow[:, tm - 1:tm], carry_row.shape)


def _in_proj(x2d, nw, w_all, wff_row, fb_row, batch, seq, tm):
    T, D = x2d.shape
    nt = seq // tm
    n_all = w_all.shape[1]
    row = lambda b, i: (b * nt + i, 0)
    const = lambda b, i: (0, 0)
    return pl.pallas_call(
        _inproj_kernel,
        grid=(batch, nt),
        in_specs=[
            pl.BlockSpec((tm, D), row),
            pl.BlockSpec((1, D), const),
            pl.BlockSpec((D, n_all), const),
            pl.BlockSpec((FOX_HEADS, D), const),
            pl.BlockSpec((FOX_HEADS, LANES), const),
        ],
        out_specs=[
            pl.BlockSpec((tm, 4 * HGRN_W), row),
            pl.BlockSpec((tm, 3 * FOX_W), row),
            pl.BlockSpec((tm, 2 * D), row),
            pl.BlockSpec((FOX_HEADS, tm), lambda b, i: (0, b * nt + i)),
        ],
        out_shape=[
            jax.ShapeDtypeStruct((T, 4 * HGRN_W), F32),
            jax.ShapeDtypeStruct((T, 3 * FOX_W), BF16),
            jax.ShapeDtypeStruct((T, 2 * D), F32),
            jax.ShapeDtypeStruct((FOX_HEADS, T), F32),
        ],
        scratch_shapes=[pltpu.VMEM((FOX_HEADS, LANES), F32)],
        compiler_params=pltpu.CompilerParams(
            dimension_semantics=("arbitrary", "arbitrary"), vmem_limit_bytes=VMEM_LIMIT),
        name="in_proj",
    )(x2d, nw, w_all, wff_row, fb_row)


HGRN_CHUNK = 128
HGRN_SUB = 8


def _hgrn_levels():
    out = []
    m = HGRN_SUB
    while m < HGRN_CHUNK:
        out.append(m)
        m *= 2
    return out


def _hgrn_masks():
    C, sub = HGRN_CHUNK, HGRN_SUB
    r = jnp.arange(C, dtype=jnp.int32)[:, None]
    c = jnp.arange(C, dtype=jnp.int32)[None, :]
    masks = [((r // sub) == (c // sub)) & ((c % sub) <= (r % sub))]
    for m in _hgrn_levels():
        masks.append(((r // (2 * m)) == (c // (2 * m))) & (((r // m) % 2) == 1) & (((c // m) % 2) == 0))
    return jnp.stack(masks).astype(F32)


def _hgrn_kernel(hg_ref, lbl_ref, nw_ref, rsel_ref, tril_ref, mask_ref, a_ref, state_ref):
    ci = pl.program_id(1)
    C = HGRN_CHUNK
    dh = HGRN_DH
    sub = HGRN_SUB

    @pl.when(ci == 0)
    def _():
        state_ref[...] = jnp.zeros_like(state_ref)

    lg = lbl_ref[...]
    e = jnp.exp(lg - jnp.max(lg, axis=0, keepdims=True))
    lb_all = e[0:1, :] / jnp.sum(e, axis=0, keepdims=True)
    tril = tril_ref[...]

    for h in range(HGRN_HEADS):
        sl = slice(h * dh, (h + 1) * dh)
        qraw = hg_ref[:, sl]
        q = qraw * jax.nn.sigmoid(qraw)
        lb = lb_all[:, sl]
        f = lb + (1.0 - lb) * jax.nn.sigmoid(hg_ref[:, HGRN_W + h * dh:HGRN_W + (h + 1) * dh])
        log2f = jnp.log(f) * LOG2E
        k = 1.0 - f
        v = hg_ref[:, 2 * HGRN_W + h * dh:2 * HGRN_W + (h + 1) * dh]
        g = hg_ref[:, 3 * HGRN_W + h * dh:3 * HGRN_W + (h + 1) * dh]
        v_bf = v.astype(BF16)
        cum = _split_dot(_nn, tril, log2f)

        nb = C // sub
        q3 = q.reshape(nb, sub, dh)
        k3 = k.reshape(nb, sub, dh)
        c3 = cum.reshape(nb, sub, dh)
        parts = []
        for s in range(sub):
            kb = jnp.broadcast_to(k3[:, s:s + 1, :], (nb, sub, dh))
            cb = jnp.broadcast_to(c3[:, s:s + 1, :], (nb, sub, dh))
            es = q3 * kb * jnp.exp2(jnp.minimum(c3 - cb, 0.0))
            parts.append(es.reshape(C, dh).astype(BF16))
        p_mat = mask_ref[0] * _nn(jnp.concatenate(parts, axis=1), rsel_ref[...])

        for li, m in enumerate(_hgrn_levels()):
            nbm = C // m
            qm = q.reshape(nbm, m, dh)
            km = k.reshape(nbm, m, dh)
            cm = cum.reshape(nbm, m, dh)
            end = cm[:, m - 1:m, :]
            prev_end = jnp.concatenate([jnp.zeros((1, 1, dh), F32), end[:nbm - 1]], axis=0)
            qd = qm * jnp.exp2(jnp.minimum(cm - jnp.broadcast_to(prev_end, (nbm, m, dh)), 0.0))
            kd = km * jnp.exp2(jnp.minimum(jnp.broadcast_to(end, (nbm, m, dh)) - cm, 0.0))
            sc = _nt(qd.reshape(C, dh).astype(BF16), kd.reshape(C, dh).astype(BF16))
            p_mat = p_mat + mask_ref[1 + li] * sc

        st = state_ref[h]
        o = _nn(p_mat.astype(BF16), v_bf) + _nt((q * jnp.exp2(cum)).astype(BF16), st.astype(BF16))
        last = cum[C - 1:C, :]
        kdec = (k * jnp.exp2(last - cum)).astype(BF16)
        state_ref[h] = jnp.exp2(last) * st + _tn(v_bf, kdec)

        o = o * lax.rsqrt(jnp.mean(o * o, axis=-1, keepdims=True) + RMS_EPS) * nw_ref[:, sl]
        a_ref[:, sl] = (o * (g * jax.nn.sigmoid(g))).astype(BF16)


def _hgrn(hg, lb_logits, norm_w, rsel, batch, seq):
    T = hg.shape[0]
    C = HGRN_CHUNK
    nc = seq // C
    row = lambda b, i: (b * nc + i, 0)
    const = lambda b, i: (0, 0)
    tril = jnp.tril(jnp.ones((C, C), F32)).astype(BF16)
    masks = _hgrn_masks()
    return pl.pallas_call(
        _hgrn_kernel,
        grid=(batch, nc),
        in_specs=[
            pl.BlockSpec((C, 4 * HGRN_W), row),
            pl.BlockSpec(lb_logits.shape, const),
            pl.BlockSpec((1, HGRN_W), const),
            pl.BlockSpec(rsel.shape, const),
            pl.BlockSpec((C, C), const),
            pl.BlockSpec(masks.shape, lambda b, i: (0, 0, 0)),
        ],
        out_specs=pl.BlockSpec((C, HGRN_W), row),
        out_shape=jax.ShapeDtypeStruct((T, HGRN_W), BF16),
        scratch_shapes=[pltpu.VMEM((HGRN_HEADS, HGRN_DH, HGRN_DH), F32)],
        compiler_params=pltpu.CompilerParams(
            dimension_semantics=("arbitrary", "arbitrary"), vmem_limit_bytes=VMEM_LIMIT),
        name="hgrn",
    )(hg, lb_logits, norm_w, rsel, tril, masks)


def _fox_kernel(q_ref, k_ref, v_ref, crow_ref, o_ref, m_ref, acc_ref, qs_ref, vs_ref):
    qi = pl.program_id(1)
    tq = q_ref.shape[0]
    tk = tq
    pairs = FOX_HEADS // 2
    lane = lax.broadcasted_iota(jnp.int32, (1, LANES), 1)
    low = lane < FOX_DH
    sel = (jnp.where(low, 1.0, 0.0).astype(BF16), jnp.where(low, 0.0, 1.0).astype(BF16))
    ones_lane = (FOX_DH, 0)
    one_hot = tuple(jnp.where(lane == ol, 1.0, 0.0).astype(BF16) for ol in ones_lane)
    rr = lax.broadcasted_iota(jnp.int32, (tq, tk), 0)
    cc = lax.broadcasted_iota(jnp.int32, (tq, tk), 1)
    causal = cc <= rr

    @pl.when(qi == 0)
    def _():
        for p in range(pairs):
            vb = v_ref[:, p * LANES:(p + 1) * LANES]
            for hh in range(2):
                vs_ref[2 * p + hh] = vb * sel[hh] + one_hot[hh]

    for p in range(pairs):
        q = q_ref[:, p * LANES:(p + 1) * LANES]
        for hh in range(2):
            qs_ref[2 * p + hh] = q * sel[hh]
    m_ref[...] = jnp.full_like(m_ref, NEG_BIG)
    acc_ref[...] = jnp.zeros_like(acc_ref)

    def kv_block(j, masked):
        start = pl.multiple_of(j * tk, tk)
        for p in range(pairs):
            kb = k_ref[pl.ds(start, tk), p * LANES:(p + 1) * LANES]
            for hh in range(2):
                h = 2 * p + hh
                ck = crow_ref[h:h + 1, pl.ds(start, tk)]
                s = _nt(qs_ref[h], kb) - ck
                if masked:
                    s = jnp.where(causal, s, NEG_BIG)
                m_prev = m_ref[h]
                m_next = jnp.maximum(m_prev, jnp.max(s, axis=1, keepdims=True))
                pexp = jnp.exp2(s - jnp.concatenate([m_next] * (tk // LANES), axis=1))
                alpha = jnp.exp2(m_prev - m_next)
                acc_ref[h] = alpha * acc_ref[h] + _nn(pexp.astype(BF16), vs_ref[h, pl.ds(start, tk), :])
                m_ref[h] = m_next

    def body(j, carry):
        kv_block(j, False)
        return carry

    lax.fori_loop(0, qi, body, 0)
    kv_block(qi, True)

    for p in range(pairs):
        a0 = acc_ref[2 * p]
        a1 = acc_ref[2 * p + 1]
        o0 = a0 / a0[:, ones_lane[0]:ones_lane[0] + 1]
        o1 = a1 / a1[:, ones_lane[1]:ones_lane[1] + 1]
        o_ref[:, p * LANES:(p + 1) * LANES] = jnp.where(low, o0, o1).astype(BF16)


def _fox(fox, crow, batch, seq, tq):
    T = fox.shape[0]
    nq = seq // tq
    return pl.pallas_call(
        _fox_kernel,
        grid=(batch, nq),
        in_specs=[
            pl.BlockSpec((tq, FOX_W), lambda b, i: (b * nq + i, 0)),
            pl.BlockSpec((seq, FOX_W), lambda b, i: (b, 1)),
            pl.BlockSpec((seq, FOX_W), lambda b, i: (b, 2)),
            pl.BlockSpec((FOX_HEADS, seq), lambda b, i: (0, b)),
        ],
        out_specs=pl.BlockSpec((tq, FOX_W), lambda b, i: (b * nq + i, 0)),
        out_shape=jax.ShapeDtypeStruct((T, FOX_W), BF16),
        scratch_shapes=[pltpu.VMEM((FOX_HEADS, tq, LANES), F32), pltpu.VMEM((FOX_HEADS, tq, LANES), F32),
                        pltpu.VMEM((FOX_HEADS, tq, LANES), BF16), pltpu.VMEM((FOX_HEADS, seq, LANES), BF16)],
        compiler_params=pltpu.CompilerParams(
            dimension_semantics=("arbitrary", "arbitrary"), vmem_limit_bytes=VMEM_LIMIT),
        name="fox",
    )(fox, fox, fox, crow)


def _merge_kernel(x_ref, a_ref, b_ref, gate_ref, wa_ref, wb_ref, wo_ref, nw_ref, wq_ref, keys_ref,
                  h1_ref, xn_ref, st_ref):
    D = x_ref.shape[1]
    merged = gate_ref[:, :D] * _nn(a_ref[...], wa_ref[...]) + gate_ref[:, D:] * _nn(b_ref[...], wb_ref[...])
    h1 = x_ref[...] + _nn(merged.astype(BF16), wo_ref[...])
    h1_ref[...] = h1
    xn = _rms(h1, nw_ref[...]).astype(BF16)
    xn_ref[...] = xn
    q = _nn(xn, wq_ref[...]).astype(BF16)
    for hp in range(2 * PEER_HEADS):
        st_ref[hp] = _nt(keys_ref[hp], q[:, hp * PEER_HALF:(hp + 1) * PEER_HALF])


def _merge(x2d, a, b, gates, wa, wb, wo, nw, wq, keys, tm):
    T, D = x2d.shape
    row = lambda i: (i, 0)
    const = lambda i: (0, 0)
    return pl.pallas_call(
        _merge_kernel,
        grid=(T // tm,),
        in_specs=[
            pl.BlockSpec((tm, D), row),
            pl.BlockSpec((tm, HGRN_W), row),
            pl.BlockSpec((tm, FOX_W), row),
            pl.BlockSpec((tm, 2 * D), row),
            pl.BlockSpec(wa.shape, const),
            pl.BlockSpec(wb.shape, const),
            pl.BlockSpec(wo.shape, const),
            pl.BlockSpec((1, D), const),
            pl.BlockSpec(wq.shape, const),
            pl.BlockSpec(keys.shape, lambda i: (0, 0, 0)),
        ],
        out_specs=[
            pl.BlockSpec((tm, D), row),
            pl.BlockSpec((tm, D), row),
            pl.BlockSpec((2 * PEER_HEADS, N_KEYS, tm), lambda i: (0, 0, i)),
        ],
        out_shape=[
            jax.ShapeDtypeStruct((T, D), F32),
            jax.ShapeDtypeStruct((T, D), BF16),
            jax.ShapeDtypeStruct((2 * PEER_HEADS, N_KEYS, T), F32),
        ],
        compiler_params=pltpu.CompilerParams(
            dimension_semantics=("arbitrary",), vmem_limit_bytes=VMEM_LIMIT),
        name="merge",
    )(x2d, a, b, gates, wa, wb, wo, nw, wq, keys)


def _rank_pairs():
    n = TOPK + 1
    return [(r, c) for r in range(n) for c in range(n) if (r + 1) * (c + 1) <= n]


def _topk_kernel(st_ref, cnt_ref, p1_ref, rank_ref, p2_ref, rank1_ref):
    tt = st_ref.shape[2]
    n_top = TOPK + 1

    def top_sorted(s):
        vals = []
        cur = s
        rank = jnp.full(s.shape, float(n_top), F32)
        for r in range(n_top):
            mx = jnp.max(cur, axis=0, keepdims=True)
            vals.append(mx)
            hit = cur == mx
            rank = jnp.where(hit, float(r), rank)
            cur = jnp.where(hit, -jnp.inf, cur)
        return vals, rank

    tops = []
    for h in range(PEER_HEADS):
        v1, r1 = top_sorted(st_ref[2 * h])
        v2, r2 = top_sorted(st_ref[2 * h + 1])
        rank1_ref[h] = r1
        rank_ref[h] = r2
        tops.append((v1, v2))
    a = [jnp.concatenate([tops[h][0][r] for h in range(PEER_HEADS)], axis=0) for r in range(n_top)]
    b = [jnp.concatenate([tops[h][1][r] for h in range(PEER_HEADS)], axis=0) for r in range(n_top)]
    cand = [a[r] + b[c] for r, c in _rank_pairs()]
    tau = jnp.full((PEER_HEADS, tt), -jnp.inf, F32)
    nxt = jnp.full((PEER_HEADS, tt), -jnp.inf, F32)
    for xi in cand:
        cnt = jnp.zeros((PEER_HEADS, tt), F32)
        for xj in cand:
            cnt = cnt + jnp.where(xj >= xi, 1.0, 0.0)
        tau = jnp.maximum(tau, jnp.where(cnt >= TOPK, xi, -jnp.inf))
        nxt = jnp.maximum(nxt, jnp.where(cnt >= TOPK + 1, xi, -jnp.inf))
    top = a[0] + b[0]
    z = jnp.zeros((PEER_HEADS, tt), F32)
    for xi in cand:
        z = z + jnp.where(xi >= tau, jnp.exp(xi - top), 0.0)
    inv_z = 1.0 / z
    cut = 0.5 * (tau + nxt)
    n_sel = [jnp.zeros((PEER_HEADS, tt), F32) for _ in range(n_top)]
    for r, c in _rank_pairs():
        n_sel[r] = n_sel[r] + jnp.where(a[r] + b[c] >= cut, 1.0, 0.0)
    for h in range(PEER_HEADS):
        s1 = st_ref[2 * h]
        s2 = st_ref[2 * h + 1]
        r1 = rank1_ref[h]
        cnt = jnp.zeros(r1.shape, F32)
        for r in range(n_top):
            cnt = jnp.where(r1 == float(r), n_sel[r][h:h + 1, :], cnt)
        cnt_ref[h] = cnt
        p1_ref[h] = jnp.exp(s1 - a[0][h:h + 1, :]) * inv_z[h:h + 1, :]
        p2_ref[h] = jnp.exp(s2 - b[0][h:h + 1, :])


def _topk(st, tt):
    n, nk, T = st.shape
    out32 = jax.ShapeDtypeStruct((PEER_HEADS, nk, T), F32)
    spec = pl.BlockSpec((PEER_HEADS, nk, tt), lambda i: (0, 0, i))
    return pl.pallas_call(
        _topk_kernel,
        grid=(T // tt,),
        in_specs=[pl.BlockSpec((n, nk, tt), lambda i: (0, 0, i))],
        out_specs=[spec, spec, spec, spec],
        out_shape=[out32, out32, out32, out32],
        scratch_shapes=[pltpu.VMEM((PEER_HEADS, nk, tt), F32)],
        compiler_params=pltpu.CompilerParams(
            dimension_semantics=("arbitrary",), vmem_limit_bytes=VMEM_LIMIT),
        name="topk",
    )(st)


PEER_KEY_TILE = 64
PEER_TOK_TILE = 128
PEER_PAIR = 2
SUBLANES = 8


def _peer_kernel(xn_ref, u_ref, v_ref, cnt_ref, p1_ref, rank_ref, p2_ref, h1_ref, nw_ref,
                 o_ref, acc_ref, ht_ref, g_ref, cnt_rows, p1_rows):
    j = pl.program_id(1)
    nj = pl.num_programs(1)
    eb, tb = ht_ref.shape
    groups = eb // N_KEYS
    cur = j % 2
    jb = jnp.minimum(j, nj - 2)

    @pl.when(j == 0)
    def _():
        acc_ref[...] = jnp.zeros_like(acc_ref)
        g_ref[...] = jnp.zeros_like(g_ref)

    acc_ref[...] += _tn(g_ref[1 - cur], v_ref[...])

    ht_ref[...] = _nt(u_ref[...], xn_ref[...])
    kt = PEER_KEY_TILE // SUBLANES
    for g0 in range(0, groups, PEER_PAIR):
        for gi in range(PEER_PAIR):
            i1 = jb * groups + g0 + gi
            for h in range(PEER_HEADS):
                r = gi * PEER_HEADS + h
                cnt_rows[r] = jnp.broadcast_to(cnt_ref[h, pl.ds(i1, 1), :], (SUBLANES, tb))
                p1_rows[r] = jnp.broadcast_to(p1_ref[h, pl.ds(i1, 1), :], (SUBLANES, tb))
        for t0 in range(0, tb, PEER_TOK_TILE):
            ts = slice(t0, t0 + PEER_TOK_TILE)
            for k0 in range(0, N_KEYS, PEER_KEY_TILE):
                ks = slice(k0, k0 + PEER_KEY_TILE)
                gate = [None] * PEER_PAIR
                for h in range(PEER_HEADS):
                    rk = rank_ref[h, ks, ts]
                    p2 = p2_ref[h, ks, ts]
                    for gi in range(PEER_PAIR):
                        r = gi * PEER_HEADS + h
                        cn = jnp.concatenate([cnt_rows[r, :, ts]] * kt, axis=0)
                        p1 = jnp.concatenate([p1_rows[r, :, ts]] * kt, axis=0)
                        term = jnp.where(rk < cn, p2, 0.0) * p1
                        gate[gi] = term if gate[gi] is None else gate[gi] + term
                for gi in range(PEER_PAIR):
                    es = slice((g0 + gi) * N_KEYS + k0, (g0 + gi) * N_KEYS + k0 + PEER_KEY_TILE)
                    ht = ht_ref[es, ts]
                    act = 0.5 * ht * (1.0 + lax.erf(ht * (2.0 ** -0.5)))
                    g_ref[cur, es, ts] = (act * gate[gi]).astype(BF16)

    @pl.when(j == nj - 1)
    def _():
        o_ref[...] = _rms(h1_ref[...] + acc_ref[...], nw_ref[...])


def _peer(xn, u, v, cnt, p1, rank2, p2, h1, nw, tb, eb):
    T, D = xn.shape
    n_exp = u.shape[0]
    tok3 = lambda i, j: (0, 0, i)
    nb = n_exp // eb
    return pl.pallas_call(
        _peer_kernel,
        grid=(T // tb, nb + 1),
        in_specs=[
            pl.BlockSpec((tb, D), lambda i, j: (i, 0)),
            pl.BlockSpec((eb, D), lambda i, j: (jnp.minimum(j, nb - 1), 0)),
            pl.BlockSpec((eb, D), lambda i, j: (jnp.maximum(j - 1, 0), 0)),
            pl.BlockSpec((PEER_HEADS, N_KEYS, tb), tok3),
            pl.BlockSpec((PEER_HEADS, N_KEYS, tb), tok3),
            pl.BlockSpec((PEER_HEADS, N_KEYS, tb), tok3),
            pl.BlockSpec((PEER_HEADS, N_KEYS, tb), tok3),
            pl.BlockSpec((tb, D), lambda i, j: (i, 0)),
            pl.BlockSpec((1, D), lambda i, j: (0, 0)),
        ],
        out_specs=pl.BlockSpec((tb, D), lambda i, j: (i, 0)),
        out_shape=jax.ShapeDtypeStruct((T, D), F32),
        scratch_shapes=[pltpu.VMEM((tb, D), F32), pltpu.VMEM((eb, tb), F32), pltpu.VMEM((2, eb, tb), BF16),
                        pltpu.VMEM((PEER_PAIR * PEER_HEADS, SUBLANES, tb), F32),
                        pltpu.VMEM((PEER_PAIR * PEER_HEADS, SUBLANES, tb), F32)],
        compiler_params=pltpu.CompilerParams(
            dimension_semantics=("arbitrary", "arbitrary"), vmem_limit_bytes=VMEM_LIMIT),
        name="peer",
    )(xn, u, v, cnt, p1, rank2, p2, h1, nw)


ROUTE_SLOTS = 64


def _route_pairs():
    return [(r, c) for r in range(TOPK) for c in range(TOPK) if (r + 1) * (c + 1) <= TOPK]


def _routes_kernel(st_ref, ids_ref, w_ref):
    tt = st_ref.shape[2]
    key_id = lax.broadcasted_iota(jnp.int32, (N_KEYS, tt), 0).astype(F32)

    def top_sorted(s):
        vals, idxs = [], []
        cur = s
        for _ in range(TOPK):
            mx = jnp.max(cur, axis=0, keepdims=True)
            hit = cur == mx
            vals.append(mx)
            idxs.append(jnp.max(jnp.where(hit, key_id, -1.0), axis=0, keepdims=True))
            cur = jnp.where(hit, -jnp.inf, cur)
        return vals, idxs

    tops = [(top_sorted(st_ref[2 * h]), top_sorted(st_ref[2 * h + 1])) for h in range(PEER_HEADS)]

    def stack(half, which, r):
        return jnp.concatenate([tops[h][half][which][r] for h in range(PEER_HEADS)], axis=0)

    a = [stack(0, 0, r) for r in range(TOPK)]
    b = [stack(1, 0, r) for r in range(TOPK)]
    ia = [stack(0, 1, r) for r in range(TOPK)]
    ib = [stack(1, 1, r) for r in range(TOPK)]
    pairs = _route_pairs()
    cand = [a[r] + b[c] for r, c in pairs]
    tau = jnp.full((PEER_HEADS, tt), -jnp.inf, F32)
    for xi in cand:
        cnt = jnp.zeros((PEER_HEADS, tt), F32)
        for xj in cand:
            cnt = cnt + jnp.where(xj >= xi, 1.0, 0.0)
        tau = jnp.maximum(tau, jnp.where(cnt >= TOPK, xi, -jnp.inf))
    top = a[0] + b[0]
    e = [jnp.where(xi >= tau, jnp.exp(xi - top), 0.0) for xi in cand]
    z = e[0]
    for ei in e[1:]:
        z = z + ei
    inv_z = 1.0 / z
    ids = [jnp.clip(ia[r] * float(N_KEYS) + ib[c], 0.0, float(N_KEYS * N_KEYS - 1)) for r, c in pairs]
    wts = [ek * inv_z for ek in e]
    pad = [jnp.zeros((PEER_HEADS, tt), F32)] * (ROUTE_SLOTS - len(pairs))
    ids_ref[...] = jnp.concatenate(ids + pad, axis=0).T.astype(jnp.int32)
    w_ref[...] = jnp.concatenate(wts + pad, axis=0).T


def _routes(st, tt, t0, tc):
    n, nk, _ = st.shape
    off = t0 // tt
    width = ROUTE_SLOTS * PEER_HEADS
    spec = pl.BlockSpec((tt, width), lambda i: (i, 0))
    return pl.pallas_call(
        _routes_kernel,
        grid=(tc // tt,),
        in_specs=[pl.BlockSpec((n, nk, tt), lambda i: (0, 0, i + off))],
        out_specs=[spec, spec],
        out_shape=[jax.ShapeDtypeStruct((tc, width), jnp.int32),
                   jax.ShapeDtypeStruct((tc, width), F32)],
        compiler_params=pltpu.CompilerParams(
            dimension_semantics=("arbitrary",), vmem_limit_bytes=VMEM_LIMIT),
        name="routes",
    )(st)


SC_LANES = 16
SC_TOKENS_PER_CHUNK = 8
PEER_CHUNKS = 8


def _gate_matrix(ids, w, n_exp, heads):
    T, E = ids.shape
    info = plsc.get_sparse_core_info()
    workers = info.num_cores * info.num_subcores
    per_worker = T // workers
    ch = SC_TOKENS_PER_CHUNK
    mesh = plsc.VectorSubcoreMesh(core_axis_name="c", subcore_axis_name="s")
    ids_flat = ids.reshape(T * E)
    w_flat = w.reshape(T * E)

    @functools.partial(
        pl.kernel, mesh=mesh,
        out_type=jax.ShapeDtypeStruct((T, n_exp), F32),
        scratch_types=[pltpu.VMEM((ch * E,), jnp.int32), pltpu.VMEM((ch * E,), F32),
                       pltpu.VMEM((n_exp,), F32)],
        compiler_params=pltpu.CompilerParams(needs_layout_passes=False),
        name="gate_matrix",
    )
    def scatter(ids_hbm, w_hbm, out_hbm, ids_v, w_v, row_v):
        wid = lax.axis_index("s") * info.num_cores + lax.axis_index("c")
        base = wid * per_worker
        zeros = jnp.zeros((SC_LANES,), F32)
        lane = lax.iota(jnp.int32, SC_LANES)

        @pl.loop(0, n_exp, step=SC_LANES)
        def _(i):
            row_v[pl.ds(i, SC_LANES)] = zeros

        @pl.loop(0, per_worker // ch)
        def _(ci):
            t0 = base + ci * ch
            pltpu.sync_copy(ids_hbm.at[pl.ds(t0 * E, ch * E)], ids_v)
            pltpu.sync_copy(w_hbm.at[pl.ds(t0 * E, ch * E)], w_v)
            for tl in range(ch):
                @pl.loop(0, heads)
                def _(h):
                    for m in range(E // heads // SC_LANES):
                        pos = (lane + (tl * E // heads + m * SC_LANES)) * heads + h
                        idx = plsc.load_gather(ids_v, [pos])
                        val = plsc.load_gather(w_v, [pos])
                        plsc.addupdate_scatter(row_v, [idx], val)

                pltpu.sync_copy(row_v, out_hbm.at[t0 + tl])

                @pl.loop(0, E, step=SC_LANES)
                def _(e0):
                    sl = pl.ds(tl * E + e0, SC_LANES)
                    plsc.store_scatter(row_v, [ids_v[sl]], zeros)

    return scatter(ids_flat, w_flat)


def _experts_kernel(xn_ref, u_ref, v_ref, gate_ref, h1_ref, nw_ref, o_ref, acc_ref):
    j = pl.program_id(1)
    nj = pl.num_programs(1)

    @pl.when(j == 0)
    def _():
        acc_ref[...] = jnp.zeros_like(acc_ref)

    h = _nt(xn_ref[...], u_ref[...])
    act = 0.5 * h * (1.0 + lax.erf(h * (2.0 ** -0.5)))
    acc_ref[...] += _nn((act * gate_ref[...]).astype(BF16), v_ref[...])

    @pl.when(j == nj - 1)
    def _():
        o_ref[...] = _rms(h1_ref[...] + acc_ref[...], nw_ref[...])


def _experts(xn, u, v, gate, h1, nw, tb, eb, t0):
    tc = gate.shape[0]
    D = xn.shape[1]
    n_exp = u.shape[0]
    off = t0 // tb
    return pl.pallas_call(
        _experts_kernel,
        grid=(tc // tb, n_exp // eb),
        in_specs=[
            pl.BlockSpec((tb, D), lambda i, j: (i + off, 0)),
            pl.BlockSpec((eb, D), lambda i, j: (j, 0)),
            pl.BlockSpec((eb, D), lambda i, j: (j, 0)),
            pl.BlockSpec((tb, eb), lambda i, j: (i, j)),
            pl.BlockSpec((tb, D), lambda i, j: (i + off, 0)),
            pl.BlockSpec((1, D), lambda i, j: (0, 0)),
        ],
        out_specs=pl.BlockSpec((tb, D), lambda i, j: (i, 0)),
        out_shape=jax.ShapeDtypeStruct((tc, D), F32),
        scratch_shapes=[pltpu.VMEM((tb, D), F32)],
        compiler_params=pltpu.CompilerParams(
            dimension_semantics=("arbitrary", "arbitrary"), vmem_limit_bytes=VMEM_LIMIT),
        name="experts",
    )(xn, u, v, gate, h1, nw)


def _block_select_matrix():
    part = jnp.arange(HGRN_SUB * HGRN_DH, dtype=jnp.int32) // HGRN_DH
    col = jnp.arange(HGRN_CHUNK, dtype=jnp.int32) % HGRN_SUB
    return (part[:, None] == col[None, :]).astype(BF16)


def _forward(x, norm_mix_w, w_in, hgrn_lb_logits, hgrn_norm_w, fox_f_bias, w_branch_hgrn,
             w_branch_fox, w_out, norm_ffn_w, peer_w_q, peer_sub_keys, peer_u, peer_v,
             norm_final_w, *, tm_in, tq, tm_merge, tt, tb, eb):
    B, S, D = x.shape
    T = B * S
    x2d = x.reshape(T, D)
    n_h = 4 * HGRN_W
    n_f = 3 * FOX_W
    wi = w_in[0]
    w_all = jnp.concatenate([wi[:, :n_h + n_f], wi[:, n_h + n_f + FOX_HEADS:]], axis=1).astype(BF16)
    wff = wi[:, n_h + n_f:n_h + n_f + FOX_HEADS]
    wff_row = wff.T.astype(BF16)
    fb = fox_f_bias[0].astype(F32)
    fb_row = jnp.broadcast_to(fb.reshape(FOX_HEADS, 1), (FOX_HEADS, LANES))

    hg, fox, gates, crow = _in_proj(
        x2d, norm_mix_w[0].reshape(1, D), w_all, wff_row, fb_row, B, S, tm_in)
    a = _hgrn(hg, hgrn_lb_logits, hgrn_norm_w[0].reshape(1, HGRN_W), _block_select_matrix(), B, S)
    b = _fox(fox, crow, B, S, tq)
    keys = peer_sub_keys[0].reshape(2 * PEER_HEADS, N_KEYS, PEER_HALF).astype(BF16)
    h1, xn2, st = _merge(
        x2d, a, b, gates, w_branch_hgrn[0].astype(BF16), w_branch_fox[0].astype(BF16),
        w_out[0].astype(BF16), norm_ffn_w[0].reshape(1, D), peer_w_q[0].astype(BF16), keys, tm_merge)
    n_exp = peer_u.shape[1]
    u_bf = peer_u[0].astype(BF16)
    v_bf = peer_v[0].astype(BF16)
    tc = T // PEER_CHUNKS
    outs = []
    for c in range(PEER_CHUNKS):
        ids, wts = _routes(st, tt, c * tc, tc)
        gate = _gate_matrix(ids, wts, n_exp, PEER_HEADS)
        outs.append(_experts(xn2, u_bf, v_bf, gate, h1, norm_final_w.reshape(1, D), tb, eb, c * tc))
    return jnp.concatenate(outs, axis=0).reshape(B, S, D)


def kernel(x, norm_mix_w, w_in, hgrn_lb_logits, hgrn_norm_w, fox_f_bias, w_branch_hgrn, w_branch_fox, w_out, norm_ffn_w, peer_w_q, peer_sub_keys, peer_u, peer_v, norm_final_w):
    return _forward(x, norm_mix_w, w_in, hgrn_lb_logits, hgrn_norm_w, fox_f_bias, w_branch_hgrn,
                    w_branch_fox, w_out, norm_ffn_w, peer_w_q, peer_sub_keys, peer_u, peer_v,
                    norm_final_w, tm_in=256, tq=512, tm_merge=256, tt=256, tb=1024, eb=1024)
```

```python
import functools
import math

import jax
import jax.numpy as jnp
from jax import lax
from jax.experimental import pallas as pl
from jax.experimental.pallas import tpu as pltpu
from jax.experimental.pallas import tpu_sc as plsc

F32 = jnp.float32
BF16 = jnp.bfloat16
RMS_EPS = 1e-6
NEG_BIG = -1e30

HGRN_HEADS = 4
HGRN_DH = 128
HGRN_W = HGRN_HEADS * HGRN_DH
FOX_HEADS = 8
FOX_DH = 64
FOX_W = FOX_HEADS * FOX_DH
PEER_HEADS = 8
PEER_HALF = 128
N_KEYS = 128
TOPK = 16
LANES = 128

VMEM_LIMIT = 56 * 1024 * 1024


def _nt(a, b):
    return lax.dot_general(a, b, (((1,), (1,)), ((), ())), preferred_element_type=F32)


def _tn(a, b):
    return lax.dot_general(a, b, (((0,), (0,)), ((), ())), preferred_element_type=F32)


def _nn(a, b):
    return jnp.dot(a, b, preferred_element_type=F32)


def _split_dot(fn, tri, x):
    hi = x.astype(BF16)
    lo = (x - hi.astype(F32)).astype(BF16)
    return fn(tri, hi) + fn(tri, lo)


def _log_sigmoid(x):
    return jnp.minimum(x, 0.0) - jnp.log1p(jnp.exp(-jnp.abs(x)))


def _rms(x, w):
    return x * lax.rsqrt(jnp.mean(x * x, axis=-1, keepdims=True) + RMS_EPS) * w


LOG2E = math.log2(math.e)


def _inproj_kernel(x_ref, nw_ref, w_ref, wffr_ref, fbr_ref,
                   hg_ref, fox_ref, gate_ref, crow_ref, carry_row):
    i = pl.program_id(1)
    tm = x_ref.shape[0]

    @pl.when(i == 0)
    def _():
        carry_row[...] = jnp.zeros_like(carry_row)

    xn = _rms(x_ref[...], nw_ref[...]).astype(BF16)
    proj = _nn(xn, w_ref[...])
    n_h = 4 * HGRN_W
    hg_ref[...] = proj[:, :n_h]
    fox_ref[:, :FOX_W] = (proj[:, n_h:n_h + FOX_W] * (FOX_DH ** -0.5 * LOG2E)).astype(BF16)
    fox_ref[:, FOX_W:] = proj[:, n_h + FOX_W:n_h + 3 * FOX_W].astype(BF16)
    gate_ref[...] = jax.nn.sigmoid(proj[:, n_h + 3 * FOX_W:])

    r = lax.broadcasted_iota(jnp.int32, (tm, tm), 0)
    c = lax.broadcasted_iota(jnp.int32, (tm, tm), 1)
    triu = (r <= c).astype(BF16)
    ls_row = _log_sigmoid(_nt(wffr_ref[...], xn) + fbr_ref[:, 0:1]) * LOG2E
    hi = ls_row.astype(BF16)
    lo = (ls_row - hi.astype(F32)).astype(BF16)
    crow = _nn(hi, triu) + _nn(lo, triu) + carry_row[:, 0:1]
    crow_ref[...] = crow
    carry_row[...] = jnp.broadcast_to(crow[:, tm - 1:tm], carry_row.shape)


def _in_proj(x2d, nw, w_all, wff_row, fb_row, batch, seq, tm):
    T, D = x2d.shape
    nt = seq // tm
    n_all = w_all.shape[1]
    row = lambda b, i: (b * nt + i, 0)
    const = lambda b, i: (0, 0)
    return pl.pallas_call(
        _inproj_kernel,
        grid=(batch, nt),
        in_specs=[
            pl.BlockSpec((tm, D), row),
            pl.BlockSpec((1, D), const),
            pl.BlockSpec((D, n_all), const),
            pl.BlockSpec((FOX_HEADS, D), const),
            pl.BlockSpec((FOX_HEADS, LANES), const),
        ],
        out_specs=[
            pl.BlockSpec((tm, 4 * HGRN_W), row),
            pl.BlockSpec((tm, 3 * FOX_W), row),
            pl.BlockSpec((tm, 2 * D), row),
            pl.BlockSpec((FOX_HEADS, tm), lambda b, i: (0, b * nt + i)),
        ],
        out_shape=[
            jax.ShapeDtypeStruct((T, 4 * HGRN_W), F32),
            jax.ShapeDtypeStruct((T, 3 * FOX_W), BF16),
            jax.ShapeDtypeStruct((T, 2 * D), F32),
            jax.ShapeDtypeStruct((FOX_HEADS, T), F32),
        ],
        scratch_shapes=[pltpu.VMEM((FOX_HEADS, LANES), F32)],
        compiler_params=pltpu.CompilerParams(
            dimension_semantics=("arbitrary", "arbitrary"), vmem_limit_bytes=VMEM_LIMIT),
        name="in_proj",
    )(x2d, nw, w_all, wff_row, fb_row)


HGRN_CHUNK = 128
HGRN_SUB = 8


def _hgrn_levels():
    out = []
    m = HGRN_SUB
    while m < HGRN_CHUNK:
        out.append(m)
        m *= 2
    return out


def _hgrn_masks():
    C, sub = HGRN_CHUNK, HGRN_SUB
    r = jnp.arange(C, dtype=jnp.int32)[:, None]
    c = jnp.arange(C, dtype=jnp.int32)[None, :]
    masks = [((r // sub) == (c // sub)) & ((c % sub) <= (r % sub))]
    for m in _hgrn_levels():
        masks.append(((r // (2 * m)) == (c // (2 * m))) & (((r // m) % 2) == 1) & (((c // m) % 2) == 0))
    return jnp.stack(masks).astype(F32)


def _hgrn_kernel(hg_ref, lbl_ref, nw_ref, rsel_ref, tril_ref, mask_ref, a_ref, state_ref):
    ci = pl.program_id(1)
    C = HGRN_CHUNK
    dh = HGRN_DH
    sub = HGRN_SUB

    @pl.when(ci == 0)
    def _():
        state_ref[...] = jnp.zeros_like(state_ref)

    lg = lbl_ref[...]
    e = jnp.exp(lg - jnp.max(lg, axis=0, keepdims=True))
    lb_all = e[0:1, :] / jnp.sum(e, axis=0, keepdims=True)
    tril = tril_ref[...]

    for h in range(HGRN_HEADS):
        sl = slice(h * dh, (h + 1) * dh)
        qraw = hg_ref[:, sl]
        q = qraw * jax.nn.sigmoid(qraw)
        lb = lb_all[:, sl]
        f = lb + (1.0 - lb) * jax.nn.sigmoid(hg_ref[:, HGRN_W + h * dh:HGRN_W + (h + 1) * dh])
        log2f = jnp.log(f) * LOG2E
        k = 1.0 - f
        v = hg_ref[:, 2 * HGRN_W + h * dh:2 * HGRN_W + (h + 1) * dh]
        g = hg_ref[:, 3 * HGRN_W + h * dh:3 * HGRN_W + (h + 1) * dh]
        v_bf = v.astype(BF16)
        cum = _split_dot(_nn, tril, log2f)

        nb = C // sub
        q3 = q.reshape(nb, sub, dh)
        k3 = k.reshape(nb, sub, dh)
        c3 = cum.reshape(nb, sub, dh)
        parts = []
        for s in range(sub):
            kb = jnp.broadcast_to(k3[:, s:s + 1, :], (nb, sub, dh))
            cb = jnp.broadcast_to(c3[:, s:s + 1, :], (nb, sub, dh))
            es = q3 * kb * jnp.exp2(jnp.minimum(c3 - cb, 0.0))
            parts.append(es.reshape(C, dh).astype(BF16))
        p_mat = mask_ref[0] * _nn(jnp.concatenate(parts, axis=1), rsel_ref[...])

        for li, m in enumerate(_hgrn_levels()):
            nbm = C // m
            qm = q.reshape(nbm, m, dh)
            km = k.reshape(nbm, m, dh)
            cm = cum.reshape(nbm, m, dh)
            end = cm[:, m - 1:m, :]
            prev_end = jnp.concatenate([jnp.zeros((1, 1, dh), F32), end[:nbm - 1]], axis=0)
            qd = qm * jnp.exp2(jnp.minimum(cm - jnp.broadcast_to(prev_end, (nbm, m, dh)), 0.0))
            kd = km * jnp.exp2(jnp.minimum(jnp.broadcast_to(end, (nbm, m, dh)) - cm, 0.0))
            sc = _nt(qd.reshape(C, dh).astype(BF16), kd.reshape(C, dh).astype(BF16))
            p_mat = p_mat + mask_ref[1 + li] * sc

        st = state_ref[h]
        o = _nn(p_mat.astype(BF16), v_bf) + _nt((q * jnp.exp2(cum)).astype(BF16), st.astype(BF16))
        last = cum[C - 1:C, :]
        kdec = (k * jnp.exp2(last - cum)).astype(BF16)
        state_ref[h] = jnp.exp2(last) * st + _tn(v_bf, kdec)

        o = o * lax.rsqrt(jnp.mean(o * o, axis=-1, keepdims=True) + RMS_EPS) * nw_ref[:, sl]
        a_ref[:, sl] = (o * (g * jax.nn.sigmoid(g))).astype(BF16)


def _hgrn(hg, lb_logits, norm_w, rsel, batch, seq):
    T = hg.shape[0]
    C = HGRN_CHUNK
    nc = seq // C
    row = lambda b, i: (b * nc + i, 0)
    const = lambda b, i: (0, 0)
    tril = jnp.tril(jnp.ones((C, C), F32)).astype(BF16)
    masks = _hgrn_masks()
    return pl.pallas_call(
        _hgrn_kernel,
        grid=(batch, nc),
        in_specs=[
            pl.BlockSpec((C, 4 * HGRN_W), row),
            pl.BlockSpec(lb_logits.shape, const),
            pl.BlockSpec((1, HGRN_W), const),
            pl.BlockSpec(rsel.shape, const),
            pl.BlockSpec((C, C), const),
            pl.BlockSpec(masks.shape, lambda b, i: (0, 0, 0)),
        ],
        out_specs=pl.BlockSpec((C, HGRN_W), row),
        out_shape=jax.ShapeDtypeStruct((T, HGRN_W), BF16),
        scratch_shapes=[pltpu.VMEM((HGRN_HEADS, HGRN_DH, HGRN_DH), F32)],
        compiler_params=pltpu.CompilerParams(
            dimension_semantics=("arbitrary", "arbitrary"), vmem_limit_bytes=VMEM_LIMIT),
        name="hgrn",
    )(hg, lb_logits, norm_w, rsel, tril, masks)


def _fox_kernel(q_ref, k_ref, v_ref, crow_ref, o_ref, m_ref, acc_ref, qs_ref, vs_ref):
    qi = pl.program_id(1)
    tq = q_ref.shape[0]
    tk = tq
    pairs = FOX_HEADS // 2
    lane = lax.broadcasted_iota(jnp.int32, (1, LANES), 1)
    low = lane < FOX_DH
    sel = (jnp.where(low, 1.0, 0.0).astype(BF16), jnp.where(low, 0.0, 1.0).astype(BF16))
    ones_lane = (FOX_DH, 0)
    one_hot = tuple(jnp.where(lane == ol, 1.0, 0.0).astype(BF16) for ol in ones_lane)
    rr = lax.broadcasted_iota(jnp.int32, (tq, tk), 0)
    cc = lax.broadcasted_iota(jnp.int32, (tq, tk), 1)
    causal = cc <= rr

    @pl.when(qi == 0)
    def _():
        for p in range(pairs):
            vb = v_ref[:, p * LANES:(p + 1) * LANES]
            for hh in range(2):
                vs_ref[2 * p + hh] = vb * sel[hh] + one_hot[hh]

    for p in range(pairs):
        q = q_ref[:, p * LANES:(p + 1) * LANES]
        for hh in range(2):
            qs_ref[2 * p + hh] = q * sel[hh]
    m_ref[...] = jnp.full_like(m_ref, NEG_BIG)
    acc_ref[...] = jnp.zeros_like(acc_ref)

    def kv_block(j, masked):
        start = pl.multiple_of(j * tk, tk)
        for p in range(pairs):
            kb = k_ref[pl.ds(start, tk), p * LANES:(p + 1) * LANES]
            for hh in range(2):
                h = 2 * p + hh
                ck = crow_ref[h:h + 1, pl.ds(start, tk)]
                s = _nt(qs_ref[h], kb) - ck
                if masked:
                    s = jnp.where(causal, s, NEG_BIG)
                m_prev = m_ref[h]
                m_next = jnp.maximum(m_prev, jnp.max(s, axis=1, keepdims=True))
                pexp = jnp.exp2(s - jnp.concatenate([m_next] * (tk // LANES), axis=1))
                alpha = jnp.exp2(m_prev - m_next)
                acc_ref[h] = alpha * acc_ref[h] + _nn(pexp.astype(BF16), vs_ref[h, pl.ds(start, tk), :])
                m_ref[h] = m_next

    def body(j, carry):
        kv_block(j, False)
        return carry

    lax.fori_loop(0, qi, body, 0)
    kv_block(qi, True)

    for p in range(pairs):
        a0 = acc_ref[2 * p]
        a1 = acc_ref[2 * p + 1]
        o0 = a0 / a0[:, ones_lane[0]:ones_lane[0] + 1]
        o1 = a1 / a1[:, ones_lane[1]:ones_lane[1] + 1]
        o_ref[:, p * LANES:(p + 1) * LANES] = jnp.where(low, o0, o1).astype(BF16)


def _fox(fox, crow, batch, seq, tq):
    T = fox.shape[0]
    nq = seq // tq
    return pl.pallas_call(
        _fox_kernel,
        grid=(batch, nq),
        in_specs=[
            pl.BlockSpec((tq, FOX_W), lambda b, i: (b * nq + i, 0)),
            pl.BlockSpec((seq, FOX_W), lambda b, i: (b, 1)),
            pl.BlockSpec((seq, FOX_W), lambda b, i: (b, 2)),
            pl.BlockSpec((FOX_HEADS, seq), lambda b, i: (0, b)),
        ],
        out_specs=pl.BlockSpec((tq, FOX_W), lambda b, i: (b * nq + i, 0)),
        out_shape=jax.ShapeDtypeStruct((T, FOX_W), BF16),
        scratch_shapes=[pltpu.VMEM((FOX_HEADS, tq, LANES), F32), pltpu.VMEM((FOX_HEADS, tq, LANES), F32),
                        pltpu.VMEM((FOX_HEADS, tq, LANES), BF16), pltpu.VMEM((FOX_HEADS, seq, LANES), BF16)],
        compiler_params=pltpu.CompilerParams(
            dimension_semantics=("arbitrary", "arbitrary"), vmem_limit_bytes=VMEM_LIMIT),
        name="fox",
    )(fox, fox, fox, crow)


def _merge_kernel(x_ref, a_ref, b_ref, gate_ref, wa_ref, wb_ref, wo_ref, nw_ref, wq_ref, keys_ref,
                  h1_ref, xn_ref, st_ref):
    D = x_ref.shape[1]
    merged = gate_ref[:, :D] * _nn(a_ref[...], wa_ref[...]) + gate_ref[:, D:] * _nn(b_ref[...], wb_ref[...])
    h1 = x_ref[...] + _nn(merged.astype(BF16), wo_ref[...])
    h1_ref[...] = h1
    xn = _rms(h1, nw_ref[...]).astype(BF16)
    xn_ref[...] = xn
    q = _nn(xn, wq_ref[...]).astype(BF16)
    for hp in range(2 * PEER_HEADS):
        st_ref[hp] = _nt(keys_ref[hp], q[:, hp * PEER_HALF:(hp + 1) * PEER_HALF])


def _merge(x2d, a, b, gates, wa, wb, wo, nw, wq, keys, tm):
    T, D = x2d.shape
    row = lambda i: (i, 0)
    const = lambda i: (0, 0)
    return pl.pallas_call(
        _merge_kernel,
        grid=(T // tm,),
        in_specs=[
            pl.BlockSpec((tm, D), row),
            pl.BlockSpec((tm, HGRN_W), row),
            pl.BlockSpec((tm, FOX_W), row),
            pl.BlockSpec((tm, 2 * D), row),
            pl.BlockSpec(wa.shape, const),
            pl.BlockSpec(wb.shape, const),
            pl.BlockSpec(wo.shape, const),
            pl.BlockSpec((1, D), const),
            pl.BlockSpec(wq.shape, const),
            pl.BlockSpec(keys.shape, lambda i: (0, 0, 0)),
        ],
        out_specs=[
            pl.BlockSpec((tm, D), row),
            pl.BlockSpec((tm, D), row),
            pl.BlockSpec((2 * PEER_HEADS, N_KEYS, tm), lambda i: (0, 0, i)),
        ],
        out_shape=[
            jax.ShapeDtypeStruct((T, D), F32),
            jax.ShapeDtypeStruct((T, D), BF16),
            jax.ShapeDtypeStruct((2 * PEER_HEADS, N_KEYS, T), F32),
        ],
        compiler_params=pltpu.CompilerParams(
            dimension_semantics=("arbitrary",), vmem_limit_bytes=VMEM_LIMIT),
        name="merge",
    )(x2d, a, b, gates, wa, wb, wo, nw, wq, keys)


def _rank_pairs():
    n = TOPK + 1
    return [(r, c) for r in range(n) for c in range(n) if (r + 1) * (c + 1) <= n]


def _topk_kernel(st_ref, cnt_ref, p1_ref, rank_ref, p2_ref, rank1_ref):
    tt = st_ref.shape[2]
    n_top = TOPK + 1

    def top_sorted(s):
        vals = []
        cur = s
        rank = jnp.full(s.shape, float(n_top), F32)
        for r in range(n_top):
            mx = jnp.max(cur, axis=0, keepdims=True)
            vals.append(mx)
            hit = cur == mx
            rank = jnp.where(hit, float(r), rank)
            cur = jnp.where(hit, -jnp.inf, cur)
        return vals, rank

    tops = []
    for h in range(PEER_HEADS):
        v1, r1 = top_sorted(st_ref[2 * h])
        v2, r2 = top_sorted(st_ref[2 * h + 1])
        rank1_ref[h] = r1
        rank_ref[h] = r2
        tops.append((v1, v2))
    a = [jnp.concatenate([tops[h][0][r] for h in range(PEER_HEADS)], axis=0) for r in range(n_top)]
    b = [jnp.concatenate([tops[h][1][r] for h in range(PEER_HEADS)], axis=0) for r in range(n_top)]
    cand = [a[r] + b[c] for r, c in _rank_pairs()]
    tau = jnp.full((PEER_HEADS, tt), -jnp.inf, F32)
    nxt = jnp.full((PEER_HEADS, tt), -jnp.inf, F32)
    for xi in cand:
        cnt = jnp.zeros((PEER_HEADS, tt), F32)
        for xj in cand:
            cnt = cnt + jnp.where(xj >= xi, 1.0, 0.0)
        tau = jnp.maximum(tau, jnp.where(cnt >= TOPK, xi, -jnp.inf))
        nxt = jnp.maximum(nxt, jnp.where(cnt >= TOPK + 1, xi, -jnp.inf))
    top = a[0] + b[0]
    z = jnp.zeros((PEER_HEADS, tt), F32)
    for xi in cand:
        z = z + jnp.where(xi >= tau, jnp.exp(xi - top), 0.0)
    inv_z = 1.0 / z
    cut = 0.5 * (tau + nxt)
    n_sel = [jnp.zeros((PEER_HEADS, tt), F32) for _ in range(n_top)]
    for r, c in _rank_pairs():
        n_sel[r] = n_sel[r] + jnp.where(a[r] + b[c] >= cut, 1.0, 0.0)
    for h in range(PEER_HEADS):
        s1 = st_ref[2 * h]
        s2 = st_ref[2 * h + 1]
        r1 = rank1_ref[h]
        cnt = jnp.zeros(r1.shape, F32)
        for r in range(n_top):
            cnt = jnp.where(r1 == float(r), n_sel[r][h:h + 1, :], cnt)
        cnt_ref[h] = cnt
        p1_ref[h] = jnp.exp(s1 - a[0][h:h + 1, :]) * inv_z[h:h + 1, :]
        p2_ref[h] = jnp.exp(s2 - b[0][h:h + 1, :])


def _topk(st, tt):
    n, nk, T = st.shape
    out32 = jax.ShapeDtypeStruct((PEER_HEADS, nk, T), F32)
    spec = pl.BlockSpec((PEER_HEADS, nk, tt), lambda i: (0, 0, i))
    return pl.pallas_call(
        _topk_kernel,
        grid=(T // tt,),
        in_specs=[pl.BlockSpec((n, nk, tt), lambda i: (0, 0, i))],
        out_specs=[spec, spec, spec, spec],
        out_shape=[out32, out32, out32, out32],
        scratch_shapes=[pltpu.VMEM((PEER_HEADS, nk, tt), F32)],
        compiler_params=pltpu.CompilerParams(
            dimension_semantics=("arbitrary",), vmem_limit_bytes=VMEM_LIMIT),
        name="topk",
    )(st)


PEER_KEY_TILE = 64
PEER_TOK_TILE = 128
PEER_PAIR = 2
SUBLANES = 8


def _peer_kernel(xn_ref, u_ref, v_ref, cnt_ref, p1_ref, rank_ref, p2_ref, h1_ref, nw_ref,
                 o_ref, acc_ref, ht_ref, g_ref, cnt_rows, p1_rows):
    j = pl.program_id(1)
    nj = pl.num_programs(1)
    eb, tb = ht_ref.shape
    groups = eb // N_KEYS
    cur = j % 2
    jb = jnp.minimum(j, nj - 2)

    @pl.when(j == 0)
    def _():
        acc_ref[...] = jnp.zeros_like(acc_ref)
        g_ref[...] = jnp.zeros_like(g_ref)

    acc_ref[...] += _tn(g_ref[1 - cur], v_ref[...])

    ht_ref[...] = _nt(u_ref[...], xn_ref[...])
    kt = PEER_KEY_TILE // SUBLANES
    for g0 in range(0, groups, PEER_PAIR):
        for gi in range(PEER_PAIR):
            i1 = jb * groups + g0 + gi
            for h in range(PEER_HEADS):
                r = gi * PEER_HEADS + h
                cnt_rows[r] = jnp.broadcast_to(cnt_ref[h, pl.ds(i1, 1), :], (SUBLANES, tb))
                p1_rows[r] = jnp.broadcast_to(p1_ref[h, pl.ds(i1, 1), :], (SUBLANES, tb))
        for t0 in range(0, tb, PEER_TOK_TILE):
            ts = slice(t0, t0 + PEER_TOK_TILE)
            for k0 in range(0, N_KEYS, PEER_KEY_TILE):
                ks = slice(k0, k0 + PEER_KEY_TILE)
                gate = [None] * PEER_PAIR
                for h in range(PEER_HEADS):
                    rk = rank_ref[h, ks, ts]
                    p2 = p2_ref[h, ks, ts]
                    for gi in range(PEER_PAIR):
                        r = gi * PEER_HEADS + h
                        cn = jnp.concatenate([cnt_rows[r, :, ts]] * kt, axis=0)
                        p1 = jnp.concatenate([p1_rows[r, :, ts]] * kt, axis=0)
                        term = jnp.where(rk < cn, p2, 0.0) * p1
                        gate[gi] = term if gate[gi] is None else gate[gi] + term
                for gi in range(PEER_PAIR):
                    es = slice((g0 + gi) * N_KEYS + k0, (g0 + gi) * N_KEYS + k0 + PEER_KEY_TILE)
                    ht = ht_ref[es, ts]
                    act = 0.5 * ht * (1.0 + lax.erf(ht * (2.0 ** -0.5)))
                    g_ref[cur, es, ts] = (act * gate[gi]).astype(BF16)

    @pl.when(j == nj - 1)
    def _():
        o_ref[...] = _rms(h1_ref[...] + acc_ref[...], nw_ref[...])


def _peer(xn, u, v, cnt, p1, rank2, p2, h1, nw, tb, eb):
    T, D = xn.shape
    n_exp = u.shape[0]
    tok3 = lambda i, j: (0, 0, i)
    nb = n_exp // eb
    return pl.pallas_call(
        _peer_kernel,
        grid=(T // tb, nb + 1),
        in_specs=[
            pl.BlockSpec((tb, D), lambda i, j: (i, 0)),
            pl.BlockSpec((eb, D), lambda i, j: (jnp.minimum(j, nb - 1), 0)),
            pl.BlockSpec((eb, D), lambda i, j: (jnp.maximum(j - 1, 0), 0)),
            pl.BlockSpec((PEER_HEADS, N_KEYS, tb), tok3),
            pl.BlockSpec((PEER_HEADS, N_KEYS, tb), tok3),
            pl.BlockSpec((PEER_HEADS, N_KEYS, tb), tok3),
            pl.BlockSpec((PEER_HEADS, N_KEYS, tb), tok3),
            pl.BlockSpec((tb, D), lambda i, j: (i, 0)),
            pl.BlockSpec((1, D), lambda i, j: (0, 0)),
        ],
        out_specs=pl.BlockSpec((tb, D), lambda i, j: (i, 0)),
        out_shape=jax.ShapeDtypeStruct((T, D), F32),
        scratch_shapes=[pltpu.VMEM((tb, D), F32), pltpu.VMEM((eb, tb), F32), pltpu.VMEM((2, eb, tb), BF16),
                        pltpu.VMEM((PEER_PAIR * PEER_HEADS, SUBLANES, tb), F32),
                        pltpu.VMEM((PEER_PAIR * PEER_HEADS, SUBLANES, tb), F32)],
        compiler_params=pltpu.CompilerParams(
            dimension_semantics=("arbitrary", "arbitrary"), vmem_limit_bytes=VMEM_LIMIT),
        name="peer",
    )(xn, u, v, cnt, p1, rank2, p2, h1, nw)


ROUTE_SLOTS = 64


def _route_pairs():
    return [(r, c) for r in range(TOPK) for c in range(TOPK) if (r + 1) * (c + 1) <= TOPK]


def _bitonic_sort_desc(vals, idxs=None):
    n = len(vals)
    k = 2
    while k <= n:
        j = k // 2
        while j >= 1:
            for i in range(n):
                l = i ^ j
                if l > i:
                    hi, lo = (i, l) if (i & k) == 0 else (l, i)
                    a, b = vals[i], vals[l]
                    if idxs is not None:
                        gt = a > b
                        ia, ib = idxs[i], idxs[l]
                        idxs[hi] = jnp.where(gt, ia, ib)
                        idxs[lo] = jnp.where(gt, ib, ia)
                    vals[hi] = jnp.maximum(a, b)
                    vals[lo] = jnp.minimum(a, b)
            j //= 2
        k *= 2


def _top_of_two_sorted(a, b):
    n = len(a)
    out = [jnp.maximum(a[i], b[n - 1 - i]) for i in range(n)]
    j = n // 2
    while j >= 1:
        for i in range(n):
            l = i ^ j
            if l > i:
                x, y = out[i], out[l]
                out[i] = jnp.maximum(x, y)
                out[l] = jnp.minimum(x, y)
        j //= 2
    return out


def _routes_kernel(st_ref, ids_ref, w_ref):
    tt = st_ref.shape[2]
    sub = 8
    groups = N_KEYS // sub
    sub_id = lax.broadcasted_iota(jnp.int32, (sub, tt), 0).astype(F32)

    def top_sorted(s):
        vals = [s[g * sub:(g + 1) * sub, :] for g in range(groups)]
        idxs = [sub_id + float(g * sub) for g in range(groups)]
        _bitonic_sort_desc(vals, idxs)
        tops, keys = [], []
        for t in range(TOPK):
            mx = jnp.max(vals[0], axis=0, keepdims=True)
            win = vals[0] == mx
            tops.append(mx)
            keys.append(jnp.max(jnp.where(win, idxs[0], -1.0), axis=0, keepdims=True))
            for r in range(TOPK - 1 - t):
                vals[r] = jnp.where(win, vals[r + 1], vals[r])
                idxs[r] = jnp.where(win, idxs[r + 1], idxs[r])
        return tops, keys

    tops = [(top_sorted(st_ref[2 * h]), top_sorted(st_ref[2 * h + 1])) for h in range(PEER_HEADS)]

    def stack(half, which, r):
        return jnp.concatenate([tops[h][half][which][r] for h in range(PEER_HEADS)], axis=0)

    a = [stack(0, 0, r) for r in range(TOPK)]
    b = [stack(1, 0, r) for r in range(TOPK)]
    ia = [stack(0, 1, r) for r in range(TOPK)]
    ib = [stack(1, 1, r) for r in range(TOPK)]
    pairs = _route_pairs()
    cand = [a[r] + b[c] for r, c in pairs]
    neg = jnp.full((PEER_HEADS, tt), -jnp.inf, F32)
    padded = cand + [neg] * (-len(cand) % TOPK)
    best = None
    for g0 in range(0, len(padded), TOPK):
        grp = list(padded[g0:g0 + TOPK])
        _bitonic_sort_desc(grp)
        best = grp if best is None else _top_of_two_sorted(best, grp)
    tau = best[TOPK - 1]
    top = a[0] + b[0]
    e = [jnp.where(xi >= tau, jnp.exp(xi - top), 0.0) for xi in cand]
    z = e[0]
    for ei in e[1:]:
        z = z + ei
    inv_z = 1.0 / z
    ids = [jnp.clip(ia[r] * float(N_KEYS) + ib[c], 0.0, float(N_KEYS * N_KEYS - 1)) for r, c in pairs]
    wts = [ek * inv_z for ek in e]
    pad = [jnp.zeros((PEER_HEADS, tt), F32)] * (ROUTE_SLOTS - len(pairs))
    ids_ref[...] = jnp.concatenate(ids + pad, axis=0).T.astype(jnp.int32)
    w_ref[...] = jnp.concatenate(wts + pad, axis=0).T


def _routes(st, tt, t0, tc):
    n, nk, _ = st.shape
    off = t0 // tt
    width = ROUTE_SLOTS * PEER_HEADS
    spec = pl.BlockSpec((tt, width), lambda i: (i, 0))
    return pl.pallas_call(
        _routes_kernel,
        grid=(tc // tt,),
        in_specs=[pl.BlockSpec((n, nk, tt), lambda i: (0, 0, i + off))],
        out_specs=[spec, spec],
        out_shape=[jax.ShapeDtypeStruct((tc, width), jnp.int32),
                   jax.ShapeDtypeStruct((tc, width), F32)],
        compiler_params=pltpu.CompilerParams(
            dimension_semantics=("arbitrary",), vmem_limit_bytes=VMEM_LIMIT),
        name="routes",
    )(st)


SC_LANES = 16
SC_TOKENS_PER_CHUNK = 8
PEER_CHUNKS = 8


def _gate_matrix(ids, w, n_exp, heads):
    T, E = ids.shape
    info = plsc.get_sparse_core_info()
    workers = info.num_cores * info.num_subcores
    per_worker = T // workers
    ch = SC_TOKENS_PER_CHUNK
    mesh = plsc.VectorSubcoreMesh(core_axis_name="c", subcore_axis_name="s")
    ids_flat = ids.reshape(T * E)
    w_flat = w.reshape(T * E)

    @functools.partial(
        pl.kernel, mesh=mesh,
        out_type=jax.ShapeDtypeStruct((T, n_exp), F32),
        scratch_types=[pltpu.VMEM((ch * E,), jnp.int32), pltpu.VMEM((ch * E,), F32),
                       pltpu.VMEM((n_exp,), F32)],
        compiler_params=pltpu.CompilerParams(needs_layout_passes=False),
        name="gate_matrix",
    )
    def scatter(ids_hbm, w_hbm, out_hbm, ids_v, w_v, row_v):
        wid = lax.axis_index("s") * info.num_cores + lax.axis_index("c")
        base = wid * per_worker
        zeros = jnp.zeros((SC_LANES,), F32)
        lane = lax.iota(jnp.int32, SC_LANES)

        @pl.loop(0, n_exp, step=SC_LANES)
        def _(i):
            row_v[pl.ds(i, SC_LANES)] = zeros

        @pl.loop(0, per_worker // ch)
        def _(ci):
            t0 = base + ci * ch
            pltpu.sync_copy(ids_hbm.at[pl.ds(t0 * E, ch * E)], ids_v)
            pltpu.sync_copy(w_hbm.at[pl.ds(t0 * E, ch * E)], w_v)
            for tl in range(ch):
                @pl.loop(0, heads)
                def _(h):
                    for m in range(E // heads // SC_LANES):
                        pos = (lane + (tl * E // heads + m * SC_LANES)) * heads + h
                        idx = plsc.load_gather(ids_v, [pos])
                        val = plsc.load_gather(w_v, [pos])
                        plsc.addupdate_scatter(row_v, [idx], val)

                pltpu.sync_copy(row_v, out_hbm.at[t0 + tl])

                @pl.loop(0, E, step=SC_LANES)
                def _(e0):
                    sl = pl.ds(tl * E + e0, SC_LANES)
                    plsc.store_scatter(row_v, [ids_v[sl]], zeros)

    return scatter(ids_flat, w_flat)


def _experts_kernel(xn_ref, u_ref, v_ref, gate_ref, h1_ref, nw_ref, o_ref, acc_ref):
    j = pl.program_id(1)
    nj = pl.num_programs(1)

    @pl.when(j == 0)
    def _():
        acc_ref[...] = jnp.zeros_like(acc_ref)

    h = _nt(xn_ref[...], u_ref[...])
    act = 0.5 * h * (1.0 + lax.erf(h * (2.0 ** -0.5)))
    acc_ref[...] += _nn((act * gate_ref[...]).astype(BF16), v_ref[...])

    @pl.when(j == nj - 1)
    def _():
        o_ref[...] = _rms(h1_ref[...] + acc_ref[...], nw_ref[...])


def _experts(xn, u, v, gate, h1, nw, tb, eb, t0):
    tc = gate.shape[0]
    D = xn.shape[1]
    n_exp = u.shape[0]
    off = t0 // tb
    return pl.pallas_call(
        _experts_kernel,
        grid=(tc // tb, n_exp // eb),
        in_specs=[
            pl.BlockSpec((tb, D), lambda i, j: (i + off, 0)),
            pl.BlockSpec((eb, D), lambda i, j: (j, 0)),
            pl.BlockSpec((eb, D), lambda i, j: (j, 0)),
            pl.BlockSpec((tb, eb), lambda i, j: (i, j)),
            pl.BlockSpec((tb, D), lambda i, j: (i + off, 0)),
            pl.BlockSpec((1, D), lambda i, j: (0, 0)),
        ],
        out_specs=pl.BlockSpec((tb, D), lambda i, j: (i, 0)),
        out_shape=jax.ShapeDtypeStruct((tc, D), F32),
        scratch_shapes=[pltpu.VMEM((tb, D), F32)],
        compiler_params=pltpu.CompilerParams(
            dimension_semantics=("arbitrary", "arbitrary"), vmem_limit_bytes=VMEM_LIMIT),
        name="experts",
    )(xn, u, v, gate, h1, nw)


def _block_select_matrix():
    part = jnp.arange(HGRN_SUB * HGRN_DH, dtype=jnp.int32) // HGRN_DH
    col = jnp.arange(HGRN_CHUNK, dtype=jnp.int32) % HGRN_SUB
    return (part[:, None] == col[None, :]).astype(BF16)


def _forward(x, norm_mix_w, w_in, hgrn_lb_logits, hgrn_norm_w, fox_f_bias, w_branch_hgrn,
             w_branch_fox, w_out, norm_ffn_w, peer_w_q, peer_sub_keys, peer_u, peer_v,
             norm_final_w, *, tm_in, tq, tm_merge, tt, tb, eb):
    B, S, D = x.shape
    T = B * S
    x2d = x.reshape(T, D)
    n_h = 4 * HGRN_W
    n_f = 3 * FOX_W
    wi = w_in[0]
    w_all = jnp.concatenate([wi[:, :n_h + n_f], wi[:, n_h + n_f + FOX_HEADS:]], axis=1).astype(BF16)
    wff = wi[:, n_h + n_f:n_h + n_f + FOX_HEADS]
    wff_row = wff.T.astype(BF16)
    fb = fox_f_bias[0].astype(F32)
    fb_row = jnp.broadcast_to(fb.reshape(FOX_HEADS, 1), (FOX_HEADS, LANES))

    hg, fox, gates, crow = _in_proj(
        x2d, norm_mix_w[0].reshape(1, D), w_all, wff_row, fb_row, B, S, tm_in)
    a = _hgrn(hg, hgrn_lb_logits, hgrn_norm_w[0].reshape(1, HGRN_W), _block_select_matrix(), B, S)
    b = _fox(fox, crow, B, S, tq)
    keys = peer_sub_keys[0].reshape(2 * PEER_HEADS, N_KEYS, PEER_HALF).astype(BF16)
    h1, xn2, st = _merge(
        x2d, a, b, gates, w_branch_hgrn[0].astype(BF16), w_branch_fox[0].astype(BF16),
        w_out[0].astype(BF16), norm_ffn_w[0].reshape(1, D), peer_w_q[0].astype(BF16), keys, tm_merge)
    n_exp = peer_u.shape[1]
    u_bf = peer_u[0].astype(BF16)
    v_bf = peer_v[0].astype(BF16)
    tc = T // PEER_CHUNKS
    outs = []
    for c in range(PEER_CHUNKS):
        ids, wts = _routes(st, tt, c * tc, tc)
        gate = _gate_matrix(ids, wts, n_exp, PEER_HEADS)
        outs.append(_experts(xn2, u_bf, v_bf, gate, h1, norm_final_w.reshape(1, D), tb, eb, c * tc))
    return jnp.concatenate(outs, axis=0).reshape(B, S, D)


def kernel(x, norm_mix_w, w_in, hgrn_lb_logits, hgrn_norm_w, fox_f_bias, w_branch_hgrn, w_branch_fox, w_out, norm_ffn_w, peer_w_q, peer_sub_keys, peer_u, peer_v, norm_final_w):
    return _forward(x, norm_mix_w, w_in, hgrn_lb_logits, hgrn_norm_w, fox_f_bias, w_branch_hgrn,
                    w_branch_fox, w_out, norm_ffn_w, peer_w_q, peer_sub_keys, peer_u, peer_v,
                    norm_final_w, tm_in=256, tq=512, tm_merge=256, tt=256, tb=1024, eb=1024)
```

```python
import functools
import math

import jax
import jax.numpy as jnp
from jax import lax
from jax.experimental import pallas as pl
from jax.experimental.pallas import tpu as pltpu
from jax.experimental.pallas import tpu_sc as plsc

F32 = jnp.float32
BF16 = jnp.bfloat16
RMS_EPS = 1e-6
NEG_BIG = -1e30

HGRN_HEADS = 4
HGRN_DH = 128
HGRN_W = HGRN_HEADS * HGRN_DH
FOX_HEADS = 8
FOX_DH = 64
FOX_W = FOX_HEADS * FOX_DH
PEER_HEADS = 8
PEER_HALF = 128
N_KEYS = 128
TOPK = 16
LANES = 128

VMEM_LIMIT = 56 * 1024 * 1024


def _nt(a, b):
    return lax.dot_general(a, b, (((1,), (1,)), ((), ())), preferred_element_type=F32)


def _tn(a, b):
    return lax.dot_general(a, b, (((0,), (0,)), ((), ())), preferred_element_type=F32)


def _nn(a, b):
    return jnp.dot(a, b, preferred_element_type=F32)


def _split_dot(fn, tri, x):
    hi = x.astype(BF16)
    lo = (x - hi.astype(F32)).astype(BF16)
    return fn(tri, hi) + fn(tri, lo)


def _log_sigmoid(x):
    return jnp.minimum(x, 0.0) - jnp.log1p(jnp.exp(-jnp.abs(x)))


def _rms(x, w):
    return x * lax.rsqrt(jnp.mean(x * x, axis=-1, keepdims=True) + RMS_EPS) * w


LOG2E = math.log2(math.e)


def _inproj_kernel(x_ref, nw_ref, w_ref, wffr_ref, fbr_ref,
                   hg_ref, fox_ref, gate_ref, crow_ref, carry_row):
    i = pl.program_id(1)
    tm = x_ref.shape[0]

    @pl.when(i == 0)
    def _():
        carry_row[...] = jnp.zeros_like(carry_row)

    xn = _rms(x_ref[...], nw_ref[...]).astype(BF16)
    proj = _nn(xn, w_ref[...])
    n_h = 4 * HGRN_W
    hg_ref[...] = proj[:, :n_h]
    fox_ref[:, :FOX_W] = (proj[:, n_h:n_h + FOX_W] * (FOX_DH ** -0.5 * LOG2E)).astype(BF16)
    fox_ref[:, FOX_W:] = proj[:, n_h + FOX_W:n_h + 3 * FOX_W].astype(BF16)
    gate_ref[...] = jax.nn.sigmoid(proj[:, n_h + 3 * FOX_W:])

    r = lax.broadcasted_iota(jnp.int32, (tm, tm), 0)
    c = lax.broadcasted_iota(jnp.int32, (tm, tm), 1)
    triu = (r <= c).astype(BF16)
    ls_row = _log_sigmoid(_nt(wffr_ref[...], xn) + fbr_ref[:, 0:1]) * LOG2E
    hi = ls_row.astype(BF16)
    lo = (ls_row - hi.astype(F32)).astype(BF16)
    crow = _nn(hi, triu) + _nn(lo, triu) + carry_row[:, 0:1]
    crow_ref[...] = crow
    carry_row[...] = jnp.broadcast_to(crow[:, tm - 1:tm], carry_row.shape)


def _in_proj(x2d, nw, w_all, wff_row, fb_row, batch, seq, tm):
    T, D = x2d.shape
    nt = seq // tm
    n_all = w_all.shape[1]
    row = lambda b, i: (b * nt + i, 0)
    const = lambda b, i: (0, 0)
    return pl.pallas_call(
        _inproj_kernel,
        grid=(batch, nt),
        in_specs=[
            pl.BlockSpec((tm, D), row),
            pl.BlockSpec((1, D), const),
            pl.BlockSpec((D, n_all), const),
            pl.BlockSpec((FOX_HEADS, D), const),
            pl.BlockSpec((FOX_HEADS, LANES), const),
        ],
        out_specs=[
            pl.BlockSpec((tm, 4 * HGRN_W), row),
            pl.BlockSpec((tm, 3 * FOX_W), row),
            pl.BlockSpec((tm, 2 * D), row),
            pl.BlockSpec((FOX_HEADS, tm), lambda b, i: (0, b * nt + i)),
        ],
        out_shape=[
            jax.ShapeDtypeStruct((T, 4 * HGRN_W), F32),
            jax.ShapeDtypeStruct((T, 3 * FOX_W), BF16),
            jax.ShapeDtypeStruct((T, 2 * D), F32),
            jax.ShapeDtypeStruct((FOX_HEADS, T), F32),
        ],
        scratch_shapes=[pltpu.VMEM((FOX_HEADS, LANES), F32)],
        compiler_params=pltpu.CompilerParams(
            dimension_semantics=("arbitrary", "arbitrary"), vmem_limit_bytes=VMEM_LIMIT),
        name="in_proj",
    )(x2d, nw, w_all, wff_row, fb_row)


HGRN_CHUNK = 128
HGRN_SUB = 8


def _hgrn_levels():
    out = []
    m = HGRN_SUB
    while m < HGRN_CHUNK:
        out.append(m)
        m *= 2
    return out


def _hgrn_masks():
    C, sub = HGRN_CHUNK, HGRN_SUB
    r = jnp.arange(C, dtype=jnp.int32)[:, None]
    c = jnp.arange(C, dtype=jnp.int32)[None, :]
    masks = [((r // sub) == (c // sub)) & ((c % sub) <= (r % sub))]
    for m in _hgrn_levels():
        masks.append(((r // (2 * m)) == (c // (2 * m))) & (((r // m) % 2) == 1) & (((c // m) % 2) == 0))
    return jnp.stack(masks).astype(F32)


def _hgrn_kernel(hg_ref, lbl_ref, nw_ref, rsel_ref, tril_ref, mask_ref, a_ref, state_ref):
    ci = pl.program_id(1)
    C = HGRN_CHUNK
    dh = HGRN_DH
    sub = HGRN_SUB

    @pl.when(ci == 0)
    def _():
        state_ref[...] = jnp.zeros_like(state_ref)

    lg = lbl_ref[...]
    e = jnp.exp(lg - jnp.max(lg, axis=0, keepdims=True))
    lb_all = e[0:1, :] / jnp.sum(e, axis=0, keepdims=True)
    tril = tril_ref[...]

    for h in range(HGRN_HEADS):
        sl = slice(h * dh, (h + 1) * dh)
        qraw = hg_ref[:, sl]
        q = qraw * jax.nn.sigmoid(qraw)
        lb = lb_all[:, sl]
        f = lb + (1.0 - lb) * jax.nn.sigmoid(hg_ref[:, HGRN_W + h * dh:HGRN_W + (h + 1) * dh])
        log2f = jnp.log(f) * LOG2E
        k = 1.0 - f
        v = hg_ref[:, 2 * HGRN_W + h * dh:2 * HGRN_W + (h + 1) * dh]
        g = hg_ref[:, 3 * HGRN_W + h * dh:3 * HGRN_W + (h + 1) * dh]
        v_bf = v.astype(BF16)
        cum = _split_dot(_nn, tril, log2f)

        nb = C // sub
        q3 = q.reshape(nb, sub, dh)
        k3 = k.reshape(nb, sub, dh)
        c3 = cum.reshape(nb, sub, dh)
        parts = []
        for s in range(sub):
            kb = jnp.broadcast_to(k3[:, s:s + 1, :], (nb, sub, dh))
            cb = jnp.broadcast_to(c3[:, s:s + 1, :], (nb, sub, dh))
            es = q3 * kb * jnp.exp2(jnp.minimum(c3 - cb, 0.0))
            parts.append(es.reshape(C, dh).astype(BF16))
        p_mat = mask_ref[0] * _nn(jnp.concatenate(parts, axis=1), rsel_ref[...])

        for li, m in enumerate(_hgrn_levels()):
            nbm = C // m
            qm = q.reshape(nbm, m, dh)
            km = k.reshape(nbm, m, dh)
            cm = cum.reshape(nbm, m, dh)
            end = cm[:, m - 1:m, :]
            prev_end = jnp.concatenate([jnp.zeros((1, 1, dh), F32), end[:nbm - 1]], axis=0)
            qd = qm * jnp.exp2(jnp.minimum(cm - jnp.broadcast_to(prev_end, (nbm, m, dh)), 0.0))
            kd = km * jnp.exp2(jnp.minimum(jnp.broadcast_to(end, (nbm, m, dh)) - cm, 0.0))
            sc = _nt(qd.reshape(C, dh).astype(BF16), kd.reshape(C, dh).astype(BF16))
            p_mat = p_mat + mask_ref[1 + li] * sc

        st = state_ref[h]
        o = _nn(p_mat.astype(BF16), v_bf) + _nt((q * jnp.exp2(cum)).astype(BF16), st.astype(BF16))
        last = cum[C - 1:C, :]
        kdec = (k * jnp.exp2(last - cum)).astype(BF16)
        state_ref[h] = jnp.exp2(last) * st + _tn(v_bf, kdec)

        o = o * lax.rsqrt(jnp.mean(o * o, axis=-1, keepdims=True) + RMS_EPS) * nw_ref[:, sl]
        a_ref[:, sl] = (o * (g * jax.nn.sigmoid(g))).astype(BF16)


def _hgrn(hg, lb_logits, norm_w, rsel, batch, seq):
    T = hg.shape[0]
    C = HGRN_CHUNK
    nc = seq // C
    row = lambda b, i: (b * nc + i, 0)
    const = lambda b, i: (0, 0)
    tril = jnp.tril(jnp.ones((C, C), F32)).astype(BF16)
    masks = _hgrn_masks()
    return pl.pallas_call(
        _hgrn_kernel,
        grid=(batch, nc),
        in_specs=[
            pl.BlockSpec((C, 4 * HGRN_W), row),
            pl.BlockSpec(lb_logits.shape, const),
            pl.BlockSpec((1, HGRN_W), const),
            pl.BlockSpec(rsel.shape, const),
            pl.BlockSpec((C, C), const),
            pl.BlockSpec(masks.shape, lambda b, i: (0, 0, 0)),
        ],
        out_specs=pl.BlockSpec((C, HGRN_W), row),
        out_shape=jax.ShapeDtypeStruct((T, HGRN_W), BF16),
        scratch_shapes=[pltpu.VMEM((HGRN_HEADS, HGRN_DH, HGRN_DH), F32)],
        compiler_params=pltpu.CompilerParams(
            dimension_semantics=("arbitrary", "arbitrary"), vmem_limit_bytes=VMEM_LIMIT),
        name="hgrn",
    )(hg, lb_logits, norm_w, rsel, tril, masks)


def _fox_kernel(q_ref, k_ref, v_ref, crow_ref, o_ref, m_ref, acc_ref, qs_ref, vs_ref):
    qi = pl.program_id(1)
    tq = q_ref.shape[0]
    tk = tq
    pairs = FOX_HEADS // 2
    lane = lax.broadcasted_iota(jnp.int32, (1, LANES), 1)
    low = lane < FOX_DH
    sel = (jnp.where(low, 1.0, 0.0).astype(BF16), jnp.where(low, 0.0, 1.0).astype(BF16))
    ones_lane = (FOX_DH, 0)
    one_hot = tuple(jnp.where(lane == ol, 1.0, 0.0).astype(BF16) for ol in ones_lane)
    rr = lax.broadcasted_iota(jnp.int32, (tq, tk), 0)
    cc = lax.broadcasted_iota(jnp.int32, (tq, tk), 1)
    causal = cc <= rr

    @pl.when(qi == 0)
    def _():
        for p in range(pairs):
            vb = v_ref[:, p * LANES:(p + 1) * LANES]
            for hh in range(2):
                vs_ref[2 * p + hh] = vb * sel[hh] + one_hot[hh]

    for p in range(pairs):
        q = q_ref[:, p * LANES:(p + 1) * LANES]
        for hh in range(2):
            qs_ref[2 * p + hh] = q * sel[hh]
    m_ref[...] = jnp.full_like(m_ref, NEG_BIG)
    acc_ref[...] = jnp.zeros_like(acc_ref)

    def kv_block(j, masked):
        start = pl.multiple_of(j * tk, tk)
        for p in range(pairs):
            kb = k_ref[pl.ds(start, tk), p * LANES:(p + 1) * LANES]
            for hh in range(2):
                h = 2 * p + hh
                ck = crow_ref[h:h + 1, pl.ds(start, tk)]
                s = _nt(qs_ref[h], kb) - ck
                if masked:
                    s = jnp.where(causal, s, NEG_BIG)
                m_prev = m_ref[h]
                m_next = jnp.maximum(m_prev, jnp.max(s, axis=1, keepdims=True))
                pexp = jnp.exp2(s - jnp.concatenate([m_next] * (tk // LANES), axis=1))
                alpha = jnp.exp2(m_prev - m_next)
                acc_ref[h] = alpha * acc_ref[h] + _nn(pexp.astype(BF16), vs_ref[h, pl.ds(start, tk), :])
                m_ref[h] = m_next

    def body(j, carry):
        kv_block(j, False)
        return carry

    lax.fori_loop(0, qi, body, 0)
    kv_block(qi, True)

    for p in range(pairs):
        a0 = acc_ref[2 * p]
        a1 = acc_ref[2 * p + 1]
        o0 = a0 / a0[:, ones_lane[0]:ones_lane[0] + 1]
        o1 = a1 / a1[:, ones_lane[1]:ones_lane[1] + 1]
        o_ref[:, p * LANES:(p + 1) * LANES] = jnp.where(low, o0, o1).astype(BF16)


def _fox(fox, crow, batch, seq, tq):
    T = fox.shape[0]
    nq = seq // tq
    return pl.pallas_call(
        _fox_kernel,
        grid=(batch, nq),
        in_specs=[
            pl.BlockSpec((tq, FOX_W), lambda b, i: (b * nq + i, 0)),
            pl.BlockSpec((seq, FOX_W), lambda b, i: (b, 1)),
            pl.BlockSpec((seq, FOX_W), lambda b, i: (b, 2)),
            pl.BlockSpec((FOX_HEADS, seq), lambda b, i: (0, b)),
        ],
        out_specs=pl.BlockSpec((tq, FOX_W), lambda b, i: (b * nq + i, 0)),
        out_shape=jax.ShapeDtypeStruct((T, FOX_W), BF16),
        scratch_shapes=[pltpu.VMEM((FOX_HEADS, tq, LANES), F32), pltpu.VMEM((FOX_HEADS, tq, LANES), F32),
                        pltpu.VMEM((FOX_HEADS, tq, LANES), BF16), pltpu.VMEM((FOX_HEADS, seq, LANES), BF16)],
        compiler_params=pltpu.CompilerParams(
            dimension_semantics=("arbitrary", "arbitrary"), vmem_limit_bytes=VMEM_LIMIT),
        name="fox",
    )(fox, fox, fox, crow)


def _merge_kernel(x_ref, a_ref, b_ref, gate_ref, wa_ref, wb_ref, wo_ref, nw_ref, wq_ref, keys_ref,
                  h1_ref, xn_ref, st_ref):
    D = x_ref.shape[1]
    merged = gate_ref[:, :D] * _nn(a_ref[...], wa_ref[...]) + gate_ref[:, D:] * _nn(b_ref[...], wb_ref[...])
    h1 = x_ref[...] + _nn(merged.astype(BF16), wo_ref[...])
    h1_ref[...] = h1
    xn = _rms(h1, nw_ref[...]).astype(BF16)
    xn_ref[...] = xn
    q = _nn(xn, wq_ref[...]).astype(BF16)
    for hp in range(2 * PEER_HEADS):
        st_ref[hp] = _nt(keys_ref[hp], q[:, hp * PEER_HALF:(hp + 1) * PEER_HALF])


def _merge(x2d, a, b, gates, wa, wb, wo, nw, wq, keys, tm):
    T, D = x2d.shape
    row = lambda i: (i, 0)
    const = lambda i: (0, 0)
    return pl.pallas_call(
        _merge_kernel,
        grid=(T // tm,),
        in_specs=[
            pl.BlockSpec((tm, D), row),
            pl.BlockSpec((tm, HGRN_W), row),
            pl.BlockSpec((tm, FOX_W), row),
            pl.BlockSpec((tm, 2 * D), row),
            pl.BlockSpec(wa.shape, const),
            pl.BlockSpec(wb.shape, const),
            pl.BlockSpec(wo.shape, const),
            pl.BlockSpec((1, D), const),
            pl.BlockSpec(wq.shape, const),
            pl.BlockSpec(keys.shape, lambda i: (0, 0, 0)),
        ],
        out_specs=[
            pl.BlockSpec((tm, D), row),
            pl.BlockSpec((tm, D), row),
            pl.BlockSpec((2 * PEER_HEADS, N_KEYS, tm), lambda i: (0, 0, i)),
        ],
        out_shape=[
            jax.ShapeDtypeStruct((T, D), F32),
            jax.ShapeDtypeStruct((T, D), BF16),
            jax.ShapeDtypeStruct((2 * PEER_HEADS, N_KEYS, T), F32),
        ],
        compiler_params=pltpu.CompilerParams(
            dimension_semantics=("arbitrary",), vmem_limit_bytes=VMEM_LIMIT),
        name="merge",
    )(x2d, a, b, gates, wa, wb, wo, nw, wq, keys)


ROUTE_SLOTS = 64


def _route_pairs():
    return [(r, c) for r in range(TOPK) for c in range(TOPK) if (r + 1) * (c + 1) <= TOPK]


def _bitonic_sort_desc(vals, idxs=None):
    n = len(vals)
    k = 2
    while k <= n:
        j = k // 2
        while j >= 1:
            for i in range(n):
                l = i ^ j
                if l > i:
                    hi, lo = (i, l) if (i & k) == 0 else (l, i)
                    a, b = vals[i], vals[l]
                    if idxs is not None:
                        gt = a > b
                        ia, ib = idxs[i], idxs[l]
                        idxs[hi] = jnp.where(gt, ia, ib)
                        idxs[lo] = jnp.where(gt, ib, ia)
                    vals[hi] = jnp.maximum(a, b)
                    vals[lo] = jnp.minimum(a, b)
            j //= 2
        k *= 2


def _top_of_two_sorted(a, b):
    n = len(a)
    out = [jnp.maximum(a[i], b[n - 1 - i]) for i in range(n)]
    j = n // 2
    while j >= 1:
        for i in range(n):
            l = i ^ j
            if l > i:
                x, y = out[i], out[l]
                out[i] = jnp.maximum(x, y)
                out[l] = jnp.minimum(x, y)
        j //= 2
    return out


def _routes_kernel(st_ref, ids_ref, w_ref):
    tt = st_ref.shape[2]
    sub = 8
    groups = N_KEYS // sub
    sub_id = lax.broadcasted_iota(jnp.int32, (sub, tt), 0).astype(F32)

    def top_sorted(s):
        vals = [s[g * sub:(g + 1) * sub, :] for g in range(groups)]
        idxs = [sub_id + float(g * sub) for g in range(groups)]
        _bitonic_sort_desc(vals, idxs)
        tops, keys = [], []
        for t in range(TOPK):
            mx = jnp.max(vals[0], axis=0, keepdims=True)
            win = vals[0] == mx
            tops.append(mx)
            keys.append(jnp.max(jnp.where(win, idxs[0], -1.0), axis=0, keepdims=True))
            for r in range(TOPK - 1 - t):
                vals[r] = jnp.where(win, vals[r + 1], vals[r])
                idxs[r] = jnp.where(win, idxs[r + 1], idxs[r])
        return tops, keys

    tops = [(top_sorted(st_ref[2 * h]), top_sorted(st_ref[2 * h + 1])) for h in range(PEER_HEADS)]

    def stack(half, which, r):
        return jnp.concatenate([tops[h][half][which][r] for h in range(PEER_HEADS)], axis=0)

    a = [stack(0, 0, r) for r in range(TOPK)]
    b = [stack(1, 0, r) for r in range(TOPK)]
    ia = [stack(0, 1, r) for r in range(TOPK)]
    ib = [stack(1, 1, r) for r in range(TOPK)]
    pairs = _route_pairs()
    cand = [a[r] + b[c] for r, c in pairs]
    neg = jnp.full((PEER_HEADS, tt), -jnp.inf, F32)
    padded = cand + [neg] * (-len(cand) % TOPK)
    best = None
    for g0 in range(0, len(padded), TOPK):
        grp = list(padded[g0:g0 + TOPK])
        _bitonic_sort_desc(grp)
        best = grp if best is None else _top_of_two_sorted(best, grp)
    tau = best[TOPK - 1]
    top = a[0] + b[0]
    e = [jnp.where(xi >= tau, jnp.exp(xi - top), 0.0) for xi in cand]
    z = e[0]
    for ei in e[1:]:
        z = z + ei
    inv_z = 1.0 / z
    ids = [jnp.clip(ia[r] * float(N_KEYS) + ib[c], 0.0, float(N_KEYS * N_KEYS - 1)) for r, c in pairs]
    wts = [ek * inv_z for ek in e]
    pad = [jnp.zeros((PEER_HEADS, tt), F32)] * (ROUTE_SLOTS - len(pairs))
    ids_ref[...] = jnp.concatenate(ids + pad, axis=0).T.astype(jnp.int32)
    w_ref[...] = jnp.concatenate(wts + pad, axis=0).T


def _routes(st, tt, t0, tc):
    n, nk, _ = st.shape
    off = t0 // tt
    width = ROUTE_SLOTS * PEER_HEADS
    spec = pl.BlockSpec((tt, width), lambda i: (i, 0))
    return pl.pallas_call(
        _routes_kernel,
        grid=(tc // tt,),
        in_specs=[pl.BlockSpec((n, nk, tt), lambda i: (0, 0, i + off))],
        out_specs=[spec, spec],
        out_shape=[jax.ShapeDtypeStruct((tc, width), jnp.int32),
                   jax.ShapeDtypeStruct((tc, width), F32)],
        compiler_params=pltpu.CompilerParams(
            dimension_semantics=("arbitrary",), vmem_limit_bytes=VMEM_LIMIT),
        name="routes",
    )(st)


SC_LANES = 16
SC_TOKENS_PER_CHUNK = 8
PEER_CHUNKS = 8


def _gate_matrix(ids, w, n_exp, heads):
    T, E = ids.shape
    info = plsc.get_sparse_core_info()
    workers = info.num_cores * info.num_subcores
    per_worker = T // workers
    ch = SC_TOKENS_PER_CHUNK
    mesh = plsc.VectorSubcoreMesh(core_axis_name="c", subcore_axis_name="s")
    ids_flat = ids.reshape(T * E)
    w_flat = w.reshape(T * E)

    @functools.partial(
        pl.kernel, mesh=mesh,
        out_type=jax.ShapeDtypeStruct((T, n_exp), F32),
        scratch_types=[pltpu.VMEM((ch * E,), jnp.int32), pltpu.VMEM((ch * E,), F32),
                       pltpu.VMEM((n_exp,), F32)],
        compiler_params=pltpu.CompilerParams(needs_layout_passes=False),
        name="gate_matrix",
    )
    def scatter(ids_hbm, w_hbm, out_hbm, ids_v, w_v, row_v):
        wid = lax.axis_index("s") * info.num_cores + lax.axis_index("c")
        base = wid * per_worker
        zeros = jnp.zeros((SC_LANES,), F32)
        lane = lax.iota(jnp.int32, SC_LANES)

        @pl.loop(0, n_exp, step=SC_LANES)
        def _(i):
            row_v[pl.ds(i, SC_LANES)] = zeros

        @pl.loop(0, per_worker // ch)
        def _(ci):
            t0 = base + ci * ch
            pltpu.sync_copy(ids_hbm.at[pl.ds(t0 * E, ch * E)], ids_v)
            pltpu.sync_copy(w_hbm.at[pl.ds(t0 * E, ch * E)], w_v)
            for tl in range(ch):
                @pl.loop(0, heads)
                def _(h):
                    for m in range(E // heads // SC_LANES):
                        pos = (lane + (tl * E // heads + m * SC_LANES)) * heads + h
                        idx = plsc.load_gather(ids_v, [pos])
                        val = plsc.load_gather(w_v, [pos])
                        plsc.addupdate_scatter(row_v, [idx], val)

                pltpu.sync_copy(row_v, out_hbm.at[t0 + tl])

                @pl.loop(0, E, step=SC_LANES)
                def _(e0):
                    sl = pl.ds(tl * E + e0, SC_LANES)
                    plsc.store_scatter(row_v, [ids_v[sl]], zeros)

    return scatter(ids_flat, w_flat)


def _experts_kernel(xn_ref, u_ref, v_ref, gate_ref, h1_ref, nw_ref, o_ref, acc_ref):
    j = pl.program_id(1)
    nj = pl.num_programs(1)

    @pl.when(j == 0)
    def _():
        acc_ref[...] = jnp.zeros_like(acc_ref)

    h = _nt(xn_ref[...], u_ref[...])
    act = 0.5 * h * (1.0 + lax.erf(h * (2.0 ** -0.5)))
    acc_ref[...] += _nn((act * gate_ref[...]).astype(BF16), v_ref[...])

    @pl.when(j == nj - 1)
    def _():
        o_ref[...] = _rms(h1_ref[...] + acc_ref[...], nw_ref[...])


def _experts(xn, u, v, gate, h1, nw, tb, eb, t0):
    tc = gate.shape[0]
    D = xn.shape[1]
    n_exp = u.shape[0]
    off = t0 // tb
    return pl.pallas_call(
        _experts_kernel,
        grid=(tc // tb, n_exp // eb),
        in_specs=[
            pl.BlockSpec((tb, D), lambda i, j: (i + off, 0)),
            pl.BlockSpec((eb, D), lambda i, j: (j, 0)),
            pl.BlockSpec((eb, D), lambda i, j: (j, 0)),
            pl.BlockSpec((tb, eb), lambda i, j: (i, j)),
            pl.BlockSpec((tb, D), lambda i, j: (i + off, 0)),
            pl.BlockSpec((1, D), lambda i, j: (0, 0)),
        ],
        out_specs=pl.BlockSpec((tb, D), lambda i, j: (i, 0)),
        out_shape=jax.ShapeDtypeStruct((tc, D), F32),
        scratch_shapes=[pltpu.VMEM((tb, D), F32)],
        compiler_params=pltpu.CompilerParams(
            dimension_semantics=("arbitrary", "arbitrary"), vmem_limit_bytes=VMEM_LIMIT),
        name="experts",
    )(xn, u, v, gate, h1, nw)


def _block_select_matrix():
    part = jnp.arange(HGRN_SUB * HGRN_DH, dtype=jnp.int32) // HGRN_DH
    col = jnp.arange(HGRN_CHUNK, dtype=jnp.int32) % HGRN_SUB
    return (part[:, None] == col[None, :]).astype(BF16)


def _forward(x, norm_mix_w, w_in, hgrn_lb_logits, hgrn_norm_w, fox_f_bias, w_branch_hgrn,
             w_branch_fox, w_out, norm_ffn_w, peer_w_q, peer_sub_keys, peer_u, peer_v,
             norm_final_w, *, tm_in, tq, tm_merge, tt, tb, eb):
    B, S, D = x.shape
    T = B * S
    x2d = x.reshape(T, D)
    n_h = 4 * HGRN_W
    n_f = 3 * FOX_W
    wi = w_in[0]
    w_all = jnp.concatenate([wi[:, :n_h + n_f], wi[:, n_h + n_f + FOX_HEADS:]], axis=1).astype(BF16)
    wff = wi[:, n_h + n_f:n_h + n_f + FOX_HEADS]
    wff_row = wff.T.astype(BF16)
    fb = fox_f_bias[0].astype(F32)
    fb_row = jnp.broadcast_to(fb.reshape(FOX_HEADS, 1), (FOX_HEADS, LANES))

    hg, fox, gates, crow = _in_proj(
        x2d, norm_mix_w[0].reshape(1, D), w_all, wff_row, fb_row, B, S, tm_in)
    a = _hgrn(hg, hgrn_lb_logits, hgrn_norm_w[0].reshape(1, HGRN_W), _block_select_matrix(), B, S)
    b = _fox(fox, crow, B, S, tq)
    keys = peer_sub_keys[0].reshape(2 * PEER_HEADS, N_KEYS, PEER_HALF).astype(BF16)
    h1, xn2, st = _merge(
        x2d, a, b, gates, w_branch_hgrn[0].astype(BF16), w_branch_fox[0].astype(BF16),
        w_out[0].astype(BF16), norm_ffn_w[0].reshape(1, D), peer_w_q[0].astype(BF16), keys, tm_merge)
    n_exp = peer_u.shape[1]
    u_bf = peer_u[0].astype(BF16)
    v_bf = peer_v[0].astype(BF16)
    tc = T // PEER_CHUNKS
    outs = []
    for c in range(PEER_CHUNKS):
        ids, wts = _routes(st, tt, c * tc, tc)
        gate = _gate_matrix(ids, wts, n_exp, PEER_HEADS)
        outs.append(_experts(xn2, u_bf, v_bf, gate, h1, norm_final_w.reshape(1, D), tb, eb, c * tc))
    return jnp.concatenate(outs, axis=0).reshape(B, S, D)


def kernel(x, norm_mix_w, w_in, hgrn_lb_logits, hgrn_norm_w, fox_f_bias, w_branch_hgrn, w_branch_fox, w_out, norm_ffn_w, peer_w_q, peer_sub_keys, peer_u, peer_v, norm_final_w):
    return _forward(x, norm_mix_w, w_in, hgrn_lb_logits, hgrn_norm_w, fox_f_bias, w_branch_hgrn,
                    w_branch_fox, w_out, norm_ffn_w, peer_w_q, peer_sub_keys, peer_u, peer_v,
                    norm_final_w, tm_in=512, tq=512, tm_merge=512, tt=256, tb=1024, eb=1024)
```

```python
import functools
import math

import jax
import jax.numpy as jnp
from jax import lax
from jax.experimental import pallas as pl
from jax.experimental.pallas import tpu as pltpu
from jax.experimental.pallas import tpu_sc as plsc

F32 = jnp.float32
BF16 = jnp.bfloat16
RMS_EPS = 1e-6
NEG_BIG = -1e30

HGRN_HEADS = 4
HGRN_DH = 128
HGRN_W = HGRN_HEADS * HGRN_DH
FOX_HEADS = 8
FOX_DH = 64
FOX_W = FOX_HEADS * FOX_DH
PEER_HEADS = 8
PEER_HALF = 128
N_KEYS = 128
TOPK = 16
LANES = 128

VMEM_LIMIT = 56 * 1024 * 1024


def _nt(a, b):
    return lax.dot_general(a, b, (((1,), (1,)), ((), ())), preferred_element_type=F32)


def _tn(a, b):
    return lax.dot_general(a, b, (((0,), (0,)), ((), ())), preferred_element_type=F32)


def _nn(a, b):
    return jnp.dot(a, b, preferred_element_type=F32)


def _split_dot(fn, tri, x):
    hi = x.astype(BF16)
    lo = (x - hi.astype(F32)).astype(BF16)
    return fn(tri, hi) + fn(tri, lo)


def _log_sigmoid(x):
    return jnp.minimum(x, 0.0) - jnp.log1p(jnp.exp(-jnp.abs(x)))


def _rms(x, w):
    return x * lax.rsqrt(jnp.mean(x * x, axis=-1, keepdims=True) + RMS_EPS) * w


LOG2E = math.log2(math.e)


def _inproj_kernel(x_ref, nw_ref, w_ref, wffr_ref, fbr_ref,
                   hg_ref, fox_ref, gate_ref, crow_ref, carry_row):
    i = pl.program_id(1)
    tm = x_ref.shape[0]

    @pl.when(i == 0)
    def _():
        carry_row[...] = jnp.zeros_like(carry_row)

    xn = _rms(x_ref[...], nw_ref[...]).astype(BF16)
    proj = _nn(xn, w_ref[...])
    n_h = 4 * HGRN_W
    hg_ref[...] = proj[:, :n_h]
    fox_ref[:, :FOX_W] = (proj[:, n_h:n_h + FOX_W] * (FOX_DH ** -0.5 * LOG2E)).astype(BF16)
    fox_ref[:, FOX_W:] = proj[:, n_h + FOX_W:n_h + 3 * FOX_W].astype(BF16)
    gate_ref[...] = jax.nn.sigmoid(proj[:, n_h + 3 * FOX_W:])

    r = lax.broadcasted_iota(jnp.int32, (tm, tm), 0)
    c = lax.broadcasted_iota(jnp.int32, (tm, tm), 1)
    triu = (r <= c).astype(BF16)
    ls_row = _log_sigmoid(_nt(wffr_ref[...], xn) + fbr_ref[:, 0:1]) * LOG2E
    hi = ls_row.astype(BF16)
    lo = (ls_row - hi.astype(F32)).astype(BF16)
    crow = _nn(hi, triu) + _nn(lo, triu) + carry_row[:, 0:1]
    crow_ref[...] = crow
    carry_row[...] = jnp.broadcast_to(crow[:, tm - 1:tm], carry_row.shape)


def _in_proj(x2d, nw, w_all, wff_row, fb_row, batch, seq, tm):
    T, D = x2d.shape
    nt = seq // tm
    n_all = w_all.shape[1]
    row = lambda b, i: (b * nt + i, 0)
    const = lambda b, i: (0, 0)
    return pl.pallas_call(
        _inproj_kernel,
        grid=(batch, nt),
        in_specs=[
            pl.BlockSpec((tm, D), row),
            pl.BlockSpec((1, D), const),
            pl.BlockSpec((D, n_all), const),
            pl.BlockSpec((FOX_HEADS, D), const),
            pl.BlockSpec((FOX_HEADS, LANES), const),
        ],
        out_specs=[
            pl.BlockSpec((tm, 4 * HGRN_W), row),
            pl.BlockSpec((tm, 3 * FOX_W), row),
            pl.BlockSpec((tm, 2 * D), row),
            pl.BlockSpec((FOX_HEADS, tm), lambda b, i: (0, b * nt + i)),
        ],
        out_shape=[
            jax.ShapeDtypeStruct((T, 4 * HGRN_W), F32),
            jax.ShapeDtypeStruct((T, 3 * FOX_W), BF16),
            jax.ShapeDtypeStruct((T, 2 * D), F32),
            jax.ShapeDtypeStruct((FOX_HEADS, T), F32),
        ],
        scratch_shapes=[pltpu.VMEM((FOX_HEADS, LANES), F32)],
        compiler_params=pltpu.CompilerParams(
            dimension_semantics=("arbitrary", "arbitrary"), vmem_limit_bytes=VMEM_LIMIT),
        name="in_proj",
    )(x2d, nw, w_all, wff_row, fb_row)


HGRN_CHUNK = 128
HGRN_SUB = 8


def _hgrn_levels():
    out = []
    m = HGRN_SUB
    while m < HGRN_CHUNK:
        out.append(m)
        m *= 2
    return out


def _hgrn_masks():
    C, sub = HGRN_CHUNK, HGRN_SUB
    r = jnp.arange(C, dtype=jnp.int32)[:, None]
    c = jnp.arange(C, dtype=jnp.int32)[None, :]
    masks = [((r // sub) == (c // sub)) & ((c % sub) <= (r % sub))]
    for m in _hgrn_levels():
        masks.append(((r // (2 * m)) == (c // (2 * m))) & (((r // m) % 2) == 1) & (((c // m) % 2) == 0))
    return jnp.stack(masks).astype(F32)


def _hgrn_kernel(hg_ref, lbl_ref, nw_ref, rsel_ref, tril_ref, mask_ref, a_ref, state_ref):
    ci = pl.program_id(1)
    C = HGRN_CHUNK
    dh = HGRN_DH
    sub = HGRN_SUB

    @pl.when(ci == 0)
    def _():
        state_ref[...] = jnp.zeros_like(state_ref)

    lg = lbl_ref[...]
    e = jnp.exp(lg - jnp.max(lg, axis=0, keepdims=True))
    lb_all = e[0:1, :] / jnp.sum(e, axis=0, keepdims=True)
    tril = tril_ref[...]

    for h in range(HGRN_HEADS):
        sl = slice(h * dh, (h + 1) * dh)
        qraw = hg_ref[:, sl]
        q = qraw * jax.nn.sigmoid(qraw)
        lb = lb_all[:, sl]
        f = lb + (1.0 - lb) * jax.nn.sigmoid(hg_ref[:, HGRN_W + h * dh:HGRN_W + (h + 1) * dh])
        log2f = jnp.log(f) * LOG2E
        k = 1.0 - f
        v = hg_ref[:, 2 * HGRN_W + h * dh:2 * HGRN_W + (h + 1) * dh]
        g = hg_ref[:, 3 * HGRN_W + h * dh:3 * HGRN_W + (h + 1) * dh]
        v_bf = v.astype(BF16)
        cum = _split_dot(_nn, tril, log2f)

        nb = C // sub
        q3 = q.reshape(nb, sub, dh)
        k3 = k.reshape(nb, sub, dh)
        c3 = cum.reshape(nb, sub, dh)
        parts = []
        for s in range(sub):
            kb = jnp.broadcast_to(k3[:, s:s + 1, :], (nb, sub, dh))
            cb = jnp.broadcast_to(c3[:, s:s + 1, :], (nb, sub, dh))
            es = q3 * kb * jnp.exp2(jnp.minimum(c3 - cb, 0.0))
            parts.append(es.reshape(C, dh).astype(BF16))
        p_mat = mask_ref[0] * _nn(jnp.concatenate(parts, axis=1), rsel_ref[...])

        for li, m in enumerate(_hgrn_levels()):
            nbm = C // m
            qm = q.reshape(nbm, m, dh)
            km = k.reshape(nbm, m, dh)
            cm = cum.reshape(nbm, m, dh)
            end = cm[:, m - 1:m, :]
            prev_end = jnp.concatenate([jnp.zeros((1, 1, dh), F32), end[:nbm - 1]], axis=0)
            qd = qm * jnp.exp2(jnp.minimum(cm - jnp.broadcast_to(prev_end, (nbm, m, dh)), 0.0))
            kd = km * jnp.exp2(jnp.minimum(jnp.broadcast_to(end, (nbm, m, dh)) - cm, 0.0))
            sc = _nt(qd.reshape(C, dh).astype(BF16), kd.reshape(C, dh).astype(BF16))
            p_mat = p_mat + mask_ref[1 + li] * sc

        st = state_ref[h]
        o = _nn(p_mat.astype(BF16), v_bf) + _nt((q * jnp.exp2(cum)).astype(BF16), st.astype(BF16))
        last = cum[C - 1:C, :]
        kdec = (k * jnp.exp2(last - cum)).astype(BF16)
        state_ref[h] = jnp.exp2(last) * st + _tn(v_bf, kdec)

        o = o * lax.rsqrt(jnp.mean(o * o, axis=-1, keepdims=True) + RMS_EPS) * nw_ref[:, sl]
        a_ref[:, sl] = (o * (g * jax.nn.sigmoid(g))).astype(BF16)


def _hgrn(hg, lb_logits, norm_w, rsel, batch, seq):
    T = hg.shape[0]
    C = HGRN_CHUNK
    nc = seq // C
    row = lambda b, i: (b * nc + i, 0)
    const = lambda b, i: (0, 0)
    tril = jnp.tril(jnp.ones((C, C), F32)).astype(BF16)
    masks = _hgrn_masks()
    return pl.pallas_call(
        _hgrn_kernel,
        grid=(batch, nc),
        in_specs=[
            pl.BlockSpec((C, 4 * HGRN_W), row),
            pl.BlockSpec(lb_logits.shape, const),
            pl.BlockSpec((1, HGRN_W), const),
            pl.BlockSpec(rsel.shape, const),
            pl.BlockSpec((C, C), const),
            pl.BlockSpec(masks.shape, lambda b, i: (0, 0, 0)),
        ],
        out_specs=pl.BlockSpec((C, HGRN_W), row),
        out_shape=jax.ShapeDtypeStruct((T, HGRN_W), BF16),
        scratch_shapes=[pltpu.VMEM((HGRN_HEADS, HGRN_DH, HGRN_DH), F32)],
        compiler_params=pltpu.CompilerParams(
            dimension_semantics=("arbitrary", "arbitrary"), vmem_limit_bytes=VMEM_LIMIT),
        name="hgrn",
    )(hg, lb_logits, norm_w, rsel, tril, masks)


def _fox_kernel(q_ref, k_ref, v_ref, crow_ref, o_ref, m_ref, acc_ref, qs_ref, vs_ref):
    qi = pl.program_id(1)
    tq = q_ref.shape[0]
    tk = tq
    pairs = FOX_HEADS // 2
    lane = lax.broadcasted_iota(jnp.int32, (1, LANES), 1)
    low = lane < FOX_DH
    sel = (jnp.where(low, 1.0, 0.0).astype(BF16), jnp.where(low, 0.0, 1.0).astype(BF16))
    ones_lane = (FOX_DH, 0)
    one_hot = tuple(jnp.where(lane == ol, 1.0, 0.0).astype(BF16) for ol in ones_lane)
    rr = lax.broadcasted_iota(jnp.int32, (tq, tk), 0)
    cc = lax.broadcasted_iota(jnp.int32, (tq, tk), 1)
    causal = cc <= rr

    @pl.when(qi == 0)
    def _():
        for p in range(pairs):
            vb = v_ref[:, p * LANES:(p + 1) * LANES]
            for hh in range(2):
                vs_ref[2 * p + hh] = vb * sel[hh] + one_hot[hh]

    for p in range(pairs):
        q = q_ref[:, p * LANES:(p + 1) * LANES]
        for hh in range(2):
            qs_ref[2 * p + hh] = q * sel[hh]
    m_ref[...] = jnp.full_like(m_ref, NEG_BIG)
    acc_ref[...] = jnp.zeros_like(acc_ref)

    def kv_block(j, masked):
        start = pl.multiple_of(j * tk, tk)
        for p in range(pairs):
            kb = k_ref[pl.ds(start, tk), p * LANES:(p + 1) * LANES]
            s_pair = _nt(qs_ref[2 * p:2 * p + 2].reshape(2 * tq, LANES), kb)
            for hh in range(2):
                h = 2 * p + hh
                ck = crow_ref[h:h + 1, pl.ds(start, tk)]
                s = s_pair[hh * tq:(hh + 1) * tq] - ck
                if masked:
                    s = jnp.where(causal, s, NEG_BIG)
                m_prev = m_ref[h]
                m_next = jnp.maximum(m_prev, jnp.max(s, axis=1, keepdims=True))
                pexp = jnp.exp2(s - jnp.concatenate([m_next] * (tk // LANES), axis=1))
                alpha = jnp.exp2(m_prev - m_next)
                acc_ref[h] = alpha * acc_ref[h] + _nn(pexp.astype(BF16), vs_ref[h, pl.ds(start, tk), :])
                m_ref[h] = m_next

    def body(j, carry):
        kv_block(j, False)
        return carry

    lax.fori_loop(0, qi, body, 0)
    kv_block(qi, True)

    for p in range(pairs):
        a0 = acc_ref[2 * p]
        a1 = acc_ref[2 * p + 1]
        o0 = a0 / a0[:, ones_lane[0]:ones_lane[0] + 1]
        o1 = a1 / a1[:, ones_lane[1]:ones_lane[1] + 1]
        o_ref[:, p * LANES:(p + 1) * LANES] = jnp.where(low, o0, o1).astype(BF16)


def _fox(fox, crow, batch, seq, tq):
    T = fox.shape[0]
    nq = seq // tq
    return pl.pallas_call(
        _fox_kernel,
        grid=(batch, nq),
        in_specs=[
            pl.BlockSpec((tq, FOX_W), lambda b, i: (b * nq + i, 0)),
            pl.BlockSpec((seq, FOX_W), lambda b, i: (b, 1)),
            pl.BlockSpec((seq, FOX_W), lambda b, i: (b, 2)),
            pl.BlockSpec((FOX_HEADS, seq), lambda b, i: (0, b)),
        ],
        out_specs=pl.BlockSpec((tq, FOX_W), lambda b, i: (b * nq + i, 0)),
        out_shape=jax.ShapeDtypeStruct((T, FOX_W), BF16),
        scratch_shapes=[pltpu.VMEM((FOX_HEADS, tq, LANES), F32), pltpu.VMEM((FOX_HEADS, tq, LANES), F32),
                        pltpu.VMEM((FOX_HEADS, tq, LANES), BF16), pltpu.VMEM((FOX_HEADS, seq, LANES), BF16)],
        compiler_params=pltpu.CompilerParams(
            dimension_semantics=("arbitrary", "arbitrary"), vmem_limit_bytes=VMEM_LIMIT),
        name="fox",
    )(fox, fox, fox, crow)


def _merge_kernel(x_ref, a_ref, b_ref, gate_ref, wa_ref, wb_ref, wo_ref, nw_ref, wq_ref, keys_ref,
                  h1_ref, xn_ref, st_ref):
    D = x_ref.shape[1]
    merged = gate_ref[:, :D] * _nn(a_ref[...], wa_ref[...]) + gate_ref[:, D:] * _nn(b_ref[...], wb_ref[...])
    h1 = x_ref[...] + _nn(merged.astype(BF16), wo_ref[...])
    h1_ref[...] = h1
    xn = _rms(h1, nw_ref[...]).astype(BF16)
    xn_ref[...] = xn
    q = _nn(xn, wq_ref[...]).astype(BF16)
    for hp in range(2 * PEER_HEADS):
        st_ref[hp] = _nt(keys_ref[hp], q[:, hp * PEER_HALF:(hp + 1) * PEER_HALF])


def _merge(x2d, a, b, gates, wa, wb, wo, nw, wq, keys, tm):
    T, D = x2d.shape
    row = lambda i: (i, 0)
    const = lambda i: (0, 0)
    return pl.pallas_call(
        _merge_kernel,
        grid=(T // tm,),
        in_specs=[
            pl.BlockSpec((tm, D), row),
            pl.BlockSpec((tm, HGRN_W), row),
            pl.BlockSpec((tm, FOX_W), row),
            pl.BlockSpec((tm, 2 * D), row),
            pl.BlockSpec(wa.shape, const),
            pl.BlockSpec(wb.shape, const),
            pl.BlockSpec(wo.shape, const),
            pl.BlockSpec((1, D), const),
            pl.BlockSpec(wq.shape, const),
            pl.BlockSpec(keys.shape, lambda i: (0, 0, 0)),
        ],
        out_specs=[
            pl.BlockSpec((tm, D), row),
            pl.BlockSpec((tm, D), row),
            pl.BlockSpec((2 * PEER_HEADS, N_KEYS, tm), lambda i: (0, 0, i)),
        ],
        out_shape=[
            jax.ShapeDtypeStruct((T, D), F32),
            jax.ShapeDtypeStruct((T, D), BF16),
            jax.ShapeDtypeStruct((2 * PEER_HEADS, N_KEYS, T), F32),
        ],
        compiler_params=pltpu.CompilerParams(
            dimension_semantics=("arbitrary",), vmem_limit_bytes=VMEM_LIMIT),
        name="merge",
    )(x2d, a, b, gates, wa, wb, wo, nw, wq, keys)


ROUTE_SLOTS = 64


def _route_pairs():
    return [(r, c) for r in range(TOPK) for c in range(TOPK) if (r + 1) * (c + 1) <= TOPK]


def _bitonic_sort_desc(vals, idxs=None):
    n = len(vals)
    k = 2
    while k <= n:
        j = k // 2
        while j >= 1:
            for i in range(n):
                l = i ^ j
                if l > i:
                    hi, lo = (i, l) if (i & k) == 0 else (l, i)
                    a, b = vals[i], vals[l]
                    if idxs is not None:
                        gt = a > b
                        ia, ib = idxs[i], idxs[l]
                        idxs[hi] = jnp.where(gt, ia, ib)
                        idxs[lo] = jnp.where(gt, ib, ia)
                    vals[hi] = jnp.maximum(a, b)
                    vals[lo] = jnp.minimum(a, b)
            j //= 2
        k *= 2


def _top_of_two_sorted(a, b):
    n = len(a)
    out = [jnp.maximum(a[i], b[n - 1 - i]) for i in range(n)]
    j = n // 2
    while j >= 1:
        for i in range(n):
            l = i ^ j
            if l > i:
                x, y = out[i], out[l]
                out[i] = jnp.maximum(x, y)
                out[l] = jnp.minimum(x, y)
        j //= 2
    return out


def _routes_kernel(st_ref, ids_ref, w_ref):
    tt = st_ref.shape[2]
    sub = 8
    groups = N_KEYS // sub
    sub_id = lax.broadcasted_iota(jnp.int32, (sub, tt), 0).astype(F32)

    def top_sorted(s):
        vals = [s[g * sub:(g + 1) * sub, :] for g in range(groups)]
        idxs = [sub_id + float(g * sub) for g in range(groups)]
        _bitonic_sort_desc(vals, idxs)
        tops, keys = [], []
        for t in range(TOPK):
            mx = jnp.max(vals[0], axis=0, keepdims=True)
            win = vals[0] == mx
            tops.append(mx)
            keys.append(jnp.max(jnp.where(win, idxs[0], -1.0), axis=0, keepdims=True))
            for r in range(TOPK - 1 - t):
                vals[r] = jnp.where(win, vals[r + 1], vals[r])
                idxs[r] = jnp.where(win, idxs[r + 1], idxs[r])
        return tops, keys

    tops = [(top_sorted(st_ref[2 * h]), top_sorted(st_ref[2 * h + 1])) for h in range(PEER_HEADS)]

    def stack(half, which, r):
        return jnp.concatenate([tops[h][half][which][r] for h in range(PEER_HEADS)], axis=0)

    a = [stack(0, 0, r) for r in range(TOPK)]
    b = [stack(1, 0, r) for r in range(TOPK)]
    ia = [stack(0, 1, r) for r in range(TOPK)]
    ib = [stack(1, 1, r) for r in range(TOPK)]
    pairs = _route_pairs()
    cand = [a[r] + b[c] for r, c in pairs]
    neg = jnp.full((PEER_HEADS, tt), -jnp.inf, F32)
    padded = cand + [neg] * (-len(cand) % TOPK)
    best = None
    for g0 in range(0, len(padded), TOPK):
        grp = list(padded[g0:g0 + TOPK])
        _bitonic_sort_desc(grp)
        best = grp if best is None else _top_of_two_sorted(best, grp)
    tau = best[TOPK - 1]
    top = a[0] + b[0]
    e = [jnp.where(xi >= tau, jnp.exp(xi - top), 0.0) for xi in cand]
    z = e[0]
    for ei in e[1:]:
        z = z + ei
    inv_z = 1.0 / z
    ids = [jnp.clip(ia[r] * float(N_KEYS) + ib[c], 0.0, float(N_KEYS * N_KEYS - 1)) for r, c in pairs]
    wts = [ek * inv_z for ek in e]
    pad = [jnp.zeros((PEER_HEADS, tt), F32)] * (ROUTE_SLOTS - len(pairs))
    ids_ref[...] = jnp.concatenate(ids + pad, axis=0).T.astype(jnp.int32)
    w_ref[...] = jnp.concatenate(wts + pad, axis=0).T


def _routes(st, tt, t0, tc):
    n, nk, _ = st.shape
    off = t0 // tt
    width = ROUTE_SLOTS * PEER_HEADS
    spec = pl.BlockSpec((tt, width), lambda i: (i, 0))
    return pl.pallas_call(
        _routes_kernel,
        grid=(tc // tt,),
        in_specs=[pl.BlockSpec((n, nk, tt), lambda i: (0, 0, i + off))],
        out_specs=[spec, spec],
        out_shape=[jax.ShapeDtypeStruct((tc, width), jnp.int32),
                   jax.ShapeDtypeStruct((tc, width), F32)],
        compiler_params=pltpu.CompilerParams(
            dimension_semantics=("arbitrary",), vmem_limit_bytes=VMEM_LIMIT),
        name="routes",
    )(st)


SC_LANES = 16
SC_TOKENS_PER_CHUNK = 8
PEER_CHUNKS = 8


def _gate_matrix(ids, w, n_exp, heads):
    T, E = ids.shape
    info = plsc.get_sparse_core_info()
    workers = info.num_cores * info.num_subcores
    per_worker = T // workers
    ch = SC_TOKENS_PER_CHUNK
    mesh = plsc.VectorSubcoreMesh(core_axis_name="c", subcore_axis_name="s")
    ids_flat = ids.reshape(T * E)
    w_flat = w.reshape(T * E)

    @functools.partial(
        pl.kernel, mesh=mesh,
        out_type=jax.ShapeDtypeStruct((T, n_exp), F32),
        scratch_types=[pltpu.VMEM((ch * E,), jnp.int32), pltpu.VMEM((ch * E,), F32),
                       pltpu.VMEM((n_exp,), F32)],
        compiler_params=pltpu.CompilerParams(needs_layout_passes=False),
        name="gate_matrix",
    )
    def scatter(ids_hbm, w_hbm, out_hbm, ids_v, w_v, row_v):
        wid = lax.axis_index("s") * info.num_cores + lax.axis_index("c")
        base = wid * per_worker
        zeros = jnp.zeros((SC_LANES,), F32)
        lane = lax.iota(jnp.int32, SC_LANES)

        @pl.loop(0, n_exp, step=SC_LANES)
        def _(i):
            row_v[pl.ds(i, SC_LANES)] = zeros

        @pl.loop(0, per_worker // ch)
        def _(ci):
            t0 = base + ci * ch
            pltpu.sync_copy(ids_hbm.at[pl.ds(t0 * E, ch * E)], ids_v)
            pltpu.sync_copy(w_hbm.at[pl.ds(t0 * E, ch * E)], w_v)
            for tl in range(ch):
                @pl.loop(0, heads)
                def _(h):
                    for m in range(E // heads // SC_LANES):
                        pos = (lane + (tl * E // heads + m * SC_LANES)) * heads + h
                        idx = plsc.load_gather(ids_v, [pos])
                        val = plsc.load_gather(w_v, [pos])
                        plsc.addupdate_scatter(row_v, [idx], val)

                pltpu.sync_copy(row_v, out_hbm.at[t0 + tl])

                @pl.loop(0, E, step=SC_LANES)
                def _(e0):
                    sl = pl.ds(tl * E + e0, SC_LANES)
                    plsc.store_scatter(row_v, [ids_v[sl]], zeros)

    return scatter(ids_flat, w_flat)


def _experts_kernel(xn_ref, u_ref, v_ref, gate_ref, h1_ref, nw_ref, o_ref, acc_ref):
    j = pl.program_id(1)
    nj = pl.num_programs(1)

    @pl.when(j == 0)
    def _():
        acc_ref[...] = jnp.zeros_like(acc_ref)

    h = _nt(xn_ref[...], u_ref[...])
    act = 0.5 * h * (1.0 + lax.erf(h * (2.0 ** -0.5)))
    acc_ref[...] += _nn((act * gate_ref[...]).astype(BF16), v_ref[...])

    @pl.when(j == nj - 1)
    def _():
        o_ref[...] = _rms(h1_ref[...] + acc_ref[...], nw_ref[...])


def _experts(xn, u, v, gate, h1, nw, tb, eb, t0):
    tc = gate.shape[0]
    D = xn.shape[1]
    n_exp = u.shape[0]
    off = t0 // tb
    return pl.pallas_call(
        _experts_kernel,
        grid=(tc // tb, n_exp // eb),
        in_specs=[
            pl.BlockSpec((tb, D), lambda i, j: (i + off, 0)),
            pl.BlockSpec((eb, D), lambda i, j: (j, 0)),
            pl.BlockSpec((eb, D), lambda i, j: (j, 0)),
            pl.BlockSpec((tb, eb), lambda i, j: (i, j)),
            pl.BlockSpec((tb, D), lambda i, j: (i + off, 0)),
            pl.BlockSpec((1, D), lambda i, j: (0, 0)),
        ],
        out_specs=pl.BlockSpec((tb, D), lambda i, j: (i, 0)),
        out_shape=jax.ShapeDtypeStruct((tc, D), F32),
        scratch_shapes=[pltpu.VMEM((tb, D), F32)],
        compiler_params=pltpu.CompilerParams(
            dimension_semantics=("arbitrary", "arbitrary"), vmem_limit_bytes=VMEM_LIMIT),
        name="experts",
    )(xn, u, v, gate, h1, nw)


def _block_select_matrix():
    part = jnp.arange(HGRN_SUB * HGRN_DH, dtype=jnp.int32) // HGRN_DH
    col = jnp.arange(HGRN_CHUNK, dtype=jnp.int32) % HGRN_SUB
    return (part[:, None] == col[None, :]).astype(BF16)


def _forward(x, norm_mix_w, w_in, hgrn_lb_logits, hgrn_norm_w, fox_f_bias, w_branch_hgrn,
             w_branch_fox, w_out, norm_ffn_w, peer_w_q, peer_sub_keys, peer_u, peer_v,
             norm_final_w, *, tm_in, tq, tm_merge, tt, tb, eb):
    B, S, D = x.shape
    T = B * S
    x2d = x.reshape(T, D)
    n_h = 4 * HGRN_W
    n_f = 3 * FOX_W
    wi = w_in[0]
    w_all = jnp.concatenate([wi[:, :n_h + n_f], wi[:, n_h + n_f + FOX_HEADS:]], axis=1).astype(BF16)
    wff = wi[:, n_h + n_f:n_h + n_f + FOX_HEADS]
    wff_row = wff.T.astype(BF16)
    fb = fox_f_bias[0].astype(F32)
    fb_row = jnp.broadcast_to(fb.reshape(FOX_HEADS, 1), (FOX_HEADS, LANES))

    hg, fox, gates, crow = _in_proj(
        x2d, norm_mix_w[0].reshape(1, D), w_all, wff_row, fb_row, B, S, tm_in)
    a = _hgrn(hg, hgrn_lb_logits, hgrn_norm_w[0].reshape(1, HGRN_W), _block_select_matrix(), B, S)
    b = _fox(fox, crow, B, S, tq)
    keys = peer_sub_keys[0].reshape(2 * PEER_HEADS, N_KEYS, PEER_HALF).astype(BF16)
    h1, xn2, st = _merge(
        x2d, a, b, gates, w_branch_hgrn[0].astype(BF16), w_branch_fox[0].astype(BF16),
        w_out[0].astype(BF16), norm_ffn_w[0].reshape(1, D), peer_w_q[0].astype(BF16), keys, tm_merge)
    n_exp = peer_u.shape[1]
    u_bf = peer_u[0].astype(BF16)
    v_bf = peer_v[0].astype(BF16)
    tc = T // PEER_CHUNKS
    outs = []
    for c in range(PEER_CHUNKS):
        ids, wts = _routes(st, tt, c * tc, tc)
        gate = _gate_matrix(ids, wts, n_exp, PEER_HEADS)
        outs.append(_experts(xn2, u_bf, v_bf, gate, h1, norm_final_w.reshape(1, D), tb, eb, c * tc))
    return jnp.concatenate(outs, axis=0).reshape(B, S, D)


def kernel(x, norm_mix_w, w_in, hgrn_lb_logits, hgrn_norm_w, fox_f_bias, w_branch_hgrn, w_branch_fox, w_out, norm_ffn_w, peer_w_q, peer_sub_keys, peer_u, peer_v, norm_final_w):
    return _forward(x, norm_mix_w, w_in, hgrn_lb_logits, hgrn_norm_w, fox_f_bias, w_branch_hgrn,
                    w_branch_fox, w_out, norm_ffn_w, peer_w_q, peer_sub_keys, peer_u, peer_v,
                    norm_final_w, tm_in=512, tq=512, tm_merge=512, tt=512, tb=1024, eb=1024)
```

```python
import functools
import math

import jax
import jax.numpy as jnp
from jax import lax
from jax.experimental import pallas as pl
from jax.experimental.pallas import tpu as pltpu
from jax.experimental.pallas import tpu_sc as plsc

F32 = jnp.float32
BF16 = jnp.bfloat16
RMS_EPS = 1e-6
NEG_BIG = -1e30

HGRN_HEADS = 4
HGRN_DH = 128
HGRN_W = HGRN_HEADS * HGRN_DH
FOX_HEADS = 8
FOX_DH = 64
FOX_W = FOX_HEADS * FOX_DH
PEER_HEADS = 8
PEER_HALF = 128
N_KEYS = 128
TOPK = 16
LANES = 128

VMEM_LIMIT = 56 * 1024 * 1024


def _nt(a, b):
    return lax.dot_general(a, b, (((1,), (1,)), ((), ())), preferred_element_type=F32)


def _tn(a, b):
    return lax.dot_general(a, b, (((0,), (0,)), ((), ())), preferred_element_type=F32)


def _nn(a, b):
    return jnp.dot(a, b, preferred_element_type=F32)


def _split_dot(fn, tri, x):
    hi = x.astype(BF16)
    lo = (x - hi.astype(F32)).astype(BF16)
    return fn(tri, hi) + fn(tri, lo)


def _log_sigmoid(x):
    return jnp.minimum(x, 0.0) - jnp.log1p(jnp.exp(-jnp.abs(x)))


def _rms(x, w):
    return x * lax.rsqrt(jnp.mean(x * x, axis=-1, keepdims=True) + RMS_EPS) * w


LOG2E = math.log2(math.e)


def _inproj_kernel(x_ref, nw_ref, w_ref, wffr_ref, fbr_ref,
                   hg_ref, fox_ref, gate_ref, crow_ref, carry_row):
    i = pl.program_id(1)
    tm = x_ref.shape[0]

    @pl.when(i == 0)
    def _():
        carry_row[...] = jnp.zeros_like(carry_row)

    xn = _rms(x_ref[...], nw_ref[...]).astype(BF16)
    proj = _nn(xn, w_ref[...])
    n_h = 4 * HGRN_W
    hg_ref[...] = proj[:, :n_h]
    fox_ref[:, :FOX_W] = (proj[:, n_h:n_h + FOX_W] * (FOX_DH ** -0.5 * LOG2E)).astype(BF16)
    fox_ref[:, FOX_W:] = proj[:, n_h + FOX_W:n_h + 3 * FOX_W].astype(BF16)
    gate_ref[...] = jax.nn.sigmoid(proj[:, n_h + 3 * FOX_W:])

    r = lax.broadcasted_iota(jnp.int32, (tm, tm), 0)
    c = lax.broadcasted_iota(jnp.int32, (tm, tm), 1)
    triu = (r <= c).astype(BF16)
    ls_row = _log_sigmoid(_nt(wffr_ref[...], xn) + fbr_ref[:, 0:1]) * LOG2E
    hi = ls_row.astype(BF16)
    lo = (ls_row - hi.astype(F32)).astype(BF16)
    crow = _nn(hi, triu) + _nn(lo, triu) + carry_row[:, 0:1]
    crow_ref[...] = crow
    carry_row[...] = jnp.broadcast_to(crow[:, tm - 1:tm], carry_row.shape)


def _in_proj(x2d, nw, w_all, wff_row, fb_row, batch, seq, tm):
    T, D = x2d.shape
    nt = seq // tm
    n_all = w_all.shape[1]
    row = lambda b, i: (b * nt + i, 0)
    const = lambda b, i: (0, 0)
    return pl.pallas_call(
        _inproj_kernel,
        grid=(batch, nt),
        in_specs=[
            pl.BlockSpec((tm, D), row),
            pl.BlockSpec((1, D), const),
            pl.BlockSpec((D, n_all), const),
            pl.BlockSpec((FOX_HEADS, D), const),
            pl.BlockSpec((FOX_HEADS, LANES), const),
        ],
        out_specs=[
            pl.BlockSpec((tm, 4 * HGRN_W), row),
            pl.BlockSpec((tm, 3 * FOX_W), row),
            pl.BlockSpec((tm, 2 * D), row),
            pl.BlockSpec((FOX_HEADS, tm), lambda b, i: (0, b * nt + i)),
        ],
        out_shape=[
            jax.ShapeDtypeStruct((T, 4 * HGRN_W), F32),
            jax.ShapeDtypeStruct((T, 3 * FOX_W), BF16),
            jax.ShapeDtypeStruct((T, 2 * D), F32),
            jax.ShapeDtypeStruct((FOX_HEADS, T), F32),
        ],
        scratch_shapes=[pltpu.VMEM((FOX_HEADS, LANES), F32)],
        compiler_params=pltpu.CompilerParams(
            dimension_semantics=("arbitrary", "arbitrary"), vmem_limit_bytes=VMEM_LIMIT),
        name="in_proj",
    )(x2d, nw, w_all, wff_row, fb_row)


HGRN_CHUNK = 128
HGRN_SUB = 8


def _hgrn_levels():
    out = []
    m = HGRN_SUB
    while m < HGRN_CHUNK:
        out.append(m)
        m *= 2
    return out


def _hgrn_masks():
    C, sub = HGRN_CHUNK, HGRN_SUB
    r = jnp.arange(C, dtype=jnp.int32)[:, None]
    c = jnp.arange(C, dtype=jnp.int32)[None, :]
    masks = [((r // sub) == (c // sub)) & ((c % sub) <= (r % sub))]
    for m in _hgrn_levels():
        masks.append(((r // (2 * m)) == (c // (2 * m))) & (((r // m) % 2) == 1) & (((c // m) % 2) == 0))
    return jnp.stack(masks).astype(F32)


def _hgrn_kernel(hg_ref, lbl_ref, nw_ref, rsel_ref, tril_ref, mask_ref, a_ref, state_ref):
    ci = pl.program_id(1)
    C = HGRN_CHUNK
    dh = HGRN_DH
    sub = HGRN_SUB

    @pl.when(ci == 0)
    def _():
        state_ref[...] = jnp.zeros_like(state_ref)

    lg = lbl_ref[...]
    e = jnp.exp(lg - jnp.max(lg, axis=0, keepdims=True))
    lb_all = e[0:1, :] / jnp.sum(e, axis=0, keepdims=True)
    tril = tril_ref[...]

    for h in range(HGRN_HEADS):
        sl = slice(h * dh, (h + 1) * dh)
        qraw = hg_ref[:, sl]
        q = qraw * jax.nn.sigmoid(qraw)
        lb = lb_all[:, sl]
        f = lb + (1.0 - lb) * jax.nn.sigmoid(hg_ref[:, HGRN_W + h * dh:HGRN_W + (h + 1) * dh])
        log2f = jnp.log(f) * LOG2E
        k = 1.0 - f
        v = hg_ref[:, 2 * HGRN_W + h * dh:2 * HGRN_W + (h + 1) * dh]
        g = hg_ref[:, 3 * HGRN_W + h * dh:3 * HGRN_W + (h + 1) * dh]
        v_bf = v.astype(BF16)
        cum = _split_dot(_nn, tril, log2f)

        nb = C // sub
        q3 = q.reshape(nb, sub, dh)
        k3 = k.reshape(nb, sub, dh)
        c3 = cum.reshape(nb, sub, dh)
        parts = []
        for s in range(sub):
            kb = jnp.broadcast_to(k3[:, s:s + 1, :], (nb, sub, dh))
            cb = jnp.broadcast_to(c3[:, s:s + 1, :], (nb, sub, dh))
            es = q3 * kb * jnp.exp2(jnp.minimum(c3 - cb, 0.0))
            parts.append(es.reshape(C, dh).astype(BF16))
        p_mat = mask_ref[0] * _nn(jnp.concatenate(parts, axis=1), rsel_ref[...])

        for li, m in enumerate(_hgrn_levels()):
            nbm = C // m
            qm = q.reshape(nbm, m, dh)
            km = k.reshape(nbm, m, dh)
            cm = cum.reshape(nbm, m, dh)
            end = cm[:, m - 1:m, :]
            prev_end = jnp.concatenate([jnp.zeros((1, 1, dh), F32), end[:nbm - 1]], axis=0)
            qd = qm * jnp.exp2(jnp.minimum(cm - jnp.broadcast_to(prev_end, (nbm, m, dh)), 0.0))
            kd = km * jnp.exp2(jnp.minimum(jnp.broadcast_to(end, (nbm, m, dh)) - cm, 0.0))
            sc = _nt(qd.reshape(C, dh).astype(BF16), kd.reshape(C, dh).astype(BF16))
            p_mat = p_mat + mask_ref[1 + li] * sc

        st = state_ref[h]
        o = _nn(p_mat.astype(BF16), v_bf) + _nt((q * jnp.exp2(cum)).astype(BF16), st.astype(BF16))
        last = cum[C - 1:C, :]
        kdec = (k * jnp.exp2(last - cum)).astype(BF16)
        state_ref[h] = jnp.exp2(last) * st + _tn(v_bf, kdec)

        o = o * lax.rsqrt(jnp.mean(o * o, axis=-1, keepdims=True) + RMS_EPS) * nw_ref[:, sl]
        a_ref[:, sl] = (o * (g * jax.nn.sigmoid(g))).astype(BF16)


def _hgrn(hg, lb_logits, norm_w, rsel, batch, seq):
    T = hg.shape[0]
    C = HGRN_CHUNK
    nc = seq // C
    row = lambda b, i: (b * nc + i, 0)
    const = lambda b, i: (0, 0)
    tril = jnp.tril(jnp.ones((C, C), F32)).astype(BF16)
    masks = _hgrn_masks()
    return pl.pallas_call(
        _hgrn_kernel,
        grid=(batch, nc),
        in_specs=[
            pl.BlockSpec((C, 4 * HGRN_W), row),
            pl.BlockSpec(lb_logits.shape, const),
            pl.BlockSpec((1, HGRN_W), const),
            pl.BlockSpec(rsel.shape, const),
            pl.BlockSpec((C, C), const),
            pl.BlockSpec(masks.shape, lambda b, i: (0, 0, 0)),
        ],
        out_specs=pl.BlockSpec((C, HGRN_W), row),
        out_shape=jax.ShapeDtypeStruct((T, HGRN_W), BF16),
        scratch_shapes=[pltpu.VMEM((HGRN_HEADS, HGRN_DH, HGRN_DH), F32)],
        compiler_params=pltpu.CompilerParams(
            dimension_semantics=("arbitrary", "arbitrary"), vmem_limit_bytes=VMEM_LIMIT),
        name="hgrn",
    )(hg, lb_logits, norm_w, rsel, tril, masks)


def _fox_kernel(q_ref, k_ref, v_ref, crow_ref, o_ref, m_ref, acc_ref, qs_ref, vs_ref):
    qi = pl.program_id(1)
    tq = q_ref.shape[0]
    tk = tq
    pairs = FOX_HEADS // 2
    lane = lax.broadcasted_iota(jnp.int32, (1, LANES), 1)
    low = lane < FOX_DH
    sel = (jnp.where(low, 1.0, 0.0).astype(BF16), jnp.where(low, 0.0, 1.0).astype(BF16))
    ones_lane = (FOX_DH, 0)
    one_hot = tuple(jnp.where(lane == ol, 1.0, 0.0).astype(BF16) for ol in ones_lane)
    rr = lax.broadcasted_iota(jnp.int32, (tq, tk), 0)
    cc = lax.broadcasted_iota(jnp.int32, (tq, tk), 1)
    causal = cc <= rr

    @pl.when(qi == 0)
    def _():
        for p in range(pairs):
            vb = v_ref[:, p * LANES:(p + 1) * LANES]
            for hh in range(2):
                vs_ref[2 * p + hh] = vb * sel[hh] + one_hot[hh]

    for p in range(pairs):
        q = q_ref[:, p * LANES:(p + 1) * LANES]
        for hh in range(2):
            qs_ref[2 * p + hh] = q * sel[hh]
    m_ref[...] = jnp.full_like(m_ref, NEG_BIG)
    acc_ref[...] = jnp.zeros_like(acc_ref)

    def kv_block(j, masked):
        start = pl.multiple_of(j * tk, tk)
        for p in range(pairs):
            kb = k_ref[pl.ds(start, tk), p * LANES:(p + 1) * LANES]
            s_pair = _nt(qs_ref[2 * p:2 * p + 2].reshape(2 * tq, LANES), kb)
            for hh in range(2):
                h = 2 * p + hh
                ck = crow_ref[h:h + 1, pl.ds(start, tk)]
                s = s_pair[hh * tq:(hh + 1) * tq] - ck
                if masked:
                    s = jnp.where(causal, s, NEG_BIG)
                m_prev = m_ref[h]
                m_next = jnp.maximum(m_prev, jnp.max(s, axis=1, keepdims=True))
                pexp = jnp.exp2(s - jnp.concatenate([m_next] * (tk // LANES), axis=1))
                alpha = jnp.exp2(m_prev - m_next)
                acc_ref[h] = alpha * acc_ref[h] + _nn(pexp.astype(BF16), vs_ref[h, pl.ds(start, tk), :])
                m_ref[h] = m_next

    def body(j, carry):
        kv_block(j, False)
        return carry

    lax.fori_loop(0, qi, body, 0)
    kv_block(qi, True)

    for p in range(pairs):
        a0 = acc_ref[2 * p]
        a1 = acc_ref[2 * p + 1]
        o0 = a0 / a0[:, ones_lane[0]:ones_lane[0] + 1]
        o1 = a1 / a1[:, ones_lane[1]:ones_lane[1] + 1]
        o_ref[:, p * LANES:(p + 1) * LANES] = jnp.where(low, o0, o1).astype(BF16)


def _fox(fox, crow, batch, seq, tq):
    T = fox.shape[0]
    nq = seq // tq
    return pl.pallas_call(
        _fox_kernel,
        grid=(batch, nq),
        in_specs=[
            pl.BlockSpec((tq, FOX_W), lambda b, i: (b * nq + i, 0)),
            pl.BlockSpec((seq, FOX_W), lambda b, i: (b, 1)),
            pl.BlockSpec((seq, FOX_W), lambda b, i: (b, 2)),
            pl.BlockSpec((FOX_HEADS, seq), lambda b, i: (0, b)),
        ],
        out_specs=pl.BlockSpec((tq, FOX_W), lambda b, i: (b * nq + i, 0)),
        out_shape=jax.ShapeDtypeStruct((T, FOX_W), BF16),
        scratch_shapes=[pltpu.VMEM((FOX_HEADS, tq, LANES), F32), pltpu.VMEM((FOX_HEADS, tq, LANES), F32),
                        pltpu.VMEM((FOX_HEADS, tq, LANES), BF16), pltpu.VMEM((FOX_HEADS, seq, LANES), BF16)],
        compiler_params=pltpu.CompilerParams(
            dimension_semantics=("arbitrary", "arbitrary"), vmem_limit_bytes=VMEM_LIMIT),
        name="fox",
    )(fox, fox, fox, crow)


def _merge_kernel(x_ref, a_ref, b_ref, gate_ref, wa_ref, wb_ref, wo_ref, nw_ref, wq_ref, keys_ref,
                  h1_ref, xn_ref, st_ref):
    D = x_ref.shape[1]
    merged = gate_ref[:, :D] * _nn(a_ref[...], wa_ref[...]) + gate_ref[:, D:] * _nn(b_ref[...], wb_ref[...])
    h1 = x_ref[...] + _nn(merged.astype(BF16), wo_ref[...])
    h1_ref[...] = h1
    xn = _rms(h1, nw_ref[...]).astype(BF16)
    xn_ref[...] = xn
    q = _nn(xn, wq_ref[...]).astype(BF16)
    for hp in range(2 * PEER_HEADS):
        st_ref[hp] = _nt(keys_ref[hp], q[:, hp * PEER_HALF:(hp + 1) * PEER_HALF])


def _merge(x2d, a, b, gates, wa, wb, wo, nw, wq, keys, tm):
    T, D = x2d.shape
    row = lambda i: (i, 0)
    const = lambda i: (0, 0)
    return pl.pallas_call(
        _merge_kernel,
        grid=(T // tm,),
        in_specs=[
            pl.BlockSpec((tm, D), row),
            pl.BlockSpec((tm, HGRN_W), row),
            pl.BlockSpec((tm, FOX_W), row),
            pl.BlockSpec((tm, 2 * D), row),
            pl.BlockSpec(wa.shape, const),
            pl.BlockSpec(wb.shape, const),
            pl.BlockSpec(wo.shape, const),
            pl.BlockSpec((1, D), const),
            pl.BlockSpec(wq.shape, const),
            pl.BlockSpec(keys.shape, lambda i: (0, 0, 0)),
        ],
        out_specs=[
            pl.BlockSpec((tm, D), row),
            pl.BlockSpec((tm, D), row),
            pl.BlockSpec((2 * PEER_HEADS, N_KEYS, tm), lambda i: (0, 0, i)),
        ],
        out_shape=[
            jax.ShapeDtypeStruct((T, D), F32),
            jax.ShapeDtypeStruct((T, D), BF16),
            jax.ShapeDtypeStruct((2 * PEER_HEADS, N_KEYS, T), F32),
        ],
        compiler_params=pltpu.CompilerParams(
            dimension_semantics=("arbitrary",), vmem_limit_bytes=VMEM_LIMIT),
        name="merge",
    )(x2d, a, b, gates, wa, wb, wo, nw, wq, keys)


ROUTE_SLOTS = 64


def _route_pairs():
    return [(r, c) for r in range(TOPK) for c in range(TOPK) if (r + 1) * (c + 1) <= TOPK]


def _bitonic_sort_desc(vals, idxs=None):
    n = len(vals)
    k = 2
    while k <= n:
        j = k // 2
        while j >= 1:
            for i in range(n):
                l = i ^ j
                if l > i:
                    hi, lo = (i, l) if (i & k) == 0 else (l, i)
                    a, b = vals[i], vals[l]
                    if idxs is not None:
                        gt = a > b
                        ia, ib = idxs[i], idxs[l]
                        idxs[hi] = jnp.where(gt, ia, ib)
                        idxs[lo] = jnp.where(gt, ib, ia)
                    vals[hi] = jnp.maximum(a, b)
                    vals[lo] = jnp.minimum(a, b)
            j //= 2
        k *= 2


def _top_of_two_sorted(a, b):
    n = len(a)
    out = [jnp.maximum(a[i], b[n - 1 - i]) for i in range(n)]
    j = n // 2
    while j >= 1:
        for i in range(n):
            l = i ^ j
            if l > i:
                x, y = out[i], out[l]
                out[i] = jnp.maximum(x, y)
                out[l] = jnp.minimum(x, y)
        j //= 2
    return out


def _routes_kernel(st_ref, ids_ref, w_ref):
    tt = st_ref.shape[2]
    sub = 8
    groups = N_KEYS // sub
    sub_id = lax.broadcasted_iota(jnp.int32, (sub, tt), 0).astype(F32)

    def top_sorted(s):
        vals = [s[g * sub:(g + 1) * sub, :] for g in range(groups)]
        idxs = [sub_id + float(g * sub) for g in range(groups)]
        _bitonic_sort_desc(vals, idxs)
        tops, keys = [], []
        for t in range(TOPK):
            mx = jnp.max(vals[0], axis=0, keepdims=True)
            win = vals[0] == mx
            tops.append(mx)
            keys.append(jnp.max(jnp.where(win, idxs[0], -1.0), axis=0, keepdims=True))
            for r in range(TOPK - 1 - t):
                vals[r] = jnp.where(win, vals[r + 1], vals[r])
                idxs[r] = jnp.where(win, idxs[r + 1], idxs[r])
        return tops, keys

    tops = [(top_sorted(st_ref[2 * h]), top_sorted(st_ref[2 * h + 1])) for h in range(PEER_HEADS)]

    def stack(half, which, r):
        return jnp.concatenate([tops[h][half][which][r] for h in range(PEER_HEADS)], axis=0)

    a = [stack(0, 0, r) for r in range(TOPK)]
    b = [stack(1, 0, r) for r in range(TOPK)]
    ia = [stack(0, 1, r) for r in range(TOPK)]
    ib = [stack(1, 1, r) for r in range(TOPK)]
    pairs = _route_pairs()
    cand = [a[r] + b[c] for r, c in pairs]
    neg = jnp.full((PEER_HEADS, tt), -jnp.inf, F32)
    padded = cand + [neg] * (-len(cand) % TOPK)
    best = None
    for g0 in range(0, len(padded), TOPK):
        grp = list(padded[g0:g0 + TOPK])
        _bitonic_sort_desc(grp)
        best = grp if best is None else _top_of_two_sorted(best, grp)
    tau = best[TOPK - 1]
    top = a[0] + b[0]
    e = [jnp.where(xi >= tau, jnp.exp(xi - top), 0.0) for xi in cand]
    z = e[0]
    for ei in e[1:]:
        z = z + ei
    inv_z = 1.0 / z
    ids = [jnp.clip(ia[r] * float(N_KEYS) + ib[c], 0.0, float(N_KEYS * N_KEYS - 1)) for r, c in pairs]
    wts = [ek * inv_z for ek in e]
    pad = [jnp.zeros((PEER_HEADS, tt), F32)] * (ROUTE_SLOTS - len(pairs))
    ids_ref[...] = jnp.concatenate(ids + pad, axis=0).T.astype(jnp.int32)
    w_ref[...] = jnp.concatenate(wts + pad, axis=0).T


def _routes(st, tt, t0, tc):
    n, nk, _ = st.shape
    off = t0 // tt
    width = ROUTE_SLOTS * PEER_HEADS
    spec = pl.BlockSpec((tt, width), lambda i: (i, 0))
    return pl.pallas_call(
        _routes_kernel,
        grid=(tc // tt,),
        in_specs=[pl.BlockSpec((n, nk, tt), lambda i: (0, 0, i + off))],
        out_specs=[spec, spec],
        out_shape=[jax.ShapeDtypeStruct((tc, width), jnp.int32),
                   jax.ShapeDtypeStruct((tc, width), F32)],
        compiler_params=pltpu.CompilerParams(
            dimension_semantics=("arbitrary",), vmem_limit_bytes=VMEM_LIMIT),
        name="routes",
    )(st)


SC_LANES = 16
SC_TOKENS_PER_CHUNK = 8
PEER_CHUNKS = 8


def _gate_matrix(ids, w, n_exp, heads):
    T, E = ids.shape
    info = plsc.get_sparse_core_info()
    workers = info.num_cores * info.num_subcores
    per_worker = T // workers
    ch = SC_TOKENS_PER_CHUNK
    mesh = plsc.VectorSubcoreMesh(core_axis_name="c", subcore_axis_name="s")
    ids_flat = ids.reshape(T * E)
    w_flat = w.reshape(T * E)

    @functools.partial(
        pl.kernel, mesh=mesh,
        out_type=jax.ShapeDtypeStruct((T, n_exp), F32),
        scratch_types=[pltpu.VMEM((ch * E,), jnp.int32), pltpu.VMEM((ch * E,), F32),
                       pltpu.VMEM((n_exp,), F32)],
        compiler_params=pltpu.CompilerParams(needs_layout_passes=False),
        name="gate_matrix",
    )
    def scatter(ids_hbm, w_hbm, out_hbm, ids_v, w_v, row_v):
        wid = lax.axis_index("s") * info.num_cores + lax.axis_index("c")
        base = wid * per_worker
        zeros = jnp.zeros((SC_LANES,), F32)
        lane = lax.iota(jnp.int32, SC_LANES)

        @pl.loop(0, n_exp, step=SC_LANES)
        def _(i):
            row_v[pl.ds(i, SC_LANES)] = zeros

        @pl.loop(0, per_worker // ch)
        def _(ci):
            t0 = base + ci * ch
            pltpu.sync_copy(ids_hbm.at[pl.ds(t0 * E, ch * E)], ids_v)
            pltpu.sync_copy(w_hbm.at[pl.ds(t0 * E, ch * E)], w_v)
            for tl in range(ch):
                @pl.loop(0, heads)
                def _(h):
                    for m in range(E // heads // SC_LANES):
                        pos = (lane + (tl * E // heads + m * SC_LANES)) * heads + h
                        idx = plsc.load_gather(ids_v, [pos])
                        val = plsc.load_gather(w_v, [pos])
                        plsc.addupdate_scatter(row_v, [idx], val)

                pltpu.sync_copy(row_v, out_hbm.at[t0 + tl])

                @pl.loop(0, E, step=SC_LANES)
                def _(e0):
                    sl = pl.ds(tl * E + e0, SC_LANES)
                    plsc.store_scatter(row_v, [ids_v[sl]], zeros)

    return scatter(ids_flat, w_flat)


def _experts_kernel(xn_ref, u_ref, v_ref, gate_ref, h1_ref, nw_ref, *rest):
    o_ref, acc_ref = rest[-2:]
    j = pl.program_id(1)
    nj = pl.num_programs(1)

    @pl.when(j == 0)
    def _():
        acc_ref[...] = jnp.zeros_like(acc_ref)

    h = _nt(xn_ref[...], u_ref[...])
    act = 0.5 * h * (1.0 + lax.erf(h * (2.0 ** -0.5)))
    acc_ref[...] += _nn((act * gate_ref[...]).astype(BF16), v_ref[...])

    @pl.when(j == nj - 1)
    def _():
        o_ref[...] = _rms(h1_ref[...] + acc_ref[...], nw_ref[...])


def _experts(xn, u, v, gate, h1, nw, tb, eb, t0, out_prev):
    tc = gate.shape[0]
    T, D = xn.shape
    n_exp = u.shape[0]
    off = t0 // tb
    in_specs = [
        pl.BlockSpec((tb, D), lambda i, j: (i + off, 0)),
        pl.BlockSpec((eb, D), lambda i, j: (j, 0)),
        pl.BlockSpec((eb, D), lambda i, j: (j, 0)),
        pl.BlockSpec((tb, eb), lambda i, j: (i, j)),
        pl.BlockSpec((tb, D), lambda i, j: (i + off, 0)),
        pl.BlockSpec((1, D), lambda i, j: (0, 0)),
    ]
    args = [xn, u, v, gate, h1, nw]
    aliases = {}
    if out_prev is not None:
        in_specs.append(pl.BlockSpec(memory_space=pl.ANY))
        args.append(out_prev)
        aliases = {len(args) - 1: 0}
    return pl.pallas_call(
        _experts_kernel,
        grid=(tc // tb, n_exp // eb),
        in_specs=in_specs,
        out_specs=pl.BlockSpec((tb, D), lambda i, j: (i + off, 0)),
        out_shape=jax.ShapeDtypeStruct((T, D), F32),
        scratch_shapes=[pltpu.VMEM((tb, D), F32)],
        input_output_aliases=aliases,
        compiler_params=pltpu.CompilerParams(
            dimension_semantics=("arbitrary", "arbitrary"), vmem_limit_bytes=VMEM_LIMIT),
        name="experts",
    )(*args)


def _block_select_matrix():
    part = jnp.arange(HGRN_SUB * HGRN_DH, dtype=jnp.int32) // HGRN_DH
    col = jnp.arange(HGRN_CHUNK, dtype=jnp.int32) % HGRN_SUB
    return (part[:, None] == col[None, :]).astype(BF16)


def _forward(x, norm_mix_w, w_in, hgrn_lb_logits, hgrn_norm_w, fox_f_bias, w_branch_hgrn,
             w_branch_fox, w_out, norm_ffn_w, peer_w_q, peer_sub_keys, peer_u, peer_v,
             norm_final_w, *, tm_in, tq, tm_merge, tt, tb, eb):
    B, S, D = x.shape
    T = B * S
    x2d = x.reshape(T, D)
    n_h = 4 * HGRN_W
    n_f = 3 * FOX_W
    wi = w_in[0]
    w_all = jnp.concatenate([wi[:, :n_h + n_f], wi[:, n_h + n_f + FOX_HEADS:]], axis=1).astype(BF16)
    wff = wi[:, n_h + n_f:n_h + n_f + FOX_HEADS]
    wff_row = wff.T.astype(BF16)
    fb = fox_f_bias[0].astype(F32)
    fb_row = jnp.broadcast_to(fb.reshape(FOX_HEADS, 1), (FOX_HEADS, LANES))

    hg, fox, gates, crow = _in_proj(
        x2d, norm_mix_w[0].reshape(1, D), w_all, wff_row, fb_row, B, S, tm_in)
    a = _hgrn(hg, hgrn_lb_logits, hgrn_norm_w[0].reshape(1, HGRN_W), _block_select_matrix(), B, S)
    b = _fox(fox, crow, B, S, tq)
    keys = peer_sub_keys[0].reshape(2 * PEER_HEADS, N_KEYS, PEER_HALF).astype(BF16)
    h1, xn2, st = _merge(
        x2d, a, b, gates, w_branch_hgrn[0].astype(BF16), w_branch_fox[0].astype(BF16),
        w_out[0].astype(BF16), norm_ffn_w[0].reshape(1, D), peer_w_q[0].astype(BF16), keys, tm_merge)
    n_exp = peer_u.shape[1]
    u_bf = peer_u[0].astype(BF16)
    v_bf = peer_v[0].astype(BF16)
    tc = T // PEER_CHUNKS
    bounds = [0, tb, tc] + [c * tc for c in range(2, PEER_CHUNKS + 1)]
    out = None
    for t0, t1 in zip(bounds[:-1], bounds[1:]):
        ids, wts = _routes(st, tt, t0, t1 - t0)
        gate = _gate_matrix(ids, wts, n_exp, PEER_HEADS)
        out = _experts(xn2, u_bf, v_bf, gate, h1, norm_final_w.reshape(1, D), tb, eb, t0, out)
    return out.reshape(B, S, D)


def kernel(x, norm_mix_w, w_in, hgrn_lb_logits, hgrn_norm_w, fox_f_bias, w_branch_hgrn, w_branch_fox, w_out, norm_ffn_w, peer_w_q, peer_sub_keys, peer_u, peer_v, norm_final_w):
    return _forward(x, norm_mix_w, w_in, hgrn_lb_logits, hgrn_norm_w, fox_f_bias, w_branch_hgrn,
                    w_branch_fox, w_out, norm_ffn_w, peer_w_q, peer_sub_keys, peer_u, peer_v,
                    norm_final_w, tm_in=512, tq=512, tm_merge=512, tt=512, tb=1024, eb=1024)
```

```python
import functools
import math

import jax
import jax.numpy as jnp
from jax import lax
from jax.experimental import pallas as pl
from jax.experimental.pallas import tpu as pltpu
from jax.experimental.pallas import tpu_sc as plsc

F32 = jnp.float32
BF16 = jnp.bfloat16
RMS_EPS = 1e-6
NEG_BIG = -1e30

HGRN_HEADS = 4
HGRN_DH = 128
HGRN_W = HGRN_HEADS * HGRN_DH
FOX_HEADS = 8
FOX_DH = 64
FOX_W = FOX_HEADS * FOX_DH
PEER_HEADS = 8
PEER_HALF = 128
N_KEYS = 128
TOPK = 16
LANES = 128

V7X_VMEM_BYTES = 64 * 1024 * 1024
VMEM_LIMIT = V7X_VMEM_BYTES * 7 // 8

TILES = dict(
    tm_in=512,
    tq=512,
    tm_merge=512,
    tt=512,
    tb=1024,
    eb=1024,
)


def _nt(a, b):
    return lax.dot_general(a, b, (((1,), (1,)), ((), ())), preferred_element_type=F32)


def _tn(a, b):
    return lax.dot_general(a, b, (((0,), (0,)), ((), ())), preferred_element_type=F32)


def _nn(a, b):
    return jnp.dot(a, b, preferred_element_type=F32)


def _split_dot(fn, tri, x):
    hi = x.astype(BF16)
    lo = (x - hi.astype(F32)).astype(BF16)
    return fn(tri, hi) + fn(tri, lo)


def _log_sigmoid(x):
    return jnp.minimum(x, 0.0) - jnp.log1p(jnp.exp(-jnp.abs(x)))


def _rms(x, w):
    return x * lax.rsqrt(jnp.mean(x * x, axis=-1, keepdims=True) + RMS_EPS) * w


LOG2E = math.log2(math.e)


def _inproj_kernel(x_ref, nw_ref, w_ref, wffr_ref, fbr_ref,
                   hg_ref, fox_ref, gate_ref, crow_ref, carry_row):
    i = pl.program_id(1)
    tm = x_ref.shape[0]

    @pl.when(i == 0)
    def _():
        carry_row[...] = jnp.zeros_like(carry_row)

    xn = _rms(x_ref[...], nw_ref[...]).astype(BF16)
    proj = _nn(xn, w_ref[...])
    n_h = 4 * HGRN_W
    hg_ref[...] = proj[:, :n_h]
    fox_ref[:, :FOX_W] = (proj[:, n_h:n_h + FOX_W] * (FOX_DH ** -0.5 * LOG2E)).astype(BF16)
    fox_ref[:, FOX_W:] = proj[:, n_h + FOX_W:n_h + 3 * FOX_W].astype(BF16)
    gate_ref[...] = jax.nn.sigmoid(proj[:, n_h + 3 * FOX_W:])

    r = lax.broadcasted_iota(jnp.int32, (tm, tm), 0)
    c = lax.broadcasted_iota(jnp.int32, (tm, tm), 1)
    triu = (r <= c).astype(BF16)
    ls_row = _log_sigmoid(_nt(wffr_ref[...], xn) + fbr_ref[:, 0:1]) * LOG2E
    hi = ls_row.astype(BF16)
    lo = (ls_row - hi.astype(F32)).astype(BF16)
    crow = _nn(hi, triu) + _nn(lo, triu) + carry_row[:, 0:1]
    crow_ref[...] = crow
    carry_row[...] = jnp.broadcast_to(crow[:, tm - 1:tm], carry_row.shape)


def _in_proj(x2d, nw, w_all, wff_row, fb_row, batch, seq, tm):
    T, D = x2d.shape
    nt = seq // tm
    n_all = w_all.shape[1]
    row = lambda b, i: (b * nt + i, 0)
    const = lambda b, i: (0, 0)
    return pl.pallas_call(
        _inproj_kernel,
        grid=(batch, nt),
        in_specs=[
            pl.BlockSpec((tm, D), row),
            pl.BlockSpec((1, D), const),
            pl.BlockSpec((D, n_all), const),
            pl.BlockSpec((FOX_HEADS, D), const),
            pl.BlockSpec((FOX_HEADS, LANES), const),
        ],
        out_specs=[
            pl.BlockSpec((tm, 4 * HGRN_W), row),
            pl.BlockSpec((tm, 3 * FOX_W), row),
            pl.BlockSpec((tm, 2 * D), row),
            pl.BlockSpec((FOX_HEADS, tm), lambda b, i: (0, b * nt + i)),
        ],
        out_shape=[
            jax.ShapeDtypeStruct((T, 4 * HGRN_W), F32),
            jax.ShapeDtypeStruct((T, 3 * FOX_W), BF16),
            jax.ShapeDtypeStruct((T, 2 * D), F32),
            jax.ShapeDtypeStruct((FOX_HEADS, T), F32),
        ],
        scratch_shapes=[pltpu.VMEM((FOX_HEADS, LANES), F32)],
        compiler_params=pltpu.CompilerParams(
            dimension_semantics=("arbitrary", "arbitrary"), vmem_limit_bytes=VMEM_LIMIT),
        name="in_proj",
    )(x2d, nw, w_all, wff_row, fb_row)


HGRN_CHUNK = 128
HGRN_SUB = 8


def _hgrn_levels():
    out = []
    m = HGRN_SUB
    while m < HGRN_CHUNK:
        out.append(m)
        m *= 2
    return out


def _hgrn_masks():
    C, sub = HGRN_CHUNK, HGRN_SUB
    r = jnp.arange(C, dtype=jnp.int32)[:, None]
    c = jnp.arange(C, dtype=jnp.int32)[None, :]
    masks = [((r // sub) == (c // sub)) & ((c % sub) <= (r % sub))]
    for m in _hgrn_levels():
        masks.append(((r // (2 * m)) == (c // (2 * m))) & (((r // m) % 2) == 1) & (((c // m) % 2) == 0))
    return jnp.stack(masks).astype(F32)


def _hgrn_kernel(hg_ref, lbl_ref, nw_ref, rsel_ref, tril_ref, mask_ref, a_ref, state_ref):
    ci = pl.program_id(1)
    C = HGRN_CHUNK
    dh = HGRN_DH
    sub = HGRN_SUB

    @pl.when(ci == 0)
    def _():
        state_ref[...] = jnp.zeros_like(state_ref)

    lg = lbl_ref[...]
    e = jnp.exp(lg - jnp.max(lg, axis=0, keepdims=True))
    lb_all = e[0:1, :] / jnp.sum(e, axis=0, keepdims=True)
    tril = tril_ref[...]

    for h in range(HGRN_HEADS):
        sl = slice(h * dh, (h + 1) * dh)
        qraw = hg_ref[:, sl]
        q = qraw * jax.nn.sigmoid(qraw)
        lb = lb_all[:, sl]
        f = lb + (1.0 - lb) * jax.nn.sigmoid(hg_ref[:, HGRN_W + h * dh:HGRN_W + (h + 1) * dh])
        log2f = jnp.log(f) * LOG2E
        k = 1.0 - f
        v = hg_ref[:, 2 * HGRN_W + h * dh:2 * HGRN_W + (h + 1) * dh]
        g = hg_ref[:, 3 * HGRN_W + h * dh:3 * HGRN_W + (h + 1) * dh]
        v_bf = v.astype(BF16)
        cum = _split_dot(_nn, tril, log2f)

        nb = C // sub
        q3 = q.reshape(nb, sub, dh)
        k3 = k.reshape(nb, sub, dh)
        c3 = cum.reshape(nb, sub, dh)
        parts = []
        for s in range(sub):
            kb = jnp.broadcast_to(k3[:, s:s + 1, :], (nb, sub, dh))
            cb = jnp.broadcast_to(c3[:, s:s + 1, :], (nb, sub, dh))
            es = q3 * kb * jnp.exp2(jnp.minimum(c3 - cb, 0.0))
            parts.append(es.reshape(C, dh).astype(BF16))
        p_mat = mask_ref[0] * _nn(jnp.concatenate(parts, axis=1), rsel_ref[...])

        for li, m in enumerate(_hgrn_levels()):
            nbm = C // m
            qm = q.reshape(nbm, m, dh)
            km = k.reshape(nbm, m, dh)
            cm = cum.reshape(nbm, m, dh)
            end = cm[:, m - 1:m, :]
            prev_end = jnp.concatenate([jnp.zeros((1, 1, dh), F32), end[:nbm - 1]], axis=0)
            qd = qm * jnp.exp2(jnp.minimum(cm - jnp.broadcast_to(prev_end, (nbm, m, dh)), 0.0))
            kd = km * jnp.exp2(jnp.minimum(jnp.broadcast_to(end, (nbm, m, dh)) - cm, 0.0))
            sc = _nt(qd.reshape(C, dh).astype(BF16), kd.reshape(C, dh).astype(BF16))
            p_mat = p_mat + mask_ref[1 + li] * sc

        st = state_ref[h]
        o = _nn(p_mat.astype(BF16), v_bf) + _nt((q * jnp.exp2(cum)).astype(BF16), st.astype(BF16))
        last = cum[C - 1:C, :]
        kdec = (k * jnp.exp2(last - cum)).astype(BF16)
        state_ref[h] = jnp.exp2(last) * st + _tn(v_bf, kdec)

        o = o * lax.rsqrt(jnp.mean(o * o, axis=-1, keepdims=True) + RMS_EPS) * nw_ref[:, sl]
        a_ref[:, sl] = (o * (g * jax.nn.sigmoid(g))).astype(BF16)


def _hgrn(hg, lb_logits, norm_w, rsel, batch, seq):
    T = hg.shape[0]
    C = HGRN_CHUNK
    nc = seq // C
    row = lambda b, i: (b * nc + i, 0)
    const = lambda b, i: (0, 0)
    tril = jnp.tril(jnp.ones((C, C), F32)).astype(BF16)
    masks = _hgrn_masks()
    return pl.pallas_call(
        _hgrn_kernel,
        grid=(batch, nc),
        in_specs=[
            pl.BlockSpec((C, 4 * HGRN_W), row),
            pl.BlockSpec(lb_logits.shape, const),
            pl.BlockSpec((1, HGRN_W), const),
            pl.BlockSpec(rsel.shape, const),
            pl.BlockSpec((C, C), const),
            pl.BlockSpec(masks.shape, lambda b, i: (0, 0, 0)),
        ],
        out_specs=pl.BlockSpec((C, HGRN_W), row),
        out_shape=jax.ShapeDtypeStruct((T, HGRN_W), BF16),
        scratch_shapes=[pltpu.VMEM((HGRN_HEADS, HGRN_DH, HGRN_DH), F32)],
        compiler_params=pltpu.CompilerParams(
            dimension_semantics=("arbitrary", "arbitrary"), vmem_limit_bytes=VMEM_LIMIT),
        name="hgrn",
    )(hg, lb_logits, norm_w, rsel, tril, masks)


def _fox_kernel(q_ref, k_ref, v_ref, crow_ref, o_ref, m_ref, acc_ref, qs_ref, vs_ref):
    qi = pl.program_id(1)
    tq = q_ref.shape[0]
    tk = tq
    pairs = FOX_HEADS // 2
    lane = lax.broadcasted_iota(jnp.int32, (1, LANES), 1)
    low = lane < FOX_DH
    sel = (jnp.where(low, 1.0, 0.0).astype(BF16), jnp.where(low, 0.0, 1.0).astype(BF16))
    ones_lane = (FOX_DH, 0)
    one_hot = tuple(jnp.where(lane == ol, 1.0, 0.0).astype(BF16) for ol in ones_lane)
    rr = lax.broadcasted_iota(jnp.int32, (tq, tk), 0)
    cc = lax.broadcasted_iota(jnp.int32, (tq, tk), 1)
    causal = cc <= rr

    @pl.when(qi == 0)
    def _():
        for p in range(pairs):
            vb = v_ref[:, p * LANES:(p + 1) * LANES]
            for hh in range(2):
                vs_ref[2 * p + hh] = vb * sel[hh] + one_hot[hh]

    for p in range(pairs):
        q = q_ref[:, p * LANES:(p + 1) * LANES]
        for hh in range(2):
            qs_ref[2 * p + hh] = q * sel[hh]
    m_ref[...] = jnp.full_like(m_ref, NEG_BIG)
    acc_ref[...] = jnp.zeros_like(acc_ref)

    def kv_block(j, masked):
        start = pl.multiple_of(j * tk, tk)
        for p in range(pairs):
            kb = k_ref[pl.ds(start, tk), p * LANES:(p + 1) * LANES]
            s_pair = _nt(qs_ref[2 * p:2 * p + 2].reshape(2 * tq, LANES), kb)
            for hh in range(2):
                h = 2 * p + hh
                ck = crow_ref[h:h + 1, pl.ds(start, tk)]
                s = s_pair[hh * tq:(hh + 1) * tq] - ck
                if masked:
                    s = jnp.where(causal, s, NEG_BIG)
                m_prev = m_ref[h]
                m_next = jnp.maximum(m_prev, jnp.max(s, axis=1, keepdims=True))
                pexp = jnp.exp2(s - jnp.concatenate([m_next] * (tk // LANES), axis=1))
                alpha = jnp.exp2(m_prev - m_next)
                acc_ref[h] = alpha * acc_ref[h] + _nn(pexp.astype(BF16), vs_ref[h, pl.ds(start, tk), :])
                m_ref[h] = m_next

    def body(j, carry):
        kv_block(j, False)
        return carry

    lax.fori_loop(0, qi, body, 0)
    kv_block(qi, True)

    for p in range(pairs):
        a0 = acc_ref[2 * p]
        a1 = acc_ref[2 * p + 1]
        o0 = a0 / a0[:, ones_lane[0]:ones_lane[0] + 1]
        o1 = a1 / a1[:, ones_lane[1]:ones_lane[1] + 1]
        o_ref[:, p * LANES:(p + 1) * LANES] = jnp.where(low, o0, o1).astype(BF16)


def _fox(fox, crow, batch, seq, tq):
    T = fox.shape[0]
    nq = seq // tq
    return pl.pallas_call(
        _fox_kernel,
        grid=(batch, nq),
        in_specs=[
            pl.BlockSpec((tq, FOX_W), lambda b, i: (b * nq + i, 0)),
            pl.BlockSpec((seq, FOX_W), lambda b, i: (b, 1)),
            pl.BlockSpec((seq, FOX_W), lambda b, i: (b, 2)),
            pl.BlockSpec((FOX_HEADS, seq), lambda b, i: (0, b)),
        ],
        out_specs=pl.BlockSpec((tq, FOX_W), lambda b, i: (b * nq + i, 0)),
        out_shape=jax.ShapeDtypeStruct((T, FOX_W), BF16),
        scratch_shapes=[pltpu.VMEM((FOX_HEADS, tq, LANES), F32), pltpu.VMEM((FOX_HEADS, tq, LANES), F32),
                        pltpu.VMEM((FOX_HEADS, tq, LANES), BF16), pltpu.VMEM((FOX_HEADS, seq, LANES), BF16)],
        compiler_params=pltpu.CompilerParams(
            dimension_semantics=("arbitrary", "arbitrary"), vmem_limit_bytes=VMEM_LIMIT),
        name="fox",
    )(fox, fox, fox, crow)


def _merge_kernel(x_ref, a_ref, b_ref, gate_ref, wa_ref, wb_ref, wo_ref, nw_ref, wq_ref, keys_ref,
                  h1_ref, xn_ref, st_ref):
    D = x_ref.shape[1]
    merged = gate_ref[:, :D] * _nn(a_ref[...], wa_ref[...]) + gate_ref[:, D:] * _nn(b_ref[...], wb_ref[...])
    h1 = x_ref[...] + _nn(merged.astype(BF16), wo_ref[...])
    h1_ref[...] = h1
    xn = _rms(h1, nw_ref[...]).astype(BF16)
    xn_ref[...] = xn
    q = _nn(xn, wq_ref[...]).astype(BF16)
    for hp in range(2 * PEER_HEADS):
        st_ref[hp] = _nt(keys_ref[hp], q[:, hp * PEER_HALF:(hp + 1) * PEER_HALF])


def _merge(x2d, a, b, gates, wa, wb, wo, nw, wq, keys, tm):
    T, D = x2d.shape
    row = lambda i: (i, 0)
    const = lambda i: (0, 0)
    return pl.pallas_call(
        _merge_kernel,
        grid=(T // tm,),
        in_specs=[
            pl.BlockSpec((tm, D), row),
            pl.BlockSpec((tm, HGRN_W), row),
            pl.BlockSpec((tm, FOX_W), row),
            pl.BlockSpec((tm, 2 * D), row),
            pl.BlockSpec(wa.shape, const),
            pl.BlockSpec(wb.shape, const),
            pl.BlockSpec(wo.shape, const),
            pl.BlockSpec((1, D), const),
            pl.BlockSpec(wq.shape, const),
            pl.BlockSpec(keys.shape, lambda i: (0, 0, 0)),
        ],
        out_specs=[
            pl.BlockSpec((tm, D), row),
            pl.BlockSpec((tm, D), row),
            pl.BlockSpec((2 * PEER_HEADS, N_KEYS, tm), lambda i: (0, 0, i)),
        ],
        out_shape=[
            jax.ShapeDtypeStruct((T, D), F32),
            jax.ShapeDtypeStruct((T, D), BF16),
            jax.ShapeDtypeStruct((2 * PEER_HEADS, N_KEYS, T), F32),
        ],
        compiler_params=pltpu.CompilerParams(
            dimension_semantics=("arbitrary",), vmem_limit_bytes=VMEM_LIMIT),
        name="merge",
    )(x2d, a, b, gates, wa, wb, wo, nw, wq, keys)


ROUTE_SLOTS = 64


def _route_pairs():
    return [(r, c) for r in range(TOPK) for c in range(TOPK) if (r + 1) * (c + 1) <= TOPK]


def _bitonic_sort_desc(vals, idxs=None):
    n = len(vals)
    k = 2
    while k <= n:
        j = k // 2
        while j >= 1:
            for i in range(n):
                l = i ^ j
                if l > i:
                    hi, lo = (i, l) if (i & k) == 0 else (l, i)
                    a, b = vals[i], vals[l]
                    if idxs is not None:
                        gt = a > b
                        ia, ib = idxs[i], idxs[l]
                        idxs[hi] = jnp.where(gt, ia, ib)
                        idxs[lo] = jnp.where(gt, ib, ia)
                    vals[hi] = jnp.maximum(a, b)
                    vals[lo] = jnp.minimum(a, b)
            j //= 2
        k *= 2


def _top_of_two_sorted(a, b):
    n = len(a)
    out = [jnp.maximum(a[i], b[n - 1 - i]) for i in range(n)]
    j = n // 2
    while j >= 1:
        for i in range(n):
            l = i ^ j
            if l > i:
                x, y = out[i], out[l]
                out[i] = jnp.maximum(x, y)
                out[l] = jnp.minimum(x, y)
        j //= 2
    return out


def _routes_kernel(st_ref, ids_ref, w_ref):
    tt = st_ref.shape[2]
    sub = 8
    groups = N_KEYS // sub
    sub_id = lax.broadcasted_iota(jnp.int32, (sub, tt), 0).astype(F32)

    def top_sorted(s):
        vals = [s[g * sub:(g + 1) * sub, :] for g in range(groups)]
        idxs = [sub_id + float(g * sub) for g in range(groups)]
        _bitonic_sort_desc(vals, idxs)
        tops, keys = [], []
        for t in range(TOPK):
            mx = jnp.max(vals[0], axis=0, keepdims=True)
            win = vals[0] == mx
            tops.append(mx)
            keys.append(jnp.max(jnp.where(win, idxs[0], -1.0), axis=0, keepdims=True))
            for r in range(TOPK - 1 - t):
                vals[r] = jnp.where(win, vals[r + 1], vals[r])
                idxs[r] = jnp.where(win, idxs[r + 1], idxs[r])
        return tops, keys

    tops = [(top_sorted(st_ref[2 * h]), top_sorted(st_ref[2 * h + 1])) for h in range(PEER_HEADS)]

    def stack(half, which, r):
        return jnp.concatenate([tops[h][half][which][r] for h in range(PEER_HEADS)], axis=0)

    a = [stack(0, 0, r) for r in range(TOPK)]
    b = [stack(1, 0, r) for r in range(TOPK)]
    ia = [stack(0, 1, r) for r in range(TOPK)]
    ib = [stack(1, 1, r) for r in range(TOPK)]
    pairs = _route_pairs()
    cand = [a[r] + b[c] for r, c in pairs]
    neg = jnp.full((PEER_HEADS, tt), -jnp.inf, F32)
    padded = cand + [neg] * (-len(cand) % TOPK)
    best = None
    for g0 in range(0, len(padded), TOPK):
        grp = list(padded[g0:g0 + TOPK])
        _bitonic_sort_desc(grp)
        best = grp if best is None else _top_of_two_sorted(best, grp)
    tau = best[TOPK - 1]
    top = a[0] + b[0]
    e = [jnp.where(xi >= tau, jnp.exp(xi - top), 0.0) for xi in cand]
    z = e[0]
    for ei in e[1:]:
        z = z + ei
    inv_z = 1.0 / z
    ids = [jnp.clip(ia[r] * float(N_KEYS) + ib[c], 0.0, float(N_KEYS * N_KEYS - 1)) for r, c in pairs]
    wts = [ek * inv_z for ek in e]
    pad = [jnp.zeros((PEER_HEADS, tt), F32)] * (ROUTE_SLOTS - len(pairs))
    ids_ref[...] = jnp.concatenate(ids + pad, axis=0).T.astype(jnp.int32)
    w_ref[...] = jnp.concatenate(wts + pad, axis=0).T


def _routes(st, tt, t0, tc):
    n, nk, _ = st.shape
    off = t0 // tt
    width = ROUTE_SLOTS * PEER_HEADS
    spec = pl.BlockSpec((tt, width), lambda i: (i, 0))
    return pl.pallas_call(
        _routes_kernel,
        grid=(tc // tt,),
        in_specs=[pl.BlockSpec((n, nk, tt), lambda i: (0, 0, i + off))],
        out_specs=[spec, spec],
        out_shape=[jax.ShapeDtypeStruct((tc, width), jnp.int32),
                   jax.ShapeDtypeStruct((tc, width), F32)],
        compiler_params=pltpu.CompilerParams(
            dimension_semantics=("arbitrary",), vmem_limit_bytes=VMEM_LIMIT),
        name="routes",
    )(st)


SC_LANES = 16
SC_TOKENS_PER_CHUNK = 8
PEER_CHUNKS = 8


def _gate_matrix(ids, w, n_exp, heads):
    T, E = ids.shape
    info = plsc.get_sparse_core_info()
    workers = info.num_cores * info.num_subcores
    per_worker = T // workers
    ch = SC_TOKENS_PER_CHUNK
    assert info.num_lanes == SC_LANES and T % (workers * ch) == 0
    assert E % (heads * SC_LANES) == 0 and n_exp % SC_LANES == 0
    mesh = plsc.VectorSubcoreMesh(core_axis_name="c", subcore_axis_name="s")
    ids_flat = ids.reshape(T * E)
    w_flat = w.reshape(T * E)

    @functools.partial(
        pl.kernel, mesh=mesh,
        out_type=jax.ShapeDtypeStruct((T, n_exp), F32),
        scratch_types=[pltpu.VMEM((ch * E,), jnp.int32), pltpu.VMEM((ch * E,), F32),
                       pltpu.VMEM((n_exp,), F32)],
        compiler_params=pltpu.CompilerParams(needs_layout_passes=False),
        name="gate_matrix",
    )
    def scatter(ids_hbm, w_hbm, out_hbm, ids_v, w_v, row_v):
        wid = lax.axis_index("s") * info.num_cores + lax.axis_index("c")
        base = wid * per_worker
        zeros = jnp.zeros((SC_LANES,), F32)
        lane = lax.iota(jnp.int32, SC_LANES)

        @pl.loop(0, n_exp, step=SC_LANES)
        def _(i):
            row_v[pl.ds(i, SC_LANES)] = zeros

        @pl.loop(0, per_worker // ch)
        def _(ci):
            t0 = base + ci * ch
            pltpu.sync_copy(ids_hbm.at[pl.ds(t0 * E, ch * E)], ids_v)
            pltpu.sync_copy(w_hbm.at[pl.ds(t0 * E, ch * E)], w_v)
            for tl in range(ch):
                @pl.loop(0, heads)
                def _(h):
                    for m in range(E // heads // SC_LANES):
                        pos = (lane + (tl * E // heads + m * SC_LANES)) * heads + h
                        idx = plsc.load_gather(ids_v, [pos])
                        val = plsc.load_gather(w_v, [pos])
                        plsc.addupdate_scatter(row_v, [idx], val)

                pltpu.sync_copy(row_v, out_hbm.at[t0 + tl])

                @pl.loop(0, E, step=SC_LANES)
                def _(e0):
                    sl = pl.ds(tl * E + e0, SC_LANES)
                    plsc.store_scatter(row_v, [ids_v[sl]], zeros)

    return scatter(ids_flat, w_flat)


def _experts_kernel(xn_ref, u_ref, v_ref, gate_ref, h1_ref, nw_ref, *rest):
    o_ref, acc_ref = rest[-2:]
    j = pl.program_id(1)
    nj = pl.num_programs(1)

    @pl.when(j == 0)
    def _():
        acc_ref[...] = jnp.zeros_like(acc_ref)

    h = _nt(xn_ref[...], u_ref[...])
    act = 0.5 * h * (1.0 + lax.erf(h * (2.0 ** -0.5)))
    acc_ref[...] += _nn((act * gate_ref[...]).astype(BF16), v_ref[...])

    @pl.when(j == nj - 1)
    def _():
        o_ref[...] = _rms(h1_ref[...] + acc_ref[...], nw_ref[...])


def _experts(xn, u, v, gate, h1, nw, tb, eb, t0, out_prev):
    tc = gate.shape[0]
    T, D = xn.shape
    n_exp = u.shape[0]
    off = t0 // tb
    in_specs = [
        pl.BlockSpec((tb, D), lambda i, j: (i + off, 0)),
        pl.BlockSpec((eb, D), lambda i, j: (j, 0)),
        pl.BlockSpec((eb, D), lambda i, j: (j, 0)),
        pl.BlockSpec((tb, eb), lambda i, j: (i, j)),
        pl.BlockSpec((tb, D), lambda i, j: (i + off, 0)),
        pl.BlockSpec((1, D), lambda i, j: (0, 0)),
    ]
    args = [xn, u, v, gate, h1, nw]
    aliases = {}
    if out_prev is not None:
        in_specs.append(pl.BlockSpec(memory_space=pl.ANY))
        args.append(out_prev)
        aliases = {len(args) - 1: 0}
    return pl.pallas_call(
        _experts_kernel,
        grid=(tc // tb, n_exp // eb),
        in_specs=in_specs,
        out_specs=pl.BlockSpec((tb, D), lambda i, j: (i + off, 0)),
        out_shape=jax.ShapeDtypeStruct((T, D), F32),
        scratch_shapes=[pltpu.VMEM((tb, D), F32)],
        input_output_aliases=aliases,
        compiler_params=pltpu.CompilerParams(
            dimension_semantics=("arbitrary", "arbitrary"), vmem_limit_bytes=VMEM_LIMIT),
        name="experts",
    )(*args)


def _block_select_matrix():
    part = jnp.arange(HGRN_SUB * HGRN_DH, dtype=jnp.int32) // HGRN_DH
    col = jnp.arange(HGRN_CHUNK, dtype=jnp.int32) % HGRN_SUB
    return (part[:, None] == col[None, :]).astype(BF16)


def _forward(x, norm_mix_w, w_in, hgrn_lb_logits, hgrn_norm_w, fox_f_bias, w_branch_hgrn,
             w_branch_fox, w_out, norm_ffn_w, peer_w_q, peer_sub_keys, peer_u, peer_v,
             norm_final_w, *, tm_in, tq, tm_merge, tt, tb, eb):
    B, S, D = x.shape
    T = B * S
    n_h = 4 * HGRN_W
    n_f = 3 * FOX_W
    n_exp = N_KEYS * N_KEYS
    assert w_in.shape == (1, D, n_h + n_f + FOX_HEADS + 2 * D), w_in.shape
    assert hgrn_lb_logits.shape == (2, HGRN_W) and fox_f_bias.shape == (1, FOX_HEADS)
    assert peer_w_q.shape == (1, D, 2 * PEER_HEADS * PEER_HALF)
    assert peer_sub_keys.shape == (1, PEER_HEADS, 2, N_KEYS, PEER_HALF)
    assert peer_u.shape == (1, n_exp, D) and peer_v.shape == (1, n_exp, D)
    assert S % max(tm_in, tq, HGRN_CHUNK) == 0 and T % (PEER_CHUNKS * tb) == 0
    assert tb % tt == 0 and tb % tm_merge == 0 and n_exp % eb == 0
    x2d = x.reshape(T, D)
    wi = w_in[0]
    w_all = jnp.concatenate([wi[:, :n_h + n_f], wi[:, n_h + n_f + FOX_HEADS:]], axis=1).astype(BF16)
    wff = wi[:, n_h + n_f:n_h + n_f + FOX_HEADS]
    wff_row = wff.T.astype(BF16)
    fb = fox_f_bias[0].astype(F32)
    fb_row = jnp.broadcast_to(fb.reshape(FOX_HEADS, 1), (FOX_HEADS, LANES))

    hg, fox, gates, crow = _in_proj(
        x2d, norm_mix_w[0].reshape(1, D), w_all, wff_row, fb_row, B, S, tm_in)
    a = _hgrn(hg, hgrn_lb_logits, hgrn_norm_w[0].reshape(1, HGRN_W), _block_select_matrix(), B, S)
    b = _fox(fox, crow, B, S, tq)
    keys = peer_sub_keys[0].reshape(2 * PEER_HEADS, N_KEYS, PEER_HALF).astype(BF16)
    h1, xn2, st = _merge(
        x2d, a, b, gates, w_branch_hgrn[0].astype(BF16), w_branch_fox[0].astype(BF16),
        w_out[0].astype(BF16), norm_ffn_w[0].reshape(1, D), peer_w_q[0].astype(BF16), keys, tm_merge)
    u_bf = peer_u[0].astype(BF16)
    v_bf = peer_v[0].astype(BF16)
    tc = T // PEER_CHUNKS
    bounds = [0, tb, tc] + [c * tc for c in range(2, PEER_CHUNKS + 1)]
    out = None
    for t0, t1 in zip(bounds[:-1], bounds[1:]):
        ids, wts = _routes(st, tt, t0, t1 - t0)
        gate = _gate_matrix(ids, wts, n_exp, PEER_HEADS)
        out = _experts(xn2, u_bf, v_bf, gate, h1, norm_final_w.reshape(1, D), tb, eb, t0, out)
    return out.reshape(B, S, D)


def kernel(x, norm_mix_w, w_in, hgrn_lb_logits, hgrn_norm_w, fox_f_bias, w_branch_hgrn, w_branch_fox, w_out, norm_ffn_w, peer_w_q, peer_sub_keys, peer_u, peer_v, norm_final_w):
    return _forward(x, norm_mix_w, w_in, hgrn_lb_logits, hgrn_norm_w, fox_f_bias, w_branch_hgrn,
                    w_branch_fox, w_out, norm_ffn_w, peer_w_q, peer_sub_keys, peer_u, peer_v,
                    norm_final_w, **TILES)
```

```python
import functools
import math

import jax
import jax.numpy as jnp
from jax import lax
from jax.experimental import pallas as pl
from jax.experimental.pallas import tpu as pltpu
from jax.experimental.pallas import tpu_sc as plsc

F32 = jnp.float32
BF16 = jnp.bfloat16
RMS_EPS = 1e-6
NEG_BIG = -1e30

HGRN_HEADS = 4
HGRN_DH = 128
HGRN_W = HGRN_HEADS * HGRN_DH
FOX_HEADS = 8
FOX_DH = 64
FOX_W = FOX_HEADS * FOX_DH
PEER_HEADS = 8
PEER_HALF = 128
N_KEYS = 128
TOPK = 16
LANES = 128

V7X_VMEM_BYTES = 64 * 1024 * 1024
VMEM_LIMIT = V7X_VMEM_BYTES * 7 // 8

TILES = dict(
    tm_in=512,
    tq=512,
    tm_merge=512,
    tt=512,
    tb=1024,
    eb=1024,
)


def _nt(a, b):
    return lax.dot_general(a, b, (((1,), (1,)), ((), ())), preferred_element_type=F32)


def _tn(a, b):
    return lax.dot_general(a, b, (((0,), (0,)), ((), ())), preferred_element_type=F32)


def _nn(a, b):
    return jnp.dot(a, b, preferred_element_type=F32)


def _split_dot(fn, tri, x):
    hi = x.astype(BF16)
    lo = (x - hi.astype(F32)).astype(BF16)
    return fn(tri, hi) + fn(tri, lo)


def _log_sigmoid(x):
    return jnp.minimum(x, 0.0) - jnp.log1p(jnp.exp(-jnp.abs(x)))


def _rms(x, w):
    return x * lax.rsqrt(jnp.mean(x * x, axis=-1, keepdims=True) + RMS_EPS) * w


LOG2E = math.log2(math.e)


def _inproj_kernel(x_ref, nw_ref, w_ref, wffr_ref, fbr_ref,
                   hg_ref, fox_ref, gate_ref, crow_ref, carry_row):
    i = pl.program_id(1)
    tm = x_ref.shape[0]

    @pl.when(i == 0)
    def _():
        carry_row[...] = jnp.zeros_like(carry_row)

    xn = _rms(x_ref[...], nw_ref[...]).astype(BF16)
    proj = _nn(xn, w_ref[...])
    n_h = 4 * HGRN_W
    hg_ref[...] = proj[:, :n_h]
    fox_ref[:, :FOX_W] = (proj[:, n_h:n_h + FOX_W] * (FOX_DH ** -0.5 * LOG2E)).astype(BF16)
    fox_ref[:, FOX_W:] = proj[:, n_h + FOX_W:n_h + 3 * FOX_W].astype(BF16)
    gate_ref[...] = jax.nn.sigmoid(proj[:, n_h + 3 * FOX_W:])

    r = lax.broadcasted_iota(jnp.int32, (tm, tm), 0)
    c = lax.broadcasted_iota(jnp.int32, (tm, tm), 1)
    triu = (r <= c).astype(BF16)
    ls_row = _log_sigmoid(_nt(wffr_ref[...], xn) + fbr_ref[:, 0:1]) * LOG2E
    hi = ls_row.astype(BF16)
    lo = (ls_row - hi.astype(F32)).astype(BF16)
    crow = _nn(hi, triu) + _nn(lo, triu) + carry_row[:, 0:1]
    crow_ref[...] = crow
    carry_row[...] = jnp.broadcast_to(crow[:, tm - 1:tm], carry_row.shape)


def _in_proj(x2d, nw, w_all, wff_row, fb_row, batch, seq, tm):
    T, D = x2d.shape
    nt = seq // tm
    n_all = w_all.shape[1]
    row = lambda b, i: (b * nt + i, 0)
    const = lambda b, i: (0, 0)
    return pl.pallas_call(
        _inproj_kernel,
        grid=(batch, nt),
        in_specs=[
            pl.BlockSpec((tm, D), row),
            pl.BlockSpec((1, D), const),
            pl.BlockSpec((D, n_all), const),
            pl.BlockSpec((FOX_HEADS, D), const),
            pl.BlockSpec((FOX_HEADS, LANES), const),
        ],
        out_specs=[
            pl.BlockSpec((tm, 4 * HGRN_W), row),
            pl.BlockSpec((tm, 3 * FOX_W), row),
            pl.BlockSpec((tm, 2 * D), row),
            pl.BlockSpec((FOX_HEADS, tm), lambda b, i: (0, b * nt + i)),
        ],
        out_shape=[
            jax.ShapeDtypeStruct((T, 4 * HGRN_W), F32),
            jax.ShapeDtypeStruct((T, 3 * FOX_W), BF16),
            jax.ShapeDtypeStruct((T, 2 * D), F32),
            jax.ShapeDtypeStruct((FOX_HEADS, T), F32),
        ],
        scratch_shapes=[pltpu.VMEM((FOX_HEADS, LANES), F32)],
        compiler_params=pltpu.CompilerParams(
            dimension_semantics=("arbitrary", "arbitrary"), vmem_limit_bytes=VMEM_LIMIT),
        name="in_proj",
    )(x2d, nw, w_all, wff_row, fb_row)


HGRN_CHUNK = 128
HGRN_SUB = 8


def _hgrn_levels():
    out = []
    m = HGRN_SUB
    while m < HGRN_CHUNK:
        out.append(m)
        m *= 2
    return out


def _hgrn_masks():
    C, sub = HGRN_CHUNK, HGRN_SUB
    r = jnp.arange(C, dtype=jnp.int32)[:, None]
    c = jnp.arange(C, dtype=jnp.int32)[None, :]
    masks = [((r // sub) == (c // sub)) & ((c % sub) <= (r % sub))]
    for m in _hgrn_levels():
        masks.append(((r // (2 * m)) == (c // (2 * m))) & (((r // m) % 2) == 1) & (((c // m) % 2) == 0))
    return jnp.stack(masks).astype(F32)


def _hgrn_kernel(hg_ref, lbl_ref, nw_ref, rsel_ref, tril_ref, mask_ref, a_ref, state_ref):
    ci = pl.program_id(1)
    C = HGRN_CHUNK
    dh = HGRN_DH
    sub = HGRN_SUB

    @pl.when(ci == 0)
    def _():
        state_ref[...] = jnp.zeros_like(state_ref)

    lg = lbl_ref[...]
    e = jnp.exp(lg - jnp.max(lg, axis=0, keepdims=True))
    lb_all = e[0:1, :] / jnp.sum(e, axis=0, keepdims=True)
    tril = tril_ref[...]

    for h in range(HGRN_HEADS):
        sl = slice(h * dh, (h + 1) * dh)
        qraw = hg_ref[:, sl]
        q = qraw * jax.nn.sigmoid(qraw)
        lb = lb_all[:, sl]
        f = lb + (1.0 - lb) * jax.nn.sigmoid(hg_ref[:, HGRN_W + h * dh:HGRN_W + (h + 1) * dh])
        log2f = jnp.log(f) * LOG2E
        k = 1.0 - f
        v = hg_ref[:, 2 * HGRN_W + h * dh:2 * HGRN_W + (h + 1) * dh]
        g = hg_ref[:, 3 * HGRN_W + h * dh:3 * HGRN_W + (h + 1) * dh]
        v_bf = v.astype(BF16)
        cum = _split_dot(_nn, tril, log2f)

        nb = C // sub
        q3 = q.reshape(nb, sub, dh)
        k3 = k.reshape(nb, sub, dh)
        c3 = cum.reshape(nb, sub, dh)
        parts = []
        for s in range(sub):
            kb = jnp.broadcast_to(k3[:, s:s + 1, :], (nb, sub, dh))
            cb = jnp.broadcast_to(c3[:, s:s + 1, :], (nb, sub, dh))
            es = q3 * kb * jnp.exp2(jnp.minimum(c3 - cb, 0.0))
            parts.append(es.reshape(C, dh).astype(BF16))
        p_mat = mask_ref[0] * _nn(jnp.concatenate(parts, axis=1), rsel_ref[...])

        for li, m in enumerate(_hgrn_levels()):
            nbm = C // m
            qm = q.reshape(nbm, m, dh)
            km = k.reshape(nbm, m, dh)
            cm = cum.reshape(nbm, m, dh)
            end = cm[:, m - 1:m, :]
            prev_end = jnp.concatenate([jnp.zeros((1, 1, dh), F32), end[:nbm - 1]], axis=0)
            qd = qm * jnp.exp2(jnp.minimum(cm - jnp.broadcast_to(prev_end, (nbm, m, dh)), 0.0))
            kd = km * jnp.exp2(jnp.minimum(jnp.broadcast_to(end, (nbm, m, dh)) - cm, 0.0))
            sc = _nt(qd.reshape(C, dh).astype(BF16), kd.reshape(C, dh).astype(BF16))
            p_mat = p_mat + mask_ref[1 + li] * sc

        st = state_ref[h]
        o = _nn(p_mat.astype(BF16), v_bf) + _nt((q * jnp.exp2(cum)).astype(BF16), st.astype(BF16))
        last = cum[C - 1:C, :]
        kdec = (k * jnp.exp2(last - cum)).astype(BF16)
        state_ref[h] = jnp.exp2(last) * st + _tn(v_bf, kdec)

        o = o * lax.rsqrt(jnp.mean(o * o, axis=-1, keepdims=True) + RMS_EPS) * nw_ref[:, sl]
        a_ref[:, sl] = (o * (g * jax.nn.sigmoid(g))).astype(BF16)


def _hgrn(hg, lb_logits, norm_w, rsel, batch, seq):
    T = hg.shape[0]
    C = HGRN_CHUNK
    nc = seq // C
    row = lambda b, i: (b * nc + i, 0)
    const = lambda b, i: (0, 0)
    tril = jnp.tril(jnp.ones((C, C), F32)).astype(BF16)
    masks = _hgrn_masks()
    return pl.pallas_call(
        _hgrn_kernel,
        grid=(batch, nc),
        in_specs=[
            pl.BlockSpec((C, 4 * HGRN_W), row),
            pl.BlockSpec(lb_logits.shape, const),
            pl.BlockSpec((1, HGRN_W), const),
            pl.BlockSpec(rsel.shape, const),
            pl.BlockSpec((C, C), const),
            pl.BlockSpec(masks.shape, lambda b, i: (0, 0, 0)),
        ],
        out_specs=pl.BlockSpec((C, HGRN_W), row),
        out_shape=jax.ShapeDtypeStruct((T, HGRN_W), BF16),
        scratch_shapes=[pltpu.VMEM((HGRN_HEADS, HGRN_DH, HGRN_DH), F32)],
        compiler_params=pltpu.CompilerParams(
            dimension_semantics=("arbitrary", "arbitrary"), vmem_limit_bytes=VMEM_LIMIT),
        name="hgrn",
    )(hg, lb_logits, norm_w, rsel, tril, masks)


def _fox_kernel(q_ref, k_ref, v_ref, crow_ref, o_ref, m_ref, acc_ref, qs_ref, vs_ref):
    qi = pl.program_id(1)
    tq = q_ref.shape[0]
    tk = tq
    pairs = FOX_HEADS // 2
    lane = lax.broadcasted_iota(jnp.int32, (1, LANES), 1)
    low = lane < FOX_DH
    sel = (jnp.where(low, 1.0, 0.0).astype(BF16), jnp.where(low, 0.0, 1.0).astype(BF16))
    ones_lane = (FOX_DH, 0)
    one_hot = tuple(jnp.where(lane == ol, 1.0, 0.0).astype(BF16) for ol in ones_lane)
    rr = lax.broadcasted_iota(jnp.int32, (tq, tk), 0)
    cc = lax.broadcasted_iota(jnp.int32, (tq, tk), 1)
    causal = cc <= rr

    @pl.when(qi == 0)
    def _():
        for p in range(pairs):
            vb = v_ref[:, p * LANES:(p + 1) * LANES]
            for hh in range(2):
                vs_ref[2 * p + hh] = vb * sel[hh] + one_hot[hh]

    for p in range(pairs):
        q = q_ref[:, p * LANES:(p + 1) * LANES]
        for hh in range(2):
            qs_ref[2 * p + hh] = q * sel[hh]
    m_ref[...] = jnp.full_like(m_ref, NEG_BIG)
    acc_ref[...] = jnp.zeros_like(acc_ref)

    def kv_block(j, masked):
        start = pl.multiple_of(j * tk, tk)
        for p in range(pairs):
            kb = k_ref[pl.ds(start, tk), p * LANES:(p + 1) * LANES]
            s_pair = _nt(qs_ref[2 * p:2 * p + 2].reshape(2 * tq, LANES), kb)
            for hh in range(2):
                h = 2 * p + hh
                ck = crow_ref[h:h + 1, pl.ds(start, tk)]
                s = s_pair[hh * tq:(hh + 1) * tq] - ck
                if masked:
                    s = jnp.where(causal, s, NEG_BIG)
                m_prev = m_ref[h]
                m_next = jnp.maximum(m_prev, jnp.max(s, axis=1, keepdims=True))
                pexp = jnp.exp2(s - jnp.concatenate([m_next] * (tk // LANES), axis=1))
                alpha = jnp.exp2(m_prev - m_next)
                acc_ref[h] = alpha * acc_ref[h] + _nn(pexp.astype(BF16), vs_ref[h, pl.ds(start, tk), :])
                m_ref[h] = m_next

    def body(j, carry):
        kv_block(j, False)
        return carry

    lax.fori_loop(0, qi, body, 0)
    kv_block(qi, True)

    for p in range(pairs):
        a0 = acc_ref[2 * p]
        a1 = acc_ref[2 * p + 1]
        o0 = a0 / a0[:, ones_lane[0]:ones_lane[0] + 1]
        o1 = a1 / a1[:, ones_lane[1]:ones_lane[1] + 1]
        o_ref[:, p * LANES:(p + 1) * LANES] = jnp.where(low, o0, o1).astype(BF16)


def _fox(fox, crow, batch, seq, tq):
    T = fox.shape[0]
    nq = seq // tq
    return pl.pallas_call(
        _fox_kernel,
        grid=(batch, nq),
        in_specs=[
            pl.BlockSpec((tq, FOX_W), lambda b, i: (b * nq + i, 0)),
            pl.BlockSpec((seq, FOX_W), lambda b, i: (b, 1)),
            pl.BlockSpec((seq, FOX_W), lambda b, i: (b, 2)),
            pl.BlockSpec((FOX_HEADS, seq), lambda b, i: (0, b)),
        ],
        out_specs=pl.BlockSpec((tq, FOX_W), lambda b, i: (b * nq + i, 0)),
        out_shape=jax.ShapeDtypeStruct((T, FOX_W), BF16),
        scratch_shapes=[pltpu.VMEM((FOX_HEADS, tq, LANES), F32), pltpu.VMEM((FOX_HEADS, tq, LANES), F32),
                        pltpu.VMEM((FOX_HEADS, tq, LANES), BF16), pltpu.VMEM((FOX_HEADS, seq, LANES), BF16)],
        compiler_params=pltpu.CompilerParams(
            dimension_semantics=("arbitrary", "arbitrary"), vmem_limit_bytes=VMEM_LIMIT),
        name="fox",
    )(fox, fox, fox, crow)


def _merge_kernel(x_ref, a_ref, b_ref, gate_ref, wa_ref, wb_ref, wo_ref, nw_ref, wq_ref, keys_ref,
                  h1_ref, xn_ref, st_ref):
    D = x_ref.shape[1]
    merged = gate_ref[:, :D] * _nn(a_ref[...], wa_ref[...]) + gate_ref[:, D:] * _nn(b_ref[...], wb_ref[...])
    h1 = x_ref[...] + _nn(merged.astype(BF16), wo_ref[...])
    h1_ref[...] = h1
    xn = _rms(h1, nw_ref[...]).astype(BF16)
    xn_ref[...] = xn
    q = _nn(xn, wq_ref[...]).astype(BF16)
    for hp in range(2 * PEER_HEADS):
        st_ref[hp] = _nt(keys_ref[hp], q[:, hp * PEER_HALF:(hp + 1) * PEER_HALF])


def _merge(x2d, a, b, gates, wa, wb, wo, nw, wq, keys, tm):
    T, D = x2d.shape
    row = lambda i: (i, 0)
    const = lambda i: (0, 0)
    return pl.pallas_call(
        _merge_kernel,
        grid=(T // tm,),
        in_specs=[
            pl.BlockSpec((tm, D), row),
            pl.BlockSpec((tm, HGRN_W), row),
            pl.BlockSpec((tm, FOX_W), row),
            pl.BlockSpec((tm, 2 * D), row),
            pl.BlockSpec(wa.shape, const),
            pl.BlockSpec(wb.shape, const),
            pl.BlockSpec(wo.shape, const),
            pl.BlockSpec((1, D), const),
            pl.BlockSpec(wq.shape, const),
            pl.BlockSpec(keys.shape, lambda i: (0, 0, 0)),
        ],
        out_specs=[
            pl.BlockSpec((tm, D), row),
            pl.BlockSpec((tm, D), row),
            pl.BlockSpec((2 * PEER_HEADS, N_KEYS, tm), lambda i: (0, 0, i)),
        ],
        out_shape=[
            jax.ShapeDtypeStruct((T, D), F32),
            jax.ShapeDtypeStruct((T, D), BF16),
            jax.ShapeDtypeStruct((2 * PEER_HEADS, N_KEYS, T), F32),
        ],
        compiler_params=pltpu.CompilerParams(
            dimension_semantics=("arbitrary",), vmem_limit_bytes=VMEM_LIMIT),
        name="merge",
    )(x2d, a, b, gates, wa, wb, wo, nw, wq, keys)


ROUTE_SLOTS = 64


def _route_pairs():
    return [(r, c) for r in range(TOPK) for c in range(TOPK) if (r + 1) * (c + 1) <= TOPK]


def _bitonic_sort_desc(vals, idxs=None):
    n = len(vals)
    k = 2
    while k <= n:
        j = k // 2
        while j >= 1:
            for i in range(n):
                l = i ^ j
                if l > i:
                    hi, lo = (i, l) if (i & k) == 0 else (l, i)
                    a, b = vals[i], vals[l]
                    if idxs is not None:
                        gt = a > b
                        ia, ib = idxs[i], idxs[l]
                        idxs[hi] = jnp.where(gt, ia, ib)
                        idxs[lo] = jnp.where(gt, ib, ia)
                    vals[hi] = jnp.maximum(a, b)
                    vals[lo] = jnp.minimum(a, b)
            j //= 2
        k *= 2


def _top_of_two_sorted(a, b):
    n = len(a)
    out = [jnp.maximum(a[i], b[n - 1 - i]) for i in range(n)]
    j = n // 2
    while j >= 1:
        for i in range(n):
            l = i ^ j
            if l > i:
                x, y = out[i], out[l]
                out[i] = jnp.maximum(x, y)
                out[l] = jnp.minimum(x, y)
        j //= 2
    return out


def _routes_kernel(st_ref, ids_ref, w_ref):
    tt = st_ref.shape[2]
    sub = 8
    groups = N_KEYS // sub
    sub_id = lax.broadcasted_iota(jnp.int32, (sub, tt), 0).astype(F32)

    def top_sorted(s):
        vals = [s[g * sub:(g + 1) * sub, :] for g in range(groups)]
        idxs = [sub_id + float(g * sub) for g in range(groups)]
        _bitonic_sort_desc(vals, idxs)
        tops, keys = [], []
        for t in range(TOPK):
            mx = jnp.max(vals[0], axis=0, keepdims=True)
            win = vals[0] == mx
            tops.append(mx)
            keys.append(jnp.max(jnp.where(win, idxs[0], -1.0), axis=0, keepdims=True))
            for r in range(TOPK - 1 - t):
                vals[r] = jnp.where(win, vals[r + 1], vals[r])
                idxs[r] = jnp.where(win, idxs[r + 1], idxs[r])
        return tops, keys

    tops = [(top_sorted(st_ref[2 * h]), top_sorted(st_ref[2 * h + 1])) for h in range(PEER_HEADS)]

    def stack(half, which, r):
        return jnp.concatenate([tops[h][half][which][r] for h in range(PEER_HEADS)], axis=0)

    a = [stack(0, 0, r) for r in range(TOPK)]
    b = [stack(1, 0, r) for r in range(TOPK)]
    ia = [stack(0, 1, r) for r in range(TOPK)]
    ib = [stack(1, 1, r) for r in range(TOPK)]
    pairs = _route_pairs()
    cand = [a[r] + b[c] for r, c in pairs]
    neg = jnp.full((PEER_HEADS, tt), -jnp.inf, F32)
    padded = cand + [neg] * (-len(cand) % TOPK)
    best = None
    for g0 in range(0, len(padded), TOPK):
        grp = list(padded[g0:g0 + TOPK])
        _bitonic_sort_desc(grp)
        best = grp if best is None else _top_of_two_sorted(best, grp)
    tau = best[TOPK - 1]
    top = a[0] + b[0]
    e = [jnp.where(xi >= tau, jnp.exp(xi - top), 0.0) for xi in cand]
    z = e[0]
    for ei in e[1:]:
        z = z + ei
    inv_z = 1.0 / z
    ids = [jnp.clip(ia[r] * float(N_KEYS) + ib[c], 0.0, float(N_KEYS * N_KEYS - 1)) for r, c in pairs]
    wts = [ek * inv_z for ek in e]
    pad = [jnp.zeros((PEER_HEADS, tt), F32)] * (ROUTE_SLOTS - len(pairs))
    ids_ref[...] = jnp.concatenate(ids + pad, axis=0).T.astype(jnp.int32)
    w_ref[...] = jnp.concatenate(wts + pad, axis=0).T


def _routes(st, tt, t0, tc):
    n, nk, _ = st.shape
    off = t0 // tt
    width = ROUTE_SLOTS * PEER_HEADS
    spec = pl.BlockSpec((tt, width), lambda i: (i, 0))
    return pl.pallas_call(
        _routes_kernel,
        grid=(tc // tt,),
        in_specs=[pl.BlockSpec((n, nk, tt), lambda i: (0, 0, i + off))],
        out_specs=[spec, spec],
        out_shape=[jax.ShapeDtypeStruct((tc, width), jnp.int32),
                   jax.ShapeDtypeStruct((tc, width), F32)],
        compiler_params=pltpu.CompilerParams(
            dimension_semantics=("arbitrary",), vmem_limit_bytes=VMEM_LIMIT),
        name="routes",
    )(st)


SC_LANES = 16
SC_TOKENS_PER_CHUNK = 8
PEER_CHUNKS = 8


def _split_expert_halves(table, eb):
    n, d = table.shape
    return table.reshape(n // eb, eb // (2 * SC_LANES), 2, SC_LANES, d).transpose(0, 2, 1, 3, 4).reshape(n, d)


def _gate_matrix(ids, w, n_exp, heads):
    T, E = ids.shape
    info = plsc.get_sparse_core_info()
    workers = info.num_cores * info.num_subcores
    per_worker = T // workers
    ch = SC_TOKENS_PER_CHUNK
    assert info.num_lanes == SC_LANES and T % (workers * ch) == 0
    assert E % (heads * SC_LANES) == 0 and n_exp % SC_LANES == 0
    mesh = plsc.VectorSubcoreMesh(core_axis_name="c", subcore_axis_name="s")
    ids_flat = ids.reshape(T * E)
    w_flat = w.reshape(T * E)

    @functools.partial(
        pl.kernel, mesh=mesh,
        out_type=jax.ShapeDtypeStruct((T, n_exp // 2), jnp.int32),
        scratch_types=[pltpu.VMEM((ch * E,), jnp.int32), pltpu.VMEM((ch * E,), F32),
                       pltpu.VMEM((n_exp,), F32), pltpu.VMEM((n_exp // 2,), jnp.int32)],
        compiler_params=pltpu.CompilerParams(needs_layout_passes=False),
        name="gate_matrix",
    )
    def scatter(ids_hbm, w_hbm, out_hbm, ids_v, w_v, row_v, row16_v):
        wid = lax.axis_index("s") * info.num_cores + lax.axis_index("c")
        base = wid * per_worker
        zeros = jnp.zeros((SC_LANES,), F32)
        lane = lax.iota(jnp.int32, SC_LANES)

        @pl.loop(0, n_exp, step=SC_LANES)
        def _(i):
            row_v[pl.ds(i, SC_LANES)] = zeros

        @pl.loop(0, per_worker // ch)
        def _(ci):
            t0 = base + ci * ch
            pltpu.sync_copy(ids_hbm.at[pl.ds(t0 * E, ch * E)], ids_v)
            pltpu.sync_copy(w_hbm.at[pl.ds(t0 * E, ch * E)], w_v)
            for tl in range(ch):
                @pl.loop(0, heads)
                def _(h):
                    for m in range(E // heads // SC_LANES):
                        pos = (lane + (tl * E // heads + m * SC_LANES)) * heads + h
                        idx = plsc.load_gather(ids_v, [pos])
                        val = plsc.load_gather(w_v, [pos])
                        plsc.addupdate_scatter(row_v, [idx], val)

                @pl.loop(0, n_exp // (2 * SC_LANES))
                def _(g):
                    lo = row_v[pl.ds(g * 2 * SC_LANES, SC_LANES)]
                    hi = row_v[pl.ds(g * 2 * SC_LANES + SC_LANES, SC_LANES)]
                    pair = plsc.pack(lo, hi, format=plsc.PackFormat.INTERLEAVED)
                    row16_v[pl.ds(g * SC_LANES, SC_LANES)] = plsc.bitcast(pair, jnp.int32)

                pltpu.sync_copy(row16_v, out_hbm.at[t0 + tl])

                @pl.loop(0, E, step=SC_LANES)
                def _(e0):
                    sl = pl.ds(tl * E + e0, SC_LANES)
                    plsc.store_scatter(row_v, [ids_v[sl]], zeros)

    return scatter(ids_flat, w_flat)


def _experts_kernel(xn_ref, u_ref, v_ref, gate_ref, h1_ref, nw_ref, *rest):
    o_ref, acc_ref = rest[-2:]
    j = pl.program_id(1)
    nj = pl.num_programs(1)

    @pl.when(j == 0)
    def _():
        acc_ref[...] = jnp.zeros_like(acc_ref)

    h = _nt(xn_ref[...], u_ref[...])
    act = 0.5 * h * (1.0 + lax.erf(h * (2.0 ** -0.5)))
    words = gate_ref[...]
    g_lo = lax.bitcast_convert_type(words << 16, F32)
    g_hi = lax.bitcast_convert_type(words & jnp.int32(-65536), F32)
    gate = jnp.concatenate([g_lo, g_hi], axis=1)
    acc_ref[...] += _nn((act * gate).astype(BF16), v_ref[...])

    @pl.when(j == nj - 1)
    def _():
        o_ref[...] = _rms(h1_ref[...] + acc_ref[...], nw_ref[...])


def _experts(xn, u, v, gate, h1, nw, tb, eb, t0, out_prev):
    tc = gate.shape[0]
    T, D = xn.shape
    n_exp = u.shape[0]
    off = t0 // tb
    in_specs = [
        pl.BlockSpec((tb, D), lambda i, j: (i + off, 0)),
        pl.BlockSpec((eb, D), lambda i, j: (j, 0)),
        pl.BlockSpec((eb, D), lambda i, j: (j, 0)),
        pl.BlockSpec((tb, eb // 2), lambda i, j: (i, j)),
        pl.BlockSpec((tb, D), lambda i, j: (i + off, 0)),
        pl.BlockSpec((1, D), lambda i, j: (0, 0)),
    ]
    args = [xn, u, v, gate, h1, nw]
    aliases = {}
    if out_prev is not None:
        in_specs.append(pl.BlockSpec(memory_space=pl.ANY))
        args.append(out_prev)
        aliases = {len(args) - 1: 0}
    return pl.pallas_call(
        _experts_kernel,
        grid=(tc // tb, n_exp // eb),
        in_specs=in_specs,
        out_specs=pl.BlockSpec((tb, D), lambda i, j: (i + off, 0)),
        out_shape=jax.ShapeDtypeStruct((T, D), F32),
        scratch_shapes=[pltpu.VMEM((tb, D), F32)],
        input_output_aliases=aliases,
        compiler_params=pltpu.CompilerParams(
            dimension_semantics=("arbitrary", "arbitrary"), vmem_limit_bytes=VMEM_LIMIT),
        name="experts",
    )(*args)


def _block_select_matrix():
    part = jnp.arange(HGRN_SUB * HGRN_DH, dtype=jnp.int32) // HGRN_DH
    col = jnp.arange(HGRN_CHUNK, dtype=jnp.int32) % HGRN_SUB
    return (part[:, None] == col[None, :]).astype(BF16)


def _forward(x, norm_mix_w, w_in, hgrn_lb_logits, hgrn_norm_w, fox_f_bias, w_branch_hgrn,
             w_branch_fox, w_out, norm_ffn_w, peer_w_q, peer_sub_keys, peer_u, peer_v,
             norm_final_w, *, tm_in, tq, tm_merge, tt, tb, eb):
    B, S, D = x.shape
    T = B * S
    n_h = 4 * HGRN_W
    n_f = 3 * FOX_W
    n_exp = N_KEYS * N_KEYS
    assert w_in.shape == (1, D, n_h + n_f + FOX_HEADS + 2 * D), w_in.shape
    assert hgrn_lb_logits.shape == (2, HGRN_W) and fox_f_bias.shape == (1, FOX_HEADS)
    assert peer_w_q.shape == (1, D, 2 * PEER_HEADS * PEER_HALF)
    assert peer_sub_keys.shape == (1, PEER_HEADS, 2, N_KEYS, PEER_HALF)
    assert peer_u.shape == (1, n_exp, D) and peer_v.shape == (1, n_exp, D)
    assert S % max(tm_in, tq, HGRN_CHUNK) == 0 and T % (PEER_CHUNKS * tb) == 0
    assert tb % tt == 0 and tb % tm_merge == 0 and n_exp % eb == 0
    x2d = x.reshape(T, D)
    wi = w_in[0]
    w_all = jnp.concatenate([wi[:, :n_h + n_f], wi[:, n_h + n_f + FOX_HEADS:]], axis=1).astype(BF16)
    wff = wi[:, n_h + n_f:n_h + n_f + FOX_HEADS]
    wff_row = wff.T.astype(BF16)
    fb = fox_f_bias[0].astype(F32)
    fb_row = jnp.broadcast_to(fb.reshape(FOX_HEADS, 1), (FOX_HEADS, LANES))

    hg, fox, gates, crow = _in_proj(
        x2d, norm_mix_w[0].reshape(1, D), w_all, wff_row, fb_row, B, S, tm_in)
    a = _hgrn(hg, hgrn_lb_logits, hgrn_norm_w[0].reshape(1, HGRN_W), _block_select_matrix(), B, S)
    b = _fox(fox, crow, B, S, tq)
    keys = peer_sub_keys[0].reshape(2 * PEER_HEADS, N_KEYS, PEER_HALF).astype(BF16)
    h1, xn2, st = _merge(
        x2d, a, b, gates, w_branch_hgrn[0].astype(BF16), w_branch_fox[0].astype(BF16),
        w_out[0].astype(BF16), norm_ffn_w[0].reshape(1, D), peer_w_q[0].astype(BF16), keys, tm_merge)
    u_bf = _split_expert_halves(peer_u[0].astype(BF16), eb)
    v_bf = _split_expert_halves(peer_v[0].astype(BF16), eb)
    tc = T // PEER_CHUNKS
    bounds = [0, tb, tc] + [c * tc for c in range(2, PEER_CHUNKS + 1)]
    out = None
    for t0, t1 in zip(bounds[:-1], bounds[1:]):
        ids, wts = _routes(st, tt, t0, t1 - t0)
        gate = _gate_matrix(ids, wts, n_exp, PEER_HEADS)
        out = _experts(xn2, u_bf, v_bf, gate, h1, norm_final_w.reshape(1, D), tb, eb, t0, out)
    return out.reshape(B, S, D)


def kernel(x, norm_mix_w, w_in, hgrn_lb_logits, hgrn_norm_w, fox_f_bias, w_branch_hgrn, w_branch_fox, w_out, norm_ffn_w, peer_w_q, peer_sub_keys, peer_u, peer_v, norm_final_w):
    return _forward(x, norm_mix_w, w_in, hgrn_lb_logits, hgrn_norm_w, fox_f_bias, w_branch_hgrn,
                    w_branch_fox, w_out, norm_ffn_w, peer_w_q, peer_sub_keys, peer_u, peer_v,
                    norm_final_w, **TILES)
```

```python
import functools
import math

import jax
import jax.numpy as jnp
from jax import lax
from jax.experimental import pallas as pl
from jax.experimental.pallas import tpu as pltpu
from jax.experimental.pallas import tpu_sc as plsc

F32 = jnp.float32
BF16 = jnp.bfloat16
RMS_EPS = 1e-6
NEG_BIG = -1e30

HGRN_HEADS = 4
HGRN_DH = 128
HGRN_W = HGRN_HEADS * HGRN_DH
FOX_HEADS = 8
FOX_DH = 64
FOX_W = FOX_HEADS * FOX_DH
PEER_HEADS = 8
PEER_HALF = 128
N_KEYS = 128
TOPK = 16
LANES = 128

V7X_VMEM_BYTES = 64 * 1024 * 1024
VMEM_LIMIT = V7X_VMEM_BYTES * 7 // 8

TILES = dict(
    tm_in=512,
    tq=512,
    tm_merge=512,
    tt=512,
    tb=1024,
    eb=1024,
)


def _nt(a, b):
    return lax.dot_general(a, b, (((1,), (1,)), ((), ())), preferred_element_type=F32)


def _tn(a, b):
    return lax.dot_general(a, b, (((0,), (0,)), ((), ())), preferred_element_type=F32)


def _nn(a, b):
    return jnp.dot(a, b, preferred_element_type=F32)


def _split_dot(fn, tri, x):
    hi = x.astype(BF16)
    lo = (x - hi.astype(F32)).astype(BF16)
    return fn(tri, hi) + fn(tri, lo)


def _log_sigmoid(x):
    return jnp.minimum(x, 0.0) - jnp.log1p(jnp.exp(-jnp.abs(x)))


def _rms(x, w):
    return x * lax.rsqrt(jnp.mean(x * x, axis=-1, keepdims=True) + RMS_EPS) * w


LOG2E = math.log2(math.e)


def _inproj_kernel(x_ref, nw_ref, w_ref, wffr_ref, fbr_ref,
                   hg_ref, fox_ref, gate_ref, crow_ref, carry_row):
    i = pl.program_id(1)
    tm = x_ref.shape[0]

    @pl.when(i == 0)
    def _():
        carry_row[...] = jnp.zeros_like(carry_row)

    xn = _rms(x_ref[...], nw_ref[...]).astype(BF16)
    proj = _nn(xn, w_ref[...])
    n_h = 4 * HGRN_W
    hg_ref[...] = proj[:, :n_h]
    fox_ref[:, :FOX_W] = (proj[:, n_h:n_h + FOX_W] * (FOX_DH ** -0.5 * LOG2E)).astype(BF16)
    fox_ref[:, FOX_W:] = proj[:, n_h + FOX_W:n_h + 3 * FOX_W].astype(BF16)
    gate_ref[...] = jax.nn.sigmoid(proj[:, n_h + 3 * FOX_W:])

    r = lax.broadcasted_iota(jnp.int32, (tm, tm), 0)
    c = lax.broadcasted_iota(jnp.int32, (tm, tm), 1)
    triu = (r <= c).astype(BF16)
    ls_row = _log_sigmoid(_nt(wffr_ref[...], xn) + fbr_ref[:, 0:1]) * LOG2E
    hi = ls_row.astype(BF16)
    lo = (ls_row - hi.astype(F32)).astype(BF16)
    crow = _nn(hi, triu) + _nn(lo, triu) + carry_row[:, 0:1]
    crow_ref[...] = crow
    carry_row[...] = jnp.broadcast_to(crow[:, tm - 1:tm], carry_row.shape)


def _in_proj(x2d, nw, w_all, wff_row, fb_row, batch, seq, tm):
    T, D = x2d.shape
    nt = seq // tm
    n_all = w_all.shape[1]
    row = lambda b, i: (b * nt + i, 0)
    const = lambda b, i: (0, 0)
    return pl.pallas_call(
        _inproj_kernel,
        grid=(batch, nt),
        in_specs=[
            pl.BlockSpec((tm, D), row),
            pl.BlockSpec((1, D), const),
            pl.BlockSpec((D, n_all), const),
            pl.BlockSpec((FOX_HEADS, D), const),
            pl.BlockSpec((FOX_HEADS, LANES), const),
        ],
        out_specs=[
            pl.BlockSpec((tm, 4 * HGRN_W), row),
            pl.BlockSpec((tm, 3 * FOX_W), row),
            pl.BlockSpec((tm, 2 * D), row),
            pl.BlockSpec((FOX_HEADS, tm), lambda b, i: (0, b * nt + i)),
        ],
        out_shape=[
            jax.ShapeDtypeStruct((T, 4 * HGRN_W), F32),
            jax.ShapeDtypeStruct((T, 3 * FOX_W), BF16),
            jax.ShapeDtypeStruct((T, 2 * D), F32),
            jax.ShapeDtypeStruct((FOX_HEADS, T), F32),
        ],
        scratch_shapes=[pltpu.VMEM((FOX_HEADS, LANES), F32)],
        compiler_params=pltpu.CompilerParams(
            dimension_semantics=("arbitrary", "arbitrary"), vmem_limit_bytes=VMEM_LIMIT),
        name="in_proj",
    )(x2d, nw, w_all, wff_row, fb_row)


HGRN_CHUNK = 128
HGRN_SUB = 8


def _hgrn_levels():
    out = []
    m = HGRN_SUB
    while m < HGRN_CHUNK:
        out.append(m)
        m *= 2
    return out


def _hgrn_masks():
    C, sub = HGRN_CHUNK, HGRN_SUB
    r = jnp.arange(C, dtype=jnp.int32)[:, None]
    c = jnp.arange(C, dtype=jnp.int32)[None, :]
    masks = [((r // sub) == (c // sub)) & ((c % sub) <= (r % sub))]
    for m in _hgrn_levels():
        masks.append(((r // (2 * m)) == (c // (2 * m))) & (((r // m) % 2) == 1) & (((c // m) % 2) == 0))
    return jnp.stack(masks).astype(F32)


def _hgrn_kernel(hg_ref, lbl_ref, nw_ref, rsel_ref, tril_ref, mask_ref, a_ref, state_ref):
    ci = pl.program_id(1)
    C = HGRN_CHUNK
    dh = HGRN_DH
    sub = HGRN_SUB

    @pl.when(ci == 0)
    def _():
        state_ref[...] = jnp.zeros_like(state_ref)

    lg = lbl_ref[...]
    e = jnp.exp(lg - jnp.max(lg, axis=0, keepdims=True))
    lb_all = e[0:1, :] / jnp.sum(e, axis=0, keepdims=True)
    tril = tril_ref[...]

    for h in range(HGRN_HEADS):
        sl = slice(h * dh, (h + 1) * dh)
        qraw = hg_ref[:, sl]
        q = qraw * jax.nn.sigmoid(qraw)
        lb = lb_all[:, sl]
        f = lb + (1.0 - lb) * jax.nn.sigmoid(hg_ref[:, HGRN_W + h * dh:HGRN_W + (h + 1) * dh])
        log2f = jnp.log(f) * LOG2E
        k = 1.0 - f
        v = hg_ref[:, 2 * HGRN_W + h * dh:2 * HGRN_W + (h + 1) * dh]
        g = hg_ref[:, 3 * HGRN_W + h * dh:3 * HGRN_W + (h + 1) * dh]
        v_bf = v.astype(BF16)
        cum = _split_dot(_nn, tril, log2f)

        nb = C // sub
        q3 = q.reshape(nb, sub, dh)
        k3 = k.reshape(nb, sub, dh)
        c3 = cum.reshape(nb, sub, dh)
        parts = []
        for s in range(sub):
            kb = jnp.broadcast_to(k3[:, s:s + 1, :], (nb, sub, dh))
            cb = jnp.broadcast_to(c3[:, s:s + 1, :], (nb, sub, dh))
            es = q3 * kb * jnp.exp2(jnp.minimum(c3 - cb, 0.0))
            parts.append(es.reshape(C, dh).astype(BF16))
        p_mat = mask_ref[0] * _nn(jnp.concatenate(parts, axis=1), rsel_ref[...])

        for li, m in enumerate(_hgrn_levels()):
            nbm = C // m
            qm = q.reshape(nbm, m, dh)
            km = k.reshape(nbm, m, dh)
            cm = cum.reshape(nbm, m, dh)
            end = cm[:, m - 1:m, :]
            prev_end = jnp.concatenate([jnp.zeros((1, 1, dh), F32), end[:nbm - 1]], axis=0)
            qd = qm * jnp.exp2(jnp.minimum(cm - jnp.broadcast_to(prev_end, (nbm, m, dh)), 0.0))
            kd = km * jnp.exp2(jnp.minimum(jnp.broadcast_to(end, (nbm, m, dh)) - cm, 0.0))
            sc = _nt(qd.reshape(C, dh).astype(BF16), kd.reshape(C, dh).astype(BF16))
            p_mat = p_mat + mask_ref[1 + li] * sc

        st = state_ref[h]
        o = _nn(p_mat.astype(BF16), v_bf) + _nt((q * jnp.exp2(cum)).astype(BF16), st.astype(BF16))
        last = cum[C - 1:C, :]
        kdec = (k * jnp.exp2(last - cum)).astype(BF16)
        state_ref[h] = jnp.exp2(last) * st + _tn(v_bf, kdec)

        o = o * lax.rsqrt(jnp.mean(o * o, axis=-1, keepdims=True) + RMS_EPS) * nw_ref[:, sl]
        a_ref[:, sl] = (o * (g * jax.nn.sigmoid(g))).astype(BF16)


def _hgrn(hg, lb_logits, norm_w, rsel, batch, seq):
    T = hg.shape[0]
    C = HGRN_CHUNK
    nc = seq // C
    row = lambda b, i: (b * nc + i, 0)
    const = lambda b, i: (0, 0)
    tril = jnp.tril(jnp.ones((C, C), F32)).astype(BF16)
    masks = _hgrn_masks()
    return pl.pallas_call(
        _hgrn_kernel,
        grid=(batch, nc),
        in_specs=[
            pl.BlockSpec((C, 4 * HGRN_W), row),
            pl.BlockSpec(lb_logits.shape, const),
            pl.BlockSpec((1, HGRN_W), const),
            pl.BlockSpec(rsel.shape, const),
            pl.BlockSpec((C, C), const),
            pl.BlockSpec(masks.shape, lambda b, i: (0, 0, 0)),
        ],
        out_specs=pl.BlockSpec((C, HGRN_W), row),
        out_shape=jax.ShapeDtypeStruct((T, HGRN_W), BF16),
        scratch_shapes=[pltpu.VMEM((HGRN_HEADS, HGRN_DH, HGRN_DH), F32)],
        compiler_params=pltpu.CompilerParams(
            dimension_semantics=("arbitrary", "arbitrary"), vmem_limit_bytes=VMEM_LIMIT),
        name="hgrn",
    )(hg, lb_logits, norm_w, rsel, tril, masks)


def _fox_kernel(q_ref, k_ref, v_ref, crow_ref, o_ref, m_ref, acc_ref, qs_ref, vs_ref):
    qi = pl.program_id(1)
    tq = q_ref.shape[0]
    tk = tq
    pairs = FOX_HEADS // 2
    lane = lax.broadcasted_iota(jnp.int32, (1, LANES), 1)
    low = lane < FOX_DH
    sel = (jnp.where(low, 1.0, 0.0).astype(BF16), jnp.where(low, 0.0, 1.0).astype(BF16))
    ones_lane = (FOX_DH, 0)
    one_hot = tuple(jnp.where(lane == ol, 1.0, 0.0).astype(BF16) for ol in ones_lane)
    rr = lax.broadcasted_iota(jnp.int32, (tq, tk), 0)
    cc = lax.broadcasted_iota(jnp.int32, (tq, tk), 1)
    causal = cc <= rr

    @pl.when(qi == 0)
    def _():
        for p in range(pairs):
            vb = v_ref[:, p * LANES:(p + 1) * LANES]
            for hh in range(2):
                vs_ref[2 * p + hh] = vb * sel[hh] + one_hot[hh]

    for p in range(pairs):
        q = q_ref[:, p * LANES:(p + 1) * LANES]
        for hh in range(2):
            qs_ref[2 * p + hh] = q * sel[hh]
    m_ref[...] = jnp.full_like(m_ref, NEG_BIG)
    acc_ref[...] = jnp.zeros_like(acc_ref)

    def kv_block(j, masked):
        start = pl.multiple_of(j * tk, tk)
        for p in range(pairs):
            kb = k_ref[pl.ds(start, tk), p * LANES:(p + 1) * LANES]
            s_pair = _nt(qs_ref[2 * p:2 * p + 2].reshape(2 * tq, LANES), kb)
            for hh in range(2):
                h = 2 * p + hh
                ck = crow_ref[h:h + 1, pl.ds(start, tk)]
                s = s_pair[hh * tq:(hh + 1) * tq] - ck
                if masked:
                    s = jnp.where(causal, s, NEG_BIG)
                m_prev = m_ref[h]
                m_next = jnp.maximum(m_prev, jnp.max(s, axis=1, keepdims=True))
                pexp = jnp.exp2(s - jnp.concatenate([m_next] * (tk // LANES), axis=1))
                alpha = jnp.exp2(m_prev - m_next)
                acc_ref[h] = alpha * acc_ref[h] + _nn(pexp.astype(BF16), vs_ref[h, pl.ds(start, tk), :])
                m_ref[h] = m_next

    def body(j, carry):
        kv_block(j, False)
        return carry

    lax.fori_loop(0, qi, body, 0)
    kv_block(qi, True)

    for p in range(pairs):
        a0 = acc_ref[2 * p]
        a1 = acc_ref[2 * p + 1]
        o0 = a0 / a0[:, ones_lane[0]:ones_lane[0] + 1]
        o1 = a1 / a1[:, ones_lane[1]:ones_lane[1] + 1]
        o_ref[:, p * LANES:(p + 1) * LANES] = jnp.where(low, o0, o1).astype(BF16)


def _fox(fox, crow, batch, seq, tq):
    T = fox.shape[0]
    nq = seq // tq
    return pl.pallas_call(
        _fox_kernel,
        grid=(batch, nq),
        in_specs=[
            pl.BlockSpec((tq, FOX_W), lambda b, i: (b * nq + i, 0)),
            pl.BlockSpec((seq, FOX_W), lambda b, i: (b, 1)),
            pl.BlockSpec((seq, FOX_W), lambda b, i: (b, 2)),
            pl.BlockSpec((FOX_HEADS, seq), lambda b, i: (0, b)),
        ],
        out_specs=pl.BlockSpec((tq, FOX_W), lambda b, i: (b * nq + i, 0)),
        out_shape=jax.ShapeDtypeStruct((T, FOX_W), BF16),
        scratch_shapes=[pltpu.VMEM((FOX_HEADS, tq, LANES), F32), pltpu.VMEM((FOX_HEADS, tq, LANES), F32),
                        pltpu.VMEM((FOX_HEADS, tq, LANES), BF16), pltpu.VMEM((FOX_HEADS, seq, LANES), BF16)],
        compiler_params=pltpu.CompilerParams(
            dimension_semantics=("arbitrary", "arbitrary"), vmem_limit_bytes=VMEM_LIMIT),
        name="fox",
    )(fox, fox, fox, crow)


def _merge_kernel(x_ref, a_ref, b_ref, gate_ref, wa_ref, wb_ref, wo_ref, nw_ref, wq_ref, keys_ref,
                  h1_ref, xn_ref, st_ref):
    D = x_ref.shape[1]
    merged = gate_ref[:, :D] * _nn(a_ref[...], wa_ref[...]) + gate_ref[:, D:] * _nn(b_ref[...], wb_ref[...])
    h1 = x_ref[...] + _nn(merged.astype(BF16), wo_ref[...])
    h1_ref[...] = h1
    xn = _rms(h1, nw_ref[...]).astype(BF16)
    xn_ref[...] = xn
    q = _nn(xn, wq_ref[...]).astype(BF16)
    for hp in range(2 * PEER_HEADS):
        st_ref[hp] = _nt(keys_ref[hp], q[:, hp * PEER_HALF:(hp + 1) * PEER_HALF])


def _merge(x2d, a, b, gates, wa, wb, wo, nw, wq, keys, tm):
    T, D = x2d.shape
    row = lambda i: (i, 0)
    const = lambda i: (0, 0)
    return pl.pallas_call(
        _merge_kernel,
        grid=(T // tm,),
        in_specs=[
            pl.BlockSpec((tm, D), row),
            pl.BlockSpec((tm, HGRN_W), row),
            pl.BlockSpec((tm, FOX_W), row),
            pl.BlockSpec((tm, 2 * D), row),
            pl.BlockSpec(wa.shape, const),
            pl.BlockSpec(wb.shape, const),
            pl.BlockSpec(wo.shape, const),
            pl.BlockSpec((1, D), const),
            pl.BlockSpec(wq.shape, const),
            pl.BlockSpec(keys.shape, lambda i: (0, 0, 0)),
        ],
        out_specs=[
            pl.BlockSpec((tm, D), row),
            pl.BlockSpec((tm, D), row),
            pl.BlockSpec((2 * PEER_HEADS, N_KEYS, tm), lambda i: (0, 0, i)),
        ],
        out_shape=[
            jax.ShapeDtypeStruct((T, D), F32),
            jax.ShapeDtypeStruct((T, D), BF16),
            jax.ShapeDtypeStruct((2 * PEER_HEADS, N_KEYS, T), F32),
        ],
        compiler_params=pltpu.CompilerParams(
            dimension_semantics=("arbitrary",), vmem_limit_bytes=VMEM_LIMIT),
        name="merge",
    )(x2d, a, b, gates, wa, wb, wo, nw, wq, keys)


ROUTE_SLOTS = 64


def _route_pairs():
    return [(r, c) for r in range(TOPK) for c in range(TOPK) if (r + 1) * (c + 1) <= TOPK]


def _bitonic_sort_desc(vals, idxs=None):
    n = len(vals)
    k = 2
    while k <= n:
        j = k // 2
        while j >= 1:
            for i in range(n):
                l = i ^ j
                if l > i:
                    hi, lo = (i, l) if (i & k) == 0 else (l, i)
                    a, b = vals[i], vals[l]
                    if idxs is not None:
                        gt = a > b
                        ia, ib = idxs[i], idxs[l]
                        idxs[hi] = jnp.where(gt, ia, ib)
                        idxs[lo] = jnp.where(gt, ib, ia)
                    vals[hi] = jnp.maximum(a, b)
                    vals[lo] = jnp.minimum(a, b)
            j //= 2
        k *= 2


def _top_of_two_sorted(a, b):
    n = len(a)
    out = [jnp.maximum(a[i], b[n - 1 - i]) for i in range(n)]
    j = n // 2
    while j >= 1:
        for i in range(n):
            l = i ^ j
            if l > i:
                x, y = out[i], out[l]
                out[i] = jnp.maximum(x, y)
                out[l] = jnp.minimum(x, y)
        j //= 2
    return out


def _routes_kernel(st_ref, ids_ref, w_ref):
    tt = st_ref.shape[2]
    sub = 8
    groups = N_KEYS // sub
    sub_id = lax.broadcasted_iota(jnp.int32, (sub, tt), 0).astype(F32)

    def top_sorted(s):
        vals = [s[g * sub:(g + 1) * sub, :] for g in range(groups)]
        idxs = [sub_id + float(g * sub) for g in range(groups)]
        _bitonic_sort_desc(vals, idxs)
        tops, keys = [], []
        for t in range(TOPK):
            mx = jnp.max(vals[0], axis=0, keepdims=True)
            win = vals[0] == mx
            tops.append(mx)
            keys.append(jnp.max(jnp.where(win, idxs[0], -1.0), axis=0, keepdims=True))
            for r in range(TOPK - 1 - t):
                vals[r] = jnp.where(win, vals[r + 1], vals[r])
                idxs[r] = jnp.where(win, idxs[r + 1], idxs[r])
        return tops, keys

    tops = [(top_sorted(st_ref[2 * h]), top_sorted(st_ref[2 * h + 1])) for h in range(PEER_HEADS)]

    def stack(half, which, r):
        return jnp.concatenate([tops[h][half][which][r] for h in range(PEER_HEADS)], axis=0)

    a = [stack(0, 0, r) for r in range(TOPK)]
    b = [stack(1, 0, r) for r in range(TOPK)]
    ia = [stack(0, 1, r) for r in range(TOPK)]
    ib = [stack(1, 1, r) for r in range(TOPK)]
    pairs = _route_pairs()
    cand = [a[r] + b[c] for r, c in pairs]
    neg = jnp.full((PEER_HEADS, tt), -jnp.inf, F32)
    padded = cand + [neg] * (-len(cand) % TOPK)
    best = None
    for g0 in range(0, len(padded), TOPK):
        grp = list(padded[g0:g0 + TOPK])
        _bitonic_sort_desc(grp)
        best = grp if best is None else _top_of_two_sorted(best, grp)
    tau = best[TOPK - 1]
    top = a[0] + b[0]
    e = [jnp.where(xi >= tau, jnp.exp(xi - top), 0.0) for xi in cand]
    z = e[0]
    for ei in e[1:]:
        z = z + ei
    inv_z = 1.0 / z
    ids = [jnp.clip(ia[r] * float(N_KEYS) + ib[c], 0.0, float(N_KEYS * N_KEYS - 1)) for r, c in pairs]
    wts = [ek * inv_z for ek in e]
    pad = [jnp.zeros((PEER_HEADS, tt), F32)] * (ROUTE_SLOTS - len(pairs))
    ids_ref[...] = jnp.concatenate(ids + pad, axis=0).T.astype(jnp.int32)
    w_ref[...] = jnp.concatenate(wts + pad, axis=0).T


def _routes(st, tt, t0, tc):
    n, nk, _ = st.shape
    off = t0 // tt
    width = ROUTE_SLOTS * PEER_HEADS
    spec = pl.BlockSpec((tt, width), lambda i: (i, 0))
    return pl.pallas_call(
        _routes_kernel,
        grid=(tc // tt,),
        in_specs=[pl.BlockSpec((n, nk, tt), lambda i: (0, 0, i + off))],
        out_specs=[spec, spec],
        out_shape=[jax.ShapeDtypeStruct((tc, width), jnp.int32),
                   jax.ShapeDtypeStruct((tc, width), F32)],
        compiler_params=pltpu.CompilerParams(
            dimension_semantics=("arbitrary",), vmem_limit_bytes=VMEM_LIMIT),
        name="routes",
    )(st)


SC_LANES = 16
SC_TOKENS_PER_CHUNK = 8
PEER_CHUNKS = 8


def _split_expert_halves(table, eb):
    n, d = table.shape
    return table.reshape(n // eb, eb // (2 * SC_LANES), 2, SC_LANES, d).transpose(0, 2, 1, 3, 4).reshape(n, d)


def _gate_matrix(ids, w, n_exp, heads):
    T, E = ids.shape
    info = plsc.get_sparse_core_info()
    workers = info.num_cores * info.num_subcores
    per_worker = T // workers
    ch = SC_TOKENS_PER_CHUNK
    assert info.num_lanes == SC_LANES and T % (workers * ch) == 0
    assert E % (heads * SC_LANES) == 0 and n_exp % SC_LANES == 0
    mesh = plsc.VectorSubcoreMesh(core_axis_name="c", subcore_axis_name="s")
    ids_flat = ids.reshape(T * E)
    w_flat = w.reshape(T * E)

    @functools.partial(
        pl.kernel, mesh=mesh,
        out_type=jax.ShapeDtypeStruct((T, n_exp // 2), jnp.int32),
        scratch_types=[pltpu.VMEM((ch * E,), jnp.int32), pltpu.VMEM((ch * E,), F32),
                       pltpu.VMEM((n_exp,), F32), pltpu.VMEM((n_exp // 2,), jnp.int32)],
        compiler_params=pltpu.CompilerParams(needs_layout_passes=False),
        name="gate_matrix",
    )
    def scatter(ids_hbm, w_hbm, out_hbm, ids_v, w_v, row_v, row16_v):
        wid = lax.axis_index("s") * info.num_cores + lax.axis_index("c")
        base = wid * per_worker
        zeros = jnp.zeros((SC_LANES,), F32)
        lane = lax.iota(jnp.int32, SC_LANES)

        @pl.loop(0, n_exp, step=SC_LANES)
        def _(i):
            row_v[pl.ds(i, SC_LANES)] = zeros

        @pl.loop(0, per_worker // ch)
        def _(ci):
            t0 = base + ci * ch
            pltpu.sync_copy(ids_hbm.at[pl.ds(t0 * E, ch * E)], ids_v)
            pltpu.sync_copy(w_hbm.at[pl.ds(t0 * E, ch * E)], w_v)
            for tl in range(ch):
                @pl.loop(0, heads)
                def _(h):
                    for m in range(E // heads // SC_LANES):
                        pos = (lane + (tl * E // heads + m * SC_LANES)) * heads + h
                        idx = plsc.load_gather(ids_v, [pos])
                        val = plsc.load_gather(w_v, [pos])
                        plsc.addupdate_scatter(row_v, [idx], val)

                @plsc.parallel_loop(0, n_exp // (2 * SC_LANES), unroll=8)
                def _(g):
                    lo = row_v[pl.ds(g * 2 * SC_LANES, SC_LANES)]
                    hi = row_v[pl.ds(g * 2 * SC_LANES + SC_LANES, SC_LANES)]
                    pair = plsc.pack(lo, hi, format=plsc.PackFormat.INTERLEAVED)
                    row16_v[pl.ds(g * SC_LANES, SC_LANES)] = plsc.bitcast(pair, jnp.int32)

                pltpu.sync_copy(row16_v, out_hbm.at[t0 + tl])

                @pl.loop(0, E, step=SC_LANES)
                def _(e0):
                    sl = pl.ds(tl * E + e0, SC_LANES)
                    plsc.store_scatter(row_v, [ids_v[sl]], zeros)

    return scatter(ids_flat, w_flat)


def _experts_kernel(xn_ref, u_ref, v_ref, gate_ref, h1_ref, nw_ref, *rest):
    o_ref, acc_ref = rest[-2:]
    j = pl.program_id(1)
    nj = pl.num_programs(1)

    @pl.when(j == 0)
    def _():
        acc_ref[...] = jnp.zeros_like(acc_ref)

    h = _nt(xn_ref[...], u_ref[...])
    act = 0.5 * h * (1.0 + lax.erf(h * (2.0 ** -0.5)))
    words = gate_ref[...]
    g_lo = lax.bitcast_convert_type(words << 16, F32)
    g_hi = lax.bitcast_convert_type(words & jnp.int32(-65536), F32)
    gate = jnp.concatenate([g_lo, g_hi], axis=1)
    acc_ref[...] += _nn((act * gate).astype(BF16), v_ref[...])

    @pl.when(j == nj - 1)
    def _():
        o_ref[...] = _rms(h1_ref[...] + acc_ref[...], nw_ref[...])


def _experts(xn, u, v, gate, h1, nw, tb, eb, t0, out_prev):
    tc = gate.shape[0]
    T, D = xn.shape
    n_exp = u.shape[0]
    off = t0 // tb
    in_specs = [
        pl.BlockSpec((tb, D), lambda i, j: (i + off, 0)),
        pl.BlockSpec((eb, D), lambda i, j: (j, 0)),
        pl.BlockSpec((eb, D), lambda i, j: (j, 0)),
        pl.BlockSpec((tb, eb // 2), lambda i, j: (i, j)),
        pl.BlockSpec((tb, D), lambda i, j: (i + off, 0)),
        pl.BlockSpec((1, D), lambda i, j: (0, 0)),
    ]
    args = [xn, u, v, gate, h1, nw]
    aliases = {}
    if out_prev is not None:
        in_specs.append(pl.BlockSpec(memory_space=pl.ANY))
        args.append(out_prev)
        aliases = {len(args) - 1: 0}
    return pl.pallas_call(
        _experts_kernel,
        grid=(tc // tb, n_exp // eb),
        in_specs=in_specs,
        out_specs=pl.BlockSpec((tb, D), lambda i, j: (i + off, 0)),
        out_shape=jax.ShapeDtypeStruct((T, D), F32),
        scratch_shapes=[pltpu.VMEM((tb, D), F32)],
        input_output_aliases=aliases,
        compiler_params=pltpu.CompilerParams(
            dimension_semantics=("arbitrary", "arbitrary"), vmem_limit_bytes=VMEM_LIMIT),
        name="experts",
    )(*args)


def _block_select_matrix():
    part = jnp.arange(HGRN_SUB * HGRN_DH, dtype=jnp.int32) // HGRN_DH
    col = jnp.arange(HGRN_CHUNK, dtype=jnp.int32) % HGRN_SUB
    return (part[:, None] == col[None, :]).astype(BF16)


def _forward(x, norm_mix_w, w_in, hgrn_lb_logits, hgrn_norm_w, fox_f_bias, w_branch_hgrn,
             w_branch_fox, w_out, norm_ffn_w, peer_w_q, peer_sub_keys, peer_u, peer_v,
             norm_final_w, *, tm_in, tq, tm_merge, tt, tb, eb):
    B, S, D = x.shape
    T = B * S
    n_h = 4 * HGRN_W
    n_f = 3 * FOX_W
    n_exp = N_KEYS * N_KEYS
    assert w_in.shape == (1, D, n_h + n_f + FOX_HEADS + 2 * D), w_in.shape
    assert hgrn_lb_logits.shape == (2, HGRN_W) and fox_f_bias.shape == (1, FOX_HEADS)
    assert peer_w_q.shape == (1, D, 2 * PEER_HEADS * PEER_HALF)
    assert peer_sub_keys.shape == (1, PEER_HEADS, 2, N_KEYS, PEER_HALF)
    assert peer_u.shape == (1, n_exp, D) and peer_v.shape == (1, n_exp, D)
    assert S % max(tm_in, tq, HGRN_CHUNK) == 0 and T % (PEER_CHUNKS * tb) == 0
    assert tb % tt == 0 and tb % tm_merge == 0 and n_exp % eb == 0
    x2d = x.reshape(T, D)
    wi = w_in[0]
    w_all = jnp.concatenate([wi[:, :n_h + n_f], wi[:, n_h + n_f + FOX_HEADS:]], axis=1).astype(BF16)
    wff = wi[:, n_h + n_f:n_h + n_f + FOX_HEADS]
    wff_row = wff.T.astype(BF16)
    fb = fox_f_bias[0].astype(F32)
    fb_row = jnp.broadcast_to(fb.reshape(FOX_HEADS, 1), (FOX_HEADS, LANES))

    hg, fox, gates, crow = _in_proj(
        x2d, norm_mix_w[0].reshape(1, D), w_all, wff_row, fb_row, B, S, tm_in)
    a = _hgrn(hg, hgrn_lb_logits, hgrn_norm_w[0].reshape(1, HGRN_W), _block_select_matrix(), B, S)
    b = _fox(fox, crow, B, S, tq)
    keys = peer_sub_keys[0].reshape(2 * PEER_HEADS, N_KEYS, PEER_HALF).astype(BF16)
    h1, xn2, st = _merge(
        x2d, a, b, gates, w_branch_hgrn[0].astype(BF16), w_branch_fox[0].astype(BF16),
        w_out[0].astype(BF16), norm_ffn_w[0].reshape(1, D), peer_w_q[0].astype(BF16), keys, tm_merge)
    u_bf = _split_expert_halves(peer_u[0].astype(BF16), eb)
    v_bf = _split_expert_halves(peer_v[0].astype(BF16), eb)
    tc = T // PEER_CHUNKS
    bounds = [0, tb, tc] + [c * tc for c in range(2, PEER_CHUNKS + 1)]
    out = None
    for t0, t1 in zip(bounds[:-1], bounds[1:]):
        ids, wts = _routes(st, tt, t0, t1 - t0)
        gate = _gate_matrix(ids, wts, n_exp, PEER_HEADS)
        out = _experts(xn2, u_bf, v_bf, gate, h1, norm_final_w.reshape(1, D), tb, eb, t0, out)
    return out.reshape(B, S, D)


def kernel(x, norm_mix_w, w_in, hgrn_lb_logits, hgrn_norm_w, fox_f_bias, w_branch_hgrn, w_branch_fox, w_out, norm_ffn_w, peer_w_q, peer_sub_keys, peer_u, peer_v, norm_final_w):
    return _forward(x, norm_mix_w, w_in, hgrn_lb_logits, hgrn_norm_w, fox_f_bias, w_branch_hgrn,
                    w_branch_fox, w_out, norm_ffn_w, peer_w_q, peer_sub_keys, peer_u, peer_v,
                    norm_final_w, **TILES)
```

```python
import functools
import math

import jax
import jax.numpy as jnp
from jax import lax
from jax.experimental import pallas as pl
from jax.experimental.pallas import tpu as pltpu
from jax.experimental.pallas import tpu_sc as plsc

F32 = jnp.float32
BF16 = jnp.bfloat16
RMS_EPS = 1e-6
NEG_BIG = -1e30

HGRN_HEADS = 4
HGRN_DH = 128
HGRN_W = HGRN_HEADS * HGRN_DH
FOX_HEADS = 8
FOX_DH = 64
FOX_W = FOX_HEADS * FOX_DH
PEER_HEADS = 8
PEER_HALF = 128
N_KEYS = 128
TOPK = 16
LANES = 128

V7X_VMEM_BYTES = 64 * 1024 * 1024
VMEM_LIMIT = V7X_VMEM_BYTES * 7 // 8

TILES = dict(
    tm_in=512,
    tq=512,
    tm_merge=512,
    tt=512,
    tb=1024,
    eb=1024,
)


def _nt(a, b):
    return lax.dot_general(a, b, (((1,), (1,)), ((), ())), preferred_element_type=F32)


def _tn(a, b):
    return lax.dot_general(a, b, (((0,), (0,)), ((), ())), preferred_element_type=F32)


def _nn(a, b):
    return jnp.dot(a, b, preferred_element_type=F32)


def _split_dot(fn, tri, x):
    hi = x.astype(BF16)
    lo = (x - hi.astype(F32)).astype(BF16)
    return fn(tri, hi) + fn(tri, lo)


def _log_sigmoid(x):
    return jnp.minimum(x, 0.0) - jnp.log1p(jnp.exp(-jnp.abs(x)))


def _rms(x, w):
    return x * lax.rsqrt(jnp.mean(x * x, axis=-1, keepdims=True) + RMS_EPS) * w


LOG2E = math.log2(math.e)


def _inproj_kernel(x_ref, nw_ref, w_ref, wffr_ref, fbr_ref,
                   hg_ref, fox_ref, gate_ref, crow_ref, carry_row):
    i = pl.program_id(1)
    tm = x_ref.shape[0]

    @pl.when(i == 0)
    def _():
        carry_row[...] = jnp.zeros_like(carry_row)

    xn = _rms(x_ref[...], nw_ref[...]).astype(BF16)
    proj = _nn(xn, w_ref[...])
    n_h = 4 * HGRN_W
    hg_ref[...] = proj[:, :n_h]
    fox_ref[:, :FOX_W] = (proj[:, n_h:n_h + FOX_W] * (FOX_DH ** -0.5 * LOG2E)).astype(BF16)
    fox_ref[:, FOX_W:] = proj[:, n_h + FOX_W:n_h + 3 * FOX_W].astype(BF16)
    gate_ref[...] = jax.nn.sigmoid(proj[:, n_h + 3 * FOX_W:])

    r = lax.broadcasted_iota(jnp.int32, (tm, tm), 0)
    c = lax.broadcasted_iota(jnp.int32, (tm, tm), 1)
    triu = (r <= c).astype(BF16)
    ls_row = _log_sigmoid(_nt(wffr_ref[...], xn) + fbr_ref[:, 0:1]) * LOG2E
    hi = ls_row.astype(BF16)
    lo = (ls_row - hi.astype(F32)).astype(BF16)
    crow = _nn(hi, triu) + _nn(lo, triu) + carry_row[:, 0:1]
    crow_ref[...] = crow
    carry_row[...] = jnp.broadcast_to(crow[:, tm - 1:tm], carry_row.shape)


def _in_proj(x2d, nw, w_all, wff_row, fb_row, batch, seq, tm):
    T, D = x2d.shape
    nt = seq // tm
    n_all = w_all.shape[1]
    row = lambda b, i: (b * nt + i, 0)
    const = lambda b, i: (0, 0)
    return pl.pallas_call(
        _inproj_kernel,
        grid=(batch, nt),
        in_specs=[
            pl.BlockSpec((tm, D), row),
            pl.BlockSpec((1, D), const),
            pl.BlockSpec((D, n_all), const),
            pl.BlockSpec((FOX_HEADS, D), const),
            pl.BlockSpec((FOX_HEADS, LANES), const),
        ],
        out_specs=[
            pl.BlockSpec((tm, 4 * HGRN_W), row),
            pl.BlockSpec((tm, 3 * FOX_W), row),
            pl.BlockSpec((tm, 2 * D), row),
            pl.BlockSpec((FOX_HEADS, tm), lambda b, i: (0, b * nt + i)),
        ],
        out_shape=[
            jax.ShapeDtypeStruct((T, 4 * HGRN_W), F32),
            jax.ShapeDtypeStruct((T, 3 * FOX_W), BF16),
            jax.ShapeDtypeStruct((T, 2 * D), F32),
            jax.ShapeDtypeStruct((FOX_HEADS, T), F32),
        ],
        scratch_shapes=[pltpu.VMEM((FOX_HEADS, LANES), F32)],
        compiler_params=pltpu.CompilerParams(
            dimension_semantics=("arbitrary", "arbitrary"), vmem_limit_bytes=VMEM_LIMIT),
        name="in_proj",
    )(x2d, nw, w_all, wff_row, fb_row)


HGRN_CHUNK = 128
HGRN_SUB = 8


def _hgrn_levels():
    out = []
    m = HGRN_SUB
    while m < HGRN_CHUNK:
        out.append(m)
        m *= 2
    return out


def _hgrn_masks():
    C, sub = HGRN_CHUNK, HGRN_SUB
    r = jnp.arange(C, dtype=jnp.int32)[:, None]
    c = jnp.arange(C, dtype=jnp.int32)[None, :]
    masks = [((r // sub) == (c // sub)) & ((c % sub) <= (r % sub))]
    for m in _hgrn_levels():
        masks.append(((r // (2 * m)) == (c // (2 * m))) & (((r // m) % 2) == 1) & (((c // m) % 2) == 0))
    return jnp.stack(masks).astype(F32)


def _hgrn_kernel(hg_ref, lbl_ref, nw_ref, rsel_ref, tril_ref, mask_ref, a_ref, state_ref):
    ci = pl.program_id(1)
    C = HGRN_CHUNK
    dh = HGRN_DH
    sub = HGRN_SUB

    @pl.when(ci == 0)
    def _():
        state_ref[...] = jnp.zeros_like(state_ref)

    lg = lbl_ref[...]
    e = jnp.exp(lg - jnp.max(lg, axis=0, keepdims=True))
    lb_all = e[0:1, :] / jnp.sum(e, axis=0, keepdims=True)
    tril = tril_ref[...]

    for h in range(HGRN_HEADS):
        sl = slice(h * dh, (h + 1) * dh)
        qraw = hg_ref[:, sl]
        q = qraw * jax.nn.sigmoid(qraw)
        lb = lb_all[:, sl]
        f = lb + (1.0 - lb) * jax.nn.sigmoid(hg_ref[:, HGRN_W + h * dh:HGRN_W + (h + 1) * dh])
        log2f = jnp.log(f) * LOG2E
        k = 1.0 - f
        v = hg_ref[:, 2 * HGRN_W + h * dh:2 * HGRN_W + (h + 1) * dh]
        g = hg_ref[:, 3 * HGRN_W + h * dh:3 * HGRN_W + (h + 1) * dh]
        v_bf = v.astype(BF16)
        cum = _split_dot(_nn, tril, log2f)

        nb = C // sub
        q3 = q.reshape(nb, sub, dh)
        k3 = k.reshape(nb, sub, dh)
        c3 = cum.reshape(nb, sub, dh)
        parts = []
        for s in range(sub):
            kb = jnp.broadcast_to(k3[:, s:s + 1, :], (nb, sub, dh))
            cb = jnp.broadcast_to(c3[:, s:s + 1, :], (nb, sub, dh))
            es = q3 * kb * jnp.exp2(jnp.minimum(c3 - cb, 0.0))
            parts.append(es.reshape(C, dh).astype(BF16))
        p_mat = mask_ref[0] * _nn(jnp.concatenate(parts, axis=1), rsel_ref[...])

        for li, m in enumerate(_hgrn_levels()):
            nbm = C // m
            qm = q.reshape(nbm, m, dh)
            km = k.reshape(nbm, m, dh)
            cm = cum.reshape(nbm, m, dh)
            end = cm[:, m - 1:m, :]
            prev_end = jnp.concatenate([jnp.zeros((1, 1, dh), F32), end[:nbm - 1]], axis=0)
            qd = qm * jnp.exp2(jnp.minimum(cm - jnp.broadcast_to(prev_end, (nbm, m, dh)), 0.0))
            kd = km * jnp.exp2(jnp.minimum(jnp.broadcast_to(end, (nbm, m, dh)) - cm, 0.0))
            sc = _nt(qd.reshape(C, dh).astype(BF16), kd.reshape(C, dh).astype(BF16))
            p_mat = p_mat + mask_ref[1 + li] * sc

        st = state_ref[h]
        o = _nn(p_mat.astype(BF16), v_bf) + _nt((q * jnp.exp2(cum)).astype(BF16), st.astype(BF16))
        last = cum[C - 1:C, :]
        kdec = (k * jnp.exp2(last - cum)).astype(BF16)
        state_ref[h] = jnp.exp2(last) * st + _tn(v_bf, kdec)

        o = o * lax.rsqrt(jnp.mean(o * o, axis=-1, keepdims=True) + RMS_EPS) * nw_ref[:, sl]
        a_ref[:, sl] = (o * (g * jax.nn.sigmoid(g))).astype(BF16)


def _hgrn(hg, lb_logits, norm_w, rsel, batch, seq):
    T = hg.shape[0]
    C = HGRN_CHUNK
    nc = seq // C
    row = lambda b, i: (b * nc + i, 0)
    const = lambda b, i: (0, 0)
    tril = jnp.tril(jnp.ones((C, C), F32)).astype(BF16)
    masks = _hgrn_masks()
    return pl.pallas_call(
        _hgrn_kernel,
        grid=(batch, nc),
        in_specs=[
            pl.BlockSpec((C, 4 * HGRN_W), row),
            pl.BlockSpec(lb_logits.shape, const),
            pl.BlockSpec((1, HGRN_W), const),
            pl.BlockSpec(rsel.shape, const),
            pl.BlockSpec((C, C), const),
            pl.BlockSpec(masks.shape, lambda b, i: (0, 0, 0)),
        ],
        out_specs=pl.BlockSpec((C, HGRN_W), row),
        out_shape=jax.ShapeDtypeStruct((T, HGRN_W), BF16),
        scratch_shapes=[pltpu.VMEM((HGRN_HEADS, HGRN_DH, HGRN_DH), F32)],
        compiler_params=pltpu.CompilerParams(
            dimension_semantics=("arbitrary", "arbitrary"), vmem_limit_bytes=VMEM_LIMIT),
        name="hgrn",
    )(hg, lb_logits, norm_w, rsel, tril, masks)


def _fox_kernel(q_ref, k_ref, v_ref, crow_ref, o_ref, m_ref, acc_ref, qs_ref, vs_ref):
    qi = pl.program_id(1)
    tq = q_ref.shape[0]
    tk = tq
    pairs = FOX_HEADS // 2
    lane = lax.broadcasted_iota(jnp.int32, (1, LANES), 1)
    low = lane < FOX_DH
    sel = (jnp.where(low, 1.0, 0.0).astype(BF16), jnp.where(low, 0.0, 1.0).astype(BF16))
    ones_lane = (FOX_DH, 0)
    one_hot = tuple(jnp.where(lane == ol, 1.0, 0.0).astype(BF16) for ol in ones_lane)
    rr = lax.broadcasted_iota(jnp.int32, (tq, tk), 0)
    cc = lax.broadcasted_iota(jnp.int32, (tq, tk), 1)
    causal = cc <= rr

    @pl.when(qi == 0)
    def _():
        for p in range(pairs):
            vb = v_ref[:, p * LANES:(p + 1) * LANES]
            for hh in range(2):
                vs_ref[2 * p + hh] = vb * sel[hh] + one_hot[hh]

    for p in range(pairs):
        q = q_ref[:, p * LANES:(p + 1) * LANES]
        for hh in range(2):
            qs_ref[2 * p + hh] = q * sel[hh]
    m_ref[...] = jnp.full_like(m_ref, NEG_BIG)
    acc_ref[...] = jnp.zeros_like(acc_ref)

    def kv_block(j, masked):
        start = pl.multiple_of(j * tk, tk)
        for p in range(pairs):
            kb = k_ref[pl.ds(start, tk), p * LANES:(p + 1) * LANES]
            s_pair = _nt(qs_ref[2 * p:2 * p + 2].reshape(2 * tq, LANES), kb)
            for hh in range(2):
                h = 2 * p + hh
                ck = crow_ref[h:h + 1, pl.ds(start, tk)]
                s = s_pair[hh * tq:(hh + 1) * tq] - ck
                if masked:
                    s = jnp.where(causal, s, NEG_BIG)
                m_prev = m_ref[h]
                m_next = jnp.maximum(m_prev, jnp.max(s, axis=1, keepdims=True))
                pexp = jnp.exp2(s - jnp.concatenate([m_next] * (tk // LANES), axis=1))
                alpha = jnp.exp2(m_prev - m_next)
                acc_ref[h] = alpha * acc_ref[h] + _nn(pexp.astype(BF16), vs_ref[h, pl.ds(start, tk), :])
                m_ref[h] = m_next

    def body(j, carry):
        kv_block(j, False)
        return carry

    lax.fori_loop(0, qi, body, 0)
    kv_block(qi, True)

    for p in range(pairs):
        a0 = acc_ref[2 * p]
        a1 = acc_ref[2 * p + 1]
        o0 = a0 / a0[:, ones_lane[0]:ones_lane[0] + 1]
        o1 = a1 / a1[:, ones_lane[1]:ones_lane[1] + 1]
        o_ref[:, p * LANES:(p + 1) * LANES] = jnp.where(low, o0, o1).astype(BF16)


def _fox(fox, crow, batch, seq, tq):
    T = fox.shape[0]
    nq = seq // tq
    return pl.pallas_call(
        _fox_kernel,
        grid=(batch, nq),
        in_specs=[
            pl.BlockSpec((tq, FOX_W), lambda b, i: (b * nq + i, 0)),
            pl.BlockSpec((seq, FOX_W), lambda b, i: (b, 1)),
            pl.BlockSpec((seq, FOX_W), lambda b, i: (b, 2)),
            pl.BlockSpec((FOX_HEADS, seq), lambda b, i: (0, b)),
        ],
        out_specs=pl.BlockSpec((tq, FOX_W), lambda b, i: (b * nq + i, 0)),
        out_shape=jax.ShapeDtypeStruct((T, FOX_W), BF16),
        scratch_shapes=[pltpu.VMEM((FOX_HEADS, tq, LANES), F32), pltpu.VMEM((FOX_HEADS, tq, LANES), F32),
                        pltpu.VMEM((FOX_HEADS, tq, LANES), BF16), pltpu.VMEM((FOX_HEADS, seq, LANES), BF16)],
        compiler_params=pltpu.CompilerParams(
            dimension_semantics=("arbitrary", "arbitrary"), vmem_limit_bytes=VMEM_LIMIT),
        name="fox",
    )(fox, fox, fox, crow)


def _merge_kernel(x_ref, a_ref, b_ref, gate_ref, wa_ref, wb_ref, wo_ref, nw_ref, wq_ref, keys_ref,
                  h1_ref, xn_ref, st_ref):
    D = x_ref.shape[1]
    merged = gate_ref[:, :D] * _nn(a_ref[...], wa_ref[...]) + gate_ref[:, D:] * _nn(b_ref[...], wb_ref[...])
    h1 = x_ref[...] + _nn(merged.astype(BF16), wo_ref[...])
    h1_ref[...] = h1
    xn = _rms(h1, nw_ref[...]).astype(BF16)
    xn_ref[...] = xn
    q = _nn(xn, wq_ref[...]).astype(BF16)
    for hp in range(2 * PEER_HEADS):
        st_ref[hp] = _nt(keys_ref[hp], q[:, hp * PEER_HALF:(hp + 1) * PEER_HALF])


def _merge(x2d, a, b, gates, wa, wb, wo, nw, wq, keys, tm):
    T, D = x2d.shape
    row = lambda i: (i, 0)
    const = lambda i: (0, 0)
    return pl.pallas_call(
        _merge_kernel,
        grid=(T // tm,),
        in_specs=[
            pl.BlockSpec((tm, D), row),
            pl.BlockSpec((tm, HGRN_W), row),
            pl.BlockSpec((tm, FOX_W), row),
            pl.BlockSpec((tm, 2 * D), row),
            pl.BlockSpec(wa.shape, const),
            pl.BlockSpec(wb.shape, const),
            pl.BlockSpec(wo.shape, const),
            pl.BlockSpec((1, D), const),
            pl.BlockSpec(wq.shape, const),
            pl.BlockSpec(keys.shape, lambda i: (0, 0, 0)),
        ],
        out_specs=[
            pl.BlockSpec((tm, D), row),
            pl.BlockSpec((tm, D), row),
            pl.BlockSpec((2 * PEER_HEADS, N_KEYS, tm), lambda i: (0, 0, i)),
        ],
        out_shape=[
            jax.ShapeDtypeStruct((T, D), F32),
            jax.ShapeDtypeStruct((T, D), BF16),
            jax.ShapeDtypeStruct((2 * PEER_HEADS, N_KEYS, T), F32),
        ],
        compiler_params=pltpu.CompilerParams(
            dimension_semantics=("arbitrary",), vmem_limit_bytes=VMEM_LIMIT),
        name="merge",
    )(x2d, a, b, gates, wa, wb, wo, nw, wq, keys)


ROUTE_SLOTS = 64


def _route_pairs():
    return [(r, c) for r in range(TOPK) for c in range(TOPK) if (r + 1) * (c + 1) <= TOPK]


def _bitonic_sort_desc(vals, idxs=None):
    n = len(vals)
    k = 2
    while k <= n:
        j = k // 2
        while j >= 1:
            for i in range(n):
                l = i ^ j
                if l > i:
                    hi, lo = (i, l) if (i & k) == 0 else (l, i)
                    a, b = vals[i], vals[l]
                    if idxs is not None:
                        gt = a > b
                        ia, ib = idxs[i], idxs[l]
                        idxs[hi] = jnp.where(gt, ia, ib)
                        idxs[lo] = jnp.where(gt, ib, ia)
                    vals[hi] = jnp.maximum(a, b)
                    vals[lo] = jnp.minimum(a, b)
            j //= 2
        k *= 2


def _top_of_two_sorted(a, b):
    n = len(a)
    out = [jnp.maximum(a[i], b[n - 1 - i]) for i in range(n)]
    j = n // 2
    while j >= 1:
        for i in range(n):
            l = i ^ j
            if l > i:
                x, y = out[i], out[l]
                out[i] = jnp.maximum(x, y)
                out[l] = jnp.minimum(x, y)
        j //= 2
    return out


def _routes_kernel(st_ref, ids_ref, w_ref):
    tt = st_ref.shape[2]
    sub = 8
    groups = N_KEYS // sub
    sub_id = lax.broadcasted_iota(jnp.int32, (sub, tt), 0).astype(F32)

    def top_sorted(s):
        vals = [s[g * sub:(g + 1) * sub, :] for g in range(groups)]
        idxs = [sub_id + float(g * sub) for g in range(groups)]
        _bitonic_sort_desc(vals, idxs)
        tops, keys = [], []
        for t in range(TOPK):
            mx = jnp.max(vals[0], axis=0, keepdims=True)
            win = vals[0] == mx
            tops.append(mx)
            keys.append(jnp.max(jnp.where(win, idxs[0], -1.0), axis=0, keepdims=True))
            for r in range(TOPK - 1 - t):
                vals[r] = jnp.where(win, vals[r + 1], vals[r])
                idxs[r] = jnp.where(win, idxs[r + 1], idxs[r])
        return tops, keys

    tops = [(top_sorted(st_ref[2 * h]), top_sorted(st_ref[2 * h + 1])) for h in range(PEER_HEADS)]

    def stack(half, which, r):
        return jnp.concatenate([tops[h][half][which][r] for h in range(PEER_HEADS)], axis=0)

    a = [stack(0, 0, r) for r in range(TOPK)]
    b = [stack(1, 0, r) for r in range(TOPK)]
    ia = [stack(0, 1, r) for r in range(TOPK)]
    ib = [stack(1, 1, r) for r in range(TOPK)]
    pairs = _route_pairs()
    cand = [a[r] + b[c] for r, c in pairs]
    neg = jnp.full((PEER_HEADS, tt), -jnp.inf, F32)
    padded = cand + [neg] * (-len(cand) % TOPK)
    best = None
    for g0 in range(0, len(padded), TOPK):
        grp = list(padded[g0:g0 + TOPK])
        _bitonic_sort_desc(grp)
        best = grp if best is None else _top_of_two_sorted(best, grp)
    tau = best[TOPK - 1]
    top = a[0] + b[0]
    e = [jnp.where(xi >= tau, jnp.exp(xi - top), 0.0) for xi in cand]
    z = e[0]
    for ei in e[1:]:
        z = z + ei
    inv_z = 1.0 / z
    ids = [jnp.clip(ia[r] * float(N_KEYS) + ib[c], 0.0, float(N_KEYS * N_KEYS - 1)) for r, c in pairs]
    wts = [ek * inv_z for ek in e]
    pad = [jnp.zeros((PEER_HEADS, tt), F32)] * (ROUTE_SLOTS - len(pairs))
    ids_ref[...] = jnp.concatenate(ids + pad, axis=0).T.astype(jnp.int32)
    w_ref[...] = jnp.concatenate(wts + pad, axis=0).T


def _routes(st, tt, t0, tc):
    n, nk, _ = st.shape
    off = t0 // tt
    width = ROUTE_SLOTS * PEER_HEADS
    spec = pl.BlockSpec((tt, width), lambda i: (i, 0))
    return pl.pallas_call(
        _routes_kernel,
        grid=(tc // tt,),
        in_specs=[pl.BlockSpec((n, nk, tt), lambda i: (0, 0, i + off))],
        out_specs=[spec, spec],
        out_shape=[jax.ShapeDtypeStruct((tc, width), jnp.int32),
                   jax.ShapeDtypeStruct((tc, width), F32)],
        compiler_params=pltpu.CompilerParams(
            dimension_semantics=("arbitrary",), vmem_limit_bytes=VMEM_LIMIT),
        name="routes",
    )(st)


SC_LANES = 16
SC_TOKENS_PER_CHUNK = 8
PEER_CHUNKS = 8


def _split_expert_halves(table, eb):
    n, d = table.shape
    return table.reshape(n // eb, eb // (2 * SC_LANES), 2, SC_LANES, d).transpose(0, 2, 1, 3, 4).reshape(n, d)


def _gate_matrix(ids, w, n_exp, heads):
    T, E = ids.shape
    info = plsc.get_sparse_core_info()
    workers = info.num_cores * info.num_subcores
    per_worker = T // workers
    ch = SC_TOKENS_PER_CHUNK
    assert info.num_lanes == SC_LANES and T % (workers * ch) == 0
    assert E % (heads * SC_LANES) == 0 and n_exp % SC_LANES == 0
    mesh = plsc.VectorSubcoreMesh(core_axis_name="c", subcore_axis_name="s")
    ids_flat = ids.reshape(T * E)
    w_flat = w.reshape(T * E)

    @functools.partial(
        pl.kernel, mesh=mesh,
        out_type=jax.ShapeDtypeStruct((T, n_exp // 2), jnp.int32),
        scratch_types=[pltpu.VMEM((ch * E,), jnp.int32), pltpu.VMEM((ch * E,), F32),
                       pltpu.VMEM((n_exp,), F32), pltpu.VMEM((n_exp // 2,), jnp.int32)],
        compiler_params=pltpu.CompilerParams(needs_layout_passes=False),
        name="gate_matrix",
    )
    def scatter(ids_hbm, w_hbm, out_hbm, ids_v, w_v, row_v, row16_v):
        wid = lax.axis_index("s") * info.num_cores + lax.axis_index("c")
        base = wid * per_worker
        zeros = jnp.zeros((SC_LANES,), F32)
        lane = lax.iota(jnp.int32, SC_LANES)

        @plsc.parallel_loop(0, n_exp, step=SC_LANES, unroll=8)
        def _(i):
            row_v[pl.ds(i, SC_LANES)] = zeros

        @pl.loop(0, per_worker // ch)
        def _(ci):
            t0 = base + ci * ch
            pltpu.sync_copy(ids_hbm.at[pl.ds(t0 * E, ch * E)], ids_v)
            pltpu.sync_copy(w_hbm.at[pl.ds(t0 * E, ch * E)], w_v)
            for tl in range(ch):
                @pl.loop(0, heads)
                def _(h):
                    for m in range(E // heads // SC_LANES):
                        pos = (lane + (tl * E // heads + m * SC_LANES)) * heads + h
                        idx = plsc.load_gather(ids_v, [pos])
                        val = plsc.load_gather(w_v, [pos])
                        plsc.addupdate_scatter(row_v, [idx], val)

                @plsc.parallel_loop(0, n_exp // (2 * SC_LANES), unroll=8)
                def _(g):
                    lo = row_v[pl.ds(g * 2 * SC_LANES, SC_LANES)]
                    hi = row_v[pl.ds(g * 2 * SC_LANES + SC_LANES, SC_LANES)]
                    pair = plsc.pack(lo, hi, format=plsc.PackFormat.INTERLEAVED)
                    row16_v[pl.ds(g * SC_LANES, SC_LANES)] = plsc.bitcast(pair, jnp.int32)

                pltpu.sync_copy(row16_v, out_hbm.at[t0 + tl])

                @plsc.parallel_loop(0, E, step=SC_LANES, unroll=8)
                def _(e0):
                    sl = pl.ds(tl * E + e0, SC_LANES)
                    plsc.store_scatter(row_v, [ids_v[sl]], zeros)

    return scatter(ids_flat, w_flat)


def _experts_kernel(xn_ref, u_ref, v_ref, gate_ref, h1_ref, nw_ref, *rest):
    o_ref, acc_ref = rest[-2:]
    j = pl.program_id(1)
    nj = pl.num_programs(1)

    @pl.when(j == 0)
    def _():
        acc_ref[...] = jnp.zeros_like(acc_ref)

    h = _nt(xn_ref[...], u_ref[...])
    act = 0.5 * h * (1.0 + lax.erf(h * (2.0 ** -0.5)))
    words = gate_ref[...]
    g_lo = lax.bitcast_convert_type(words << 16, F32)
    g_hi = lax.bitcast_convert_type(words & jnp.int32(-65536), F32)
    gate = jnp.concatenate([g_lo, g_hi], axis=1)
    acc_ref[...] += _nn((act * gate).astype(BF16), v_ref[...])

    @pl.when(j == nj - 1)
    def _():
        o_ref[...] = _rms(h1_ref[...] + acc_ref[...], nw_ref[...])


def _experts(xn, u, v, gate, h1, nw, tb, eb, t0, out_prev):
    tc = gate.shape[0]
    T, D = xn.shape
    n_exp = u.shape[0]
    off = t0 // tb
    in_specs = [
        pl.BlockSpec((tb, D), lambda i, j: (i + off, 0)),
        pl.BlockSpec((eb, D), lambda i, j: (j, 0)),
        pl.BlockSpec((eb, D), lambda i, j: (j, 0)),
        pl.BlockSpec((tb, eb // 2), lambda i, j: (i, j)),
        pl.BlockSpec((tb, D), lambda i, j: (i + off, 0)),
        pl.BlockSpec((1, D), lambda i, j: (0, 0)),
    ]
    args = [xn, u, v, gate, h1, nw]
    aliases = {}
    if out_prev is not None:
        in_specs.append(pl.BlockSpec(memory_space=pl.ANY))
        args.append(out_prev)
        aliases = {len(args) - 1: 0}
    return pl.pallas_call(
        _experts_kernel,
        grid=(tc // tb, n_exp // eb),
        in_specs=in_specs,
        out_specs=pl.BlockSpec((tb, D), lambda i, j: (i + off, 0)),
        out_shape=jax.ShapeDtypeStruct((T, D), F32),
        scratch_shapes=[pltpu.VMEM((tb, D), F32)],
        input_output_aliases=aliases,
        compiler_params=pltpu.CompilerParams(
            dimension_semantics=("arbitrary", "arbitrary"), vmem_limit_bytes=VMEM_LIMIT),
        name="experts",
    )(*args)


def _block_select_matrix():
    part = jnp.arange(HGRN_SUB * HGRN_DH, dtype=jnp.int32) // HGRN_DH
    col = jnp.arange(HGRN_CHUNK, dtype=jnp.int32) % HGRN_SUB
    return (part[:, None] == col[None, :]).astype(BF16)


def _forward(x, norm_mix_w, w_in, hgrn_lb_logits, hgrn_norm_w, fox_f_bias, w_branch_hgrn,
             w_branch_fox, w_out, norm_ffn_w, peer_w_q, peer_sub_keys, peer_u, peer_v,
             norm_final_w, *, tm_in, tq, tm_merge, tt, tb, eb):
    B, S, D = x.shape
    T = B * S
    n_h = 4 * HGRN_W
    n_f = 3 * FOX_W
    n_exp = N_KEYS * N_KEYS
    assert w_in.shape == (1, D, n_h + n_f + FOX_HEADS + 2 * D), w_in.shape
    assert hgrn_lb_logits.shape == (2, HGRN_W) and fox_f_bias.shape == (1, FOX_HEADS)
    assert peer_w_q.shape == (1, D, 2 * PEER_HEADS * PEER_HALF)
    assert peer_sub_keys.shape == (1, PEER_HEADS, 2, N_KEYS, PEER_HALF)
    assert peer_u.shape == (1, n_exp, D) and peer_v.shape == (1, n_exp, D)
    assert S % max(tm_in, tq, HGRN_CHUNK) == 0 and T % (PEER_CHUNKS * tb) == 0
    assert tb % tt == 0 and tb % tm_merge == 0 and n_exp % eb == 0
    x2d = x.reshape(T, D)
    wi = w_in[0]
    w_all = jnp.concatenate([wi[:, :n_h + n_f], wi[:, n_h + n_f + FOX_HEADS:]], axis=1).astype(BF16)
    wff = wi[:, n_h + n_f:n_h + n_f + FOX_HEADS]
    wff_row = wff.T.astype(BF16)
    fb = fox_f_bias[0].astype(F32)
    fb_row = jnp.broadcast_to(fb.reshape(FOX_HEADS, 1), (FOX_HEADS, LANES))

    hg, fox, gates, crow = _in_proj(
        x2d, norm_mix_w[0].reshape(1, D), w_all, wff_row, fb_row, B, S, tm_in)
    a = _hgrn(hg, hgrn_lb_logits, hgrn_norm_w[0].reshape(1, HGRN_W), _block_select_matrix(), B, S)
    b = _fox(fox, crow, B, S, tq)
    keys = peer_sub_keys[0].reshape(2 * PEER_HEADS, N_KEYS, PEER_HALF).astype(BF16)
    h1, xn2, st = _merge(
        x2d, a, b, gates, w_branch_hgrn[0].astype(BF16), w_branch_fox[0].astype(BF16),
        w_out[0].astype(BF16), norm_ffn_w[0].reshape(1, D), peer_w_q[0].astype(BF16), keys, tm_merge)
    u_bf = _split_expert_halves(peer_u[0].astype(BF16), eb)
    v_bf = _split_expert_halves(peer_v[0].astype(BF16), eb)
    tc = T // PEER_CHUNKS
    bounds = [0, tb, tc] + [c * tc for c in range(2, PEER_CHUNKS + 1)]
    out = None
    for t0, t1 in zip(bounds[:-1], bounds[1:]):
        ids, wts = _routes(st, tt, t0, t1 - t0)
        gate = _gate_matrix(ids, wts, n_exp, PEER_HEADS)
        out = _experts(xn2, u_bf, v_bf, gate, h1, norm_final_w.reshape(1, D), tb, eb, t0, out)
    return out.reshape(B, S, D)


def kernel(x, norm_mix_w, w_in, hgrn_lb_logits, hgrn_norm_w, fox_f_bias, w_branch_hgrn, w_branch_fox, w_out, norm_ffn_w, peer_w_q, peer_sub_keys, peer_u, peer_v, norm_final_w):
    return _forward(x, norm_mix_w, w_in, hgrn_lb_logits, hgrn_norm_w, fox_f_bias, w_branch_hgrn,
                    w_branch_fox, w_out, norm_ffn_w, peer_w_q, peer_sub_keys, peer_u, peer_v,
                    norm_final_w, **TILES)
```

```python
import functools
import math

import jax
import jax.numpy as jnp
from jax import lax
from jax.experimental import pallas as pl
from jax.experimental.pallas import tpu as pltpu
from jax.experimental.pallas import tpu_sc as plsc

F32 = jnp.float32
BF16 = jnp.bfloat16
RMS_EPS = 1e-6
NEG_BIG = -1e30

HGRN_HEADS = 4
HGRN_DH = 128
HGRN_W = HGRN_HEADS * HGRN_DH
FOX_HEADS = 8
FOX_DH = 64
FOX_W = FOX_HEADS * FOX_DH
PEER_HEADS = 8
PEER_HALF = 128
N_KEYS = 128
TOPK = 16
LANES = 128

V7X_VMEM_BYTES = 64 * 1024 * 1024
VMEM_LIMIT = V7X_VMEM_BYTES * 7 // 8

TILES = dict(
    tm_in=512,
    tq=512,
    tm_merge=512,
    tt=512,
    tb=1024,
    eb=1024,
)


def _nt(a, b):
    return lax.dot_general(a, b, (((1,), (1,)), ((), ())), preferred_element_type=F32)


def _tn(a, b):
    return lax.dot_general(a, b, (((0,), (0,)), ((), ())), preferred_element_type=F32)


def _nn(a, b):
    return jnp.dot(a, b, preferred_element_type=F32)


def _split_dot(fn, tri, x):
    hi = x.astype(BF16)
    lo = (x - hi.astype(F32)).astype(BF16)
    return fn(tri, hi) + fn(tri, lo)


def _log_sigmoid(x):
    return jnp.minimum(x, 0.0) - jnp.log1p(jnp.exp(-jnp.abs(x)))


def _rms(x, w):
    return x * lax.rsqrt(jnp.mean(x * x, axis=-1, keepdims=True) + RMS_EPS) * w


LOG2E = math.log2(math.e)


def _inproj_kernel(x_ref, nw_ref, w_ref, wg_ref, wffr_ref, fbr_ref,
                   hg_ref, fox_ref, gate_ref, crow_ref, carry_row):
    i = pl.program_id(1)
    tm = x_ref.shape[0]

    @pl.when(i == 0)
    def _():
        carry_row[...] = jnp.zeros_like(carry_row)

    xn = _rms(x_ref[...], nw_ref[...]).astype(BF16)
    proj = _nn(xn, w_ref[...])
    n_h = 4 * HGRN_W
    hg_ref[...] = proj[:, :n_h]
    fox_ref[:, :FOX_W] = (proj[:, n_h:n_h + FOX_W] * (FOX_DH ** -0.5 * LOG2E)).astype(BF16)
    fox_ref[:, FOX_W:] = proj[:, n_h + FOX_W:n_h + 3 * FOX_W].astype(BF16)
    gate_ref[...] = jax.nn.sigmoid(_nn(xn, wg_ref[...]))

    r = lax.broadcasted_iota(jnp.int32, (tm, tm), 0)
    c = lax.broadcasted_iota(jnp.int32, (tm, tm), 1)
    triu = (r <= c).astype(BF16)
    ls_row = _log_sigmoid(_nt(wffr_ref[...], xn) + fbr_ref[:, 0:1]) * LOG2E
    hi = ls_row.astype(BF16)
    lo = (ls_row - hi.astype(F32)).astype(BF16)
    crow = _nn(hi, triu) + _nn(lo, triu) + carry_row[:, 0:1]
    crow_ref[...] = crow
    carry_row[...] = jnp.broadcast_to(crow[:, tm - 1:tm], carry_row.shape)


def _in_proj(x2d, nw, w_all, w_gate, wff_row, fb_row, batch, seq, tm):
    T, D = x2d.shape
    nt = seq // tm
    n_all = w_all.shape[1]
    row = lambda b, i: (b * nt + i, 0)
    const = lambda b, i: (0, 0)
    return pl.pallas_call(
        _inproj_kernel,
        grid=(batch, nt),
        in_specs=[
            pl.BlockSpec((tm, D), row),
            pl.BlockSpec((1, D), const),
            pl.BlockSpec((D, n_all), const),
            pl.BlockSpec((D, 2 * D), const),
            pl.BlockSpec((FOX_HEADS, D), const),
            pl.BlockSpec((FOX_HEADS, LANES), const),
        ],
        out_specs=[
            pl.BlockSpec((tm, 4 * HGRN_W), row),
            pl.BlockSpec((tm, 3 * FOX_W), row),
            pl.BlockSpec((tm, 2 * D), row),
            pl.BlockSpec((FOX_HEADS, tm), lambda b, i: (0, b * nt + i)),
        ],
        out_shape=[
            jax.ShapeDtypeStruct((T, 4 * HGRN_W), F32),
            jax.ShapeDtypeStruct((T, 3 * FOX_W), BF16),
            jax.ShapeDtypeStruct((T, 2 * D), F32),
            jax.ShapeDtypeStruct((FOX_HEADS, T), F32),
        ],
        scratch_shapes=[pltpu.VMEM((FOX_HEADS, LANES), F32)],
        compiler_params=pltpu.CompilerParams(
            dimension_semantics=("arbitrary", "arbitrary"), vmem_limit_bytes=VMEM_LIMIT),
        name="in_proj",
    )(x2d, nw, w_all, w_gate, wff_row, fb_row)


HGRN_CHUNK = 128
HGRN_SUB = 8


def _hgrn_levels():
    out = []
    m = HGRN_SUB
    while m < HGRN_CHUNK:
        out.append(m)
        m *= 2
    return out


def _hgrn_masks():
    C, sub = HGRN_CHUNK, HGRN_SUB
    r = jnp.arange(C, dtype=jnp.int32)[:, None]
    c = jnp.arange(C, dtype=jnp.int32)[None, :]
    masks = [((r // sub) == (c // sub)) & ((c % sub) <= (r % sub))]
    for m in _hgrn_levels():
        masks.append(((r // (2 * m)) == (c // (2 * m))) & (((r // m) % 2) == 1) & (((c // m) % 2) == 0))
    return jnp.stack(masks).astype(F32)


def _hgrn_kernel(hg_ref, lbl_ref, nw_ref, rsel_ref, tril_ref, mask_ref, a_ref, state_ref):
    ci = pl.program_id(1)
    C = HGRN_CHUNK
    dh = HGRN_DH
    sub = HGRN_SUB

    @pl.when(ci == 0)
    def _():
        state_ref[...] = jnp.zeros_like(state_ref)

    lg = lbl_ref[...]
    e = jnp.exp(lg - jnp.max(lg, axis=0, keepdims=True))
    lb_all = e[0:1, :] / jnp.sum(e, axis=0, keepdims=True)
    tril = tril_ref[...]

    for h in range(HGRN_HEADS):
        sl = slice(h * dh, (h + 1) * dh)
        qraw = hg_ref[:, sl]
        q = qraw * jax.nn.sigmoid(qraw)
        lb = lb_all[:, sl]
        f = lb + (1.0 - lb) * jax.nn.sigmoid(hg_ref[:, HGRN_W + h * dh:HGRN_W + (h + 1) * dh])
        log2f = jnp.log(f) * LOG2E
        k = 1.0 - f
        v = hg_ref[:, 2 * HGRN_W + h * dh:2 * HGRN_W + (h + 1) * dh]
        g = hg_ref[:, 3 * HGRN_W + h * dh:3 * HGRN_W + (h + 1) * dh]
        v_bf = v.astype(BF16)
        cum = _split_dot(_nn, tril, log2f)

        nb = C // sub
        q3 = q.reshape(nb, sub, dh)
        k3 = k.reshape(nb, sub, dh)
        c3 = cum.reshape(nb, sub, dh)
        parts = []
        for s in range(sub):
            kb = jnp.broadcast_to(k3[:, s:s + 1, :], (nb, sub, dh))
            cb = jnp.broadcast_to(c3[:, s:s + 1, :], (nb, sub, dh))
            es = q3 * kb * jnp.exp2(jnp.minimum(c3 - cb, 0.0))
            parts.append(es.reshape(C, dh).astype(BF16))
        p_mat = mask_ref[0] * _nn(jnp.concatenate(parts, axis=1), rsel_ref[...])

        for li, m in enumerate(_hgrn_levels()):
            nbm = C // m
            qm = q.reshape(nbm, m, dh)
            km = k.reshape(nbm, m, dh)
            cm = cum.reshape(nbm, m, dh)
            end = cm[:, m - 1:m, :]
            prev_end = jnp.concatenate([jnp.zeros((1, 1, dh), F32), end[:nbm - 1]], axis=0)
            qd = qm * jnp.exp2(jnp.minimum(cm - jnp.broadcast_to(prev_end, (nbm, m, dh)), 0.0))
            kd = km * jnp.exp2(jnp.minimum(jnp.broadcast_to(end, (nbm, m, dh)) - cm, 0.0))
            sc = _nt(qd.reshape(C, dh).astype(BF16), kd.reshape(C, dh).astype(BF16))
            p_mat = p_mat + mask_ref[1 + li] * sc

        st = state_ref[h]
        o = _nn(p_mat.astype(BF16), v_bf) + _nt((q * jnp.exp2(cum)).astype(BF16), st.astype(BF16))
        last = cum[C - 1:C, :]
        kdec = (k * jnp.exp2(last - cum)).astype(BF16)
        state_ref[h] = jnp.exp2(last) * st + _tn(v_bf, kdec)

        o = o * lax.rsqrt(jnp.mean(o * o, axis=-1, keepdims=True) + RMS_EPS) * nw_ref[:, sl]
        a_ref[:, sl] = (o * (g * jax.nn.sigmoid(g))).astype(BF16)


def _hgrn(hg, lb_logits, norm_w, rsel, batch, seq):
    T = hg.shape[0]
    C = HGRN_CHUNK
    nc = seq // C
    row = lambda b, i: (b * nc + i, 0)
    const = lambda b, i: (0, 0)
    tril = jnp.tril(jnp.ones((C, C), F32)).astype(BF16)
    masks = _hgrn_masks()
    return pl.pallas_call(
        _hgrn_kernel,
        grid=(batch, nc),
        in_specs=[
            pl.BlockSpec((C, 4 * HGRN_W), row),
            pl.BlockSpec(lb_logits.shape, const),
            pl.BlockSpec((1, HGRN_W), const),
            pl.BlockSpec(rsel.shape, const),
            pl.BlockSpec((C, C), const),
            pl.BlockSpec(masks.shape, lambda b, i: (0, 0, 0)),
        ],
        out_specs=pl.BlockSpec((C, HGRN_W), row),
        out_shape=jax.ShapeDtypeStruct((T, HGRN_W), BF16),
        scratch_shapes=[pltpu.VMEM((HGRN_HEADS, HGRN_DH, HGRN_DH), F32)],
        compiler_params=pltpu.CompilerParams(
            dimension_semantics=("arbitrary", "arbitrary"), vmem_limit_bytes=VMEM_LIMIT),
        name="hgrn",
    )(hg, lb_logits, norm_w, rsel, tril, masks)


def _fox_kernel(q_ref, k_ref, v_ref, crow_ref, o_ref, m_ref, acc_ref, qs_ref, vs_ref):
    qi = pl.program_id(1)
    tq = q_ref.shape[0]
    tk = tq
    pairs = FOX_HEADS // 2
    lane = lax.broadcasted_iota(jnp.int32, (1, LANES), 1)
    low = lane < FOX_DH
    sel = (jnp.where(low, 1.0, 0.0).astype(BF16), jnp.where(low, 0.0, 1.0).astype(BF16))
    ones_lane = (FOX_DH, 0)
    one_hot = tuple(jnp.where(lane == ol, 1.0, 0.0).astype(BF16) for ol in ones_lane)
    rr = lax.broadcasted_iota(jnp.int32, (tq, tk), 0)
    cc = lax.broadcasted_iota(jnp.int32, (tq, tk), 1)
    causal = cc <= rr

    @pl.when(qi == 0)
    def _():
        for p in range(pairs):
            vb = v_ref[:, p * LANES:(p + 1) * LANES]
            for hh in range(2):
                vs_ref[2 * p + hh] = vb * sel[hh] + one_hot[hh]

    for p in range(pairs):
        q = q_ref[:, p * LANES:(p + 1) * LANES]
        for hh in range(2):
            qs_ref[2 * p + hh] = q * sel[hh]
    m_ref[...] = jnp.full_like(m_ref, NEG_BIG)
    acc_ref[...] = jnp.zeros_like(acc_ref)

    def kv_block(j, masked):
        start = pl.multiple_of(j * tk, tk)
        for p in range(pairs):
            kb = k_ref[pl.ds(start, tk), p * LANES:(p + 1) * LANES]
            s_pair = _nt(qs_ref[2 * p:2 * p + 2].reshape(2 * tq, LANES), kb)
            for hh in range(2):
                h = 2 * p + hh
                ck = crow_ref[h:h + 1, pl.ds(start, tk)]
                s = s_pair[hh * tq:(hh + 1) * tq] - ck
                if masked:
                    s = jnp.where(causal, s, NEG_BIG)
                m_prev = m_ref[h]
                m_next = jnp.maximum(m_prev, jnp.max(s, axis=1, keepdims=True))
                pexp = jnp.exp2(s - jnp.concatenate([m_next] * (tk // LANES), axis=1))
                alpha = jnp.exp2(m_prev - m_next)
                acc_ref[h] = alpha * acc_ref[h] + _nn(pexp.astype(BF16), vs_ref[h, pl.ds(start, tk), :])
                m_ref[h] = m_next

    def body(j, carry):
        kv_block(j, False)
        return carry

    lax.fori_loop(0, qi, body, 0)
    kv_block(qi, True)

    for p in range(pairs):
        a0 = acc_ref[2 * p]
        a1 = acc_ref[2 * p + 1]
        o0 = a0 / a0[:, ones_lane[0]:ones_lane[0] + 1]
        o1 = a1 / a1[:, ones_lane[1]:ones_lane[1] + 1]
        o_ref[:, p * LANES:(p + 1) * LANES] = jnp.where(low, o0, o1).astype(BF16)


def _fox(fox, crow, batch, seq, tq):
    T = fox.shape[0]
    nq = seq // tq
    return pl.pallas_call(
        _fox_kernel,
        grid=(batch, nq),
        in_specs=[
            pl.BlockSpec((tq, FOX_W), lambda b, i: (b * nq + i, 0)),
            pl.BlockSpec((seq, FOX_W), lambda b, i: (b, 1)),
            pl.BlockSpec((seq, FOX_W), lambda b, i: (b, 2)),
            pl.BlockSpec((FOX_HEADS, seq), lambda b, i: (0, b)),
        ],
        out_specs=pl.BlockSpec((tq, FOX_W), lambda b, i: (b * nq + i, 0)),
        out_shape=jax.ShapeDtypeStruct((T, FOX_W), BF16),
        scratch_shapes=[pltpu.VMEM((FOX_HEADS, tq, LANES), F32), pltpu.VMEM((FOX_HEADS, tq, LANES), F32),
                        pltpu.VMEM((FOX_HEADS, tq, LANES), BF16), pltpu.VMEM((FOX_HEADS, seq, LANES), BF16)],
        compiler_params=pltpu.CompilerParams(
            dimension_semantics=("arbitrary", "arbitrary"), vmem_limit_bytes=VMEM_LIMIT),
        name="fox",
    )(fox, fox, fox, crow)


def _merge_kernel(x_ref, a_ref, b_ref, gate_ref, wa_ref, wb_ref, wo_ref, nw_ref, wq_ref, keys_ref,
                  h1_ref, xn_ref, st_ref):
    D = x_ref.shape[1]
    merged = gate_ref[:, :D] * _nn(a_ref[...], wa_ref[...]) + gate_ref[:, D:] * _nn(b_ref[...], wb_ref[...])
    h1 = x_ref[...] + _nn(merged.astype(BF16), wo_ref[...])
    h1_ref[...] = h1
    xn = _rms(h1, nw_ref[...]).astype(BF16)
    xn_ref[...] = xn
    q = _nn(xn, wq_ref[...]).astype(BF16)
    for hp in range(2 * PEER_HEADS):
        st_ref[hp] = _nt(keys_ref[hp], q[:, hp * PEER_HALF:(hp + 1) * PEER_HALF])


def _merge(x2d, a, b, gates, wa, wb, wo, nw, wq, keys, tm):
    T, D = x2d.shape
    row = lambda i: (i, 0)
    const = lambda i: (0, 0)
    return pl.pallas_call(
        _merge_kernel,
        grid=(T // tm,),
        in_specs=[
            pl.BlockSpec((tm, D), row),
            pl.BlockSpec((tm, HGRN_W), row),
            pl.BlockSpec((tm, FOX_W), row),
            pl.BlockSpec((tm, 2 * D), row),
            pl.BlockSpec(wa.shape, const),
            pl.BlockSpec(wb.shape, const),
            pl.BlockSpec(wo.shape, const),
            pl.BlockSpec((1, D), const),
            pl.BlockSpec(wq.shape, const),
            pl.BlockSpec(keys.shape, lambda i: (0, 0, 0)),
        ],
        out_specs=[
            pl.BlockSpec((tm, D), row),
            pl.BlockSpec((tm, D), row),
            pl.BlockSpec((2 * PEER_HEADS, N_KEYS, tm), lambda i: (0, 0, i)),
        ],
        out_shape=[
            jax.ShapeDtypeStruct((T, D), F32),
            jax.ShapeDtypeStruct((T, D), BF16),
            jax.ShapeDtypeStruct((2 * PEER_HEADS, N_KEYS, T), F32),
        ],
        compiler_params=pltpu.CompilerParams(
            dimension_semantics=("arbitrary",), vmem_limit_bytes=VMEM_LIMIT),
        name="merge",
    )(x2d, a, b, gates, wa, wb, wo, nw, wq, keys)


ROUTE_SLOTS = 64


def _route_pairs():
    return [(r, c) for r in range(TOPK) for c in range(TOPK) if (r + 1) * (c + 1) <= TOPK]


def _bitonic_sort_desc(vals, idxs=None):
    n = len(vals)
    k = 2
    while k <= n:
        j = k // 2
        while j >= 1:
            for i in range(n):
                l = i ^ j
                if l > i:
                    hi, lo = (i, l) if (i & k) == 0 else (l, i)
                    a, b = vals[i], vals[l]
                    if idxs is not None:
                        gt = a > b
                        ia, ib = idxs[i], idxs[l]
                        idxs[hi] = jnp.where(gt, ia, ib)
                        idxs[lo] = jnp.where(gt, ib, ia)
                    vals[hi] = jnp.maximum(a, b)
                    vals[lo] = jnp.minimum(a, b)
            j //= 2
        k *= 2


def _top_of_two_sorted(a, b):
    n = len(a)
    out = [jnp.maximum(a[i], b[n - 1 - i]) for i in range(n)]
    j = n // 2
    while j >= 1:
        for i in range(n):
            l = i ^ j
            if l > i:
                x, y = out[i], out[l]
                out[i] = jnp.maximum(x, y)
                out[l] = jnp.minimum(x, y)
        j //= 2
    return out


def _routes_kernel(st_ref, ids_ref, w_ref):
    tt = st_ref.shape[2]
    sub = 8
    groups = N_KEYS // sub
    sub_id = lax.broadcasted_iota(jnp.int32, (sub, tt), 0).astype(F32)

    def top_sorted(s):
        vals = [s[g * sub:(g + 1) * sub, :] for g in range(groups)]
        idxs = [sub_id + float(g * sub) for g in range(groups)]
        _bitonic_sort_desc(vals, idxs)
        tops, keys = [], []
        for t in range(TOPK):
            mx = jnp.max(vals[0], axis=0, keepdims=True)
            win = vals[0] == mx
            tops.append(mx)
            keys.append(jnp.max(jnp.where(win, idxs[0], -1.0), axis=0, keepdims=True))
            for r in range(TOPK - 1 - t):
                vals[r] = jnp.where(win, vals[r + 1], vals[r])
                idxs[r] = jnp.where(win, idxs[r + 1], idxs[r])
        return tops, keys

    tops = [(top_sorted(st_ref[2 * h]), top_sorted(st_ref[2 * h + 1])) for h in range(PEER_HEADS)]

    def stack(half, which, r):
        return jnp.concatenate([tops[h][half][which][r] for h in range(PEER_HEADS)], axis=0)

    a = [stack(0, 0, r) for r in range(TOPK)]
    b = [stack(1, 0, r) for r in range(TOPK)]
    ia = [stack(0, 1, r) for r in range(TOPK)]
    ib = [stack(1, 1, r) for r in range(TOPK)]
    pairs = _route_pairs()
    cand = [a[r] + b[c] for r, c in pairs]
    neg = jnp.full((PEER_HEADS, tt), -jnp.inf, F32)
    padded = cand + [neg] * (-len(cand) % TOPK)
    best = None
    for g0 in range(0, len(padded), TOPK):
        grp = list(padded[g0:g0 + TOPK])
        _bitonic_sort_desc(grp)
        best = grp if best is None else _top_of_two_sorted(best, grp)
    tau = best[TOPK - 1]
    top = a[0] + b[0]
    e = [jnp.where(xi >= tau, jnp.exp(xi - top), 0.0) for xi in cand]
    z = e[0]
    for ei in e[1:]:
        z = z + ei
    inv_z = 1.0 / z
    ids = [jnp.clip(ia[r] * float(N_KEYS) + ib[c], 0.0, float(N_KEYS * N_KEYS - 1)) for r, c in pairs]
    wts = [ek * inv_z for ek in e]
    pad = [jnp.zeros((PEER_HEADS, tt), F32)] * (ROUTE_SLOTS - len(pairs))
    ids_ref[...] = jnp.concatenate(ids + pad, axis=0).T.astype(jnp.int32)
    w_ref[...] = jnp.concatenate(wts + pad, axis=0).T


def _routes(st, tt, t0, tc):
    n, nk, _ = st.shape
    off = t0 // tt
    width = ROUTE_SLOTS * PEER_HEADS
    spec = pl.BlockSpec((tt, width), lambda i: (i, 0))
    return pl.pallas_call(
        _routes_kernel,
        grid=(tc // tt,),
        in_specs=[pl.BlockSpec((n, nk, tt), lambda i: (0, 0, i + off))],
        out_specs=[spec, spec],
        out_shape=[jax.ShapeDtypeStruct((tc, width), jnp.int32),
                   jax.ShapeDtypeStruct((tc, width), F32)],
        compiler_params=pltpu.CompilerParams(
            dimension_semantics=("arbitrary",), vmem_limit_bytes=VMEM_LIMIT),
        name="routes",
    )(st)


SC_LANES = 16
SC_TOKENS_PER_CHUNK = 8
PEER_CHUNKS = 8


def _split_expert_halves(table, eb):
    n, d = table.shape
    return table.reshape(n // eb, eb // (2 * SC_LANES), 2, SC_LANES, d).transpose(0, 2, 1, 3, 4).reshape(n, d)


def _gate_matrix(ids, w, n_exp, heads):
    T, E = ids.shape
    info = plsc.get_sparse_core_info()
    workers = info.num_cores * info.num_subcores
    per_worker = T // workers
    ch = SC_TOKENS_PER_CHUNK
    assert info.num_lanes == SC_LANES and T % (workers * ch) == 0
    assert E % (heads * SC_LANES) == 0 and n_exp % SC_LANES == 0
    mesh = plsc.VectorSubcoreMesh(core_axis_name="c", subcore_axis_name="s")
    ids_flat = ids.reshape(T * E)
    w_flat = w.reshape(T * E)

    @functools.partial(
        pl.kernel, mesh=mesh,
        out_type=jax.ShapeDtypeStruct((T, n_exp // 2), jnp.int32),
        scratch_types=[pltpu.VMEM((ch * E,), jnp.int32), pltpu.VMEM((ch * E,), F32),
                       pltpu.VMEM((n_exp,), F32), pltpu.VMEM((n_exp // 2,), jnp.int32)],
        compiler_params=pltpu.CompilerParams(needs_layout_passes=False),
        name="gate_matrix",
    )
    def scatter(ids_hbm, w_hbm, out_hbm, ids_v, w_v, row_v, row16_v):
        wid = lax.axis_index("s") * info.num_cores + lax.axis_index("c")
        base = wid * per_worker
        zeros = jnp.zeros((SC_LANES,), F32)
        lane = lax.iota(jnp.int32, SC_LANES)

        @plsc.parallel_loop(0, n_exp, step=SC_LANES, unroll=8)
        def _(i):
            row_v[pl.ds(i, SC_LANES)] = zeros

        @pl.loop(0, per_worker // ch)
        def _(ci):
            t0 = base + ci * ch
            pltpu.sync_copy(ids_hbm.at[pl.ds(t0 * E, ch * E)], ids_v)
            pltpu.sync_copy(w_hbm.at[pl.ds(t0 * E, ch * E)], w_v)
            for tl in range(ch):
                @pl.loop(0, heads)
                def _(h):
                    for m in range(E // heads // SC_LANES):
                        pos = (lane + (tl * E // heads + m * SC_LANES)) * heads + h
                        idx = plsc.load_gather(ids_v, [pos])
                        val = plsc.load_gather(w_v, [pos])
                        plsc.addupdate_scatter(row_v, [idx], val)

                @plsc.parallel_loop(0, n_exp // (2 * SC_LANES), unroll=8)
                def _(g):
                    lo = row_v[pl.ds(g * 2 * SC_LANES, SC_LANES)]
                    hi = row_v[pl.ds(g * 2 * SC_LANES + SC_LANES, SC_LANES)]
                    pair = plsc.pack(lo, hi, format=plsc.PackFormat.INTERLEAVED)
                    row16_v[pl.ds(g * SC_LANES, SC_LANES)] = plsc.bitcast(pair, jnp.int32)

                pltpu.sync_copy(row16_v, out_hbm.at[t0 + tl])

                @plsc.parallel_loop(0, E, step=SC_LANES, unroll=8)
                def _(e0):
                    sl = pl.ds(tl * E + e0, SC_LANES)
                    plsc.store_scatter(row_v, [ids_v[sl]], zeros)

    return scatter(ids_flat, w_flat)


def _experts_kernel(xn_ref, u_ref, v_ref, gate_ref, h1_ref, nw_ref, *rest):
    o_ref, acc_ref = rest[-2:]
    j = pl.program_id(1)
    nj = pl.num_programs(1)

    @pl.when(j == 0)
    def _():
        acc_ref[...] = jnp.zeros_like(acc_ref)

    h = _nt(xn_ref[...], u_ref[...])
    act = 0.5 * h * (1.0 + lax.erf(h * (2.0 ** -0.5)))
    words = gate_ref[...]
    g_lo = lax.bitcast_convert_type(words << 16, F32)
    g_hi = lax.bitcast_convert_type(words & jnp.int32(-65536), F32)
    gate = jnp.concatenate([g_lo, g_hi], axis=1)
    acc_ref[...] += _nn((act * gate).astype(BF16), v_ref[...])

    @pl.when(j == nj - 1)
    def _():
        o_ref[...] = _rms(h1_ref[...] + acc_ref[...], nw_ref[...])


def _experts(xn, u, v, gate, h1, nw, tb, eb, t0, out_prev):
    tc = gate.shape[0]
    T, D = xn.shape
    n_exp = u.shape[0]
    off = t0 // tb
    in_specs = [
        pl.BlockSpec((tb, D), lambda i, j: (i + off, 0)),
        pl.BlockSpec((eb, D), lambda i, j: (j, 0)),
        pl.BlockSpec((eb, D), lambda i, j: (j, 0)),
        pl.BlockSpec((tb, eb // 2), lambda i, j: (i, j)),
        pl.BlockSpec((tb, D), lambda i, j: (i + off, 0)),
        pl.BlockSpec((1, D), lambda i, j: (0, 0)),
    ]
    args = [xn, u, v, gate, h1, nw]
    aliases = {}
    if out_prev is not None:
        in_specs.append(pl.BlockSpec(memory_space=pl.ANY))
        args.append(out_prev)
        aliases = {len(args) - 1: 0}
    return pl.pallas_call(
        _experts_kernel,
        grid=(tc // tb, n_exp // eb),
        in_specs=in_specs,
        out_specs=pl.BlockSpec((tb, D), lambda i, j: (i + off, 0)),
        out_shape=jax.ShapeDtypeStruct((T, D), F32),
        scratch_shapes=[pltpu.VMEM((tb, D), F32)],
        input_output_aliases=aliases,
        compiler_params=pltpu.CompilerParams(
            dimension_semantics=("arbitrary", "arbitrary"), vmem_limit_bytes=VMEM_LIMIT),
        name="experts",
    )(*args)


def _block_select_matrix():
    part = jnp.arange(HGRN_SUB * HGRN_DH, dtype=jnp.int32) // HGRN_DH
    col = jnp.arange(HGRN_CHUNK, dtype=jnp.int32) % HGRN_SUB
    return (part[:, None] == col[None, :]).astype(BF16)


def _forward(x, norm_mix_w, w_in, hgrn_lb_logits, hgrn_norm_w, fox_f_bias, w_branch_hgrn,
             w_branch_fox, w_out, norm_ffn_w, peer_w_q, peer_sub_keys, peer_u, peer_v,
             norm_final_w, *, tm_in, tq, tm_merge, tt, tb, eb):
    B, S, D = x.shape
    T = B * S
    n_h = 4 * HGRN_W
    n_f = 3 * FOX_W
    n_exp = N_KEYS * N_KEYS
    assert w_in.shape == (1, D, n_h + n_f + FOX_HEADS + 2 * D), w_in.shape
    assert hgrn_lb_logits.shape == (2, HGRN_W) and fox_f_bias.shape == (1, FOX_HEADS)
    assert peer_w_q.shape == (1, D, 2 * PEER_HEADS * PEER_HALF)
    assert peer_sub_keys.shape == (1, PEER_HEADS, 2, N_KEYS, PEER_HALF)
    assert peer_u.shape == (1, n_exp, D) and peer_v.shape == (1, n_exp, D)
    assert S % max(tm_in, tq, HGRN_CHUNK) == 0 and T % (PEER_CHUNKS * tb) == 0
    assert tb % tt == 0 and tb % tm_merge == 0 and n_exp % eb == 0
    x2d = x.reshape(T, D)
    wi = w_in[0]
    w_all = wi[:, :n_h + n_f].astype(BF16)
    w_gate = wi[:, n_h + n_f + FOX_HEADS:].astype(BF16)
    wff = wi[:, n_h + n_f:n_h + n_f + FOX_HEADS]
    wff_row = wff.T.astype(BF16)
    fb = fox_f_bias[0].astype(F32)
    fb_row = jnp.broadcast_to(fb.reshape(FOX_HEADS, 1), (FOX_HEADS, LANES))

    hg, fox, gates, crow = _in_proj(
        x2d, norm_mix_w[0].reshape(1, D), w_all, w_gate, wff_row, fb_row, B, S, tm_in)
    a = _hgrn(hg, hgrn_lb_logits, hgrn_norm_w[0].reshape(1, HGRN_W), _block_select_matrix(), B, S)
    b = _fox(fox, crow, B, S, tq)
    keys = peer_sub_keys[0].reshape(2 * PEER_HEADS, N_KEYS, PEER_HALF).astype(BF16)
    h1, xn2, st = _merge(
        x2d, a, b, gates, w_branch_hgrn[0].astype(BF16), w_branch_fox[0].astype(BF16),
        w_out[0].astype(BF16), norm_ffn_w[0].reshape(1, D), peer_w_q[0].astype(BF16), keys, tm_merge)
    u_bf = _split_expert_halves(peer_u[0].astype(BF16), eb)
    v_bf = _split_expert_halves(peer_v[0].astype(BF16), eb)
    tc = T // PEER_CHUNKS
    bounds = [0, tb, tc] + [c * tc for c in range(2, PEER_CHUNKS + 1)]
    out = None
    for t0, t1 in zip(bounds[:-1], bounds[1:]):
        ids, wts = _routes(st, tt, t0, t1 - t0)
        gate = _gate_matrix(ids, wts, n_exp, PEER_HEADS)
        out = _experts(xn2, u_bf, v_bf, gate, h1, norm_final_w.reshape(1, D), tb, eb, t0, out)
    return out.reshape(B, S, D)


def kernel(x, norm_mix_w, w_in, hgrn_lb_logits, hgrn_norm_w, fox_f_bias, w_branch_hgrn, w_branch_fox, w_out, norm_ffn_w, peer_w_q, peer_sub_keys, peer_u, peer_v, norm_final_w):
    return _forward(x, norm_mix_w, w_in, hgrn_lb_logits, hgrn_norm_w, fox_f_bias, w_branch_hgrn,
                    w_branch_fox, w_out, norm_ffn_w, peer_w_q, peer_sub_keys, peer_u, peer_v,
                    norm_final_w, **TILES)
```

```python
import functools
import math

import jax
import jax.numpy as jnp
from jax import lax
from jax.experimental import pallas as pl
from jax.experimental.pallas import tpu as pltpu
from jax.experimental.pallas import tpu_sc as plsc

F32 = jnp.float32
BF16 = jnp.bfloat16
RMS_EPS = 1e-6
NEG_BIG = -1e30

HGRN_HEADS = 4
HGRN_DH = 128
HGRN_W = HGRN_HEADS * HGRN_DH
FOX_HEADS = 8
FOX_DH = 64
FOX_W = FOX_HEADS * FOX_DH
PEER_HEADS = 8
PEER_HALF = 128
N_KEYS = 128
TOPK = 16
LANES = 128

V7X_VMEM_BYTES = 64 * 1024 * 1024
VMEM_LIMIT = V7X_VMEM_BYTES * 7 // 8

TILES = dict(
    tm_in=512,
    tq=512,
    tm_merge=512,
    tt=512,
    tb=1024,
    eb=1024,
)


def _nt(a, b):
    return lax.dot_general(a, b, (((1,), (1,)), ((), ())), preferred_element_type=F32)


def _tn(a, b):
    return lax.dot_general(a, b, (((0,), (0,)), ((), ())), preferred_element_type=F32)


def _nn(a, b):
    return jnp.dot(a, b, preferred_element_type=F32)


def _log_sigmoid(x):
    return jnp.minimum(x, 0.0) - jnp.log1p(jnp.exp(-jnp.abs(x)))


def _rms(x, w):
    return x * lax.rsqrt(jnp.mean(x * x, axis=-1, keepdims=True) + RMS_EPS) * w


LOG2E = math.log2(math.e)


def _inproj_kernel(x_ref, nw_ref, w_ref, wg_ref, wffr_ref, fbr_ref,
                   hg_ref, fox_ref, gate_ref, crow_ref, carry_row):
    i = pl.program_id(1)
    tm = x_ref.shape[0]

    @pl.when(i == 0)
    def _():
        carry_row[...] = jnp.zeros_like(carry_row)

    xn = _rms(x_ref[...], nw_ref[...]).astype(BF16)
    proj = _nn(xn, w_ref[...])
    n_h = 4 * HGRN_W
    hg_ref[...] = proj[:, :n_h]
    fox_ref[:, :FOX_W] = (proj[:, n_h:n_h + FOX_W] * (FOX_DH ** -0.5 * LOG2E)).astype(BF16)
    fox_ref[:, FOX_W:] = proj[:, n_h + FOX_W:n_h + 3 * FOX_W].astype(BF16)
    gate_ref[...] = jax.nn.sigmoid(_nn(xn, wg_ref[...]))

    r = lax.broadcasted_iota(jnp.int32, (tm, tm), 0)
    c = lax.broadcasted_iota(jnp.int32, (tm, tm), 1)
    triu = (r <= c).astype(BF16)
    ls_row = _log_sigmoid(_nt(wffr_ref[...], xn) + fbr_ref[:, 0:1]) * LOG2E
    hi = ls_row.astype(BF16)
    lo = (ls_row - hi.astype(F32)).astype(BF16)
    crow = _nn(hi, triu) + _nn(lo, triu) + carry_row[:, 0:1]
    crow_ref[...] = crow
    carry_row[...] = jnp.broadcast_to(crow[:, tm - 1:tm], carry_row.shape)


def _in_proj(x2d, nw, w_all, w_gate, wff_row, fb_row, batch, seq, tm):
    T, D = x2d.shape
    nt = seq // tm
    n_all = w_all.shape[1]
    row = lambda b, i: (b * nt + i, 0)
    const = lambda b, i: (0, 0)
    return pl.pallas_call(
        _inproj_kernel,
        grid=(batch, nt),
        in_specs=[
            pl.BlockSpec((tm, D), row),
            pl.BlockSpec((1, D), const),
            pl.BlockSpec((D, n_all), const),
            pl.BlockSpec((D, 2 * D), const),
            pl.BlockSpec((FOX_HEADS, D), const),
            pl.BlockSpec((FOX_HEADS, LANES), const),
        ],
        out_specs=[
            pl.BlockSpec((tm, 4 * HGRN_W), row),
            pl.BlockSpec((tm, 3 * FOX_W), row),
            pl.BlockSpec((tm, 2 * D), row),
            pl.BlockSpec((FOX_HEADS, tm), lambda b, i: (0, b * nt + i)),
        ],
        out_shape=[
            jax.ShapeDtypeStruct((T, 4 * HGRN_W), F32),
            jax.ShapeDtypeStruct((T, 3 * FOX_W), BF16),
            jax.ShapeDtypeStruct((T, 2 * D), F32),
            jax.ShapeDtypeStruct((FOX_HEADS, T), F32),
        ],
        scratch_shapes=[pltpu.VMEM((FOX_HEADS, LANES), F32)],
        compiler_params=pltpu.CompilerParams(
            dimension_semantics=("arbitrary", "arbitrary"), vmem_limit_bytes=VMEM_LIMIT),
        name="in_proj",
    )(x2d, nw, w_all, w_gate, wff_row, fb_row)


HGRN_CHUNK = 128
HGRN_SUB = 8


def _hgrn_levels():
    out = []
    m = HGRN_SUB
    while m < HGRN_CHUNK:
        out.append(m)
        m *= 2
    return out


def _hgrn_masks():
    C, sub = HGRN_CHUNK, HGRN_SUB
    r = jnp.arange(C, dtype=jnp.int32)[:, None]
    c = jnp.arange(C, dtype=jnp.int32)[None, :]
    masks = [((r // sub) == (c // sub)) & ((c % sub) <= (r % sub))]
    for m in _hgrn_levels():
        masks.append(((r // (2 * m)) == (c // (2 * m))) & (((r // m) % 2) == 1) & (((c // m) % 2) == 0))
    return jnp.stack(masks).astype(F32)


def _hgrn_kernel(hg_ref, lbl_ref, nw_ref, rsel_ref, tril_ref, mask_ref, a_ref, state_ref):
    ci = pl.program_id(1)
    C = HGRN_CHUNK
    dh = HGRN_DH
    sub = HGRN_SUB

    @pl.when(ci == 0)
    def _():
        state_ref[...] = jnp.zeros_like(state_ref)

    lg = lbl_ref[...]
    e = jnp.exp(lg - jnp.max(lg, axis=0, keepdims=True))
    lb_all = e[0:1, :] / jnp.sum(e, axis=0, keepdims=True)
    f_all = lb_all + (1.0 - lb_all) * jax.nn.sigmoid(hg_ref[:, HGRN_W:2 * HGRN_W])
    log2f_all = jnp.log(f_all) * LOG2E
    hi_all = log2f_all.astype(BF16)
    lo_all = (log2f_all - hi_all.astype(F32)).astype(BF16)
    cum_parts = _nn(tril_ref[...], jnp.concatenate([hi_all, lo_all], axis=1))
    cum_all = cum_parts[:, :HGRN_W] + cum_parts[:, HGRN_W:]

    for h in range(HGRN_HEADS):
        sl = slice(h * dh, (h + 1) * dh)
        qraw = hg_ref[:, sl]
        q = qraw * jax.nn.sigmoid(qraw)
        k = 1.0 - f_all[:, sl]
        v = hg_ref[:, 2 * HGRN_W + h * dh:2 * HGRN_W + (h + 1) * dh]
        g = hg_ref[:, 3 * HGRN_W + h * dh:3 * HGRN_W + (h + 1) * dh]
        v_bf = v.astype(BF16)
        cum = cum_all[:, sl]

        nb = C // sub
        q3 = q.reshape(nb, sub, dh)
        k3 = k.reshape(nb, sub, dh)
        c3 = cum.reshape(nb, sub, dh)
        parts = []
        for s in range(sub):
            kb = jnp.broadcast_to(k3[:, s:s + 1, :], (nb, sub, dh))
            cb = jnp.broadcast_to(c3[:, s:s + 1, :], (nb, sub, dh))
            es = q3 * kb * jnp.exp2(jnp.minimum(c3 - cb, 0.0))
            parts.append(es.reshape(C, dh).astype(BF16))
        p_mat = mask_ref[0] * _nn(jnp.concatenate(parts, axis=1), rsel_ref[...])

        for li, m in enumerate(_hgrn_levels()):
            nbm = C // m
            qm = q.reshape(nbm, m, dh)
            km = k.reshape(nbm, m, dh)
            cm = cum.reshape(nbm, m, dh)
            end = cm[:, m - 1:m, :]
            prev_end = jnp.concatenate([jnp.zeros((1, 1, dh), F32), end[:nbm - 1]], axis=0)
            qd = qm * jnp.exp2(jnp.minimum(cm - jnp.broadcast_to(prev_end, (nbm, m, dh)), 0.0))
            kd = km * jnp.exp2(jnp.minimum(jnp.broadcast_to(end, (nbm, m, dh)) - cm, 0.0))
            sc = _nt(qd.reshape(C, dh).astype(BF16), kd.reshape(C, dh).astype(BF16))
            p_mat = p_mat + mask_ref[1 + li] * sc

        st = state_ref[h]
        o = _nn(p_mat.astype(BF16), v_bf) + _nt((q * jnp.exp2(cum)).astype(BF16), st.astype(BF16))
        last = cum[C - 1:C, :]
        kdec = (k * jnp.exp2(last - cum)).astype(BF16)
        state_ref[h] = jnp.exp2(last) * st + _tn(v_bf, kdec)

        o = o * lax.rsqrt(jnp.mean(o * o, axis=-1, keepdims=True) + RMS_EPS) * nw_ref[:, sl]
        a_ref[:, sl] = (o * (g * jax.nn.sigmoid(g))).astype(BF16)


def _hgrn(hg, lb_logits, norm_w, rsel, batch, seq):
    T = hg.shape[0]
    C = HGRN_CHUNK
    nc = seq // C
    row = lambda b, i: (b * nc + i, 0)
    const = lambda b, i: (0, 0)
    tril = jnp.tril(jnp.ones((C, C), F32)).astype(BF16)
    masks = _hgrn_masks()
    return pl.pallas_call(
        _hgrn_kernel,
        grid=(batch, nc),
        in_specs=[
            pl.BlockSpec((C, 4 * HGRN_W), row),
            pl.BlockSpec(lb_logits.shape, const),
            pl.BlockSpec((1, HGRN_W), const),
            pl.BlockSpec(rsel.shape, const),
            pl.BlockSpec((C, C), const),
            pl.BlockSpec(masks.shape, lambda b, i: (0, 0, 0)),
        ],
        out_specs=pl.BlockSpec((C, HGRN_W), row),
        out_shape=jax.ShapeDtypeStruct((T, HGRN_W), BF16),
        scratch_shapes=[pltpu.VMEM((HGRN_HEADS, HGRN_DH, HGRN_DH), F32)],
        compiler_params=pltpu.CompilerParams(
            dimension_semantics=("arbitrary", "arbitrary"), vmem_limit_bytes=VMEM_LIMIT),
        name="hgrn",
    )(hg, lb_logits, norm_w, rsel, tril, masks)


def _fox_kernel(q_ref, k_ref, v_ref, crow_ref, o_ref, m_ref, acc_ref, qs_ref, vs_ref):
    qi = pl.program_id(1)
    tq = q_ref.shape[0]
    tk = tq
    pairs = FOX_HEADS // 2
    lane = lax.broadcasted_iota(jnp.int32, (1, LANES), 1)
    low = lane < FOX_DH
    sel = (jnp.where(low, 1.0, 0.0).astype(BF16), jnp.where(low, 0.0, 1.0).astype(BF16))
    ones_lane = (FOX_DH, 0)
    one_hot = tuple(jnp.where(lane == ol, 1.0, 0.0).astype(BF16) for ol in ones_lane)
    rr = lax.broadcasted_iota(jnp.int32, (tq, tk), 0)
    cc = lax.broadcasted_iota(jnp.int32, (tq, tk), 1)
    causal = cc <= rr

    @pl.when(qi == 0)
    def _():
        for p in range(pairs):
            vb = v_ref[:, p * LANES:(p + 1) * LANES]
            for hh in range(2):
                vs_ref[2 * p + hh] = vb * sel[hh] + one_hot[hh]

    for p in range(pairs):
        q = q_ref[:, p * LANES:(p + 1) * LANES]
        for hh in range(2):
            qs_ref[2 * p + hh] = q * sel[hh]
    m_ref[...] = jnp.full_like(m_ref, NEG_BIG)
    acc_ref[...] = jnp.zeros_like(acc_ref)

    def kv_block(j, masked):
        start = pl.multiple_of(j * tk, tk)
        for p in range(pairs):
            kb = k_ref[pl.ds(start, tk), p * LANES:(p + 1) * LANES]
            s_pair = _nt(qs_ref[2 * p:2 * p + 2].reshape(2 * tq, LANES), kb)
            for hh in range(2):
                h = 2 * p + hh
                ck = crow_ref[h:h + 1, pl.ds(start, tk)]
                s = s_pair[hh * tq:(hh + 1) * tq] - ck
                if masked:
                    s = jnp.where(causal, s, NEG_BIG)
                m_prev = m_ref[h]
                m_next = jnp.maximum(m_prev, jnp.max(s, axis=1, keepdims=True))
                pexp = jnp.exp2(s - jnp.concatenate([m_next] * (tk // LANES), axis=1))
                alpha = jnp.exp2(m_prev - m_next)
                acc_ref[h] = alpha * acc_ref[h] + _nn(pexp.astype(BF16), vs_ref[h, pl.ds(start, tk), :])
                m_ref[h] = m_next

    def body(j, carry):
        kv_block(j, False)
        return carry

    lax.fori_loop(0, qi, body, 0)
    kv_block(qi, True)

    for p in range(pairs):
        a0 = acc_ref[2 * p]
        a1 = acc_ref[2 * p + 1]
        o0 = a0 / a0[:, ones_lane[0]:ones_lane[0] + 1]
        o1 = a1 / a1[:, ones_lane[1]:ones_lane[1] + 1]
        o_ref[:, p * LANES:(p + 1) * LANES] = jnp.where(low, o0, o1).astype(BF16)


def _fox(fox, crow, batch, seq, tq):
    T = fox.shape[0]
    nq = seq // tq
    return pl.pallas_call(
        _fox_kernel,
        grid=(batch, nq),
        in_specs=[
            pl.BlockSpec((tq, FOX_W), lambda b, i: (b * nq + i, 0)),
            pl.BlockSpec((seq, FOX_W), lambda b, i: (b, 1)),
            pl.BlockSpec((seq, FOX_W), lambda b, i: (b, 2)),
            pl.BlockSpec((FOX_HEADS, seq), lambda b, i: (0, b)),
        ],
        out_specs=pl.BlockSpec((tq, FOX_W), lambda b, i: (b * nq + i, 0)),
        out_shape=jax.ShapeDtypeStruct((T, FOX_W), BF16),
        scratch_shapes=[pltpu.VMEM((FOX_HEADS, tq, LANES), F32), pltpu.VMEM((FOX_HEADS, tq, LANES), F32),
                        pltpu.VMEM((FOX_HEADS, tq, LANES), BF16), pltpu.VMEM((FOX_HEADS, seq, LANES), BF16)],
        compiler_params=pltpu.CompilerParams(
            dimension_semantics=("arbitrary", "arbitrary"), vmem_limit_bytes=VMEM_LIMIT),
        name="fox",
    )(fox, fox, fox, crow)


def _merge_kernel(x_ref, a_ref, b_ref, gate_ref, wa_ref, wb_ref, wo_ref, nw_ref, wq_ref, keys_ref,
                  h1_ref, xn_ref, st_ref):
    D = x_ref.shape[1]
    merged = gate_ref[:, :D] * _nn(a_ref[...], wa_ref[...]) + gate_ref[:, D:] * _nn(b_ref[...], wb_ref[...])
    h1 = x_ref[...] + _nn(merged.astype(BF16), wo_ref[...])
    h1_ref[...] = h1
    xn = _rms(h1, nw_ref[...]).astype(BF16)
    xn_ref[...] = xn
    q = _nn(xn, wq_ref[...]).astype(BF16)
    for hp in range(2 * PEER_HEADS):
        st_ref[hp] = _nt(keys_ref[hp], q[:, hp * PEER_HALF:(hp + 1) * PEER_HALF])


def _merge(x2d, a, b, gates, wa, wb, wo, nw, wq, keys, tm):
    T, D = x2d.shape
    row = lambda i: (i, 0)
    const = lambda i: (0, 0)
    return pl.pallas_call(
        _merge_kernel,
        grid=(T // tm,),
        in_specs=[
            pl.BlockSpec((tm, D), row),
            pl.BlockSpec((tm, HGRN_W), row),
            pl.BlockSpec((tm, FOX_W), row),
            pl.BlockSpec((tm, 2 * D), row),
            pl.BlockSpec(wa.shape, const),
            pl.BlockSpec(wb.shape, const),
            pl.BlockSpec(wo.shape, const),
            pl.BlockSpec((1, D), const),
            pl.BlockSpec(wq.shape, const),
            pl.BlockSpec(keys.shape, lambda i: (0, 0, 0)),
        ],
        out_specs=[
            pl.BlockSpec((tm, D), row),
            pl.BlockSpec((tm, D), row),
            pl.BlockSpec((2 * PEER_HEADS, N_KEYS, tm), lambda i: (0, 0, i)),
        ],
        out_shape=[
            jax.ShapeDtypeStruct((T, D), F32),
            jax.ShapeDtypeStruct((T, D), BF16),
            jax.ShapeDtypeStruct((2 * PEER_HEADS, N_KEYS, T), F32),
        ],
        compiler_params=pltpu.CompilerParams(
            dimension_semantics=("arbitrary",), vmem_limit_bytes=VMEM_LIMIT),
        name="merge",
    )(x2d, a, b, gates, wa, wb, wo, nw, wq, keys)


ROUTE_SLOTS = 64


def _route_pairs():
    return [(r, c) for r in range(TOPK) for c in range(TOPK) if (r + 1) * (c + 1) <= TOPK]


def _bitonic_sort_desc(vals, idxs=None):
    n = len(vals)
    k = 2
    while k <= n:
        j = k // 2
        while j >= 1:
            for i in range(n):
                l = i ^ j
                if l > i:
                    hi, lo = (i, l) if (i & k) == 0 else (l, i)
                    a, b = vals[i], vals[l]
                    if idxs is not None:
                        gt = a > b
                        ia, ib = idxs[i], idxs[l]
                        idxs[hi] = jnp.where(gt, ia, ib)
                        idxs[lo] = jnp.where(gt, ib, ia)
                    vals[hi] = jnp.maximum(a, b)
                    vals[lo] = jnp.minimum(a, b)
            j //= 2
        k *= 2


def _top_of_two_sorted(a, b):
    n = len(a)
    out = [jnp.maximum(a[i], b[n - 1 - i]) for i in range(n)]
    j = n // 2
    while j >= 1:
        for i in range(n):
            l = i ^ j
            if l > i:
                x, y = out[i], out[l]
                out[i] = jnp.maximum(x, y)
                out[l] = jnp.minimum(x, y)
        j //= 2
    return out


def _routes_kernel(st_ref, ids_ref, w_ref):
    tt = st_ref.shape[2]
    sub = 8
    groups = N_KEYS // sub
    sub_id = lax.broadcasted_iota(jnp.int32, (sub, tt), 0).astype(F32)

    def top_sorted(s):
        vals = [s[g * sub:(g + 1) * sub, :] for g in range(groups)]
        idxs = [sub_id + float(g * sub) for g in range(groups)]
        _bitonic_sort_desc(vals, idxs)
        tops, keys = [], []
        for t in range(TOPK):
            mx = jnp.max(vals[0], axis=0, keepdims=True)
            win = vals[0] == mx
            tops.append(mx)
            keys.append(jnp.max(jnp.where(win, idxs[0], -1.0), axis=0, keepdims=True))
            for r in range(TOPK - 1 - t):
                vals[r] = jnp.where(win, vals[r + 1], vals[r])
                idxs[r] = jnp.where(win, idxs[r + 1], idxs[r])
        return tops, keys

    tops = [(top_sorted(st_ref[2 * h]), top_sorted(st_ref[2 * h + 1])) for h in range(PEER_HEADS)]

    def stack(half, which, r):
        return jnp.concatenate([tops[h][half][which][r] for h in range(PEER_HEADS)], axis=0)

    a = [stack(0, 0, r) for r in range(TOPK)]
    b = [stack(1, 0, r) for r in range(TOPK)]
    ia = [stack(0, 1, r) for r in range(TOPK)]
    ib = [stack(1, 1, r) for r in range(TOPK)]
    pairs = _route_pairs()
    cand = [a[r] + b[c] for r, c in pairs]
    neg = jnp.full((PEER_HEADS, tt), -jnp.inf, F32)
    padded = cand + [neg] * (-len(cand) % TOPK)
    best = None
    for g0 in range(0, len(padded), TOPK):
        grp = list(padded[g0:g0 + TOPK])
        _bitonic_sort_desc(grp)
        best = grp if best is None else _top_of_two_sorted(best, grp)
    tau = best[TOPK - 1]
    top = a[0] + b[0]
    e = [jnp.where(xi >= tau, jnp.exp(xi - top), 0.0) for xi in cand]
    z = e[0]
    for ei in e[1:]:
        z = z + ei
    inv_z = 1.0 / z
    ids = [jnp.clip(ia[r] * float(N_KEYS) + ib[c], 0.0, float(N_KEYS * N_KEYS - 1)) for r, c in pairs]
    wts = [ek * inv_z for ek in e]
    pad = [jnp.zeros((PEER_HEADS, tt), F32)] * (ROUTE_SLOTS - len(pairs))
    ids_ref[...] = jnp.concatenate(ids + pad, axis=0).T.astype(jnp.int32)
    w_ref[...] = jnp.concatenate(wts + pad, axis=0).T


def _routes(st, tt, t0, tc):
    n, nk, _ = st.shape
    off = t0 // tt
    width = ROUTE_SLOTS * PEER_HEADS
    spec = pl.BlockSpec((tt, width), lambda i: (i, 0))
    return pl.pallas_call(
        _routes_kernel,
        grid=(tc // tt,),
        in_specs=[pl.BlockSpec((n, nk, tt), lambda i: (0, 0, i + off))],
        out_specs=[spec, spec],
        out_shape=[jax.ShapeDtypeStruct((tc, width), jnp.int32),
                   jax.ShapeDtypeStruct((tc, width), F32)],
        compiler_params=pltpu.CompilerParams(
            dimension_semantics=("arbitrary",), vmem_limit_bytes=VMEM_LIMIT),
        name="routes",
    )(st)


SC_LANES = 16
SC_TOKENS_PER_CHUNK = 8
PEER_CHUNKS = 8


def _split_expert_halves(table, eb):
    n, d = table.shape
    return table.reshape(n // eb, eb // (2 * SC_LANES), 2, SC_LANES, d).transpose(0, 2, 1, 3, 4).reshape(n, d)


def _gate_matrix(ids, w, n_exp, heads):
    T, E = ids.shape
    info = plsc.get_sparse_core_info()
    workers = info.num_cores * info.num_subcores
    per_worker = T // workers
    ch = SC_TOKENS_PER_CHUNK
    assert info.num_lanes == SC_LANES and T % (workers * ch) == 0
    assert E % (heads * SC_LANES) == 0 and n_exp % SC_LANES == 0
    mesh = plsc.VectorSubcoreMesh(core_axis_name="c", subcore_axis_name="s")
    ids_flat = ids.reshape(T * E)
    w_flat = w.reshape(T * E)

    @functools.partial(
        pl.kernel, mesh=mesh,
        out_type=jax.ShapeDtypeStruct((T, n_exp // 2), jnp.int32),
        scratch_types=[pltpu.VMEM((ch * E,), jnp.int32), pltpu.VMEM((ch * E,), F32),
                       pltpu.VMEM((n_exp,), F32), pltpu.VMEM((n_exp // 2,), jnp.int32)],
        compiler_params=pltpu.CompilerParams(needs_layout_passes=False),
        name="gate_matrix",
    )
    def scatter(ids_hbm, w_hbm, out_hbm, ids_v, w_v, row_v, row16_v):
        wid = lax.axis_index("s") * info.num_cores + lax.axis_index("c")
        base = wid * per_worker
        zeros = jnp.zeros((SC_LANES,), F32)
        lane = lax.iota(jnp.int32, SC_LANES)

        @plsc.parallel_loop(0, n_exp, step=SC_LANES, unroll=8)
        def _(i):
            row_v[pl.ds(i, SC_LANES)] = zeros

        @pl.loop(0, per_worker // ch)
        def _(ci):
            t0 = base + ci * ch
            pltpu.sync_copy(ids_hbm.at[pl.ds(t0 * E, ch * E)], ids_v)
            pltpu.sync_copy(w_hbm.at[pl.ds(t0 * E, ch * E)], w_v)
            for tl in range(ch):
                @pl.loop(0, heads)
                def _(h):
                    for m in range(E // heads // SC_LANES):
                        pos = (lane + (tl * E // heads + m * SC_LANES)) * heads + h
                        idx = plsc.load_gather(ids_v, [pos])
                        val = plsc.load_gather(w_v, [pos])
                        plsc.addupdate_scatter(row_v, [idx], val)

                @plsc.parallel_loop(0, n_exp // (2 * SC_LANES), unroll=8)
                def _(g):
                    lo = row_v[pl.ds(g * 2 * SC_LANES, SC_LANES)]
                    hi = row_v[pl.ds(g * 2 * SC_LANES + SC_LANES, SC_LANES)]
                    pair = plsc.pack(lo, hi, format=plsc.PackFormat.INTERLEAVED)
                    row16_v[pl.ds(g * SC_LANES, SC_LANES)] = plsc.bitcast(pair, jnp.int32)

                pltpu.sync_copy(row16_v, out_hbm.at[t0 + tl])

                @plsc.parallel_loop(0, E, step=SC_LANES, unroll=8)
                def _(e0):
                    sl = pl.ds(tl * E + e0, SC_LANES)
                    plsc.store_scatter(row_v, [ids_v[sl]], zeros)

    return scatter(ids_flat, w_flat)


def _experts_kernel(xn_ref, u_ref, v_ref, gate_ref, h1_ref, nw_ref, *rest):
    o_ref, acc_ref = rest[-2:]
    j = pl.program_id(1)
    nj = pl.num_programs(1)

    @pl.when(j == 0)
    def _():
        acc_ref[...] = jnp.zeros_like(acc_ref)

    h = _nt(xn_ref[...], u_ref[...])
    act = 0.5 * h * (1.0 + lax.erf(h * (2.0 ** -0.5)))
    words = gate_ref[...]
    g_lo = lax.bitcast_convert_type(words << 16, F32)
    g_hi = lax.bitcast_convert_type(words & jnp.int32(-65536), F32)
    gate = jnp.concatenate([g_lo, g_hi], axis=1)
    acc_ref[...] += _nn((act * gate).astype(BF16), v_ref[...])

    @pl.when(j == nj - 1)
    def _():
        o_ref[...] = _rms(h1_ref[...] + acc_ref[...], nw_ref[...])


def _experts(xn, u, v, gate, h1, nw, tb, eb, t0, out_prev):
    tc = gate.shape[0]
    T, D = xn.shape
    n_exp = u.shape[0]
    off = t0 // tb
    in_specs = [
        pl.BlockSpec((tb, D), lambda i, j: (i + off, 0)),
        pl.BlockSpec((eb, D), lambda i, j: (j, 0)),
        pl.BlockSpec((eb, D), lambda i, j: (j, 0)),
        pl.BlockSpec((tb, eb // 2), lambda i, j: (i, j)),
        pl.BlockSpec((tb, D), lambda i, j: (i + off, 0)),
        pl.BlockSpec((1, D), lambda i, j: (0, 0)),
    ]
    args = [xn, u, v, gate, h1, nw]
    aliases = {}
    if out_prev is not None:
        in_specs.append(pl.BlockSpec(memory_space=pl.ANY))
        args.append(out_prev)
        aliases = {len(args) - 1: 0}
    return pl.pallas_call(
        _experts_kernel,
        grid=(tc // tb, n_exp // eb),
        in_specs=in_specs,
        out_specs=pl.BlockSpec((tb, D), lambda i, j: (i + off, 0)),
        out_shape=jax.ShapeDtypeStruct((T, D), F32),
        scratch_shapes=[pltpu.VMEM((tb, D), F32)],
        input_output_aliases=aliases,
        compiler_params=pltpu.CompilerParams(
            dimension_semantics=("arbitrary", "arbitrary"), vmem_limit_bytes=VMEM_LIMIT),
        name="experts",
    )(*args)


def _block_select_matrix():
    part = jnp.arange(HGRN_SUB * HGRN_DH, dtype=jnp.int32) // HGRN_DH
    col = jnp.arange(HGRN_CHUNK, dtype=jnp.int32) % HGRN_SUB
    return (part[:, None] == col[None, :]).astype(BF16)


def _forward(x, norm_mix_w, w_in, hgrn_lb_logits, hgrn_norm_w, fox_f_bias, w_branch_hgrn,
             w_branch_fox, w_out, norm_ffn_w, peer_w_q, peer_sub_keys, peer_u, peer_v,
             norm_final_w, *, tm_in, tq, tm_merge, tt, tb, eb):
    B, S, D = x.shape
    T = B * S
    n_h = 4 * HGRN_W
    n_f = 3 * FOX_W
    n_exp = N_KEYS * N_KEYS
    assert w_in.shape == (1, D, n_h + n_f + FOX_HEADS + 2 * D), w_in.shape
    assert hgrn_lb_logits.shape == (2, HGRN_W) and fox_f_bias.shape == (1, FOX_HEADS)
    assert peer_w_q.shape == (1, D, 2 * PEER_HEADS * PEER_HALF)
    assert peer_sub_keys.shape == (1, PEER_HEADS, 2, N_KEYS, PEER_HALF)
    assert peer_u.shape == (1, n_exp, D) and peer_v.shape == (1, n_exp, D)
    assert S % max(tm_in, tq, HGRN_CHUNK) == 0 and T % (PEER_CHUNKS * tb) == 0
    assert tb % tt == 0 and tb % tm_merge == 0 and n_exp % eb == 0
    x2d = x.reshape(T, D)
    wi = w_in[0]
    w_all = wi[:, :n_h + n_f].astype(BF16)
    w_gate = wi[:, n_h + n_f + FOX_HEADS:].astype(BF16)
    wff = wi[:, n_h + n_f:n_h + n_f + FOX_HEADS]
    wff_row = wff.T.astype(BF16)
    fb = fox_f_bias[0].astype(F32)
    fb_row = jnp.broadcast_to(fb.reshape(FOX_HEADS, 1), (FOX_HEADS, LANES))

    hg, fox, gates, crow = _in_proj(
        x2d, norm_mix_w[0].reshape(1, D), w_all, w_gate, wff_row, fb_row, B, S, tm_in)
    a = _hgrn(hg, hgrn_lb_logits, hgrn_norm_w[0].reshape(1, HGRN_W), _block_select_matrix(), B, S)
    b = _fox(fox, crow, B, S, tq)
    keys = peer_sub_keys[0].reshape(2 * PEER_HEADS, N_KEYS, PEER_HALF).astype(BF16)
    h1, xn2, st = _merge(
        x2d, a, b, gates, w_branch_hgrn[0].astype(BF16), w_branch_fox[0].astype(BF16),
        w_out[0].astype(BF16), norm_ffn_w[0].reshape(1, D), peer_w_q[0].astype(BF16), keys, tm_merge)
    u_bf = _split_expert_halves(peer_u[0].astype(BF16), eb)
    v_bf = _split_expert_halves(peer_v[0].astype(BF16), eb)
    tc = T // PEER_CHUNKS
    bounds = [0, tb, tc] + [c * tc for c in range(2, PEER_CHUNKS + 1)]
    out = None
    for t0, t1 in zip(bounds[:-1], bounds[1:]):
        ids, wts = _routes(st, tt, t0, t1 - t0)
        gate = _gate_matrix(ids, wts, n_exp, PEER_HEADS)
        out = _experts(xn2, u_bf, v_bf, gate, h1, norm_final_w.reshape(1, D), tb, eb, t0, out)
    return out.reshape(B, S, D)


def kernel(x, norm_mix_w, w_in, hgrn_lb_logits, hgrn_norm_w, fox_f_bias, w_branch_hgrn, w_branch_fox, w_out, norm_ffn_w, peer_w_q, peer_sub_keys, peer_u, peer_v, norm_final_w):
    return _forward(x, norm_mix_w, w_in, hgrn_lb_logits, hgrn_norm_w, fox_f_bias, w_branch_hgrn,
                    w_branch_fox, w_out, norm_ffn_w, peer_w_q, peer_sub_keys, peer_u, peer_v,
                    norm_final_w, **TILES)
```
